```python
import math
import jax
import jax.numpy as jnp
from jax import lax
import numpy as np

D_MODEL = 2048
BATCH = 8
SEQ = 2048
DEPTH = 2
DEC_BATCH = 32
DEC_SEQ = 16
PAST_LEN = 1024

CHUNK = 64
QBLOCK = 128
DH = 64
W_A = 3 * D_MODEL // 8
W_B = 3 * D_MODEL // 8
W_C = D_MODEL // 4
D_MIX = W_A + W_B + W_C
H_A = W_A // DH
H_B = W_B // DH
BAND_PAST = 8
MAX_REL = 256
NUM_REL = CHUNK + MAX_REL
SSM_GROUP = 16
G_C = W_C // SSM_GROUP
P_STATE = 64
DT_MIN = 1e-3
DT_MAX = 1e-1
IN_WIDTH = 3 * W_A + H_A + 3 * W_B + W_C
N_MEM = 256
H_M = 4
DH_M = 128
W_M = H_M * DH_M
D_FF = 5632
N_EXPERTS = 8
TOP_K = 2
D_FF_E = 2816
N_DENSE = (DEPTH + 1) // 2
N_MOE = DEPTH // 2
EPS = 1e-6
NEG_INF = -1e30

kernel_name = 'hybrid_streaming_encoder_step'

F32 = jnp.float32


def rmsnorm(x, g):
    xf = x.astype(F32)
    y = xf * lax.rsqrt(jnp.mean(xf * xf, axis=-1, keepdims=True) + EPS)
    return (y * g.astype(F32)).astype(x.dtype)


def mixer_inputs(h, w_in_l, b_f_l, g_qa_l, g_ka_l, g_qb_l, g_kb_l):
    bsz, L, _ = h.shape
    sizes = (W_A, W_A, W_A, H_A, W_B, W_B, W_B)
    idx = [sum(sizes[:i + 1]) for i in range(len(sizes))]
    qa, ka, va, fa, qb, kb, vb, uc = jnp.split(h @ w_in_l, idx, axis=-1)
    qa = rmsnorm(qa.reshape(bsz, L, H_A, DH), g_qa_l)
    ka = rmsnorm(ka.reshape(bsz, L, H_A, DH), g_ka_l)
    va = va.reshape(bsz, L, H_A, DH)
    logf = jax.nn.log_sigmoid(fa.astype(F32) + b_f_l.astype(F32))
    qb = rmsnorm(qb.reshape(bsz, L, H_B, DH), g_qb_l)
    kb = rmsnorm(kb.reshape(bsz, L, H_B, DH), g_kb_l)
    vb = vb.reshape(bsz, L, H_B, DH)
    return qa, ka, va, logf, qb, kb, vb, uc


def fox_attend(q, k, v, Fq, Fk, q_pos, k_pos):
    s = jnp.einsum('bqhd,bkhd->bhqk', q, k).astype(F32) * (DH ** -0.5)
    s = s + jnp.transpose(Fq, (0, 2, 1))[:, :, :, None] - jnp.transpose(Fk, (0, 2, 1))[:, :, None, :]
    s = jnp.where(k_pos[None, :] <= q_pos[:, None], s, NEG_INF)
    p = jax.nn.softmax(s, axis=-1).astype(v.dtype)
    return jnp.einsum('bhqk,bkhd->bqhd', p, v)


def fox_prompt(q, k, v, logf):
    bsz, L, H, D = q.shape
    F = jnp.cumsum(logf, axis=1)
    pos = jnp.arange(L)

    def block(i):
        start = i * QBLOCK
        qs = lax.dynamic_slice_in_dim(q, start, QBLOCK, axis=1)
        Fq = lax.dynamic_slice_in_dim(F, start, QBLOCK, axis=1)
        return fox_attend(qs, k, v, Fq, F, start + jnp.arange(QBLOCK), pos)

    o = lax.map(block, jnp.arange(L // QBLOCK))
    return jnp.moveaxis(o, 0, 1).reshape(bsz, L, H, D)


def fox_sample(q, k, v, logf, ck, cv, clogf):
    P = ck.shape[1]
    S = q.shape[1]
    k_all = jnp.concatenate([ck.astype(k.dtype), k], axis=1)
    v_all = jnp.concatenate([cv.astype(v.dtype), v], axis=1)
    F = jnp.cumsum(jnp.concatenate([clogf.astype(F32), logf], axis=1), axis=1)
    return fox_attend(q, k_all, v_all, F[:, P:], F, P + jnp.arange(S), jnp.arange(P + S))


def rel_bias_table(rel_bias_l, q_pos, k_pos):
    rel = jnp.clip(q_pos[:, None] - k_pos[None, :], -(CHUNK - 1), MAX_REL) + (CHUNK - 1)
    return rel_bias_l[:, rel].astype(F32)


def band_attend(q, k, v, bias, valid):
    s = jnp.einsum('...qhd,...khd->...hqk', q, k).astype(F32) * (DH ** -0.5) + bias
    if valid is not None:
        s = jnp.where(valid, s, NEG_INF)
    p = jax.nn.softmax(s, axis=-1).astype(v.dtype)
    return jnp.einsum('...hqk,...khd->...qhd', p, v)


def band_prompt(q, k, v, rel_bias_l):
    bsz, L, H, D = q.shape
    nc = L // CHUNK
    band = BAND_PAST + 1

    def padded_chunks(t):
        t = t.reshape(bsz, nc, CHUNK, H, D)
        return jnp.pad(t, ((0, 0), (BAND_PAST, 0), (0, 0), (0, 0), (0, 0)))

    idx = jnp.arange(nc)[:, None] + jnp.arange(band)[None, :]
    kb = padded_chunks(k)[:, idx].reshape(bsz, nc, band * CHUNK, H, D)
    vb = padded_chunks(v)[:, idx].reshape(bsz, nc, band * CHUNK, H, D)
    qc = q.reshape(bsz, nc, CHUNK, H, D)
    bias = rel_bias_table(rel_bias_l, BAND_PAST * CHUNK + jnp.arange(CHUNK), jnp.arange(band * CHUNK))
    valid = jnp.repeat(idx >= BAND_PAST, CHUNK, axis=1)[:, None, None, :]
    o = band_attend(qc, kb, vb, bias, valid)
    return o.reshape(bsz, L, H, D)


def band_sample(q, k, v, ck, cv, rel_bias_l):
    R = ck.shape[1]
    S = q.shape[1]
    k_all = jnp.concatenate([ck.astype(k.dtype), k], axis=1)
    v_all = jnp.concatenate([cv.astype(v.dtype), v], axis=1)
    bias = rel_bias_table(rel_bias_l, R + jnp.arange(S), jnp.arange(R + S))
    return band_attend(q, k_all, v_all, bias, None)


def s5_discretise(lam_re, lam_im, log_dt, b_re, b_im):
    lr = lam_re.astype(F32)
    li = lam_im.astype(F32)
    dt = jnp.exp(log_dt.astype(F32))[:, None]
    mag = jnp.exp(lr * dt)
    a_re = mag * jnp.cos(li * dt)
    a_im = mag * jnp.sin(li * dt)
    den = lr * lr + li * li
    num_re = a_re - 1.0
    coef_re = (num_re * lr + a_im * li) / den
    coef_im = (a_im * lr - num_re * li) / den
    br = b_re.astype(F32)
    bi = b_im.astype(F32)
    bb_re = coef_re[..., None] * br - coef_im[..., None] * bi
    bb_im = coef_re[..., None] * bi + coef_im[..., None] * br
    return a_re, a_im, bb_re, bb_im


def complex_affine(e1, e2):
    a1r, a1i, b1r, b1i = e1
    a2r, a2i, b2r, b2i = e2
    return (a2r * a1r - a2i * a1i, a2r * a1i + a2i * a1r,
            a2r * b1r - a2i * b1i + b2r, a2r * b1i + a2i * b1r + b2i)


def s5_mixer(uc, s5p, x0_re, x0_im):
    lam_re, lam_im, log_dt, b_re, b_im, c_re, c_im, d_skip, w_glu, b_glu = s5p
    bsz, L, _ = uc.shape
    u = uc.astype(F32).reshape(bsz, L, G_C, SSM_GROUP)
    a_re, a_im, bb_re, bb_im = s5_discretise(lam_re, lam_im, log_dt, b_re, b_im)
    bu_re = jnp.einsum('gpc,blgc->blgp', bb_re, u)
    bu_im = jnp.einsum('gpc,blgc->blgp', bb_im, u)
    if x0_re is not None:
        x0r = x0_re.astype(F32)
        x0i = x0_im.astype(F32)
        bu_re = bu_re.at[:, 0].add(a_re * x0r - a_im * x0i)
        bu_im = bu_im.at[:, 0].add(a_re * x0i + a_im * x0r)
    ar = jnp.broadcast_to(a_re, bu_re.shape)
    ai = jnp.broadcast_to(a_im, bu_re.shape)
    _, _, xr, xi = lax.associative_scan(complex_affine, (ar, ai, bu_re, bu_im), axis=1)
    y = (jnp.einsum('gcp,blgp->blgc', c_re.astype(F32), xr)
         - jnp.einsum('gcp,blgp->blgc', c_im.astype(F32), xi)
         + d_skip.astype(F32) * u)
    z = jax.nn.gelu(y.reshape(bsz, L, W_C))
    out = z * jax.nn.sigmoid(z @ w_glu.astype(F32) + b_glu.astype(F32))
    return out.astype(uc.dtype), xr[:, -1], xi[:, -1]


def merge_out(oa, ob, oc, g_out_l, w_out_l):
    bsz, L = oa.shape[0], oa.shape[1]
    o = jnp.concatenate([rmsnorm(oa.reshape(bsz, L, W_A), g_out_l[:W_A]),
                         rmsnorm(ob.reshape(bsz, L, W_B), g_out_l[W_A:W_A + W_B]),
                         rmsnorm(oc, g_out_l[W_A + W_B:])], axis=-1)
    return o @ w_out_l


def mem_kv(mem, g_mem_l, w_ck_l, w_cv_l, g_ck_l):
    m = rmsnorm(mem, g_mem_l)
    bsz, n, _ = m.shape
    k = rmsnorm((m @ w_ck_l).reshape(bsz, n, H_M, DH_M), g_ck_l)
    v = (m @ w_cv_l).reshape(bsz, n, H_M, DH_M)
    return k, v


def cross_attend(h, k, v, w_cq_l, g_cq_l, w_co_l):
    bsz, L, _ = h.shape
    q = rmsnorm((h @ w_cq_l).reshape(bsz, L, H_M, DH_M), g_cq_l)
    s = jnp.einsum('bqhd,bkhd->bhqk', q, k.astype(q.dtype)).astype(F32) * (DH_M ** -0.5)
    p = jax.nn.softmax(s, axis=-1).astype(v.dtype)
    o = jnp.einsum('bhqk,bkhd->bqhd', p, v).reshape(bsz, L, W_M)
    return o @ w_co_l


def swiglu(h, w1, w3, w2):
    return (jax.nn.silu(h @ w1) * (h @ w3)) @ w2


def moe_swiglu(h, w_router_i, w_e1_i, w_e3_i, w_e2_i):
    probs = jax.nn.softmax((h @ w_router_i).astype(F32), axis=-1)
    top_p, top_i = lax.top_k(probs, TOP_K)
    top_p = top_p / jnp.sum(top_p, axis=-1, keepdims=True)
    gates = jnp.sum(jax.nn.one_hot(top_i, N_EXPERTS, dtype=F32) * top_p[..., None], axis=-2)
    out = jnp.zeros(h.shape, F32)
    for e in range(N_EXPERTS):
        out = out + gates[..., e:e + 1] * swiglu(h, w_e1_i[e], w_e3_i[e], w_e2_i[e]).astype(F32)
    return out.astype(h.dtype)


def channel_mix(h, l, w_ff1, w_ff3, w_ff2, w_router, w_e1, w_e3, w_e2):
    i = l // 2
    if l % 2 == 0:
        return swiglu(h, w_ff1[i], w_ff3[i], w_ff2[i])
    return moe_swiglu(h, w_router[i], w_e1[i], w_e3[i], w_e2[i])


def setup_inputs(seed: int = 0) -> dict:
    key = jax.random.key(seed)
    ks = iter(jax.random.split(key, 64))

    def nrm(shape, scale):
        return scale * jax.random.normal(next(ks), shape, F32)

    def gain(shape):
        return 1.0 + nrm(shape, 0.02)

    band_rows = min(BAND_PAST * CHUNK, PAST_LEN)
    lam_im0 = math.pi * jnp.arange(P_STATE, dtype=F32)
    return {
        'x_prompt': nrm((BATCH, SEQ, D_MODEL), 1.0),
        'x_sample': nrm((DEC_BATCH, DEC_SEQ, D_MODEL), 1.0),
        'mem_prompt': nrm((BATCH, N_MEM, D_MODEL), 1.0),
        'cache_fox_k': nrm((DEPTH, DEC_BATCH, PAST_LEN, H_A, DH), 1.0),
        'cache_fox_v': nrm((DEPTH, DEC_BATCH, PAST_LEN, H_A, DH), 1.0),
        'cache_fox_logf': jax.nn.log_sigmoid(2.5 + nrm((DEPTH, DEC_BATCH, PAST_LEN, H_A), 1.0)),
        'cache_band_k': nrm((DEPTH, DEC_BATCH, band_rows, H_B, DH), 1.0),
        'cache_band_v': nrm((DEPTH, DEC_BATCH, band_rows, H_B, DH), 1.0),
        'state_ssm_re': nrm((DEPTH, DEC_BATCH, G_C, P_STATE), 0.5),
        'state_ssm_im': nrm((DEPTH, DEC_BATCH, G_C, P_STATE), 0.5),
        'cache_mem_k': nrm((DEPTH, DEC_BATCH, N_MEM, H_M, DH_M), 1.0),
        'cache_mem_v': nrm((DEPTH, DEC_BATCH, N_MEM, H_M, DH_M), 1.0),
        'g_mix': gain((DEPTH, D_MODEL)),
        'w_in': nrm((DEPTH, D_MODEL, IN_WIDTH), D_MODEL ** -0.5),
        'b_f': jax.random.uniform(next(ks), (DEPTH, H_A), F32, 1.0, 4.0),
        'g_qa': gain((DEPTH, DH)),
        'g_ka': gain((DEPTH, DH)),
        'g_qb': gain((DEPTH, DH)),
        'g_kb': gain((DEPTH, DH)),
        'rel_bias': nrm((DEPTH, H_B, NUM_REL), 0.1),
        'lam_re': -0.5 + nrm((DEPTH, G_C, P_STATE), 0.01),
        'lam_im': lam_im0 + nrm((DEPTH, G_C, P_STATE), 0.01),
        'log_dt': jax.random.uniform(next(ks), (DEPTH, G_C), F32, math.log(DT_MIN), math.log(DT_MAX)),
        'ssm_b_re': nrm((DEPTH, G_C, P_STATE, SSM_GROUP), (2 * SSM_GROUP) ** -0.5),
        'ssm_b_im': nrm((DEPTH, G_C, P_STATE, SSM_GROUP), (2 * SSM_GROUP) ** -0.5),
        'ssm_c_re': nrm((DEPTH, G_C, SSM_GROUP, P_STATE), (2 * P_STATE) ** -0.5),
        'ssm_c_im': nrm((DEPTH, G_C, SSM_GROUP, P_STATE), (2 * P_STATE) ** -0.5),
        'ssm_d': nrm((DEPTH, G_C, SSM_GROUP), 1.0),
        'w_glu': nrm((DEPTH, W_C, W_C), W_C ** -0.5),
        'b_glu': nrm((DEPTH, W_C), 0.02),
        'g_mix_out': gain((DEPTH, D_MIX)),
        'w_out': nrm((DEPTH, D_MIX, D_MODEL), D_MIX ** -0.5),
        'g_cross': gain((DEPTH, D_MODEL)),
        'g_mem': gain((DEPTH, D_MODEL)),
        'w_cq': nrm((DEPTH, D_MODEL, W_M), D_MODEL ** -0.5),
        'w_ck': nrm((DEPTH, D_MODEL, W_M), D_MODEL ** -0.5),
        'w_cv': nrm((DEPTH, D_MODEL, W_M), D_MODEL ** -0.5),
        'g_cq': gain((DEPTH, DH_M)),
        'g_ck': gain((DEPTH, DH_M)),
        'w_co': nrm((DEPTH, W_M, D_MODEL), W_M ** -0.5),
        'g_ffn': gain((DEPTH, D_MODEL)),
        'w_ff1': nrm((N_DENSE, D_MODEL, D_FF), D_MODEL ** -0.5),
        'w_ff3': nrm((N_DENSE, D_MODEL, D_FF), D_MODEL ** -0.5),
        'w_ff2': nrm((N_DENSE, D_FF, D_MODEL), D_FF ** -0.5),
        'w_router': nrm((N_MOE, D_MODEL, N_EXPERTS), D_MODEL ** -0.5),
        'w_e1': nrm((N_MOE, N_EXPERTS, D_MODEL, D_FF_E), D_MODEL ** -0.5),
        'w_e3': nrm((N_MOE, N_EXPERTS, D_MODEL, D_FF_E), D_MODEL ** -0.5),
        'w_e2': nrm((N_MOE, N_EXPERTS, D_FF_E, D_MODEL), D_FF_E ** -0.5),
    }


def reference(x_prompt, x_sample, mem_prompt, cache_fox_k, cache_fox_v, cache_fox_logf,
              cache_band_k, cache_band_v, state_ssm_re, state_ssm_im, cache_mem_k, cache_mem_v,
              g_mix, w_in, b_f, g_qa, g_ka, g_qb, g_kb, rel_bias, lam_re, lam_im, log_dt,
              ssm_b_re, ssm_b_im, ssm_c_re, ssm_c_im, ssm_d, w_glu, b_glu, g_mix_out, w_out,
              g_cross, g_mem, w_cq, w_ck, w_cv, g_cq, g_ck, w_co, g_ffn, w_ff1, w_ff3, w_ff2,
              w_router, w_e1, w_e3, w_e2):
    xp = x_prompt
    xs = x_sample
    p_fk, p_fv, p_fl, p_bk, p_bv, p_sr, p_si, p_mk, p_mv = [], [], [], [], [], [], [], [], []
    s_fk, s_fv, s_fl, s_bk, s_bv, s_sr, s_si = [], [], [], [], [], [], []
    n_band = min(BAND_PAST * CHUNK, xp.shape[1])
    for l in range(DEPTH):
        s5p = (lam_re[l], lam_im[l], log_dt[l], ssm_b_re[l], ssm_b_im[l],
               ssm_c_re[l], ssm_c_im[l], ssm_d[l], w_glu[l], b_glu[l])
        qa, ka, va, lf, qb, kb, vb, uc = mixer_inputs(rmsnorm(xp, g_mix[l]), w_in[l], b_f[l],
                                                      g_qa[l], g_ka[l], g_qb[l], g_kb[l])
        oa = fox_prompt(qa, ka, va, lf)
        ob = band_prompt(qb, kb, vb, rel_bias[l])
        oc, sr, si = s5_mixer(uc, s5p, None, None)
        xp = xp + merge_out(oa, ob, oc, g_mix_out[l], w_out[l])
        p_fk.append(ka)
        p_fv.append(va)
        p_fl.append(lf)
        p_bk.append(kb[:, kb.shape[1] - n_band:])
        p_bv.append(vb[:, vb.shape[1] - n_band:])
        p_sr.append(sr)
        p_si.append(si)
        qa, ka, va, lf, qb, kb, vb, uc = mixer_inputs(rmsnorm(xs, g_mix[l]), w_in[l], b_f[l],
                                                      g_qa[l], g_ka[l], g_qb[l], g_kb[l])
        oa = fox_sample(qa, ka, va, lf, cache_fox_k[l], cache_fox_v[l], cache_fox_logf[l])
        ob = band_sample(qb, kb, vb, cache_band_k[l], cache_band_v[l], rel_bias[l])
        oc, sr, si = s5_mixer(uc, s5p, state_ssm_re[l], state_ssm_im[l])
        xs = xs + merge_out(oa, ob, oc, g_mix_out[l], w_out[l])
        s_fk.append(ka)
        s_fv.append(va)
        s_fl.append(lf)
        s_bk.append(kb)
        s_bv.append(vb)
        s_sr.append(sr)
        s_si.append(si)
        mk, mv = mem_kv(mem_prompt, g_mem[l], w_ck[l], w_cv[l], g_ck[l])
        p_mk.append(mk)
        p_mv.append(mv)
        xp = xp + cross_attend(rmsnorm(xp, g_cross[l]), mk, mv, w_cq[l], g_cq[l], w_co[l])
        xs = xs + cross_attend(rmsnorm(xs, g_cross[l]), cache_mem_k[l], cache_mem_v[l],
                               w_cq[l], g_cq[l], w_co[l])
        xp = xp + channel_mix(rmsnorm(xp, g_ffn[l]), l, w_ff1, w_ff3, w_ff2, w_router, w_e1, w_e3, w_e2)
        xs = xs + channel_mix(rmsnorm(xs, g_ffn[l]), l, w_ff1, w_ff3, w_ff2, w_router, w_e1, w_e3, w_e2)
    y_prompt = xp
    y_sample = xs
    p_fox_k = jnp.stack(p_fk)
    p_fox_v = jnp.stack(p_fv)
    p_fox_logf = jnp.stack(p_fl)
    p_band_k = jnp.stack(p_bk)
    p_band_v = jnp.stack(p_bv)
    p_ssm_re = jnp.stack(p_sr)
    p_ssm_im = jnp.stack(p_si)
    p_mem_k = jnp.stack(p_mk)
    p_mem_v = jnp.stack(p_mv)
    s_fox_k = jnp.stack(s_fk)
    s_fox_v = jnp.stack(s_fv)
    s_fox_logf = jnp.stack(s_fl)
    s_band_k = jnp.stack(s_bk)
    s_band_v = jnp.stack(s_bv)
    s_ssm_re = jnp.stack(s_sr)
    s_ssm_im = jnp.stack(s_si)
    return (y_prompt, y_sample, p_fox_k, p_fox_v, p_fox_logf, p_band_k, p_band_v, p_ssm_re, p_ssm_im,
            p_mem_k, p_mem_v, s_fox_k, s_fox_v, s_fox_logf, s_band_k, s_band_v, s_ssm_re, s_ssm_im)
```

```python
import functools
import math

import jax
import jax.numpy as jnp
from jax import lax
from jax.experimental import pallas as pl
from jax.experimental.pallas import tpu as pltpu

F32 = jnp.float32
BF16 = jnp.bfloat16
I32 = jnp.int32
U32 = jnp.uint32

EPS = 1e-6
NEG_INF = -1e30
CHUNK = 64
BAND_PAST = 8
SSM_GROUP = 16
TOP_K = 2

LANES = 128
SUBLANES = 8
BF16_ROWS = 16
VMEM_CAP = 60 * 1024 * 1024


def _vmem_limit(nbytes):
    return int(min(VMEM_CAP, max(16 * 1024 * 1024, nbytes * 5 // 4 + (4 << 20))))


def _pick(n, candidates):
    for c in candidates:
        if c <= n and n % c == 0:
            return c
    raise ValueError(f"no tile for {n} in {candidates}")


def _nbytes(shape, dtype):
    return math.prod(shape) * jnp.dtype(dtype).itemsize


def _split3(x):
    hi = x.astype(BF16)
    r1 = x - hi.astype(F32)
    mid = r1.astype(BF16)
    lo = (r1 - mid.astype(F32)).astype(BF16)
    return hi, mid, lo


def _dot(a, b):
    return jnp.dot(a, b, preferred_element_type=F32)


def _dot_nt(a, b):
    return lax.dot_general(a, b, (((1,), (1,)), ((), ())), preferred_element_type=F32)


def _dot_hp(a, b, passes=3):
    ah = a.astype(BF16)
    bh = b.astype(BF16)
    if passes == 1:
        return _dot(ah, bh)
    al = (a - ah.astype(F32)).astype(BF16)
    bl = (b - bh.astype(F32)).astype(BF16)
    return _dot(ah, bh) + (_dot(ah, bl) + _dot(al, bh))


def _pair_ones():
    r = lax.broadcasted_iota(I32, (LANES, LANES), 0) // 64
    c = lax.broadcasted_iota(I32, (LANES, LANES), 1) // 64
    return (r == c).astype(BF16)


def _pair_norm(x, g, ones):
    sq = x * x
    hi = sq.astype(BF16)
    lo = (sq - hi.astype(F32)).astype(BF16)
    ss = _dot(hi, ones) + _dot(lo, ones)
    return x * lax.rsqrt(ss * (1.0 / 64.0) + EPS) * g


def _log_sigmoid(x):
    return jnp.minimum(x, 0.0) - jnp.log(1.0 + jnp.exp(-jnp.abs(x)))


def _gelu_tanh(x):
    c = math.sqrt(2.0 / math.pi)
    return 0.5 * x * (1.0 + jnp.tanh(c * (x + 0.044715 * (x * x * x))))


def _mm_body(*refs, widths, norm, n_w, epilogue, has_res, has_gg, stage_a, tm, rc, kc, k_total):
    it = iter(refs)
    a_refs = [next(it) for _ in widths]
    g_ref = next(it) if norm else None
    w_refs = [next(it) for _ in range(n_w)]
    gg_ref = next(it) if has_gg else None
    res_ref = next(it) if has_res else None
    o_ref = next(it)
    abf = next(it) if stage_a else a_refs[0]
    wbfs = [next(it) for _ in range(n_w)]
    j = pl.program_id(1)

    if stage_a:
        @pl.when(j == 0)
        def _():
            off = 0
            for a_ref, wd in zip(a_refs, widths):
                def chunk(c, carry, a_ref=a_ref, off=off, wd=wd):
                    r0 = pl.multiple_of(c * rc, rc)
                    x = a_ref[pl.ds(r0, rc), :].astype(F32)
                    if norm:
                        ms = jnp.mean(x * x, axis=-1, keepdims=True)
                        x = x * lax.rsqrt(ms + EPS) * g_ref[:, off:off + wd]
                    abf[pl.ds(r0, rc), off:off + wd] = x.astype(BF16)
                    return carry
                lax.fori_loop(0, tm // rc, chunk, 0)
                off += wd

    for w_ref, wbf in zip(w_refs, wbfs):
        def cast(c, carry, w_ref=w_ref, wbf=wbf):
            r0 = pl.multiple_of(c * kc, kc)
            wbf[pl.ds(r0, kc), :] = w_ref[pl.ds(r0, kc), :].astype(BF16)
            return carry
        lax.fori_loop(0, k_total // kc, cast, 0)

    def rows(c, carry):
        r0 = pl.multiple_of(c * rc, rc)
        a = abf[pl.ds(r0, rc), :]
        ys = [_dot(a, wbf[...]) for wbf in wbfs]
        if epilogue == "swiglu":
            y = jax.nn.silu(ys[0]) * ys[1]
        elif epilogue == "group_norm":
            parts = []
            for s in range(ys[0].shape[1] // LANES):
                ysl = ys[0][:, s * LANES:(s + 1) * LANES]
                ms = jnp.mean(ysl * ysl, axis=-1, keepdims=True)
                parts.append(ysl * lax.rsqrt(ms + EPS) * gg_ref[:, s * LANES:(s + 1) * LANES])
            y = jnp.concatenate(parts, axis=1)
        else:
            y = ys[0]
        if has_res:
            y = y + res_ref[pl.ds(r0, rc), :]
        o_ref[pl.ds(r0, rc), :] = y.astype(o_ref.dtype)
        return carry
    lax.fori_loop(0, tm // rc, rows, 0)


def _mm(a_parts, w_list, *, gain=None, group_gain=None, residual=None, epilogue="none",
        out_dtype=F32, tm, tn, name):
    m = a_parts[0].shape[0]
    widths = tuple(a.shape[1] for a in a_parts)
    k_total = sum(widths)
    n = w_list[0].shape[1]
    assert m % tm == 0 and n % tn == 0, (m, tm, n, tn)
    norm = gain is not None
    stage_a = norm or len(a_parts) > 1 or a_parts[0].dtype != BF16
    rc = _pick(tm, (512, 384, 352, 256, 176, 128, 64, 32, 16))
    kc = _pick(k_total, (512, 256, 128))
    grid = (m // tm, n // tn)
    in_specs = [pl.BlockSpec((tm, wd), lambda i, j: (i, 0)) for wd in widths]
    args = list(a_parts)
    est = sum(2 * _nbytes((tm, wd), a.dtype) for wd, a in zip(widths, a_parts))
    if norm:
        in_specs.append(pl.BlockSpec((1, k_total), lambda i, j: (0, 0)))
        args.append(gain)
    for w in w_list:
        in_specs.append(pl.BlockSpec((k_total, tn), lambda i, j: (0, j)))
        args.append(w)
        est += 2 * _nbytes((k_total, tn), F32) + _nbytes((k_total, tn), BF16)
    if group_gain is not None:
        in_specs.append(pl.BlockSpec((1, tn), lambda i, j: (0, j)))
        args.append(group_gain)
    if residual is not None:
        in_specs.append(pl.BlockSpec((tm, tn), lambda i, j: (i, j)))
        args.append(residual)
        est += 2 * _nbytes((tm, tn), F32)
    est += 2 * _nbytes((tm, tn), out_dtype) + stage_a * _nbytes((tm, k_total), BF16)
    est += 4 * _nbytes((rc, tn), F32) * len(w_list)
    body = functools.partial(
        _mm_body, widths=widths, norm=norm, n_w=len(w_list), epilogue=epilogue,
        has_res=residual is not None, has_gg=group_gain is not None, stage_a=stage_a, tm=tm,
        rc=rc, kc=kc, k_total=k_total)
    return pl.pallas_call(
        body,
        out_shape=jax.ShapeDtypeStruct((m, n), out_dtype),
        grid=grid,
        in_specs=in_specs,
        out_specs=pl.BlockSpec((tm, tn), lambda i, j: (i, j)),
        scratch_shapes=[pltpu.VMEM((tm, k_total), BF16)] * stage_a
        + [pltpu.VMEM((k_total, tn), BF16) for _ in w_list],
        compiler_params=pltpu.CompilerParams(
            dimension_semantics=("parallel", "arbitrary"),
            vmem_limit_bytes=_vmem_limit(est)),
        name=name,
    )(*args)


def _cumsum_body(x_ref, b_ref, lf_ref, f_ref, carry, *, raw_from, valid_to, tt):
    j = pl.program_id(0)

    @pl.when(j == 0)
    def _():
        carry[...] = jnp.zeros_like(carry)

    x = x_ref[...]
    lane = j * tt + lax.broadcasted_iota(I32, x.shape, 1)
    lf = jnp.where(lane >= raw_from, _log_sigmoid(x + b_ref[:, 0:1]), x)
    lf = jnp.where(lane < valid_to, lf, 0.0)
    lf_ref[...] = lf
    tri = (lax.broadcasted_iota(I32, (tt, tt), 0)
           <= lax.broadcasted_iota(I32, (tt, tt), 1)).astype(BF16)
    hi, mid, lo = _split3(lf)
    y = _dot(hi, tri) + _dot(mid, tri) + _dot(lo, tri) + carry[:, 0:1]
    f_ref[...] = y
    carry[...] = jnp.broadcast_to(y[:, tt - 1:tt], carry.shape)


def _logf_cumsum(x, bias, *, raw_from, valid_to):
    r, t = x.shape
    tt = _pick(t, (256, 128))
    return pl.pallas_call(
        functools.partial(_cumsum_body, raw_from=raw_from, valid_to=valid_to, tt=tt),
        out_shape=(jax.ShapeDtypeStruct((r, t), F32), jax.ShapeDtypeStruct((r, t), F32)),
        grid=(t // tt,),
        in_specs=[pl.BlockSpec((r, tt), lambda j: (0, j)),
                  pl.BlockSpec((r, LANES), lambda j: (0, 0))],
        out_specs=(pl.BlockSpec((r, tt), lambda j: (0, j)),
                   pl.BlockSpec((r, tt), lambda j: (0, j))),
        scratch_shapes=[pltpu.VMEM((r, LANES), F32)],
        compiler_params=pltpu.CompilerParams(dimension_semantics=("arbitrary",)),
        name="logf_cumsum",
    )(x, bias)


def _stack_heads(qn):
    lane = lax.broadcasted_iota(I32, qn.shape, 1)
    q0 = jnp.where(lane < 64, qn, 0.0)
    q1 = jnp.where(lane < 64, 0.0, qn)
    return jnp.concatenate([q0, q1], axis=0).astype(BF16)


def _unstack_heads(o, tq):
    lane = lax.broadcasted_iota(I32, (tq, LANES), 1)
    return jnp.where(lane < 64, o[:tq], o[tq:])


def _lane_column(block, h):
    lane = lax.broadcasted_iota(I32, block.shape, 1)
    return jnp.sum(jnp.where(lane == h, block, 0.0), axis=-1, keepdims=True)


def _fox_prompt_body(q_ref, k_ref, v_ref, f_ref, ft_ref, gq_ref, gk_ref,
                     o_ref, kn_ref, kbf, vbf, m_s, l_s, acc_s, *, tq, seq, dh):
    hp = pl.program_id(1)
    qi = pl.program_id(2)
    ones = _pair_ones()
    pc = 256 if seq % 256 == 0 else tq

    @pl.when(qi == 0)
    def _():
        def prep(c, carry):
            r0 = pl.multiple_of(c * pc, pc)
            kn = _pair_norm(k_ref[pl.ds(r0, pc), :], gk_ref[...], ones)
            kn_ref[pl.ds(r0, pc), :] = kn
            kbf[pl.ds(r0, pc), :] = kn.astype(BF16)
            vbf[pl.ds(r0, pc), :] = v_ref[pl.ds(r0, pc), :].astype(BF16)
            return carry
        lax.fori_loop(0, seq // pc, prep, 0)

    qn = _pair_norm(q_ref[...], gq_ref[...], ones) * (dh ** -0.5)
    qs = _stack_heads(qn)
    fblk = f_ref[...]
    fq = jnp.concatenate([_lane_column(fblk, 2 * hp), _lane_column(fblk, 2 * hp + 1)], axis=0)
    m_s[...] = jnp.full(m_s.shape, NEG_INF, F32)
    l_s[...] = jnp.zeros(l_s.shape, F32)
    acc_s[...] = jnp.zeros(acc_s.shape, F32)

    def step(kj, masked):
        r0 = pl.multiple_of(kj * tq, tq)
        s = _dot_nt(qs, kbf[pl.ds(r0, tq), :])
        fk0 = ft_ref[kj, pl.ds(2 * hp, 1), :]
        fk1 = ft_ref[kj, pl.ds(2 * hp + 1, 1), :]
        fk = jnp.concatenate([jnp.broadcast_to(fk0, (tq, tq)),
                              jnp.broadcast_to(fk1, (tq, tq))], axis=0)
        s = s + fq - fk
        if masked:
            row = lax.broadcasted_iota(I32, (tq, tq), 0)
            col = lax.broadcasted_iota(I32, (tq, tq), 1)
            keep = jnp.concatenate([col <= row, col <= row], axis=0)
            s = jnp.where(keep, s, NEG_INF)
        m_old = m_s[...]
        m_new = jnp.maximum(m_old, jnp.max(s, axis=-1, keepdims=True))
        alpha = jnp.exp(m_old - m_new)
        p = jnp.exp(s - m_new[:, 0:1])
        l_s[...] = alpha * l_s[...] + jnp.sum(p, axis=-1, keepdims=True)
        acc_s[...] = alpha * acc_s[...] + _dot(p.astype(BF16), vbf[pl.ds(r0, tq), :])
        m_s[...] = m_new

    def loop_body(kj, carry):
        step(kj, False)
        return carry
    lax.fori_loop(0, qi, loop_body, 0)
    step(qi, True)
    o = acc_s[...] / l_s[...]
    o_ref[...] = _unstack_heads(o, tq)


def _fox_prompt(proj, f_col, f_row, gq, gk, *, batch, seq, n_pairs, col_q, col_k, col_v, m_rows):
    tq = _pick(seq, (256, 128))
    nq = seq // tq
    width = n_pairs * LANES
    est = (2 * 2 * _nbytes((seq, LANES), F32) + 2 * _nbytes((seq, LANES), F32)
           + 2 * _nbytes((seq, LANES), BF16) + 16 * _nbytes((2 * tq, tq), F32))
    return pl.pallas_call(
        functools.partial(_fox_prompt_body, tq=tq, seq=seq, dh=64),
        out_shape=(jax.ShapeDtypeStruct((m_rows, width), F32),
                   jax.ShapeDtypeStruct((batch * seq, width), F32)),
        grid=(batch, n_pairs, nq),
        in_specs=[
            pl.BlockSpec((tq, LANES), lambda b, hp, qi: (b * nq + qi, col_q + hp)),
            pl.BlockSpec((seq, LANES), lambda b, hp, qi: (b, col_k + hp)),
            pl.BlockSpec((seq, LANES), lambda b, hp, qi: (b, col_v + hp)),
            pl.BlockSpec((tq, LANES), lambda b, hp, qi: (b * nq + qi, 0)),
            pl.BlockSpec((None, nq, 16, tq), lambda b, hp, qi: (b, 0, 0, 0)),
            pl.BlockSpec((1, LANES), lambda b, hp, qi: (0, 0)),
            pl.BlockSpec((1, LANES), lambda b, hp, qi: (0, 0)),
        ],
        out_specs=(pl.BlockSpec((tq, LANES), lambda b, hp, qi: (b * nq + qi, hp)),
                   pl.BlockSpec((seq, LANES), lambda b, hp, qi: (b, hp))),
        scratch_shapes=[pltpu.VMEM((seq, LANES), BF16), pltpu.VMEM((seq, LANES), BF16),
                        pltpu.VMEM((2 * tq, LANES), F32), pltpu.VMEM((2 * tq, LANES), F32),
                        pltpu.VMEM((2 * tq, LANES), F32)],
        compiler_params=pltpu.CompilerParams(
            dimension_semantics=("parallel", "parallel", "arbitrary"),
            vmem_limit_bytes=_vmem_limit(est)),
        name="fox_prompt",
    )(proj, proj, proj, f_col, f_row, gq, gk)


def _band_prompt_body(q_ref, k_ref, v_ref, tab_ref, gq_ref, gk_ref,
                      o_ref, kn_ref, kpad, vpad, *, tq, seq, past, dh):
    qi = pl.program_id(2)
    ones = _pair_ones()
    win = past + tq
    pc = 256 if seq % 256 == 0 else tq

    @pl.when(qi == 0)
    def _():
        kpad[0:past, :] = jnp.zeros((past, LANES), BF16)
        vpad[0:past, :] = jnp.zeros((past, LANES), BF16)

        def prep(c, carry):
            r0 = pl.multiple_of(c * pc, pc)
            kn = _pair_norm(k_ref[pl.ds(r0, pc), :], gk_ref[...], ones)
            kn_ref[pl.ds(r0, pc), :] = kn
            kpad[pl.ds(past + r0, pc), :] = kn.astype(BF16)
            vpad[pl.ds(past + r0, pc), :] = v_ref[pl.ds(r0, pc), :].astype(BF16)
            return carry
        lax.fori_loop(0, seq // pc, prep, 0)

    qn = _pair_norm(q_ref[...], gq_ref[...], ones) * (dh ** -0.5)
    qs = _stack_heads(qn)
    w0 = pl.multiple_of(qi * tq, tq)
    s = _dot_nt(qs, kpad[pl.ds(w0, win), :])
    s = s + jnp.concatenate([tab_ref[0], tab_ref[1]], axis=0)
    col = lax.broadcasted_iota(I32, (2 * tq, win), 1)
    s = jnp.where(col >= past - qi * tq, s, NEG_INF)
    m = jnp.max(s, axis=-1, keepdims=True)
    p = jnp.exp(s - m)
    l = jnp.sum(p, axis=-1, keepdims=True)
    o = _dot(p.astype(BF16), vpad[pl.ds(w0, win), :]) / l
    o_ref[...] = _unstack_heads(o, tq)


def _band_prompt(proj, tab, gq, gk, *, batch, seq, n_pairs, col_q, col_k, col_v, tq, m_rows):
    past = BAND_PAST * CHUNK
    nq = seq // tq
    width = n_pairs * LANES
    win = past + tq
    est = (2 * 2 * _nbytes((seq, LANES), F32) + 2 * _nbytes((seq, LANES), F32)
           + 2 * _nbytes((seq + past, LANES), BF16) + 2 * _nbytes((2, tq, win), F32)
           + 12 * _nbytes((2 * tq, win), F32))
    return pl.pallas_call(
        functools.partial(_band_prompt_body, tq=tq, seq=seq, past=past, dh=64),
        out_shape=(jax.ShapeDtypeStruct((m_rows, width), F32),
                   jax.ShapeDtypeStruct((batch * seq, width), F32)),
        grid=(batch, n_pairs, nq),
        in_specs=[
            pl.BlockSpec((tq, LANES), lambda b, hp, qi: (b * nq + qi, col_q + hp)),
            pl.BlockSpec((seq, LANES), lambda b, hp, qi: (b, col_k + hp)),
            pl.BlockSpec((seq, LANES), lambda b, hp, qi: (b, col_v + hp)),
            pl.BlockSpec((None, 2, tq, win), lambda b, hp, qi: (hp, 0, 0, 0)),
            pl.BlockSpec((1, LANES), lambda b, hp, qi: (0, 0)),
            pl.BlockSpec((1, LANES), lambda b, hp, qi: (0, 0)),
        ],
        out_specs=(pl.BlockSpec((tq, LANES), lambda b, hp, qi: (b * nq + qi, hp)),
                   pl.BlockSpec((seq, LANES), lambda b, hp, qi: (b, hp))),
        scratch_shapes=[pltpu.VMEM((seq + past, LANES), BF16),
                        pltpu.VMEM((seq + past, LANES), BF16)],
        compiler_params=pltpu.CompilerParams(
            dimension_semantics=("parallel", "parallel", "arbitrary"),
            vmem_limit_bytes=_vmem_limit(est)),
        name="band_prompt",
    )(proj, proj, proj, tab, gq, gk)


def _sample_attn_body(*refs, mode, n_pairs, s_new, n_cache, dh):
    if mode == "fox":
        (q_ref, k_ref, v_ref, ck_ref, cv_ref, fq_ref, ft_ref, gq_ref, gk_ref, _alias,
         o_ref, kn_ref) = refs
    else:
        (q_ref, k_ref, v_ref, ck_ref, cv_ref, tabc_ref, tabn_ref, gq_ref, gk_ref, _alias,
         o_ref, kn_ref) = refs
    ones = _pair_ones()
    row = lax.broadcasted_iota(I32, (2 * s_new, s_new), 0) % s_new
    col = lax.broadcasted_iota(I32, (2 * s_new, s_new), 1)
    for hp in range(n_pairs):
        sl = slice(hp * LANES, (hp + 1) * LANES)
        qn = _pair_norm(q_ref[:, sl], gq_ref[...], ones) * (dh ** -0.5)
        kn = _pair_norm(k_ref[:, sl], gk_ref[...], ones)
        kn_ref[:, sl] = kn
        qs = _stack_heads(qn)
        sc = _dot_nt(qs, ck_ref[:, sl].astype(BF16))
        sn = _dot_nt(qs, kn.astype(BF16))
        if mode == "fox":
            fblk = fq_ref[...]
            fq = jnp.concatenate([_lane_column(fblk, 2 * hp), _lane_column(fblk, 2 * hp + 1)],
                                 axis=0)
            f0 = ft_ref[2 * hp:2 * hp + 1, :]
            f1 = ft_ref[2 * hp + 1:2 * hp + 2, :]
            fkc = jnp.concatenate([jnp.broadcast_to(f0[:, :n_cache], (s_new, n_cache)),
                                   jnp.broadcast_to(f1[:, :n_cache], (s_new, n_cache))], axis=0)
            fkn = jnp.concatenate(
                [jnp.broadcast_to(f0[:, n_cache:n_cache + s_new], (s_new, s_new)),
                 jnp.broadcast_to(f1[:, n_cache:n_cache + s_new], (s_new, s_new))], axis=0)
            sc = sc + fq - fkc
            sn = jnp.where(col <= row, sn + fq - fkn, NEG_INF)
        else:
            sc = sc + jnp.concatenate([tabc_ref[2 * hp], tabc_ref[2 * hp + 1]], axis=0)
            sn = sn + jnp.concatenate([tabn_ref[2 * hp], tabn_ref[2 * hp + 1]], axis=0)
        m = jnp.maximum(jnp.max(sc, axis=-1, keepdims=True), jnp.max(sn, axis=-1, keepdims=True))
        pc = jnp.exp(sc - m)
        pn = jnp.exp(sn - m)
        l = jnp.sum(pc, axis=-1, keepdims=True) + jnp.sum(pn, axis=-1, keepdims=True)
        o = (_dot(pc.astype(BF16), cv_ref[:, sl].astype(BF16))
             + _dot(pn.astype(BF16), v_ref[:, sl].astype(BF16))) / l
        o_ref[:, sl] = _unstack_heads(o, s_new)


def _sample_attn(proj, cache_k, cache_v, extra_a, extra_b, gq, gk, o_buf, *, mode, batch, s_new,
                 n_pairs, col_q, row0):
    width = n_pairs * LANES
    n_cache = cache_k.shape[0] // batch
    rb = row0 // s_new
    if mode == "fox":
        ex_specs = [pl.BlockSpec((s_new, LANES), lambda b: (b, 0)),
                    pl.BlockSpec((None, 16, extra_b.shape[2]), lambda b: (b, 0, 0))]
    else:
        ex_specs = [pl.BlockSpec(extra_a.shape, lambda b: (0, 0, 0)),
                    pl.BlockSpec(extra_b.shape, lambda b: (0, 0, 0))]
    est = (2 * 2 * _nbytes((n_cache, width), F32) + 8 * _nbytes((s_new, width), F32)
           + 2 * _nbytes(extra_a.shape, F32) + 16 * _nbytes((2 * s_new, n_cache), F32)
           + 4 * _nbytes((n_cache, LANES), BF16))
    return pl.pallas_call(
        functools.partial(_sample_attn_body, mode=mode, n_pairs=n_pairs, s_new=s_new,
                          n_cache=n_cache, dh=64),
        out_shape=(jax.ShapeDtypeStruct(o_buf.shape, F32),
                   jax.ShapeDtypeStruct((batch * s_new, width), F32)),
        grid=(batch,),
        in_specs=[
            pl.BlockSpec((s_new, width), lambda b: (rb + b, col_q)),
            pl.BlockSpec((s_new, width), lambda b: (rb + b, col_q + 1)),
            pl.BlockSpec((s_new, width), lambda b: (rb + b, col_q + 2)),
            pl.BlockSpec((n_cache, width), lambda b: (b, 0)),
            pl.BlockSpec((n_cache, width), lambda b: (b, 0)),
            *ex_specs,
            pl.BlockSpec((1, LANES), lambda b: (0, 0)),
            pl.BlockSpec((1, LANES), lambda b: (0, 0)),
            pl.BlockSpec(memory_space=pl.ANY),
        ],
        out_specs=(pl.BlockSpec((s_new, width), lambda b: (rb + b, 0)),
                   pl.BlockSpec((s_new, width), lambda b: (b, 0))),
        input_output_aliases={9: 0},
        compiler_params=pltpu.CompilerParams(
            dimension_semantics=("parallel",), vmem_limit_bytes=_vmem_limit(est)),
        name=f"{mode}_sample",
    )(proj, proj, proj, cache_k, cache_v, extra_a, extra_b, gq, gk, o_buf)


def _cross_body(*refs, n_heads, dh, aliased):
    if aliased:
        q_ref, k_ref, v_ref, _alias, o_ref = refs
    else:
        q_ref, k_ref, v_ref, o_ref = refs
    for h in range(n_heads):
        sl = slice(h * dh, (h + 1) * dh)
        q = (q_ref[:, sl] * (dh ** -0.5)).astype(BF16)
        s = _dot_nt(q, k_ref[:, sl].astype(BF16))
        m = jnp.max(s, axis=-1, keepdims=True)
        p = jnp.exp(s - m)
        l = jnp.sum(p, axis=-1, keepdims=True)
        o = _dot(p.astype(BF16), v_ref[:, sl].astype(BF16)) / l
        o_ref[:, sl] = o.astype(o_ref.dtype)


def _cross_attn(q_all, k, v, o_buf, *, batch, q_len, n_mem, n_heads, dh, row0, m_rows):
    width = n_heads * dh
    tq = _pick(q_len, (512, 256, 128, 64, 32, 16))
    nq = q_len // tq
    rb = row0 // tq
    aliased = o_buf is not None
    in_specs = [pl.BlockSpec((tq, width), lambda b, qi: (rb + b * nq + qi, 0)),
                pl.BlockSpec((n_mem, width), lambda b, qi: (b, 0)),
                pl.BlockSpec((n_mem, width), lambda b, qi: (b, 0))]
    args = [q_all, k, v]
    if aliased:
        in_specs.append(pl.BlockSpec(memory_space=pl.ANY))
        args.append(o_buf)
    est = (2 * _nbytes((tq, width), F32) + 4 * _nbytes((n_mem, width), F32)
           + 2 * _nbytes((tq, width), BF16) + 12 * _nbytes((tq, n_mem), F32))
    return pl.pallas_call(
        functools.partial(_cross_body, n_heads=n_heads, dh=dh, aliased=aliased),
        out_shape=jax.ShapeDtypeStruct((m_rows, width), BF16),
        grid=(batch, nq),
        in_specs=in_specs,
        out_specs=pl.BlockSpec((tq, width), lambda b, qi: (rb + b * nq + qi, 0)),
        input_output_aliases={3: 0} if aliased else {},
        compiler_params=pltpu.CompilerParams(
            dimension_semantics=("parallel", "arbitrary"), vmem_limit_bytes=_vmem_limit(est)),
        name="cross_attn",
    )(*args)


def _s5_prep_body(lr_ref, li_ref, ldt_ref, br_ref, bi_ref, ar_ref, ai_ref, bbr_ref, bbi_ref):
    lr = lr_ref[...]
    li = li_ref[...]
    dt = jnp.exp(ldt_ref[...])
    mag = jnp.exp(lr * dt)
    a_re = mag * jnp.cos(li * dt)
    a_im = mag * jnp.sin(li * dt)
    den = lr * lr + li * li
    num_re = a_re - 1.0
    coef_re = (num_re * lr + a_im * li) / den
    coef_im = (a_im * lr - num_re * li) / den
    br = br_ref[...]
    bi = bi_ref[...]
    ar_ref[...] = a_re
    ai_ref[...] = a_im
    bbr_ref[...] = coef_re * br - coef_im * bi
    bbi_ref[...] = coef_re * bi + coef_im * br


def _s5_prep(lam_re, lam_im, log_dt, b_re, b_im):
    shape = lam_re.shape
    spec = pl.BlockSpec(shape, lambda: (0, 0))
    return pl.pallas_call(
        _s5_prep_body,
        out_shape=tuple(jax.ShapeDtypeStruct(shape, F32) for _ in range(4)),
        in_specs=[spec] * 5,
        out_specs=tuple([spec] * 4),
        name="s5_discretise",
    )(lam_re, lam_im, log_dt, b_re, b_im)


def _s5_body(u_ref, x0r_ref, x0i_ref, ar_ref, ai_ref, bre_ref, bim_ref, cre_ref, cim_ref,
             d_ref, wg_ref, bg_ref, o_ref, xr_out, xi_out, bur, bui, st_r, st_i,
             *, nb, t_chunk, n_blk, passes):
    i = pl.program_id(0)
    wc = u_ref.shape[1]
    ns = bur.shape[1]
    ub = wc // n_blk
    sb = ns // n_blk

    @pl.when(i == 0)
    def _():
        st_r[...] = x0r_ref[...]
        st_i[...] = x0i_ref[...]

    u = u_ref[...]
    for k in range(n_blk):
        uk = u[:, k * ub:(k + 1) * ub]
        bur[:, k * sb:(k + 1) * sb] = _dot_hp(
            uk, bre_ref[k * ub:(k + 1) * ub, k * sb:(k + 1) * sb], passes)
        bui[:, k * sb:(k + 1) * sb] = _dot_hp(
            uk, bim_ref[k * ub:(k + 1) * ub, k * sb:(k + 1) * sb], passes)

    a_re = ar_ref[...]
    a_im = ai_ref[...]

    def step(t, carry):
        xr, xi = carry
        r0 = pl.multiple_of(t * nb, nb)
        nr = a_re * xr - a_im * xi + bur[pl.ds(r0, nb), :]
        ni = a_re * xi + a_im * xr + bui[pl.ds(r0, nb), :]
        bur[pl.ds(r0, nb), :] = nr
        bui[pl.ds(r0, nb), :] = ni
        return nr, ni
    xr_f, xi_f = lax.fori_loop(0, t_chunk, step, (st_r[...], st_i[...]))
    st_r[...] = xr_f
    st_i[...] = xi_f
    xr_out[...] = xr_f
    xi_out[...] = xi_f
    ys = []
    for k in range(n_blk):
        xrk = bur[:, k * sb:(k + 1) * sb].astype(BF16)
        xik = bui[:, k * sb:(k + 1) * sb].astype(BF16)
        ys.append(_dot(xrk, cre_ref[k * sb:(k + 1) * sb, k * ub:(k + 1) * ub].astype(BF16))
                  - _dot(xik, cim_ref[k * sb:(k + 1) * sb, k * ub:(k + 1) * ub].astype(BF16)))
    y = jnp.concatenate(ys, axis=1) + d_ref[...] * u
    z = _gelu_tanh(y)
    gate = _dot(z.astype(BF16), wg_ref[...].astype(BF16)) + bg_ref[...]
    o_ref[...] = z * jax.nn.sigmoid(gate)


def _s5(u_tb, x0r, x0i, a_re, a_im, b_re, b_im, c_re, c_im, d, w_glu, b_glu, *, nb, seq, n_blk,
        passes):
    wc = u_tb.shape[1]
    ns = a_re.shape[1]
    t_chunk = _pick(seq, (64, 32, 16))
    rows = nb * t_chunk
    full = lambda shape: pl.BlockSpec(shape, lambda i: (0, 0))
    est = (4 * _nbytes((rows, wc), F32) + 2 * _nbytes((rows, ns), F32)
           + 2 * 4 * _nbytes((wc, ns), F32) + 2 * _nbytes((wc, wc), F32)
           + 12 * _nbytes((rows, ns // n_blk), F32) + 8 * _nbytes((nb, ns), F32))
    return pl.pallas_call(
        functools.partial(_s5_body, nb=nb, t_chunk=t_chunk, n_blk=n_blk, passes=passes),
        out_shape=(jax.ShapeDtypeStruct((seq * nb, wc), F32),
                   jax.ShapeDtypeStruct((nb, ns), F32), jax.ShapeDtypeStruct((nb, ns), F32)),
        grid=(seq // t_chunk,),
        in_specs=[pl.BlockSpec((rows, wc), lambda i: (i, 0)),
                  full((nb, ns)), full((nb, ns)), full((1, ns)), full((1, ns)),
                  full((wc, ns)), full((wc, ns)), full((ns, wc)), full((ns, wc)),
                  full((1, wc)), full((wc, wc)), full((1, wc))],
        out_specs=(pl.BlockSpec((rows, wc), lambda i: (i, 0)), full((nb, ns)), full((nb, ns))),
        scratch_shapes=[pltpu.VMEM((rows, ns), F32), pltpu.VMEM((rows, ns), F32),
                        pltpu.VMEM((nb, ns), F32), pltpu.VMEM((nb, ns), F32)],
        compiler_params=pltpu.CompilerParams(
            dimension_semantics=("arbitrary",), vmem_limit_bytes=_vmem_limit(est)),
        name="s5_scan",
    )(u_tb, x0r, x0i, a_re, a_im, b_re, b_im, c_re, c_im, d, w_glu, b_glu)


def _router_body(x_ref, g_ref, wr_ref, pk_ref, idx_ref, gate_ref, *, n_exp, rc):
    tm, d = x_ref.shape
    half = d // 2

    def chunk(c, carry):
        r0 = pl.multiple_of(c * rc, rc)
        x = x_ref[pl.ds(r0, rc), :]
        ms = jnp.mean(x * x, axis=-1, keepdims=True)
        h = x * lax.rsqrt(ms + EPS) * g_ref[...]
        hb = h.astype(BF16).astype(F32)
        lo = lax.shift_right_logical(pltpu.bitcast(hb[:, :half], U32), jnp.uint32(16))
        hi = pltpu.bitcast(hb[:, half:], U32) & jnp.uint32(0xFFFF0000)
        pk_ref[pl.ds(r0, rc), :] = lo | hi
        logits = _dot_hp(h, wr_ref[...])
        lane = lax.broadcasted_iota(I32, logits.shape, 1)
        logits = jnp.where(lane < n_exp, logits, NEG_INF)
        mx = jnp.max(logits, axis=-1, keepdims=True)
        e = jnp.exp(logits - mx)
        probs = e / jnp.sum(e, axis=-1, keepdims=True)
        probs = jnp.where(lane < n_exp, probs, -1.0)
        lane_f = lane.astype(F32)
        p1 = jnp.max(probs, axis=-1, keepdims=True)
        i1 = jnp.min(jnp.where(probs == p1, lane_f, float(LANES)), axis=-1, keepdims=True)
        rest = jnp.where(lane_f == i1, -1.0, probs)
        p2 = jnp.max(rest, axis=-1, keepdims=True)
        i2 = jnp.min(jnp.where(rest == p2, lane_f, float(LANES)), axis=-1, keepdims=True)
        tot = p1 + p2
        idx_ref[pl.ds(r0, rc), :] = jnp.where(lane == 0, i1, jnp.where(lane == 1, i2, 0.0)).astype(I32)
        gate_ref[pl.ds(r0, rc), :] = jnp.where(lane == 0, p1 / tot,
                                                jnp.where(lane == 1, p2 / tot, 0.0))
        return carry
    lax.fori_loop(0, tm // rc, chunk, 0)


def _router(x, g, w_router_pad, *, n_exp, tm):
    m, d = x.shape
    rc = _pick(tm, (256, 176, 128, 64, 32, 16, 8))
    est = 2 * _nbytes((tm, d), F32) + 2 * _nbytes((tm, d // 2), U32) + 16 * _nbytes((rc, d), F32)
    return pl.pallas_call(
        functools.partial(_router_body, n_exp=n_exp, rc=rc),
        out_shape=(jax.ShapeDtypeStruct((m, d // 2), U32),
                   jax.ShapeDtypeStruct((m, LANES), I32),
                   jax.ShapeDtypeStruct((m, LANES), F32)),
        grid=(m // tm,),
        in_specs=[pl.BlockSpec((tm, d), lambda i: (i, 0)),
                  pl.BlockSpec((1, d), lambda i: (0, 0)),
                  pl.BlockSpec((d, LANES), lambda i: (0, 0))],
        out_specs=(pl.BlockSpec((tm, d // 2), lambda i: (i, 0)),
                   pl.BlockSpec((tm, LANES), lambda i: (i, 0)),
                   pl.BlockSpec((tm, LANES), lambda i: (i, 0))),
        compiler_params=pltpu.CompilerParams(
            dimension_semantics=("parallel",), vmem_limit_bytes=_vmem_limit(est)),
        name="moe_router",
    )(x, g, w_router_pad)


def _dispatch_body(nt_ref, tok_ref, pk_hbm, a_ref, buf, sem, *, tm):
    i = pl.program_id(0)
    half = buf.shape[1]

    @pl.when(i < nt_ref[0])
    def _():
        def issue(r, carry):
            t = tok_ref[0, 0, r]
            pltpu.make_async_copy(pk_hbm.at[pl.ds(t, 1), :], buf.at[pl.ds(r, 1), :], sem).start()
            return carry
        lax.fori_loop(0, tm, issue, 0)

        def drain(r, carry):
            pltpu.make_async_copy(pk_hbm.at[pl.ds(0, 1), :], buf.at[pl.ds(r, 1), :], sem).wait()
            return carry
        lax.fori_loop(0, tm, drain, 0)
        pk = buf[...]
        lo = pltpu.bitcast(lax.shift_left(pk, jnp.uint32(16)), F32)
        hi = pltpu.bitcast(pk & jnp.uint32(0xFFFF0000), F32)
        a_ref[:, :half] = lo.astype(BF16)
        a_ref[:, half:] = hi.astype(BF16)


def _dispatch(n_tiles, tok_sorted, packed, *, tm, r_max):
    m, half = packed.shape
    t_max = r_max // tm
    return pl.pallas_call(
        functools.partial(_dispatch_body, tm=tm),
        out_shape=jax.ShapeDtypeStruct((r_max, 2 * half), BF16),
        grid_spec=pltpu.PrefetchScalarGridSpec(
            num_scalar_prefetch=1,
            grid=(t_max,),
            in_specs=[pl.BlockSpec((1, 1, tm), lambda i, nt: (i, 0, 0), memory_space=pltpu.SMEM),
                      pl.BlockSpec(memory_space=pl.ANY)],
            out_specs=pl.BlockSpec((tm, 2 * half), lambda i, nt: (jnp.minimum(i, nt[0] - 1), 0)),
            scratch_shapes=[pltpu.VMEM((tm, half), U32), pltpu.SemaphoreType.DMA]),
        compiler_params=pltpu.CompilerParams(dimension_semantics=("arbitrary",)),
        name="moe_dispatch",
    )(n_tiles, tok_sorted.reshape(t_max, 1, tm), packed)


def _expert_up_body(te_ref, nt_ref, a_ref, w1_ref, w3_ref, h_ref, w1bf, w3bf, *, kc):
    i = pl.program_id(1)
    d = a_ref.shape[1]
    fresh = jnp.logical_or(i == 0, te_ref[i] != te_ref[jnp.maximum(i - 1, 0)])

    @pl.when(jnp.logical_and(fresh, i < nt_ref[0]))
    def _():
        def cast(c, carry):
            r0 = pl.multiple_of(c * kc, kc)
            w1bf[pl.ds(r0, kc), :] = w1_ref[pl.ds(r0, kc), :].astype(BF16)
            w3bf[pl.ds(r0, kc), :] = w3_ref[pl.ds(r0, kc), :].astype(BF16)
            return carry
        lax.fori_loop(0, d // kc, cast, 0)

    @pl.when(i < nt_ref[0])
    def _():
        a = a_ref[...]
        h_ref[...] = (jax.nn.silu(_dot(a, w1bf[...])) * _dot(a, w3bf[...])).astype(h_ref.dtype)


def _expert_up(tile_expert, n_tiles, a_sorted, w1, w3, *, tm, tf):
    r_max, d = a_sorted.shape
    n_exp, _, fe = w1.shape
    t_max = r_max // tm
    kc = _pick(d, (512, 256, 128))
    row = lambda f, i, te, nt: jnp.minimum(i, nt[0] - 1)
    est = (2 * _nbytes((tm, d), BF16) + 2 * 2 * _nbytes((d, tf), F32) + 2 * _nbytes((d, tf), BF16)
           + 2 * _nbytes((tm, tf), BF16) + 6 * _nbytes((tm, tf), F32))
    return pl.pallas_call(
        functools.partial(_expert_up_body, kc=kc),
        out_shape=jax.ShapeDtypeStruct((r_max, fe), BF16),
        grid_spec=pltpu.PrefetchScalarGridSpec(
            num_scalar_prefetch=2,
            grid=(fe // tf, t_max),
            in_specs=[pl.BlockSpec((tm, d), lambda f, i, te, nt: (row(f, i, te, nt), 0)),
                      pl.BlockSpec((None, d, tf), lambda f, i, te, nt: (te[i], 0, f)),
                      pl.BlockSpec((None, d, tf), lambda f, i, te, nt: (te[i], 0, f))],
            out_specs=pl.BlockSpec((tm, tf), lambda f, i, te, nt: (row(f, i, te, nt), f)),
            scratch_shapes=[pltpu.VMEM((d, tf), BF16), pltpu.VMEM((d, tf), BF16)]),
        compiler_params=pltpu.CompilerParams(
            dimension_semantics=("arbitrary", "arbitrary"), vmem_limit_bytes=_vmem_limit(est)),
        name="moe_expert_up",
    )(tile_expert, n_tiles, a_sorted, w1, w3)


def _expert_down_body(te_ref, nt_ref, h_ref, w2_ref, y_ref, w2bf, *, kc):
    i = pl.program_id(1)
    fe = h_ref.shape[1]
    fresh = jnp.logical_or(i == 0, te_ref[i] != te_ref[jnp.maximum(i - 1, 0)])

    @pl.when(jnp.logical_and(fresh, i < nt_ref[0]))
    def _():
        def cast(c, carry):
            r0 = pl.multiple_of(c * kc, kc)
            w2bf[pl.ds(r0, kc), :] = w2_ref[pl.ds(r0, kc), :].astype(BF16)
            return carry
        lax.fori_loop(0, fe // kc, cast, 0)

    @pl.when(i < nt_ref[0])
    def _():
        y_ref[...] = _dot(h_ref[...], w2bf[...])


def _expert_down(tile_expert, n_tiles, h_sorted, w2, *, tm, tn):
    r_max, fe = h_sorted.shape
    d = w2.shape[2]
    t_max = r_max // tm
    kc = _pick(fe, (512, 256, 128))
    row = lambda n, i, te, nt: jnp.minimum(i, nt[0] - 1)
    est = (2 * _nbytes((tm, fe), BF16) + 2 * _nbytes((fe, tn), F32) + _nbytes((fe, tn), BF16)
           + 4 * _nbytes((tm, tn), F32))
    return pl.pallas_call(
        functools.partial(_expert_down_body, kc=kc),
        out_shape=jax.ShapeDtypeStruct((r_max, d), F32),
        grid_spec=pltpu.PrefetchScalarGridSpec(
            num_scalar_prefetch=2,
            grid=(d // tn, t_max),
            in_specs=[pl.BlockSpec((tm, fe), lambda n, i, te, nt: (row(n, i, te, nt), 0)),
                      pl.BlockSpec((None, fe, tn), lambda n, i, te, nt: (te[i], 0, n))],
            out_specs=pl.BlockSpec((tm, tn), lambda n, i, te, nt: (row(n, i, te, nt), n)),
            scratch_shapes=[pltpu.VMEM((fe, tn), BF16)]),
        compiler_params=pltpu.CompilerParams(
            dimension_semantics=("arbitrary", "arbitrary"), vmem_limit_bytes=_vmem_limit(est)),
        name="moe_expert_down",
    )(tile_expert, n_tiles, h_sorted, w2)


def _combine_body(slot_ref, x_ref, gate_ref, y_hbm, o_ref, buf, sem, *, tc):
    def issue(r, carry):
        for k in range(TOP_K):
            s = slot_ref[0, 0, TOP_K * r + k]
            pltpu.make_async_copy(y_hbm.at[pl.ds(s, 1), :], buf.at[k, pl.ds(r, 1), :], sem).start()
        return carry
    lax.fori_loop(0, tc, issue, 0)

    def drain(r, carry):
        for k in range(TOP_K):
            pltpu.make_async_copy(y_hbm.at[pl.ds(0, 1), :], buf.at[k, pl.ds(r, 1), :], sem).wait()
        return carry
    lax.fori_loop(0, tc, drain, 0)
    g = gate_ref[...]
    o_ref[...] = x_ref[...] + (g[:, 0:1] * buf[0] + g[:, 1:2] * buf[1])


def _combine(slots, x, gates, y_sorted, *, tc):
    m, d = x.shape
    est = 4 * _nbytes((tc, d), F32) + 2 * _nbytes((tc, d), F32) + 4 * _nbytes((tc, d), F32)
    return pl.pallas_call(
        functools.partial(_combine_body, tc=tc),
        out_shape=jax.ShapeDtypeStruct((m, d), F32),
        grid=(m // tc,),
        in_specs=[pl.BlockSpec((1, 1, TOP_K * tc), lambda i: (i, 0, 0), memory_space=pltpu.SMEM),
                  pl.BlockSpec((tc, d), lambda i: (i, 0)),
                  pl.BlockSpec((tc, LANES), lambda i: (i, 0)),
                  pl.BlockSpec(memory_space=pl.ANY)],
        out_specs=pl.BlockSpec((tc, d), lambda i: (i, 0)),
        scratch_shapes=[pltpu.VMEM((TOP_K, tc, d), F32), pltpu.SemaphoreType.DMA],
        compiler_params=pltpu.CompilerParams(
            dimension_semantics=("arbitrary",), vmem_limit_bytes=_vmem_limit(est)),
        name="moe_combine",
    )(slots.reshape(m // tc, 1, TOP_K * tc), x, gates, y_sorted)


def _moe(x, g_ffn, w_router, w_e1, w_e3, w_e2, *, tm_tok):
    m, d = x.shape
    n_exp, _, fe = w_e1.shape
    tm = _pick(m * TOP_K, (512, 256, 128, 64, 32, 16))
    wr = jnp.zeros((d, LANES), F32).at[:, :n_exp].set(w_router)
    packed, idx128, gate128 = _router(x, g_ffn, wr, n_exp=n_exp, tm=tm_tok)

    idx = idx128[:, :TOP_K]
    mask = jnp.sum(idx[:, :, None] == jnp.arange(n_exp, dtype=I32)[None, None, :], axis=1).astype(I32)
    counts = jnp.sum(mask, axis=0)
    padded = ((counts + tm - 1) // tm) * tm
    ends = jnp.cumsum(padded)
    starts = ends - padded
    pos = jnp.cumsum(mask, axis=0) - mask
    slot = starts[idx] + jnp.take_along_axis(pos, idx, axis=1)
    t_max = (m * TOP_K) // tm + n_exp
    r_max = t_max * tm
    tok_sorted = jnp.zeros((r_max,), I32).at[slot.reshape(-1)].set(
        jnp.repeat(jnp.arange(m, dtype=I32), TOP_K))
    n_tiles = (ends[-1] // tm).astype(I32).reshape(1)
    tile_start = jnp.minimum(jnp.arange(t_max, dtype=I32), n_tiles[0] - 1) * tm
    tile_expert = jnp.minimum(jnp.searchsorted(ends, tile_start, side="right"),
                              n_exp - 1).astype(I32)

    a_sorted = _dispatch(n_tiles, tok_sorted, packed, tm=tm, r_max=r_max)
    tf = _pick(fe, (256, 128))
    h_sorted = _expert_up(tile_expert, n_tiles, a_sorted, w_e1, w_e3, tm=tm, tf=tf)
    y_sorted = _expert_down(tile_expert, n_tiles, h_sorted, w_e2, tm=tm, tn=_pick(d, (512, 256, 128)))
    tc = _pick(m, (256, 128, 64, 32, 16, 8))
    return _combine(slot, x, gate128, y_sorted, tc=tc)


def _block_diag(blocks):
    g, r, c = blocks.shape
    eye = jnp.eye(g, dtype=blocks.dtype)
    return (blocks[:, :, None, :] * eye[:, None, :, None]).reshape(g * r, g * c)


def _rel_table(rel_bias_l, q_pos, k_pos, max_rel):
    rel = jnp.clip(q_pos[:, None] - k_pos[None, :], -(CHUNK - 1), max_rel) + (CHUNK - 1)
    return rel_bias_l[:, rel].astype(F32)


def _tile_gain(g):
    return jnp.tile(g.astype(F32), LANES // g.shape[0]).reshape(1, LANES)


def kernel(x_prompt, x_sample, mem_prompt, cache_fox_k, cache_fox_v, cache_fox_logf, cache_band_k, cache_band_v, state_ssm_re, state_ssm_im, cache_mem_k, cache_mem_v, g_mix, w_in, b_f, g_qa, g_ka, g_qb, g_kb, rel_bias, lam_re, lam_im, log_dt, ssm_b_re, ssm_b_im, ssm_c_re, ssm_c_im, ssm_d, w_glu, b_glu, g_mix_out, w_out, g_cross, g_mem, w_cq, w_ck, w_cv, g_cq, g_ck, w_co, g_ffn, w_ff1, w_ff3, w_ff2, w_router, w_e1, w_e3, w_e2):
    batch, seq, d = x_prompt.shape
    dbatch, dseq, _ = x_sample.shape
    depth = g_mix.shape[0]
    past_len, h_a, dh = cache_fox_k.shape[2:]
    band_rows, h_b = cache_band_k.shape[2:4]
    g_c, p_state = lam_re.shape[1:]
    w_a, w_b, w_c = h_a * dh, h_b * dh, g_c * SSM_GROUP
    n_mem, h_m, dh_m = cache_mem_k.shape[2:]
    w_m = h_m * dh_m
    max_rel = rel_bias.shape[2] - CHUNK
    assert dh == 64 and h_a % 2 == 0 and h_b % 2 == 0 and dh_m == LANES
    assert w_a == w_b and w_c <= w_a and h_a <= 16
    mp, ms = batch * seq, dbatch * dseq
    m = mp + ms
    pa, pb = h_a // 2, h_b // 2
    n_state = g_c * p_state
    n_blk = 2 if (w_c % 512 == 0) else 1
    nband = min(BAND_PAST * CHUNK, seq)
    tm = _pick(m, (1056, 1024, 768, 512, 256, 128, 64, 32, 16))
    tq_band = 2 * CHUNK

    x = jnp.concatenate([x_prompt.reshape(mp, d), x_sample.reshape(ms, d)], axis=0)

    outs = {k: [] for k in ("p_fk", "p_fv", "p_fl", "p_bk", "p_bv", "p_sr", "p_si", "p_mk", "p_mv",
                            "s_fk", "s_fv", "s_fl", "s_bk", "s_bv", "s_sr", "s_si")}
    for l in range(depth):
        sizes = (w_a, w_a, w_a, h_a, w_b, w_b, w_b)
        cuts = [sum(sizes[:i]) for i in range(len(sizes) + 1)]
        wl = w_in[l]
        w_cat = jnp.concatenate(
            [wl[:, :cuts[3]], wl[:, cuts[4]:cuts[7]], wl[:, cuts[7]:], wl[:, cuts[3]:cuts[4]],
             jnp.zeros((d, w_a - w_c - h_a), F32)], axis=1)
        proj = _mm([x], [w_cat], gain=g_mix[l].reshape(1, d), tm=tm, tn=w_a, name="proj_in")
        col_uc = 6 * w_a

        fa_p = proj[:mp, col_uc + w_c:col_uc + w_c + h_a].reshape(batch, seq, h_a)
        fa_p = jnp.transpose(fa_p, (0, 2, 1)).reshape(batch * h_a, seq)
        bias_p = jnp.broadcast_to(jnp.tile(b_f[l], batch)[:, None], (batch * h_a, LANES))
        lf_p, f_p = _logf_cumsum(fa_p, bias_p, raw_from=0, valid_to=seq)
        lf_p = lf_p.reshape(batch, h_a, seq)
        f_p = f_p.reshape(batch, h_a, seq)
        tq_fox = _pick(seq, (256, 128))
        f_row = jnp.pad(f_p, ((0, 0), (0, 16 - h_a), (0, 0))).reshape(batch, 16, seq // tq_fox, tq_fox)
        f_row = jnp.transpose(f_row, (0, 2, 1, 3))
        f_col = jnp.pad(jnp.transpose(f_p, (0, 2, 1)), ((0, 0), (0, 0), (0, LANES - h_a)))
        f_col = f_col.reshape(mp, LANES)

        t_all = -(-(past_len + dseq) // LANES) * LANES
        fa_s = proj[mp:, col_uc + w_c:col_uc + w_c + h_a].reshape(dbatch, dseq, h_a)
        x_s = jnp.concatenate([jnp.transpose(cache_fox_logf[l], (0, 2, 1)),
                               jnp.transpose(fa_s, (0, 2, 1)),
                               jnp.zeros((dbatch, h_a, t_all - past_len - dseq), F32)], axis=2)
        bias_s = jnp.broadcast_to(jnp.tile(b_f[l], dbatch)[:, None], (dbatch * h_a, LANES))
        lf_s, f_s = _logf_cumsum(x_s.reshape(dbatch * h_a, t_all), bias_s,
                                 raw_from=past_len, valid_to=past_len + dseq)
        lf_s = lf_s.reshape(dbatch, h_a, t_all)[:, :, past_len:past_len + dseq]
        f_s = f_s.reshape(dbatch, h_a, t_all)
        fs_row = jnp.pad(f_s, ((0, 0), (0, 16 - h_a), (0, 0)))
        fs_col = jnp.pad(jnp.transpose(f_s[:, :, past_len:past_len + dseq], (0, 2, 1)),
                         ((0, 0), (0, 0), (0, LANES - h_a))).reshape(ms, LANES)

        gqa, gka = _tile_gain(g_qa[l]), _tile_gain(g_ka[l])
        oa, kn_a = _fox_prompt(proj, f_col, f_row, gqa, gka, batch=batch, seq=seq, n_pairs=pa,
                               col_q=0, col_k=pa, col_v=2 * pa, m_rows=m)
        oa, kn_as = _sample_attn(proj, cache_fox_k[l].reshape(dbatch * past_len, w_a),
                                 cache_fox_v[l].reshape(dbatch * past_len, w_a), fs_col, fs_row,
                                 gqa, gka, oa, mode="fox", batch=dbatch, s_new=dseq, n_pairs=pa,
                                 col_q=0, row0=mp)

        gqb, gkb = _tile_gain(g_qb[l]), _tile_gain(g_kb[l])
        past = BAND_PAST * CHUNK
        qpos = past + jnp.arange(tq_band)
        kpos = jnp.arange(past + tq_band)
        tab = _rel_table(rel_bias[l], qpos, kpos, max_rel)
        dchunk = qpos[:, None] // CHUNK - kpos[None, :] // CHUNK
        tab = jnp.where((dchunk >= 0) & (dchunk <= BAND_PAST), tab, NEG_INF)
        tab = tab.reshape(pb, 2, tq_band, past + tq_band)
        ob, kn_b = _band_prompt(proj, tab, gqb, gkb, batch=batch, seq=seq, n_pairs=pb,
                                col_q=3 * pa, col_k=3 * pa + pb, col_v=3 * pa + 2 * pb, tq=tq_band,
                                m_rows=m)
        tab_s = _rel_table(rel_bias[l], band_rows + jnp.arange(dseq), jnp.arange(band_rows + dseq),
                           max_rel)
        ob, kn_bs = _sample_attn(proj, cache_band_k[l].reshape(dbatch * band_rows, w_b),
                                 cache_band_v[l].reshape(dbatch * band_rows, w_b),
                                 tab_s[:, :, :band_rows], tab_s[:, :, band_rows:], gqb, gkb, ob,
                                 mode="band", batch=dbatch, s_new=dseq, n_pairs=pb, col_q=3, row0=mp)

        rep = lambda a: jnp.repeat(a.astype(F32), SSM_GROUP, axis=0)
        a_re, a_im, bb_re, bb_im = _s5_prep(
            rep(lam_re[l]), rep(lam_im[l]),
            jnp.broadcast_to(rep(log_dt[l])[:, None], (g_c * SSM_GROUP, p_state)),
            jnp.transpose(ssm_b_re[l], (0, 2, 1)).reshape(g_c * SSM_GROUP, p_state),
            jnp.transpose(ssm_b_im[l], (0, 2, 1)).reshape(g_c * SSM_GROUP, p_state))
        a_re = a_re.reshape(g_c, SSM_GROUP, p_state)[:, 0, :].reshape(1, n_state)
        a_im = a_im.reshape(g_c, SSM_GROUP, p_state)[:, 0, :].reshape(1, n_state)
        b_re_d = _block_diag(bb_re.reshape(g_c, SSM_GROUP, p_state))
        b_im_d = _block_diag(bb_im.reshape(g_c, SSM_GROUP, p_state))
        c_re_d = _block_diag(jnp.transpose(ssm_c_re[l], (0, 2, 1)))
        c_im_d = _block_diag(jnp.transpose(ssm_c_im[l], (0, 2, 1)))
        d_row = ssm_d[l].reshape(1, w_c)
        s5_args = (a_re, a_im, b_re_d, b_im_d, c_re_d, c_im_d, d_row, w_glu[l],
                   b_glu[l].reshape(1, w_c))
        uc = proj[:, col_uc:col_uc + w_c]
        u_p = jnp.transpose(uc[:mp].reshape(batch, seq, w_c), (1, 0, 2)).reshape(mp, w_c)
        u_s = jnp.transpose(uc[mp:].reshape(dbatch, dseq, w_c), (1, 0, 2)).reshape(ms, w_c)
        zeros_p = jnp.zeros((batch, n_state), F32)
        oc_p, sr_p, si_p = _s5(u_p, zeros_p, zeros_p, *s5_args, nb=batch, seq=seq, n_blk=n_blk,
                               passes=1)
        oc_s, sr_s, si_s = _s5(u_s, state_ssm_re[l].reshape(dbatch, n_state),
                               state_ssm_im[l].reshape(dbatch, n_state), *s5_args,
                               nb=dbatch, seq=dseq, n_blk=n_blk, passes=3)
        oc = jnp.concatenate(
            [jnp.transpose(oc_p.reshape(seq, batch, w_c), (1, 0, 2)).reshape(mp, w_c),
             jnp.transpose(oc_s.reshape(dseq, dbatch, w_c), (1, 0, 2)).reshape(ms, w_c)], axis=0)

        x = _mm([oa, ob, oc], [w_out[l]], gain=g_mix_out[l].reshape(1, -1), residual=x,
                tm=tm, tn=_pick(d, (512, 256, 128)), name="merge_out")

        outs["p_fk"].append(kn_a.reshape(batch, seq, h_a, dh))
        outs["p_fv"].append(proj[:mp, 2 * w_a:3 * w_a].reshape(batch, seq, h_a, dh))
        outs["p_fl"].append(jnp.transpose(lf_p, (0, 2, 1)))
        outs["p_bk"].append(kn_b.reshape(batch, seq, h_b, dh)[:, seq - nband:])
        outs["p_bv"].append(proj[:mp, 5 * w_a:6 * w_a].reshape(batch, seq, h_b, dh)[:, seq - nband:])
        outs["p_sr"].append(sr_p.reshape(batch, g_c, p_state))
        outs["p_si"].append(si_p.reshape(batch, g_c, p_state))
        outs["s_fk"].append(kn_as.reshape(dbatch, dseq, h_a, dh))
        outs["s_fv"].append(proj[mp:, 2 * w_a:3 * w_a].reshape(dbatch, dseq, h_a, dh))
        outs["s_fl"].append(jnp.transpose(lf_s, (0, 2, 1)))
        outs["s_bk"].append(kn_bs.reshape(dbatch, dseq, h_b, dh))
        outs["s_bv"].append(proj[mp:, 5 * w_a:6 * w_a].reshape(dbatch, dseq, h_b, dh))
        outs["s_sr"].append(sr_s.reshape(dbatch, g_c, p_state))
        outs["s_si"].append(si_s.reshape(dbatch, g_c, p_state))

        mem2 = mem_prompt.reshape(batch * n_mem, d)
        tmm = _pick(batch * n_mem, (1024, 512, 256, 128))
        gck = jnp.tile(g_ck[l].astype(F32), h_m).reshape(1, w_m)
        gcq = jnp.tile(g_cq[l].astype(F32), h_m).reshape(1, w_m)
        mk = _mm([mem2], [w_ck[l]], gain=g_mem[l].reshape(1, d), group_gain=gck,
                 epilogue="group_norm", tm=tmm, tn=w_m, name="mem_k")
        mv = _mm([mem2], [w_cv[l]], gain=g_mem[l].reshape(1, d), tm=tmm, tn=w_m, name="mem_v")
        outs["p_mk"].append(mk.reshape(batch, n_mem, h_m, dh_m))
        outs["p_mv"].append(mv.reshape(batch, n_mem, h_m, dh_m))
        q_c = _mm([x], [w_cq[l]], gain=g_cross[l].reshape(1, d), group_gain=gcq,
                  epilogue="group_norm", tm=tm, tn=w_m, name="cross_q")
        o_c = _cross_attn(q_c, mk, mv, None, batch=batch, q_len=seq, n_mem=n_mem, n_heads=h_m,
                          dh=dh_m, row0=0, m_rows=m)
        o_c = _cross_attn(q_c, cache_mem_k[l].reshape(dbatch * n_mem, w_m),
                          cache_mem_v[l].reshape(dbatch * n_mem, w_m), o_c, batch=dbatch,
                          q_len=dseq, n_mem=n_mem, n_heads=h_m, dh=dh_m, row0=mp, m_rows=m)
        x = _mm([o_c], [w_co[l]], residual=x, tm=tm, tn=_pick(d, (512, 256, 128)), name="cross_out")

        i = l // 2
        if l % 2 == 0:
            hmid = _mm([x], [w_ff1[i], w_ff3[i]], gain=g_ffn[l].reshape(1, d), epilogue="swiglu",
                       out_dtype=BF16, tm=tm, tn=_pick(w_ff1.shape[2], (512, 256, 128)),
                       name="ffn_up")
            x = _mm([hmid], [w_ff2[i]], residual=x, tm=tm, tn=_pick(d, (256, 128)), name="ffn_down")
        else:
            x = _moe(x, g_ffn[l].reshape(1, d), w_router[i], w_e1[i], w_e3[i], w_e2[i], tm_tok=tm)

    st = lambda k: jnp.stack(outs[k])
    return (x[:mp].reshape(batch, seq, d), x[mp:].reshape(dbatch, dseq, d),
            st("p_fk"), st("p_fv"), st("p_fl"), st("p_bk"), st("p_bv"), st("p_sr"), st("p_si"),
            st("p_mk"), st("p_mv"), st("s_fk"), st("s_fv"), st("s_fl"), st("s_bk"), st("s_bv"),
            st("s_sr"), st("s_si"))
```

```python
import functools
import math

import jax
import jax.numpy as jnp
import numpy as np
from jax import lax
from jax.experimental import pallas as pl
from jax.experimental.pallas import tpu as pltpu

F32 = jnp.float32
BF16 = jnp.bfloat16
I32 = jnp.int32
U32 = jnp.uint32

EPS = 1e-6
NEG_INF = -1e30
CHUNK = 64
BAND_PAST = 8
SSM_GROUP = 16
TOP_K = 2

LANES = 128
SUBLANES = 8
BF16_ROWS = 16
VMEM_CAP = 60 * 1024 * 1024


def _vmem_limit(nbytes):
    return int(min(VMEM_CAP, max(16 * 1024 * 1024, nbytes * 5 // 4 + (4 << 20))))


def _pick(n, candidates):
    for c in candidates:
        if c <= n and n % c == 0:
            return c
    raise ValueError(f"no tile for {n} in {candidates}")


def _nbytes(shape, dtype):
    return math.prod(shape) * jnp.dtype(dtype).itemsize


def _split3(x):
    hi = x.astype(BF16)
    r1 = x - hi.astype(F32)
    mid = r1.astype(BF16)
    lo = (r1 - mid.astype(F32)).astype(BF16)
    return hi, mid, lo


def _dot(a, b):
    return jnp.dot(a, b, preferred_element_type=F32)


def _dot_nt(a, b):
    return lax.dot_general(a, b, (((1,), (1,)), ((), ())), preferred_element_type=F32)


def _dot_hp(a, b, passes=3):
    ah = a.astype(BF16)
    bh = b.astype(BF16)
    if passes == 1:
        return _dot(ah, bh)
    al = (a - ah.astype(F32)).astype(BF16)
    bl = (b - bh.astype(F32)).astype(BF16)
    return _dot(ah, bh) + (_dot(ah, bl) + _dot(al, bh))


def _pair_ones():
    r = lax.broadcasted_iota(I32, (LANES, LANES), 0) // 64
    c = lax.broadcasted_iota(I32, (LANES, LANES), 1) // 64
    return (r == c).astype(BF16)


def _pair_norm(x, g, ones):
    sq = x * x
    hi = sq.astype(BF16)
    lo = (sq - hi.astype(F32)).astype(BF16)
    ss = _dot(hi, ones) + _dot(lo, ones)
    return x * lax.rsqrt(ss * (1.0 / 64.0) + EPS) * g


def _log_sigmoid(x):
    return jnp.minimum(x, 0.0) - jnp.log(1.0 + jnp.exp(-jnp.abs(x)))


def _gelu_tanh(x):
    c = math.sqrt(2.0 / math.pi)
    return 0.5 * x * (1.0 + jnp.tanh(c * (x + 0.044715 * (x * x * x))))


def _mm_body(*refs, widths, norm, n_w, epilogue, has_res, has_gg, stage_a, tm, rc, kc, k_total):
    it = iter(refs)
    a_refs = [next(it) for _ in widths]
    g_ref = next(it) if norm else None
    w_refs = [next(it) for _ in range(n_w)]
    gg_ref = next(it) if has_gg else None
    res_ref = next(it) if has_res else None
    o_ref = next(it)
    abf = next(it) if stage_a else a_refs[0]
    cast_w = w_refs[0].dtype != BF16
    wbfs = [next(it) for _ in range(n_w)] if cast_w else w_refs
    j = pl.program_id(1)

    if stage_a:
        @pl.when(j == 0)
        def _():
            off = 0
            for a_ref, wd in zip(a_refs, widths):
                def chunk(c, carry, a_ref=a_ref, off=off, wd=wd):
                    r0 = pl.multiple_of(c * rc, rc)
                    x = a_ref[pl.ds(r0, rc), :].astype(F32)
                    if norm:
                        ms = jnp.mean(x * x, axis=-1, keepdims=True)
                        x = x * lax.rsqrt(ms + EPS) * g_ref[:, off:off + wd]
                    abf[pl.ds(r0, rc), off:off + wd] = x.astype(BF16)
                    return carry
                lax.fori_loop(0, tm // rc, chunk, 0)
                off += wd

    if cast_w:
        for w_ref, wbf in zip(w_refs, wbfs):
            def cast(c, carry, w_ref=w_ref, wbf=wbf):
                r0 = pl.multiple_of(c * kc, kc)
                wbf[pl.ds(r0, kc), :] = w_ref[pl.ds(r0, kc), :].astype(BF16)
                return carry
            lax.fori_loop(0, k_total // kc, cast, 0)

    def rows(c, carry):
        r0 = pl.multiple_of(c * rc, rc)
        a = abf[pl.ds(r0, rc), :]
        ys = [_dot(a, wbf[...]) for wbf in wbfs]
        if epilogue == "swiglu":
            y = jax.nn.silu(ys[0]) * ys[1]
        elif epilogue == "group_norm":
            parts = []
            for s in range(ys[0].shape[1] // LANES):
                ysl = ys[0][:, s * LANES:(s + 1) * LANES]
                ms = jnp.mean(ysl * ysl, axis=-1, keepdims=True)
                parts.append(ysl * lax.rsqrt(ms + EPS) * gg_ref[:, s * LANES:(s + 1) * LANES])
            y = jnp.concatenate(parts, axis=1)
        else:
            y = ys[0]
        if has_res:
            y = y + res_ref[pl.ds(r0, rc), :]
        o_ref[pl.ds(r0, rc), :] = y.astype(o_ref.dtype)
        return carry
    lax.fori_loop(0, tm // rc, rows, 0)


def _mm(a_parts, w_list, *, gain=None, group_gain=None, residual=None, epilogue="none",
        out_dtype=F32, tm, tn, name):
    m = a_parts[0].shape[0]
    widths = tuple(a.shape[1] for a in a_parts)
    k_total = sum(widths)
    n = w_list[0].shape[1]
    assert m % tm == 0 and n % tn == 0, (m, tm, n, tn)
    norm = gain is not None
    stage_a = norm or len(a_parts) > 1 or a_parts[0].dtype != BF16
    rc = _pick(tm, (512, 384, 352, 256, 176, 128, 64, 32, 16))
    kc = _pick(k_total, (512, 256, 128))
    grid = (m // tm, n // tn)
    in_specs = [pl.BlockSpec((tm, wd), lambda i, j: (i, 0)) for wd in widths]
    args = list(a_parts)
    est = sum(2 * _nbytes((tm, wd), a.dtype) for wd, a in zip(widths, a_parts))
    if norm:
        in_specs.append(pl.BlockSpec((1, k_total), lambda i, j: (0, 0)))
        args.append(gain)
    cast_w = w_list[0].dtype != BF16
    for w in w_list:
        in_specs.append(pl.BlockSpec((k_total, tn), lambda i, j: (0, j)))
        args.append(w)
        est += 2 * _nbytes((k_total, tn), w.dtype) + cast_w * _nbytes((k_total, tn), BF16)
    if group_gain is not None:
        in_specs.append(pl.BlockSpec((1, tn), lambda i, j: (0, j)))
        args.append(group_gain)
    if residual is not None:
        in_specs.append(pl.BlockSpec((tm, tn), lambda i, j: (i, j)))
        args.append(residual)
        est += 2 * _nbytes((tm, tn), F32)
    est += 2 * _nbytes((tm, tn), out_dtype) + stage_a * _nbytes((tm, k_total), BF16)
    est += 4 * _nbytes((rc, tn), F32) * len(w_list)
    body = functools.partial(
        _mm_body, widths=widths, norm=norm, n_w=len(w_list), epilogue=epilogue,
        has_res=residual is not None, has_gg=group_gain is not None, stage_a=stage_a, tm=tm,
        rc=rc, kc=kc, k_total=k_total)
    return pl.pallas_call(
        body,
        out_shape=jax.ShapeDtypeStruct((m, n), out_dtype),
        grid=grid,
        in_specs=in_specs,
        out_specs=pl.BlockSpec((tm, tn), lambda i, j: (i, j)),
        scratch_shapes=[pltpu.VMEM((tm, k_total), BF16)] * stage_a
        + [pltpu.VMEM((k_total, tn), BF16) for _ in w_list] * cast_w,
        compiler_params=pltpu.CompilerParams(
            dimension_semantics=("parallel", "arbitrary"),
            vmem_limit_bytes=_vmem_limit(est)),
        name=name,
    )(*args)


def _regroup_body(w_ref, o_ref, *, cuts, width):
    off = 0
    for lo, hi in cuts:
        o_ref[:, off:off + hi - lo] = w_ref[:, lo:hi].astype(o_ref.dtype)
        off += hi - lo
    if off < width:
        o_ref[:, off:width] = jnp.zeros((o_ref.shape[0], width - off), o_ref.dtype)


def _regroup_columns(w, layer, cuts, width):
    _, k, n = w.shape
    tr = _pick(k, (256, 128, 64, 32, 16))
    return pl.pallas_call(
        functools.partial(_regroup_body, cuts=cuts, width=width),
        out_shape=jax.ShapeDtypeStruct((k, width), BF16),
        grid=(k // tr,),
        in_specs=[pl.BlockSpec((None, tr, n), lambda i: (layer, i, 0))],
        out_specs=pl.BlockSpec((tr, width), lambda i: (i, 0)),
        compiler_params=pltpu.CompilerParams(
            dimension_semantics=("parallel",),
            vmem_limit_bytes=_vmem_limit(2 * _nbytes((tr, n), F32) + 2 * _nbytes((tr, width), BF16)
                                         + 4 * _nbytes((tr, width), F32))),
        name="regroup_w_in",
    )(w)


def _cumsum_body(x_ref, b_ref, lf_ref, f_ref, carry, *, raw_from, valid_to, tt):
    j = pl.program_id(0)

    @pl.when(j == 0)
    def _():
        carry[...] = jnp.zeros_like(carry)

    x = x_ref[...]
    lane = j * tt + lax.broadcasted_iota(I32, x.shape, 1)
    lf = jnp.where(lane >= raw_from, _log_sigmoid(x + b_ref[:, 0:1]), x)
    lf = jnp.where(lane < valid_to, lf, 0.0)
    lf_ref[...] = lf
    tri = (lax.broadcasted_iota(I32, (tt, tt), 0)
           <= lax.broadcasted_iota(I32, (tt, tt), 1)).astype(BF16)
    hi, mid, lo = _split3(lf)
    y = _dot(hi, tri) + _dot(mid, tri) + _dot(lo, tri) + carry[:, 0:1]
    f_ref[...] = y
    carry[...] = jnp.broadcast_to(y[:, tt - 1:tt], carry.shape)


def _logf_cumsum(x, bias, *, raw_from, valid_to):
    r, t = x.shape
    tt = _pick(t, (256, 128))
    return pl.pallas_call(
        functools.partial(_cumsum_body, raw_from=raw_from, valid_to=valid_to, tt=tt),
        out_shape=(jax.ShapeDtypeStruct((r, t), F32), jax.ShapeDtypeStruct((r, t), F32)),
        grid=(t // tt,),
        in_specs=[pl.BlockSpec((r, tt), lambda j: (0, j)),
                  pl.BlockSpec((r, LANES), lambda j: (0, 0))],
        out_specs=(pl.BlockSpec((r, tt), lambda j: (0, j)),
                   pl.BlockSpec((r, tt), lambda j: (0, j))),
        scratch_shapes=[pltpu.VMEM((r, LANES), F32)],
        compiler_params=pltpu.CompilerParams(dimension_semantics=("arbitrary",)),
        name="logf_cumsum",
    )(x, bias)


def _stack_heads(qn):
    lane = lax.broadcasted_iota(I32, qn.shape, 1)
    q0 = jnp.where(lane < 64, qn, 0.0)
    q1 = jnp.where(lane < 64, 0.0, qn)
    return jnp.concatenate([q0, q1], axis=0).astype(BF16)


def _unstack_heads(o, tq):
    lane = lax.broadcasted_iota(I32, (tq, LANES), 1)
    return jnp.where(lane < 64, o[:tq], o[tq:])


def _lane_column(block, h):
    lane = lax.broadcasted_iota(I32, block.shape, 1)
    return jnp.sum(jnp.where(lane == h, block, 0.0), axis=-1, keepdims=True)


V_ROWS = LANES + BF16_ROWS


def _split_select(x):
    hi, mid, lo = _split3(x)
    m3 = lax.broadcasted_iota(I32, x.shape, 1) % 3
    return jnp.where(m3 == 0, hi.astype(F32), jnp.where(m3 == 1, mid.astype(F32), lo.astype(F32)))


def _head_masks(shape):
    lane = lax.broadcasted_iota(I32, shape, 1)
    return lane, lane < 64


def _finish_heads(acc, tq):
    o0 = acc[0:64, 0:tq] / acc[LANES:LANES + 1, 0:tq]
    o1 = acc[64:LANES, tq:2 * tq] / acc[LANES:LANES + 1, tq:2 * tq]
    return jnp.concatenate([o0, o1], axis=0).T


def _fox_prompt_body(q_ref, k_ref, v_ref, fk_ref, fq_ref, gq_ref, gk_ref,
                     o_ref, kn_ref, kaug, vt3, m_s, acc_s, s_a, s_b, *, tq, seq, dh):
    qi = pl.program_id(2)
    ones = _pair_ones()
    nk = seq // tq

    @pl.when(qi == 0)
    def _():
        for c in range(nk):
            rows = slice(c * tq, (c + 1) * tq)
            kn = _pair_norm(k_ref[rows, :], gk_ref[...], ones)
            kn_ref[rows, :] = kn
            kaug[rows, 0:LANES] = kn.astype(BF16)
            lane, _ = _head_masks((tq, LANES))
            sp = _split_select(fk_ref[rows, :])
            kaug[rows, LANES:2 * LANES] = jnp.where(
                lane < 6, -sp, jnp.where(lane < 12, 1.0, 0.0)).astype(BF16)
            vt3[c, 0:LANES, :] = v_ref[rows, :].T.astype(BF16)
            vt3[c, LANES:V_ROWS, :] = jnp.ones((BF16_ROWS, tq), BF16)

    qn = _pair_norm(q_ref[...], gq_ref[...], ones) * (dh ** -0.5)
    lane, first = _head_masks((tq, LANES))
    sp = _split_select(fq_ref[...])
    up0 = jnp.where(lane < 3, 1.0, jnp.where((lane >= 6) & (lane < 9), sp, 0.0))
    up1 = jnp.where((lane >= 3) & (lane < 6), 1.0, jnp.where((lane >= 9) & (lane < 12), sp, 0.0))
    qs = jnp.concatenate(
        [jnp.concatenate([jnp.where(first, qn, 0.0), up0], axis=1),
         jnp.concatenate([jnp.where(first, 0.0, qn), up1], axis=1)], axis=0).astype(BF16)
    m_s[...] = jnp.full(m_s.shape, NEG_INF, F32)
    acc_s[...] = jnp.zeros(acc_s.shape, F32)

    def scores(kj):
        r0 = pl.multiple_of(kj * tq, tq)
        return _dot_nt(kaug[pl.ds(r0, tq), :], qs)

    def absorb(s_ref, kj, diagonal):
        s = s_ref[...]
        if diagonal:
            row = lax.broadcasted_iota(I32, (tq, 2 * tq), 0)
            col = lax.broadcasted_iota(I32, (tq, 2 * tq), 1) % tq
            s = jnp.where(row <= col, s, NEG_INF)
        m_old = m_s[0:1, :]
        m_new = jnp.maximum(m_old, jnp.max(s, axis=0, keepdims=True))
        alpha = jnp.exp(m_old - m_new)
        p = jnp.exp(s - m_new).astype(BF16)
        acc_s[...] = alpha * acc_s[...] + _dot(vt3[kj], p)
        m_s[0:1, :] = m_new

    s_a[...] = scores(qi)
    s_b[...] = scores(0)
    absorb(s_a, qi, True)

    def pair(t, carry):
        j0 = 2 * t
        s_a[...] = scores(jnp.minimum(j0 + 1, qi))
        absorb(s_b, j0, False)

        @pl.when(j0 + 1 < qi)
        def _():
            s_b[...] = scores(jnp.minimum(j0 + 2, qi))
            absorb(s_a, j0 + 1, False)
        return carry
    lax.fori_loop(0, lax.shift_right_logical(qi + 1, 1), pair, 0)
    o_ref[...] = _finish_heads(acc_s[...], tq)


def _fox_prompt(proj, f_aug, gq, gk, *, batch, seq, n_pairs, col_q, col_k, col_v, m_rows):
    tq = _pick(seq, (256, 128))
    nq = seq // tq
    width = n_pairs * LANES
    est = (2 * 3 * _nbytes((seq, LANES), F32) + 2 * _nbytes((seq, LANES), F32)
           + _nbytes((seq, 2 * LANES), BF16) + _nbytes((V_ROWS, seq), BF16)
           + 12 * _nbytes((tq, 2 * tq), F32))
    return pl.pallas_call(
        functools.partial(_fox_prompt_body, tq=tq, seq=seq, dh=64),
        out_shape=(jax.ShapeDtypeStruct((m_rows, width), F32),
                   jax.ShapeDtypeStruct((batch * seq, width), F32)),
        grid=(batch, n_pairs, nq),
        in_specs=[
            pl.BlockSpec((tq, LANES), lambda b, hp, qi: (b * nq + qi, col_q + hp)),
            pl.BlockSpec((seq, LANES), lambda b, hp, qi: (b, col_k + hp)),
            pl.BlockSpec((seq, LANES), lambda b, hp, qi: (b, col_v + hp)),
            pl.BlockSpec((seq, LANES), lambda b, hp, qi: (b, hp)),
            pl.BlockSpec((tq, LANES), lambda b, hp, qi: (b * nq + qi, hp)),
            pl.BlockSpec((1, LANES), lambda b, hp, qi: (0, 0)),
            pl.BlockSpec((1, LANES), lambda b, hp, qi: (0, 0)),
        ],
        out_specs=(pl.BlockSpec((tq, LANES), lambda b, hp, qi: (b * nq + qi, hp)),
                   pl.BlockSpec((seq, LANES), lambda b, hp, qi: (b, hp))),
        scratch_shapes=[pltpu.VMEM((seq, 2 * LANES), BF16), pltpu.VMEM((nq, V_ROWS, tq), BF16),
                        pltpu.VMEM((SUBLANES, 2 * tq), F32), pltpu.VMEM((V_ROWS, 2 * tq), F32),
                        pltpu.VMEM((tq, 2 * tq), F32), pltpu.VMEM((tq, 2 * tq), F32)],
        compiler_params=pltpu.CompilerParams(
            dimension_semantics=("parallel", "parallel", "arbitrary"),
            vmem_limit_bytes=_vmem_limit(est)),
        name="fox_prompt",
    )(proj, proj, proj, f_aug, f_aug, gq, gk)


def _band_prompt_body(q_ref, k_ref, v_ref, tab_ref, gq_ref, gk_ref,
                      o_ref, kn_ref, kpad, vt3, s_a, s_b, *, tq, seq, past, dh):
    ones = _pair_ones()
    nq = seq // tq
    npad = past // tq
    win = past + tq

    kpad[0:past, :] = jnp.zeros((past, LANES), BF16)
    for c in range(npad):
        vt3[c] = jnp.zeros((V_ROWS, tq), BF16)
    for c in range(nq):
        rows = slice(c * tq, (c + 1) * tq)
        kn = _pair_norm(k_ref[rows, :], gk_ref[...], ones)
        kn_ref[rows, :] = kn
        kpad[past + c * tq:past + (c + 1) * tq, :] = kn.astype(BF16)
        vt3[npad + c, 0:LANES, :] = v_ref[rows, :].T.astype(BF16)
        vt3[npad + c, LANES:V_ROWS, :] = jnp.ones((BF16_ROWS, tq), BF16)

    def scores(qi):
        r0 = pl.multiple_of(qi * tq, tq)
        qn = _pair_norm(q_ref[pl.ds(r0, tq), :], gq_ref[...], ones) * (dh ** -0.5)
        return _dot_nt(kpad[pl.ds(r0, win), :], _stack_heads(qn))

    row = lax.broadcasted_iota(I32, (win, 2 * tq), 0)

    def absorb(s_ref, qi):
        r0 = pl.multiple_of(qi * tq, tq)
        s = jnp.where(row >= past - r0, s_ref[...] + tab_ref[...], NEG_INF)
        m = jnp.max(s, axis=0, keepdims=True)
        p = jnp.exp(s - m).astype(BF16)
        acc = _dot(vt3[qi], p[0:tq, :])
        for c in range(1, win // tq):
            acc = acc + _dot(vt3[qi + c], p[c * tq:(c + 1) * tq, :])
        o_ref[pl.ds(r0, tq), :] = _finish_heads(acc, tq)

    s_a[...] = scores(0)
    if nq % 2:
        absorb(s_a, 0)
        s_a[...] = scores(min(1, nq - 1))

    def pair(t, carry):
        j0 = 2 * t + nq % 2
        s_b[...] = scores(j0 + 1)
        absorb(s_a, j0)
        s_a[...] = scores(jnp.minimum(j0 + 2, nq - 1))
        absorb(s_b, j0 + 1)
        return carry
    lax.fori_loop(0, nq // 2, pair, 0)


def _band_prompt(proj, tab, gq, gk, *, batch, seq, n_pairs, col_q, col_k, col_v, tq, m_rows):
    past = BAND_PAST * CHUNK
    assert past % tq == 0 and seq % tq == 0
    nq = seq // tq
    width = n_pairs * LANES
    win = past + tq
    est = (2 * 5 * _nbytes((seq, LANES), F32)
           + _nbytes((seq + past, LANES), BF16) + _nbytes((V_ROWS, seq + past), BF16)
           + 2 * _nbytes((win, 2 * tq), F32) + 8 * _nbytes((win, 2 * tq), F32))
    return pl.pallas_call(
        functools.partial(_band_prompt_body, tq=tq, seq=seq, past=past, dh=64),
        out_shape=(jax.ShapeDtypeStruct((m_rows, width), F32),
                   jax.ShapeDtypeStruct((batch * seq, width), F32)),
        grid=(n_pairs, batch),
        in_specs=[
            pl.BlockSpec((seq, LANES), lambda hp, b: (b, col_q + hp)),
            pl.BlockSpec((seq, LANES), lambda hp, b: (b, col_k + hp)),
            pl.BlockSpec((seq, LANES), lambda hp, b: (b, col_v + hp)),
            pl.BlockSpec((None, win, 2 * tq), lambda hp, b: (hp, 0, 0)),
            pl.BlockSpec((1, LANES), lambda hp, b: (0, 0)),
            pl.BlockSpec((1, LANES), lambda hp, b: (0, 0)),
        ],
        out_specs=(pl.BlockSpec((seq, LANES), lambda hp, b: (b, hp)),
                   pl.BlockSpec((seq, LANES), lambda hp, b: (b, hp))),
        scratch_shapes=[pltpu.VMEM((seq + past, LANES), BF16),
                        pltpu.VMEM(((seq + past) // tq, V_ROWS, tq), BF16),
                        pltpu.VMEM((win, 2 * tq), F32), pltpu.VMEM((win, 2 * tq), F32)],
        compiler_params=pltpu.CompilerParams(
            dimension_semantics=("parallel", "parallel"),
            vmem_limit_bytes=_vmem_limit(est)),
        name="band_prompt",
    )(proj, proj, proj, tab, gq, gk)


def _sample_attn_body(*refs, mode, n_pairs, s_new, n_cache, dh):
    if mode == "fox":
        (q_ref, k_ref, v_ref, ck_ref, cv_ref, fq_ref, ft_ref, gq_ref, gk_ref, _alias,
         o_ref, kn_ref) = refs
    else:
        (q_ref, k_ref, v_ref, ck_ref, cv_ref, tabc_ref, tabn_ref, gq_ref, gk_ref, _alias,
         o_ref, kn_ref) = refs
    ones = _pair_ones()
    row = lax.broadcasted_iota(I32, (2 * s_new, s_new), 0) % s_new
    col = lax.broadcasted_iota(I32, (2 * s_new, s_new), 1)
    for hp in range(n_pairs):
        sl = slice(hp * LANES, (hp + 1) * LANES)
        qn = _pair_norm(q_ref[:, sl], gq_ref[...], ones) * (dh ** -0.5)
        kn = _pair_norm(k_ref[:, sl], gk_ref[...], ones)
        kn_ref[:, sl] = kn
        qs = _stack_heads(qn)
        sc = _dot_nt(qs, ck_ref[:, sl].astype(BF16))
        sn = _dot_nt(qs, kn.astype(BF16))
        if mode == "fox":
            fblk = fq_ref[...]
            fq = jnp.concatenate([_lane_column(fblk, 2 * hp), _lane_column(fblk, 2 * hp + 1)],
                                 axis=0)
            f0 = ft_ref[2 * hp:2 * hp + 1, :]
            f1 = ft_ref[2 * hp + 1:2 * hp + 2, :]
            fkc = jnp.concatenate([jnp.broadcast_to(f0[:, :n_cache], (s_new, n_cache)),
                                   jnp.broadcast_to(f1[:, :n_cache], (s_new, n_cache))], axis=0)
            fkn = jnp.concatenate(
                [jnp.broadcast_to(f0[:, n_cache:n_cache + s_new], (s_new, s_new)),
                 jnp.broadcast_to(f1[:, n_cache:n_cache + s_new], (s_new, s_new))], axis=0)
            sc = sc + fq - fkc
            sn = jnp.where(col <= row, sn + fq - fkn, NEG_INF)
        else:
            sc = sc + jnp.concatenate([tabc_ref[2 * hp], tabc_ref[2 * hp + 1]], axis=0)
            sn = sn + jnp.concatenate([tabn_ref[2 * hp], tabn_ref[2 * hp + 1]], axis=0)
        m = jnp.maximum(jnp.max(sc, axis=-1, keepdims=True), jnp.max(sn, axis=-1, keepdims=True))
        pc = jnp.exp(sc - m)
        pn = jnp.exp(sn - m)
        l = jnp.sum(pc, axis=-1, keepdims=True) + jnp.sum(pn, axis=-1, keepdims=True)
        o = (_dot(pc.astype(BF16), cv_ref[:, sl].astype(BF16))
             + _dot(pn.astype(BF16), v_ref[:, sl].astype(BF16))) / l
        o_ref[:, sl] = _unstack_heads(o, s_new)


def _sample_attn(proj, cache_k, cache_v, extra_a, extra_b, gq, gk, o_buf, *, mode, batch, s_new,
                 n_pairs, col_q, row0):
    width = n_pairs * LANES
    n_cache = cache_k.shape[0] // batch
    rb = row0 // s_new
    if mode == "fox":
        ex_specs = [pl.BlockSpec((s_new, LANES), lambda b: (b, 0)),
                    pl.BlockSpec((None, 16, extra_b.shape[2]), lambda b: (b, 0, 0))]
    else:
        ex_specs = [pl.BlockSpec(extra_a.shape, lambda b: (0, 0, 0)),
                    pl.BlockSpec(extra_b.shape, lambda b: (0, 0, 0))]
    est = (2 * 2 * _nbytes((n_cache, width), F32) + 8 * _nbytes((s_new, width), F32)
           + 2 * _nbytes(extra_a.shape, F32) + 16 * _nbytes((2 * s_new, n_cache), F32)
           + 4 * _nbytes((n_cache, LANES), BF16))
    return pl.pallas_call(
        functools.partial(_sample_attn_body, mode=mode, n_pairs=n_pairs, s_new=s_new,
                          n_cache=n_cache, dh=64),
        out_shape=(jax.ShapeDtypeStruct(o_buf.shape, F32),
                   jax.ShapeDtypeStruct((batch * s_new, width), F32)),
        grid=(batch,),
        in_specs=[
            pl.BlockSpec((s_new, width), lambda b: (rb + b, col_q)),
            pl.BlockSpec((s_new, width), lambda b: (rb + b, col_q + 1)),
            pl.BlockSpec((s_new, width), lambda b: (rb + b, col_q + 2)),
            pl.BlockSpec((n_cache, width), lambda b: (b, 0)),
            pl.BlockSpec((n_cache, width), lambda b: (b, 0)),
            *ex_specs,
            pl.BlockSpec((1, LANES), lambda b: (0, 0)),
            pl.BlockSpec((1, LANES), lambda b: (0, 0)),
            pl.BlockSpec(memory_space=pl.ANY),
        ],
        out_specs=(pl.BlockSpec((s_new, width), lambda b: (rb + b, 0)),
                   pl.BlockSpec((s_new, width), lambda b: (b, 0))),
        input_output_aliases={9: 0},
        compiler_params=pltpu.CompilerParams(
            dimension_semantics=("parallel",), vmem_limit_bytes=_vmem_limit(est)),
        name=f"{mode}_sample",
    )(proj, proj, proj, cache_k, cache_v, extra_a, extra_b, gq, gk, o_buf)


def _cross_body(*refs, n_heads, dh, aliased):
    if aliased:
        q_ref, k_ref, v_ref, _alias, o_ref = refs
    else:
        q_ref, k_ref, v_ref, o_ref = refs
    for h in range(n_heads):
        sl = slice(h * dh, (h + 1) * dh)
        q = (q_ref[:, sl] * (dh ** -0.5)).astype(BF16)
        s = _dot_nt(q, k_ref[:, sl].astype(BF16))
        m = jnp.max(s, axis=-1, keepdims=True)
        p = jnp.exp(s - m)
        l = jnp.sum(p, axis=-1, keepdims=True)
        o = _dot(p.astype(BF16), v_ref[:, sl].astype(BF16)) / l
        o_ref[:, sl] = o.astype(o_ref.dtype)


def _cross_attn(q_all, k, v, o_buf, *, batch, q_len, n_mem, n_heads, dh, row0, m_rows):
    width = n_heads * dh
    tq = _pick(q_len, (512, 256, 128, 64, 32, 16))
    nq = q_len // tq
    rb = row0 // tq
    aliased = o_buf is not None
    in_specs = [pl.BlockSpec((tq, width), lambda b, qi: (rb + b * nq + qi, 0)),
                pl.BlockSpec((n_mem, width), lambda b, qi: (b, 0)),
                pl.BlockSpec((n_mem, width), lambda b, qi: (b, 0))]
    args = [q_all, k, v]
    if aliased:
        in_specs.append(pl.BlockSpec(memory_space=pl.ANY))
        args.append(o_buf)
    est = (2 * _nbytes((tq, width), F32) + 4 * _nbytes((n_mem, width), F32)
           + 2 * _nbytes((tq, width), BF16) + 12 * _nbytes((tq, n_mem), F32))
    return pl.pallas_call(
        functools.partial(_cross_body, n_heads=n_heads, dh=dh, aliased=aliased),
        out_shape=jax.ShapeDtypeStruct((m_rows, width), BF16),
        grid=(batch, nq),
        in_specs=in_specs,
        out_specs=pl.BlockSpec((tq, width), lambda b, qi: (rb + b * nq + qi, 0)),
        input_output_aliases={3: 0} if aliased else {},
        compiler_params=pltpu.CompilerParams(
            dimension_semantics=("parallel", "arbitrary"), vmem_limit_bytes=_vmem_limit(est)),
        name="cross_attn",
    )(*args)


def _s5_prep_body(lr_ref, li_ref, ldt_ref, br_ref, bi_ref, ar_ref, ai_ref, bbr_ref, bbi_ref):
    lr = lr_ref[...]
    li = li_ref[...]
    dt = jnp.exp(ldt_ref[...])
    mag = jnp.exp(lr * dt)
    a_re = mag * jnp.cos(li * dt)
    a_im = mag * jnp.sin(li * dt)
    den = lr * lr + li * li
    num_re = a_re - 1.0
    coef_re = (num_re * lr + a_im * li) / den
    coef_im = (a_im * lr - num_re * li) / den
    br = br_ref[...]
    bi = bi_ref[...]
    ar_ref[...] = a_re
    ai_ref[...] = a_im
    bbr_ref[...] = coef_re * br - coef_im * bi
    bbi_ref[...] = coef_re * bi + coef_im * br


def _s5_prep(lam_re, lam_im, log_dt, b_re, b_im):
    shape = lam_re.shape
    spec = pl.BlockSpec(shape, lambda: (0, 0))
    return pl.pallas_call(
        _s5_prep_body,
        out_shape=tuple(jax.ShapeDtypeStruct(shape, F32) for _ in range(4)),
        in_specs=[spec] * 5,
        out_specs=tuple([spec] * 4),
        name="s5_discretise",
    )(lam_re, lam_im, log_dt, b_re, b_im)


def _s5_body(u_ref, x0r_ref, x0i_ref, ar_ref, ai_ref, bre_ref, bim_ref, cre_ref, cim_ref,
             d_ref, wg_ref, bg_ref, o_ref, xr_out, xi_out, bur, bui, st_r, st_i,
             *, nb, t_chunk, n_blk, passes):
    i = pl.program_id(0)
    wc = u_ref.shape[1]
    ns = bur.shape[1]
    ub = wc // n_blk
    sb = ns // n_blk

    @pl.when(i == 0)
    def _():
        st_r[...] = x0r_ref[...]
        st_i[...] = x0i_ref[...]

    u = u_ref[...]
    for k in range(n_blk):
        uk = u[:, k * ub:(k + 1) * ub]
        bur[:, k * sb:(k + 1) * sb] = _dot_hp(
            uk, bre_ref[k * ub:(k + 1) * ub, k * sb:(k + 1) * sb], passes)
        bui[:, k * sb:(k + 1) * sb] = _dot_hp(
            uk, bim_ref[k * ub:(k + 1) * ub, k * sb:(k + 1) * sb], passes)

    a_re = ar_ref[...]
    a_im = ai_ref[...]

    def step(t, carry):
        xr, xi = carry
        r0 = pl.multiple_of(t * nb, nb)
        nr = a_re * xr - a_im * xi + bur[pl.ds(r0, nb), :]
        ni = a_re * xi + a_im * xr + bui[pl.ds(r0, nb), :]
        bur[pl.ds(r0, nb), :] = nr
        bui[pl.ds(r0, nb), :] = ni
        return nr, ni
    xr_f, xi_f = lax.fori_loop(0, t_chunk, step, (st_r[...], st_i[...]))
    st_r[...] = xr_f
    st_i[...] = xi_f
    xr_out[...] = xr_f
    xi_out[...] = xi_f
    ys = []
    for k in range(n_blk):
        xrk = bur[:, k * sb:(k + 1) * sb].astype(BF16)
        xik = bui[:, k * sb:(k + 1) * sb].astype(BF16)
        ys.append(_dot(xrk, cre_ref[k * sb:(k + 1) * sb, k * ub:(k + 1) * ub].astype(BF16))
                  - _dot(xik, cim_ref[k * sb:(k + 1) * sb, k * ub:(k + 1) * ub].astype(BF16)))
    y = jnp.concatenate(ys, axis=1) + d_ref[...] * u
    z = _gelu_tanh(y)
    gate = _dot(z.astype(BF16), wg_ref[...].astype(BF16)) + bg_ref[...]
    o_ref[...] = z * jax.nn.sigmoid(gate)


def _s5(u_tb, x0r, x0i, a_re, a_im, b_re, b_im, c_re, c_im, d, w_glu, b_glu, *, nb, seq, n_blk,
        passes):
    wc = u_tb.shape[1]
    ns = a_re.shape[1]
    t_chunk = _pick(seq, (64, 32, 16))
    rows = nb * t_chunk
    full = lambda shape: pl.BlockSpec(shape, lambda i: (0, 0))
    est = (4 * _nbytes((rows, wc), F32) + 2 * _nbytes((rows, ns), F32)
           + 2 * 4 * _nbytes((wc, ns), F32) + 2 * _nbytes((wc, wc), F32)
           + 12 * _nbytes((rows, ns // n_blk), F32) + 8 * _nbytes((nb, ns), F32))
    return pl.pallas_call(
        functools.partial(_s5_body, nb=nb, t_chunk=t_chunk, n_blk=n_blk, passes=passes),
        out_shape=(jax.ShapeDtypeStruct((seq * nb, wc), F32),
                   jax.ShapeDtypeStruct((nb, ns), F32), jax.ShapeDtypeStruct((nb, ns), F32)),
        grid=(seq // t_chunk,),
        in_specs=[pl.BlockSpec((rows, wc), lambda i: (i, 0)),
                  full((nb, ns)), full((nb, ns)), full((1, ns)), full((1, ns)),
                  full((wc, ns)), full((wc, ns)), full((ns, wc)), full((ns, wc)),
                  full((1, wc)), full((wc, wc)), full((1, wc))],
        out_specs=(pl.BlockSpec((rows, wc), lambda i: (i, 0)), full((nb, ns)), full((nb, ns))),
        scratch_shapes=[pltpu.VMEM((rows, ns), F32), pltpu.VMEM((rows, ns), F32),
                        pltpu.VMEM((nb, ns), F32), pltpu.VMEM((nb, ns), F32)],
        compiler_params=pltpu.CompilerParams(
            dimension_semantics=("arbitrary",), vmem_limit_bytes=_vmem_limit(est)),
        name="s5_scan",
    )(u_tb, x0r, x0i, a_re, a_im, b_re, b_im, c_re, c_im, d, w_glu, b_glu)


def _router_body(x_ref, g_ref, wr_ref, pk_ref, idx_ref, gate_ref, *, n_exp, rc):
    tm, d = x_ref.shape
    half = d // 2

    def chunk(c, carry):
        r0 = pl.multiple_of(c * rc, rc)
        x = x_ref[pl.ds(r0, rc), :]
        ms = jnp.mean(x * x, axis=-1, keepdims=True)
        h = x * lax.rsqrt(ms + EPS) * g_ref[...]
        hb = h.astype(BF16).astype(F32)
        lo = lax.shift_right_logical(pltpu.bitcast(hb[:, :half], U32), jnp.uint32(16))
        hi = pltpu.bitcast(hb[:, half:], U32) & jnp.uint32(0xFFFF0000)
        pk_ref[pl.ds(r0, rc), :] = lo | hi
        logits = _dot_hp(h, wr_ref[...])
        lane = lax.broadcasted_iota(I32, logits.shape, 1)
        logits = jnp.where(lane < n_exp, logits, NEG_INF)
        mx = jnp.max(logits, axis=-1, keepdims=True)
        e = jnp.exp(logits - mx)
        probs = e / jnp.sum(e, axis=-1, keepdims=True)
        probs = jnp.where(lane < n_exp, probs, -1.0)
        lane_f = lane.astype(F32)
        p1 = jnp.max(probs, axis=-1, keepdims=True)
        i1 = jnp.min(jnp.where(probs == p1, lane_f, float(LANES)), axis=-1, keepdims=True)
        rest = jnp.where(lane_f == i1, -1.0, probs)
        p2 = jnp.max(rest, axis=-1, keepdims=True)
        i2 = jnp.min(jnp.where(rest == p2, lane_f, float(LANES)), axis=-1, keepdims=True)
        tot = p1 + p2
        idx_ref[pl.ds(r0, rc), :] = jnp.where(lane == 0, i1, jnp.where(lane == 1, i2, 0.0)).astype(I32)
        gate_ref[pl.ds(r0, rc), :] = jnp.where(lane == 0, p1 / tot,
                                                jnp.where(lane == 1, p2 / tot, 0.0))
        return carry
    lax.fori_loop(0, tm // rc, chunk, 0)


def _router(x, g, w_router_pad, *, n_exp, tm):
    m, d = x.shape
    rc = _pick(tm, (256, 176, 128, 64, 32, 16, 8))
    est = 2 * _nbytes((tm, d), F32) + 2 * _nbytes((tm, d // 2), U32) + 16 * _nbytes((rc, d), F32)
    return pl.pallas_call(
        functools.partial(_router_body, n_exp=n_exp, rc=rc),
        out_shape=(jax.ShapeDtypeStruct((m, d // 2), U32),
                   jax.ShapeDtypeStruct((m, LANES), I32),
                   jax.ShapeDtypeStruct((m, LANES), F32)),
        grid=(m // tm,),
        in_specs=[pl.BlockSpec((tm, d), lambda i: (i, 0)),
                  pl.BlockSpec((1, d), lambda i: (0, 0)),
                  pl.BlockSpec((d, LANES), lambda i: (0, 0))],
        out_specs=(pl.BlockSpec((tm, d // 2), lambda i: (i, 0)),
                   pl.BlockSpec((tm, LANES), lambda i: (i, 0)),
                   pl.BlockSpec((tm, LANES), lambda i: (i, 0))),
        compiler_params=pltpu.CompilerParams(
            dimension_semantics=("parallel",), vmem_limit_bytes=_vmem_limit(est)),
        name="moe_router",
    )(x, g, w_router_pad)


def _dispatch_body(nt_ref, tok_ref, pk_hbm, a_ref, buf, sem, *, tm):
    i = pl.program_id(0)
    half = buf.shape[1]

    @pl.when(i < nt_ref[0])
    def _():
        def issue(r, carry):
            t = tok_ref[0, 0, r]
            pltpu.make_async_copy(pk_hbm.at[pl.ds(t, 1), :], buf.at[pl.ds(r, 1), :], sem).start()
            return carry
        lax.fori_loop(0, tm, issue, 0)

        def drain(r, carry):
            pltpu.make_async_copy(pk_hbm.at[pl.ds(0, 1), :], buf.at[pl.ds(r, 1), :], sem).wait()
            return carry
        lax.fori_loop(0, tm, drain, 0)
        pk = buf[...]
        lo = pltpu.bitcast(lax.shift_left(pk, jnp.uint32(16)), F32)
        hi = pltpu.bitcast(pk & jnp.uint32(0xFFFF0000), F32)
        a_ref[:, :half] = lo.astype(BF16)
        a_ref[:, half:] = hi.astype(BF16)


def _dispatch(n_tiles, tok_sorted, packed, *, tm, r_max):
    m, half = packed.shape
    t_max = r_max // tm
    return pl.pallas_call(
        functools.partial(_dispatch_body, tm=tm),
        out_shape=jax.ShapeDtypeStruct((r_max, 2 * half), BF16),
        grid_spec=pltpu.PrefetchScalarGridSpec(
            num_scalar_prefetch=1,
            grid=(t_max,),
            in_specs=[pl.BlockSpec((1, 1, tm), lambda i, nt: (i, 0, 0), memory_space=pltpu.SMEM),
                      pl.BlockSpec(memory_space=pl.ANY)],
            out_specs=pl.BlockSpec((tm, 2 * half), lambda i, nt: (jnp.minimum(i, nt[0] - 1), 0)),
            scratch_shapes=[pltpu.VMEM((tm, half), U32), pltpu.SemaphoreType.DMA]),
        compiler_params=pltpu.CompilerParams(dimension_semantics=("arbitrary",)),
        name="moe_dispatch",
    )(n_tiles, tok_sorted.reshape(t_max, 1, tm), packed)


def _expert_up_body(te_ref, nt_ref, a_ref, w1_ref, w3_ref, h_ref, w1bf, w3bf, *, kc):
    i = pl.program_id(1)
    d = a_ref.shape[1]
    fresh = jnp.logical_or(i == 0, te_ref[i] != te_ref[jnp.maximum(i - 1, 0)])

    @pl.when(jnp.logical_and(fresh, i < nt_ref[0]))
    def _():
        def cast(c, carry):
            r0 = pl.multiple_of(c * kc, kc)
            w1bf[pl.ds(r0, kc), :] = w1_ref[pl.ds(r0, kc), :].astype(BF16)
            w3bf[pl.ds(r0, kc), :] = w3_ref[pl.ds(r0, kc), :].astype(BF16)
            return carry
        lax.fori_loop(0, d // kc, cast, 0)

    @pl.when(i < nt_ref[0])
    def _():
        a = a_ref[...]
        h_ref[...] = (jax.nn.silu(_dot(a, w1bf[...])) * _dot(a, w3bf[...])).astype(h_ref.dtype)


def _expert_up(tile_expert, n_tiles, a_sorted, w1, w3, *, tm, tf):
    r_max, d = a_sorted.shape
    n_exp, _, fe = w1.shape
    t_max = r_max // tm
    kc = _pick(d, (512, 256, 128))
    row = lambda f, i, te, nt: jnp.minimum(i, nt[0] - 1)
    est = (2 * _nbytes((tm, d), BF16) + 2 * 2 * _nbytes((d, tf), F32) + 2 * _nbytes((d, tf), BF16)
           + 2 * _nbytes((tm, tf), BF16) + 6 * _nbytes((tm, tf), F32))
    return pl.pallas_call(
        functools.partial(_expert_up_body, kc=kc),
        out_shape=jax.ShapeDtypeStruct((r_max, fe), BF16),
        grid_spec=pltpu.PrefetchScalarGridSpec(
            num_scalar_prefetch=2,
            grid=(fe // tf, t_max),
            in_specs=[pl.BlockSpec((tm, d), lambda f, i, te, nt: (row(f, i, te, nt), 0)),
                      pl.BlockSpec((None, d, tf), lambda f, i, te, nt: (te[i], 0, f)),
                      pl.BlockSpec((None, d, tf), lambda f, i, te, nt: (te[i], 0, f))],
            out_specs=pl.BlockSpec((tm, tf), lambda f, i, te, nt: (row(f, i, te, nt), f)),
            scratch_shapes=[pltpu.VMEM((d, tf), BF16), pltpu.VMEM((d, tf), BF16)]),
        compiler_params=pltpu.CompilerParams(
            dimension_semantics=("arbitrary", "arbitrary"), vmem_limit_bytes=_vmem_limit(est)),
        name="moe_expert_up",
    )(tile_expert, n_tiles, a_sorted, w1, w3)


def _expert_down_body(te_ref, nt_ref, h_ref, w2_ref, y_ref, w2bf, *, kc):
    i = pl.program_id(1)
    fe = h_ref.shape[1]
    fresh = jnp.logical_or(i == 0, te_ref[i] != te_ref[jnp.maximum(i - 1, 0)])

    @pl.when(jnp.logical_and(fresh, i < nt_ref[0]))
    def _():
        def cast(c, carry):
            r0 = pl.multiple_of(c * kc, kc)
            w2bf[pl.ds(r0, kc), :] = w2_ref[pl.ds(r0, kc), :].astype(BF16)
            return carry
        lax.fori_loop(0, fe // kc, cast, 0)

    @pl.when(i < nt_ref[0])
    def _():
        y_ref[...] = _dot(h_ref[...], w2bf[...])


def _expert_down(tile_expert, n_tiles, h_sorted, w2, *, tm, tn):
    r_max, fe = h_sorted.shape
    d = w2.shape[2]
    t_max = r_max // tm
    kc = _pick(fe, (512, 256, 128))
    row = lambda n, i, te, nt: jnp.minimum(i, nt[0] - 1)
    est = (2 * _nbytes((tm, fe), BF16) + 2 * _nbytes((fe, tn), F32) + _nbytes((fe, tn), BF16)
           + 4 * _nbytes((tm, tn), F32))
    return pl.pallas_call(
        functools.partial(_expert_down_body, kc=kc),
        out_shape=jax.ShapeDtypeStruct((r_max, d), F32),
        grid_spec=pltpu.PrefetchScalarGridSpec(
            num_scalar_prefetch=2,
            grid=(d // tn, t_max),
            in_specs=[pl.BlockSpec((tm, fe), lambda n, i, te, nt: (row(n, i, te, nt), 0)),
                      pl.BlockSpec((None, fe, tn), lambda n, i, te, nt: (te[i], 0, n))],
            out_specs=pl.BlockSpec((tm, tn), lambda n, i, te, nt: (row(n, i, te, nt), n)),
            scratch_shapes=[pltpu.VMEM((fe, tn), BF16)]),
        compiler_params=pltpu.CompilerParams(
            dimension_semantics=("arbitrary", "arbitrary"), vmem_limit_bytes=_vmem_limit(est)),
        name="moe_expert_down",
    )(tile_expert, n_tiles, h_sorted, w2)


def _combine_body(slot_ref, x_ref, gate_ref, y_hbm, o_ref, buf, sem, *, tc):
    def issue(r, carry):
        for k in range(TOP_K):
            s = slot_ref[0, 0, TOP_K * r + k]
            pltpu.make_async_copy(y_hbm.at[pl.ds(s, 1), :], buf.at[k, pl.ds(r, 1), :], sem).start()
        return carry
    lax.fori_loop(0, tc, issue, 0)

    def drain(r, carry):
        for k in range(TOP_K):
            pltpu.make_async_copy(y_hbm.at[pl.ds(0, 1), :], buf.at[k, pl.ds(r, 1), :], sem).wait()
        return carry
    lax.fori_loop(0, tc, drain, 0)
    g = gate_ref[...]
    o_ref[...] = x_ref[...] + (g[:, 0:1] * buf[0] + g[:, 1:2] * buf[1])


def _combine(slots, x, gates, y_sorted, *, tc):
    m, d = x.shape
    est = 4 * _nbytes((tc, d), F32) + 2 * _nbytes((tc, d), F32) + 4 * _nbytes((tc, d), F32)
    return pl.pallas_call(
        functools.partial(_combine_body, tc=tc),
        out_shape=jax.ShapeDtypeStruct((m, d), F32),
        grid=(m // tc,),
        in_specs=[pl.BlockSpec((1, 1, TOP_K * tc), lambda i: (i, 0, 0), memory_space=pltpu.SMEM),
                  pl.BlockSpec((tc, d), lambda i: (i, 0)),
                  pl.BlockSpec((tc, LANES), lambda i: (i, 0)),
                  pl.BlockSpec(memory_space=pl.ANY)],
        out_specs=pl.BlockSpec((tc, d), lambda i: (i, 0)),
        scratch_shapes=[pltpu.VMEM((TOP_K, tc, d), F32), pltpu.SemaphoreType.DMA],
        compiler_params=pltpu.CompilerParams(
            dimension_semantics=("arbitrary",), vmem_limit_bytes=_vmem_limit(est)),
        name="moe_combine",
    )(slots.reshape(m // tc, 1, TOP_K * tc), x, gates, y_sorted)


def _moe(x, g_ffn, w_router, w_e1, w_e3, w_e2, *, tm_tok):
    m, d = x.shape
    n_exp, _, fe = w_e1.shape
    tm = _pick(m * TOP_K, (512, 256, 128, 64, 32, 16))
    wr = jnp.zeros((d, LANES), F32).at[:, :n_exp].set(w_router)
    packed, idx128, gate128 = _router(x, g_ffn, wr, n_exp=n_exp, tm=tm_tok)

    idx = idx128[:, :TOP_K]
    mask = jnp.sum(idx[:, :, None] == jnp.arange(n_exp, dtype=I32)[None, None, :], axis=1).astype(I32)
    counts = jnp.sum(mask, axis=0)
    padded = ((counts + tm - 1) // tm) * tm
    ends = jnp.cumsum(padded)
    starts = ends - padded
    pos = jnp.cumsum(mask, axis=0) - mask
    slot = starts[idx] + jnp.take_along_axis(pos, idx, axis=1)
    t_max = (m * TOP_K) // tm + n_exp
    r_max = t_max * tm
    tok_sorted = jnp.zeros((r_max,), I32).at[slot.reshape(-1)].set(
        jnp.repeat(jnp.arange(m, dtype=I32), TOP_K))
    n_tiles = (ends[-1] // tm).astype(I32).reshape(1)
    tile_start = jnp.minimum(jnp.arange(t_max, dtype=I32), n_tiles[0] - 1) * tm
    tile_expert = jnp.minimum(jnp.sum(ends[None, :] <= tile_start[:, None], axis=1),
                              n_exp - 1).astype(I32)

    a_sorted = _dispatch(n_tiles, tok_sorted, packed, tm=tm, r_max=r_max)
    tf = _pick(fe, (256, 128))
    h_sorted = _expert_up(tile_expert, n_tiles, a_sorted, w_e1, w_e3, tm=tm, tf=tf)
    y_sorted = _expert_down(tile_expert, n_tiles, h_sorted, w_e2, tm=tm, tn=_pick(d, (512, 256, 128)))
    tc = _pick(m, (256, 128, 64, 32, 16, 8))
    return _combine(slot, x, gate128, y_sorted, tc=tc)


def _block_diag(blocks):
    g, r, c = blocks.shape
    eye = jnp.eye(g, dtype=blocks.dtype)
    return (blocks[:, :, None, :] * eye[:, None, :, None]).reshape(g * r, g * c)


def _rel_table(rel_bias_l, q_pos, k_pos, max_rel):
    rel = jnp.clip(q_pos[:, None] - k_pos[None, :], -(CHUNK - 1), max_rel) + (CHUNK - 1)
    return rel_bias_l[:, rel].astype(F32)


def _band_table(rel_bias_l, tq, max_rel):
    n_heads = rel_bias_l.shape[0]
    past = BAND_PAST * CHUNK
    win = past + tq
    span = tq + win - 1
    idx = np.clip(np.arange(span) - (tq - 1), -(CHUNK - 1), max_rel) + (CHUNK - 1)
    u = jnp.pad(rel_bias_l.astype(F32)[:, idx], ((0, 0), (0, 1)))
    shifted = jnp.tile(u, (1, win))[:, :win * span].reshape(n_heads, win, span)
    tab = shifted[:, :, win - 1:win - 1 + tq]
    i_chunk = (np.arange(tq) + past)[None, :] // CHUNK
    j_chunk = np.arange(win)[:, None] // CHUNK
    valid = (i_chunk - j_chunk >= 0) & (i_chunk - j_chunk <= BAND_PAST)
    tab = jnp.where(valid[None], tab, NEG_INF).reshape(n_heads // 2, 2, win, tq)
    return jnp.transpose(tab, (0, 2, 1, 3)).reshape(n_heads // 2, win, 2 * tq)


def _tile_gain(g):
    return jnp.tile(g.astype(F32), LANES // g.shape[0]).reshape(1, LANES)


def kernel(x_prompt, x_sample, mem_prompt, cache_fox_k, cache_fox_v, cache_fox_logf, cache_band_k, cache_band_v, state_ssm_re, state_ssm_im, cache_mem_k, cache_mem_v, g_mix, w_in, b_f, g_qa, g_ka, g_qb, g_kb, rel_bias, lam_re, lam_im, log_dt, ssm_b_re, ssm_b_im, ssm_c_re, ssm_c_im, ssm_d, w_glu, b_glu, g_mix_out, w_out, g_cross, g_mem, w_cq, w_ck, w_cv, g_cq, g_ck, w_co, g_ffn, w_ff1, w_ff3, w_ff2, w_router, w_e1, w_e3, w_e2):
    batch, seq, d = x_prompt.shape
    dbatch, dseq, _ = x_sample.shape
    depth = g_mix.shape[0]
    past_len, h_a, dh = cache_fox_k.shape[2:]
    band_rows, h_b = cache_band_k.shape[2:4]
    g_c, p_state = lam_re.shape[1:]
    w_a, w_b, w_c = h_a * dh, h_b * dh, g_c * SSM_GROUP
    n_mem, h_m, dh_m = cache_mem_k.shape[2:]
    w_m = h_m * dh_m
    max_rel = rel_bias.shape[2] - CHUNK
    assert dh == 64 and h_a % 2 == 0 and h_b % 2 == 0 and dh_m == LANES
    assert w_a == w_b and w_c <= w_a and h_a <= 16
    mp, ms = batch * seq, dbatch * dseq
    m = mp + ms
    pa, pb = h_a // 2, h_b // 2
    n_state = g_c * p_state
    n_blk = 2 if (w_c % 512 == 0) else 1
    nband = min(BAND_PAST * CHUNK, seq)
    tm = _pick(m, (1056, 1024, 768, 512, 256, 128, 64, 32, 16))
    tq_band = _pick(seq, (256, 128, 64))

    x = jnp.concatenate([x_prompt.reshape(mp, d), x_sample.reshape(ms, d)], axis=0)

    outs = {k: [] for k in ("p_fk", "p_fv", "p_fl", "p_bk", "p_bv", "p_sr", "p_si", "p_mk", "p_mv",
                            "s_fk", "s_fv", "s_fl", "s_bk", "s_bv", "s_sr", "s_si")}
    for l in range(depth):
        sizes = (w_a, w_a, w_a, h_a, w_b, w_b, w_b)
        cuts = [sum(sizes[:i]) for i in range(len(sizes) + 1)]
        w_cat = _regroup_columns(
            w_in, l, ((0, cuts[3]), (cuts[4], cuts[7]), (cuts[7], w_in.shape[2]), (cuts[3], cuts[4])),
            7 * w_a)
        proj = _mm([x], [w_cat], gain=g_mix[l].reshape(1, d), tm=tm, tn=w_a, name="proj_in")
        col_uc = 6 * w_a

        fa_p = proj[:mp, col_uc + w_c:col_uc + w_c + h_a].reshape(batch, seq, h_a)
        fa_p = jnp.transpose(fa_p, (0, 2, 1)).reshape(batch * h_a, seq)
        bias_p = jnp.broadcast_to(jnp.tile(b_f[l], batch)[:, None], (batch * h_a, LANES))
        lf_p, f_p = _logf_cumsum(fa_p, bias_p, raw_from=0, valid_to=seq)
        lf_p = lf_p.reshape(batch, h_a, seq)
        f_p = f_p.reshape(batch, h_a, seq)
        f_t = jnp.transpose(f_p, (0, 2, 1)).reshape(batch, seq, pa, 2, 1)
        f_aug = jnp.tile(jnp.broadcast_to(f_t, (batch, seq, pa, 2, 3)).reshape(batch, seq, pa, 6),
                         (1, 1, 1, 2))
        f_aug = jnp.pad(f_aug, ((0, 0), (0, 0), (0, 0), (0, LANES - 12))).reshape(mp, pa * LANES)

        t_all = -(-(past_len + dseq) // LANES) * LANES
        fa_s = proj[mp:, col_uc + w_c:col_uc + w_c + h_a].reshape(dbatch, dseq, h_a)
        x_s = jnp.concatenate([jnp.transpose(cache_fox_logf[l], (0, 2, 1)),
                               jnp.transpose(fa_s, (0, 2, 1)),
                               jnp.zeros((dbatch, h_a, t_all - past_len - dseq), F32)], axis=2)
        bias_s = jnp.broadcast_to(jnp.tile(b_f[l], dbatch)[:, None], (dbatch * h_a, LANES))
        lf_s, f_s = _logf_cumsum(x_s.reshape(dbatch * h_a, t_all), bias_s,
                                 raw_from=past_len, valid_to=past_len + dseq)
        lf_s = lf_s.reshape(dbatch, h_a, t_all)[:, :, past_len:past_len + dseq]
        f_s = f_s.reshape(dbatch, h_a, t_all)
        fs_row = jnp.pad(f_s, ((0, 0), (0, 16 - h_a), (0, 0)))
        fs_col = jnp.pad(jnp.transpose(f_s[:, :, past_len:past_len + dseq], (0, 2, 1)),
                         ((0, 0), (0, 0), (0, LANES - h_a))).reshape(ms, LANES)

        gqa, gka = _tile_gain(g_qa[l]), _tile_gain(g_ka[l])
        oa, kn_a = _fox_prompt(proj, f_aug, gqa, gka, batch=batch, seq=seq, n_pairs=pa,
                               col_q=0, col_k=pa, col_v=2 * pa, m_rows=m)
        oa, kn_as = _sample_attn(proj, cache_fox_k[l].reshape(dbatch * past_len, w_a),
                                 cache_fox_v[l].reshape(dbatch * past_len, w_a), fs_col, fs_row,
                                 gqa, gka, oa, mode="fox", batch=dbatch, s_new=dseq, n_pairs=pa,
                                 col_q=0, row0=mp)

        gqb, gkb = _tile_gain(g_qb[l]), _tile_gain(g_kb[l])
        tab = _band_table(rel_bias[l], tq_band, max_rel)
        ob, kn_b = _band_prompt(proj, tab, gqb, gkb, batch=batch, seq=seq, n_pairs=pb,
                                col_q=3 * pa, col_k=3 * pa + pb, col_v=3 * pa + 2 * pb, tq=tq_band,
                                m_rows=m)
        tab_s = _rel_table(rel_bias[l], band_rows + jnp.arange(dseq), jnp.arange(band_rows + dseq),
                           max_rel)
        ob, kn_bs = _sample_attn(proj, cache_band_k[l].reshape(dbatch * band_rows, w_b),
                                 cache_band_v[l].reshape(dbatch * band_rows, w_b),
                                 tab_s[:, :, :band_rows], tab_s[:, :, band_rows:], gqb, gkb, ob,
                                 mode="band", batch=dbatch, s_new=dseq, n_pairs=pb, col_q=3, row0=mp)

        rep = lambda a: jnp.repeat(a.astype(F32), SSM_GROUP, axis=0)
        a_re, a_im, bb_re, bb_im = _s5_prep(
            rep(lam_re[l]), rep(lam_im[l]),
            jnp.broadcast_to(rep(log_dt[l])[:, None], (g_c * SSM_GROUP, p_state)),
            jnp.transpose(ssm_b_re[l], (0, 2, 1)).reshape(g_c * SSM_GROUP, p_state),
            jnp.transpose(ssm_b_im[l], (0, 2, 1)).reshape(g_c * SSM_GROUP, p_state))
        a_re = a_re.reshape(g_c, SSM_GROUP, p_state)[:, 0, :].reshape(1, n_state)
        a_im = a_im.reshape(g_c, SSM_GROUP, p_state)[:, 0, :].reshape(1, n_state)
        b_re_d = _block_diag(bb_re.reshape(g_c, SSM_GROUP, p_state))
        b_im_d = _block_diag(bb_im.reshape(g_c, SSM_GROUP, p_state))
        c_re_d = _block_diag(jnp.transpose(ssm_c_re[l], (0, 2, 1)))
        c_im_d = _block_diag(jnp.transpose(ssm_c_im[l], (0, 2, 1)))
        d_row = ssm_d[l].reshape(1, w_c)
        s5_args = (a_re, a_im, b_re_d, b_im_d, c_re_d, c_im_d, d_row, w_glu[l],
                   b_glu[l].reshape(1, w_c))
        uc = proj[:, col_uc:col_uc + w_c]
        u_p = jnp.transpose(uc[:mp].reshape(batch, seq, w_c), (1, 0, 2)).reshape(mp, w_c)
        u_s = jnp.transpose(uc[mp:].reshape(dbatch, dseq, w_c), (1, 0, 2)).reshape(ms, w_c)
        zeros_p = jnp.zeros((batch, n_state), F32)
        oc_p, sr_p, si_p = _s5(u_p, zeros_p, zeros_p, *s5_args, nb=batch, seq=seq, n_blk=n_blk,
                               passes=1)
        oc_s, sr_s, si_s = _s5(u_s, state_ssm_re[l].reshape(dbatch, n_state),
                               state_ssm_im[l].reshape(dbatch, n_state), *s5_args,
                               nb=dbatch, seq=dseq, n_blk=n_blk, passes=3)
        oc = jnp.concatenate(
            [jnp.transpose(oc_p.reshape(seq, batch, w_c), (1, 0, 2)).reshape(mp, w_c),
             jnp.transpose(oc_s.reshape(dseq, dbatch, w_c), (1, 0, 2)).reshape(ms, w_c)], axis=0)

        x = _mm([oa, ob, oc], [w_out[l]], gain=g_mix_out[l].reshape(1, -1), residual=x,
                tm=tm, tn=_pick(d, (512, 256, 128)), name="merge_out")

        outs["p_fk"].append(kn_a.reshape(batch, seq, h_a, dh))
        outs["p_fv"].append(proj[:mp, 2 * w_a:3 * w_a].reshape(batch, seq, h_a, dh))
        outs["p_fl"].append(jnp.transpose(lf_p, (0, 2, 1)))
        outs["p_bk"].append(kn_b.reshape(batch, seq, h_b, dh)[:, seq - nband:])
        outs["p_bv"].append(proj[:mp, 5 * w_a:6 * w_a].reshape(batch, seq, h_b, dh)[:, seq - nband:])
        outs["p_sr"].append(sr_p.reshape(batch, g_c, p_state))
        outs["p_si"].append(si_p.reshape(batch, g_c, p_state))
        outs["s_fk"].append(kn_as.reshape(dbatch, dseq, h_a, dh))
        outs["s_fv"].append(proj[mp:, 2 * w_a:3 * w_a].reshape(dbatch, dseq, h_a, dh))
        outs["s_fl"].append(jnp.transpose(lf_s, (0, 2, 1)))
        outs["s_bk"].append(kn_bs.reshape(dbatch, dseq, h_b, dh))
        outs["s_bv"].append(proj[mp:, 5 * w_a:6 * w_a].reshape(dbatch, dseq, h_b, dh))
        outs["s_sr"].append(sr_s.reshape(dbatch, g_c, p_state))
        outs["s_si"].append(si_s.reshape(dbatch, g_c, p_state))

        mem2 = mem_prompt.reshape(batch * n_mem, d)
        tmm = _pick(batch * n_mem, (1024, 512, 256, 128))
        gck = jnp.tile(g_ck[l].astype(F32), h_m).reshape(1, w_m)
        gcq = jnp.tile(g_cq[l].astype(F32), h_m).reshape(1, w_m)
        mk = _mm([mem2], [w_ck[l]], gain=g_mem[l].reshape(1, d), group_gain=gck,
                 epilogue="group_norm", tm=tmm, tn=w_m, name="mem_k")
        mv = _mm([mem2], [w_cv[l]], gain=g_mem[l].reshape(1, d), tm=tmm, tn=w_m, name="mem_v")
        outs["p_mk"].append(mk.reshape(batch, n_mem, h_m, dh_m))
        outs["p_mv"].append(mv.reshape(batch, n_mem, h_m, dh_m))
        q_c = _mm([x], [w_cq[l]], gain=g_cross[l].reshape(1, d), group_gain=gcq,
                  epilogue="group_norm", tm=tm, tn=w_m, name="cross_q")
        o_c = _cross_attn(q_c, mk, mv, None, batch=batch, q_len=seq, n_mem=n_mem, n_heads=h_m,
                          dh=dh_m, row0=0, m_rows=m)
        o_c = _cross_attn(q_c, cache_mem_k[l].reshape(dbatch * n_mem, w_m),
                          cache_mem_v[l].reshape(dbatch * n_mem, w_m), o_c, batch=dbatch,
                          q_len=dseq, n_mem=n_mem, n_heads=h_m, dh=dh_m, row0=mp, m_rows=m)
        x = _mm([o_c], [w_co[l]], residual=x, tm=tm, tn=_pick(d, (512, 256, 128)), name="cross_out")

        i = l // 2
        if l % 2 == 0:
            hmid = _mm([x], [w_ff1[i], w_ff3[i]], gain=g_ffn[l].reshape(1, d), epilogue="swiglu",
                       out_dtype=BF16, tm=tm, tn=_pick(w_ff1.shape[2], (512, 256, 128)),
                       name="ffn_up")
            x = _mm([hmid], [w_ff2[i]], residual=x, tm=tm, tn=_pick(d, (256, 128)), name="ffn_down")
        else:
            x = _moe(x, g_ffn[l].reshape(1, d), w_router[i], w_e1[i], w_e3[i], w_e2[i], tm_tok=tm)

    st = lambda k: jnp.stack(outs[k])
    return (x[:mp].reshape(batch, seq, d), x[mp:].reshape(dbatch, dseq, d),
            st("p_fk"), st("p_fv"), st("p_fl"), st("p_bk"), st("p_bv"), st("p_sr"), st("p_si"),
            st("p_mk"), st("p_mv"), st("s_fk"), st("s_fv"), st("s_fl"), st("s_bk"), st("s_bv"),
            st("s_sr"), st("s_si"))
```

```python
import functools
import math

import jax
import jax.numpy as jnp
import numpy as np
from jax import lax
from jax.experimental import pallas as pl
from jax.experimental.pallas import tpu as pltpu

F32 = jnp.float32
BF16 = jnp.bfloat16
I32 = jnp.int32
U32 = jnp.uint32

EPS = 1e-6
NEG_INF = -1e30
CHUNK = 64
BAND_PAST = 8
SSM_GROUP = 16
TOP_K = 2

LANES = 128
SUBLANES = 8
BF16_ROWS = 16
VMEM_CAP = 60 * 1024 * 1024


def _vmem_limit(nbytes):
    return int(min(VMEM_CAP, max(16 * 1024 * 1024, nbytes * 5 // 4 + (4 << 20))))


def _pick(n, candidates):
    for c in candidates:
        if c <= n and n % c == 0:
            return c
    raise ValueError(f"no tile for {n} in {candidates}")


def _nbytes(shape, dtype):
    return math.prod(shape) * jnp.dtype(dtype).itemsize


def _split3(x):
    hi = x.astype(BF16)
    r1 = x - hi.astype(F32)
    mid = r1.astype(BF16)
    lo = (r1 - mid.astype(F32)).astype(BF16)
    return hi, mid, lo


def _dot(a, b):
    return jnp.dot(a, b, preferred_element_type=F32)


def _dot_nt(a, b):
    return lax.dot_general(a, b, (((1,), (1,)), ((), ())), preferred_element_type=F32)


def _dot_hp(a, b, passes=3):
    ah = a.astype(BF16)
    bh = b.astype(BF16)
    if passes == 1:
        return _dot(ah, bh)
    al = (a - ah.astype(F32)).astype(BF16)
    bl = (b - bh.astype(F32)).astype(BF16)
    return _dot(ah, bh) + (_dot(ah, bl) + _dot(al, bh))


def _pair_ones():
    r = lax.broadcasted_iota(I32, (LANES, LANES), 0) // 64
    c = lax.broadcasted_iota(I32, (LANES, LANES), 1) // 64
    return (r == c).astype(BF16)


def _pair_norm(x, g, ones):
    sq = x * x
    hi = sq.astype(BF16)
    lo = (sq - hi.astype(F32)).astype(BF16)
    ss = _dot(hi, ones) + _dot(lo, ones)
    return x * lax.rsqrt(ss * (1.0 / 64.0) + EPS) * g


def _log_sigmoid(x):
    return jnp.minimum(x, 0.0) - jnp.log(1.0 + jnp.exp(-jnp.abs(x)))


def _gelu_tanh(x):
    c = math.sqrt(2.0 / math.pi)
    return 0.5 * x * (1.0 + jnp.tanh(c * (x + 0.044715 * (x * x * x))))


def _mm_body(*refs, widths, norm, n_w, epilogue, has_res, has_gg, stage_a, tm, rc, kc, k_total):
    it = iter(refs)
    a_refs = [next(it) for _ in widths]
    g_ref = next(it) if norm else None
    w_refs = [next(it) for _ in range(n_w)]
    gg_ref = next(it) if has_gg else None
    res_ref = next(it) if has_res else None
    o_ref = next(it)
    abf = next(it) if stage_a else a_refs[0]
    cast_w = w_refs[0].dtype != BF16
    wbfs = [next(it) for _ in range(n_w)] if cast_w else w_refs
    j = pl.program_id(1)

    if stage_a:
        @pl.when(j == 0)
        def _():
            off = 0
            for a_ref, wd in zip(a_refs, widths):
                def chunk(c, carry, a_ref=a_ref, off=off, wd=wd):
                    r0 = pl.multiple_of(c * rc, rc)
                    x = a_ref[pl.ds(r0, rc), :].astype(F32)
                    if norm:
                        ms = jnp.mean(x * x, axis=-1, keepdims=True)
                        x = x * lax.rsqrt(ms + EPS) * g_ref[:, off:off + wd]
                    abf[pl.ds(r0, rc), off:off + wd] = x.astype(BF16)
                    return carry
                lax.fori_loop(0, tm // rc, chunk, 0)
                off += wd

    if cast_w:
        for w_ref, wbf in zip(w_refs, wbfs):
            def cast(c, carry, w_ref=w_ref, wbf=wbf):
                r0 = pl.multiple_of(c * kc, kc)
                wbf[pl.ds(r0, kc), :] = w_ref[pl.ds(r0, kc), :].astype(BF16)
                return carry
            lax.fori_loop(0, k_total // kc, cast, 0)

    def rows(c, carry):
        r0 = pl.multiple_of(c * rc, rc)
        a = abf[pl.ds(r0, rc), :]
        ys = [_dot(a, wbf[...]) for wbf in wbfs]
        if epilogue == "swiglu":
            y = jax.nn.silu(ys[0]) * ys[1]
        elif epilogue == "group_norm":
            parts = []
            for s in range(ys[0].shape[1] // LANES):
                ysl = ys[0][:, s * LANES:(s + 1) * LANES]
                ms = jnp.mean(ysl * ysl, axis=-1, keepdims=True)
                parts.append(ysl * lax.rsqrt(ms + EPS) * gg_ref[:, s * LANES:(s + 1) * LANES])
            y = jnp.concatenate(parts, axis=1)
        else:
            y = ys[0]
        if has_res:
            y = y + res_ref[pl.ds(r0, rc), :]
        o_ref[pl.ds(r0, rc), :] = y.astype(o_ref.dtype)
        return carry
    lax.fori_loop(0, tm // rc, rows, 0)


def _mm(a_parts, w_list, *, gain=None, group_gain=None, residual=None, epilogue="none",
        out_dtype=F32, tm, tn, name):
    m = a_parts[0].shape[0]
    widths = tuple(a.shape[1] for a in a_parts)
    k_total = sum(widths)
    n = w_list[0].shape[1]
    assert m % tm == 0 and n % tn == 0, (m, tm, n, tn)
    norm = gain is not None
    stage_a = norm or len(a_parts) > 1 or a_parts[0].dtype != BF16
    rc = _pick(tm, (512, 384, 352, 256, 176, 128, 64, 32, 16))
    kc = _pick(k_total, (512, 256, 128))
    grid = (m // tm, n // tn)
    in_specs = [pl.BlockSpec((tm, wd), lambda i, j: (i, 0)) for wd in widths]
    args = list(a_parts)
    est = sum(2 * _nbytes((tm, wd), a.dtype) for wd, a in zip(widths, a_parts))
    if norm:
        in_specs.append(pl.BlockSpec((1, k_total), lambda i, j: (0, 0)))
        args.append(gain)
    cast_w = w_list[0].dtype != BF16
    for w in w_list:
        in_specs.append(pl.BlockSpec((k_total, tn), lambda i, j: (0, j)))
        args.append(w)
        est += 2 * _nbytes((k_total, tn), w.dtype) + cast_w * _nbytes((k_total, tn), BF16)
    if group_gain is not None:
        in_specs.append(pl.BlockSpec((1, tn), lambda i, j: (0, j)))
        args.append(group_gain)
    if residual is not None:
        in_specs.append(pl.BlockSpec((tm, tn), lambda i, j: (i, j)))
        args.append(residual)
        est += 2 * _nbytes((tm, tn), F32)
    est += 2 * _nbytes((tm, tn), out_dtype) + stage_a * _nbytes((tm, k_total), BF16)
    est += 4 * _nbytes((rc, tn), F32) * len(w_list)
    body = functools.partial(
        _mm_body, widths=widths, norm=norm, n_w=len(w_list), epilogue=epilogue,
        has_res=residual is not None, has_gg=group_gain is not None, stage_a=stage_a, tm=tm,
        rc=rc, kc=kc, k_total=k_total)
    return pl.pallas_call(
        body,
        out_shape=jax.ShapeDtypeStruct((m, n), out_dtype),
        grid=grid,
        in_specs=in_specs,
        out_specs=pl.BlockSpec((tm, tn), lambda i, j: (i, j)),
        scratch_shapes=[pltpu.VMEM((tm, k_total), BF16)] * stage_a
        + [pltpu.VMEM((k_total, tn), BF16) for _ in w_list] * cast_w,
        compiler_params=pltpu.CompilerParams(
            dimension_semantics=("parallel", "arbitrary"),
            vmem_limit_bytes=_vmem_limit(est)),
        name=name,
    )(*args)


def _regroup_body(w_ref, o_ref, *, cuts, width):
    off = 0
    for lo, hi in cuts:
        o_ref[:, off:off + hi - lo] = w_ref[:, lo:hi].astype(o_ref.dtype)
        off += hi - lo
    if off < width:
        o_ref[:, off:width] = jnp.zeros((o_ref.shape[0], width - off), o_ref.dtype)


def _regroup_columns(w, layer, cuts, width):
    _, k, n = w.shape
    tr = _pick(k, (256, 128, 64, 32, 16))
    return pl.pallas_call(
        functools.partial(_regroup_body, cuts=cuts, width=width),
        out_shape=jax.ShapeDtypeStruct((k, width), BF16),
        grid=(k // tr,),
        in_specs=[pl.BlockSpec((None, tr, n), lambda i: (layer, i, 0))],
        out_specs=pl.BlockSpec((tr, width), lambda i: (i, 0)),
        compiler_params=pltpu.CompilerParams(
            dimension_semantics=("parallel",),
            vmem_limit_bytes=_vmem_limit(2 * _nbytes((tr, n), F32) + 2 * _nbytes((tr, width), BF16)
                                         + 4 * _nbytes((tr, width), F32))),
        name="regroup_w_in",
    )(w)


def _cumsum_body(x_ref, b_ref, lf_ref, f_ref, carry, *, raw_from, valid_to, tt):
    j = pl.program_id(0)

    @pl.when(j == 0)
    def _():
        carry[...] = jnp.zeros_like(carry)

    x = x_ref[...]
    lane = j * tt + lax.broadcasted_iota(I32, x.shape, 1)
    lf = jnp.where(lane >= raw_from, _log_sigmoid(x + b_ref[:, 0:1]), x)
    lf = jnp.where(lane < valid_to, lf, 0.0)
    lf_ref[...] = lf
    tri = (lax.broadcasted_iota(I32, (tt, tt), 0)
           <= lax.broadcasted_iota(I32, (tt, tt), 1)).astype(BF16)
    hi, mid, lo = _split3(lf)
    y = _dot(hi, tri) + _dot(mid, tri) + _dot(lo, tri) + carry[:, 0:1]
    f_ref[...] = y
    carry[...] = jnp.broadcast_to(y[:, tt - 1:tt], carry.shape)


def _logf_cumsum(x, bias, *, raw_from, valid_to):
    r, t = x.shape
    tt = _pick(t, (256, 128))
    return pl.pallas_call(
        functools.partial(_cumsum_body, raw_from=raw_from, valid_to=valid_to, tt=tt),
        out_shape=(jax.ShapeDtypeStruct((r, t), F32), jax.ShapeDtypeStruct((r, t), F32)),
        grid=(t // tt,),
        in_specs=[pl.BlockSpec((r, tt), lambda j: (0, j)),
                  pl.BlockSpec((r, LANES), lambda j: (0, 0))],
        out_specs=(pl.BlockSpec((r, tt), lambda j: (0, j)),
                   pl.BlockSpec((r, tt), lambda j: (0, j))),
        scratch_shapes=[pltpu.VMEM((r, LANES), F32)],
        compiler_params=pltpu.CompilerParams(dimension_semantics=("arbitrary",)),
        name="logf_cumsum",
    )(x, bias)


def _stack_heads(qn):
    lane = lax.broadcasted_iota(I32, qn.shape, 1)
    q0 = jnp.where(lane < 64, qn, 0.0)
    q1 = jnp.where(lane < 64, 0.0, qn)
    return jnp.concatenate([q0, q1], axis=0).astype(BF16)


def _unstack_heads(o, tq):
    lane = lax.broadcasted_iota(I32, (tq, LANES), 1)
    return jnp.where(lane < 64, o[:tq], o[tq:])


def _lane_column(block, h):
    lane = lax.broadcasted_iota(I32, block.shape, 1)
    return jnp.sum(jnp.where(lane == h, block, 0.0), axis=-1, keepdims=True)


V_ROWS = LANES + BF16_ROWS


def _split_select(x):
    hi, mid, lo = _split3(x)
    m3 = lax.broadcasted_iota(I32, x.shape, 1) % 3
    return jnp.where(m3 == 0, hi.astype(F32), jnp.where(m3 == 1, mid.astype(F32), lo.astype(F32)))


def _head_masks(shape):
    lane = lax.broadcasted_iota(I32, shape, 1)
    return lane, lane < 64


def _finish_heads(acc, tq):
    o0 = acc[0:64, 0:tq] / acc[LANES:LANES + 1, 0:tq]
    o1 = acc[64:LANES, tq:2 * tq] / acc[LANES:LANES + 1, tq:2 * tq]
    return jnp.concatenate([o0, o1], axis=0).T


def _fox_prompt_body(q_ref, k_ref, v_ref, fk_ref, fq_ref, gq_ref, gk_ref,
                     o_ref, knt_ref, vt_ref, kaug, vt3, m_s, acc_s, s_a, s_b, *, tq, seq, dh):
    qi = pl.program_id(2)
    ones = _pair_ones()
    nk = seq // tq

    @pl.when(qi == 0)
    def _():
        for c in range(nk):
            rows = slice(c * tq, (c + 1) * tq)
            kn = _pair_norm(k_ref[rows, :], gk_ref[...], ones)
            knt_ref[:, rows] = kn.T
            kaug[rows, 0:LANES] = kn.astype(BF16)
            lane, _ = _head_masks((tq, LANES))
            sp = _split_select(fk_ref[rows, :])
            kaug[rows, LANES:2 * LANES] = jnp.where(
                lane < 6, -sp, jnp.where(lane < 12, 1.0, 0.0)).astype(BF16)
            vt = v_ref[rows, :].T
            vt_ref[:, rows] = vt
            vt3[c, 0:LANES, :] = vt.astype(BF16)
            vt3[c, LANES:V_ROWS, :] = jnp.ones((BF16_ROWS, tq), BF16)

    qn = _pair_norm(q_ref[...], gq_ref[...], ones) * (dh ** -0.5)
    lane, first = _head_masks((tq, LANES))
    sp = _split_select(fq_ref[...])
    up0 = jnp.where(lane < 3, 1.0, jnp.where((lane >= 6) & (lane < 9), sp, 0.0))
    up1 = jnp.where((lane >= 3) & (lane < 6), 1.0, jnp.where((lane >= 9) & (lane < 12), sp, 0.0))
    qs = jnp.concatenate(
        [jnp.concatenate([jnp.where(first, qn, 0.0), up0], axis=1),
         jnp.concatenate([jnp.where(first, 0.0, qn), up1], axis=1)], axis=0).astype(BF16)
    m_s[...] = jnp.full(m_s.shape, NEG_INF, F32)
    acc_s[...] = jnp.zeros(acc_s.shape, F32)

    def scores(kj):
        r0 = pl.multiple_of(kj * tq, tq)
        return _dot_nt(kaug[pl.ds(r0, tq), :], qs)

    def absorb(s_ref, kj, diagonal):
        s = s_ref[...]
        if diagonal:
            row = lax.broadcasted_iota(I32, (tq, 2 * tq), 0)
            col = lax.broadcasted_iota(I32, (tq, 2 * tq), 1) % tq
            s = jnp.where(row <= col, s, NEG_INF)
        m_old = m_s[0:1, :]
        m_new = jnp.maximum(m_old, jnp.max(s, axis=0, keepdims=True))
        alpha = jnp.exp(m_old - m_new)
        p = jnp.exp(s - m_new).astype(BF16)
        acc_s[...] = alpha * acc_s[...] + _dot(vt3[kj], p)
        m_s[0:1, :] = m_new

    s_a[...] = scores(qi)
    s_b[...] = scores(0)
    absorb(s_a, qi, True)

    def pair(t, carry):
        j0 = 2 * t
        s_a[...] = scores(jnp.minimum(j0 + 1, qi))
        absorb(s_b, j0, False)

        @pl.when(j0 + 1 < qi)
        def _():
            s_b[...] = scores(jnp.minimum(j0 + 2, qi))
            absorb(s_a, j0 + 1, False)
        return carry
    lax.fori_loop(0, lax.shift_right_logical(qi + 1, 1), pair, 0)
    o_ref[...] = _finish_heads(acc_s[...], tq)


def _fox_prompt(proj, f_aug, gq, gk, *, batch, seq, n_pairs, col_q, col_k, col_v, m_rows):
    tq = _pick(seq, (256, 128))
    nq = seq // tq
    width = n_pairs * LANES
    est = (2 * 3 * _nbytes((seq, LANES), F32) + 2 * _nbytes((seq, LANES), F32)
           + _nbytes((seq, 2 * LANES), BF16) + _nbytes((V_ROWS, seq), BF16)
           + 12 * _nbytes((tq, 2 * tq), F32))
    return pl.pallas_call(
        functools.partial(_fox_prompt_body, tq=tq, seq=seq, dh=64),
        out_shape=(jax.ShapeDtypeStruct((m_rows, width), F32),
                   jax.ShapeDtypeStruct((batch, width, seq), F32),
                   jax.ShapeDtypeStruct((batch, width, seq), F32)),
        grid=(batch, n_pairs, nq),
        in_specs=[
            pl.BlockSpec((tq, LANES), lambda b, hp, qi: (b * nq + qi, col_q + hp)),
            pl.BlockSpec((seq, LANES), lambda b, hp, qi: (b, col_k + hp)),
            pl.BlockSpec((seq, LANES), lambda b, hp, qi: (b, col_v + hp)),
            pl.BlockSpec((seq, LANES), lambda b, hp, qi: (b, hp)),
            pl.BlockSpec((tq, LANES), lambda b, hp, qi: (b * nq + qi, hp)),
            pl.BlockSpec((1, LANES), lambda b, hp, qi: (0, 0)),
            pl.BlockSpec((1, LANES), lambda b, hp, qi: (0, 0)),
        ],
        out_specs=(pl.BlockSpec((tq, LANES), lambda b, hp, qi: (b * nq + qi, hp)),
                   pl.BlockSpec((None, LANES, seq), lambda b, hp, qi: (b, hp, 0)),
                   pl.BlockSpec((None, LANES, seq), lambda b, hp, qi: (b, hp, 0))),
        scratch_shapes=[pltpu.VMEM((seq, 2 * LANES), BF16), pltpu.VMEM((nq, V_ROWS, tq), BF16),
                        pltpu.VMEM((SUBLANES, 2 * tq), F32), pltpu.VMEM((V_ROWS, 2 * tq), F32),
                        pltpu.VMEM((tq, 2 * tq), F32), pltpu.VMEM((tq, 2 * tq), F32)],
        compiler_params=pltpu.CompilerParams(
            dimension_semantics=("parallel", "parallel", "arbitrary"),
            vmem_limit_bytes=_vmem_limit(est)),
        name="fox_prompt",
    )(proj, proj, proj, f_aug, f_aug, gq, gk)


def _band_prompt_body(q_ref, k_ref, v_ref, tab_ref, gq_ref, gk_ref,
                      o_ref, knt_ref, vt_ref, kpad, vt3, s_a, s_b, *, tq, seq, past, dh):
    ones = _pair_ones()
    nq = seq // tq
    npad = past // tq
    win = past + tq

    kpad[0:past, :] = jnp.zeros((past, LANES), BF16)
    for c in range(npad):
        vt3[c] = jnp.zeros((V_ROWS, tq), BF16)
    for c in range(nq):
        rows = slice(c * tq, (c + 1) * tq)
        kn = _pair_norm(k_ref[rows, :], gk_ref[...], ones)
        knt_ref[:, rows] = kn.T
        kpad[past + c * tq:past + (c + 1) * tq, :] = kn.astype(BF16)
        vt = v_ref[rows, :].T
        vt_ref[:, rows] = vt
        vt3[npad + c, 0:LANES, :] = vt.astype(BF16)
        vt3[npad + c, LANES:V_ROWS, :] = jnp.ones((BF16_ROWS, tq), BF16)

    def scores(qi):
        r0 = pl.multiple_of(qi * tq, tq)
        qn = _pair_norm(q_ref[pl.ds(r0, tq), :], gq_ref[...], ones) * (dh ** -0.5)
        return _dot_nt(kpad[pl.ds(r0, win), :], _stack_heads(qn))

    row = lax.broadcasted_iota(I32, (win, 2 * tq), 0)

    def absorb(s_ref, qi):
        r0 = pl.multiple_of(qi * tq, tq)
        s = jnp.where(row >= past - r0, s_ref[...] + tab_ref[...], NEG_INF)
        m = jnp.max(s, axis=0, keepdims=True)
        p = jnp.exp(s - m).astype(BF16)
        acc = _dot(vt3[qi], p[0:tq, :])
        for c in range(1, win // tq):
            acc = acc + _dot(vt3[qi + c], p[c * tq:(c + 1) * tq, :])
        o_ref[pl.ds(r0, tq), :] = _finish_heads(acc, tq)

    s_a[...] = scores(0)
    if nq % 2:
        absorb(s_a, 0)
        s_a[...] = scores(min(1, nq - 1))

    def pair(t, carry):
        j0 = 2 * t + nq % 2
        s_b[...] = scores(j0 + 1)
        absorb(s_a, j0)
        s_a[...] = scores(jnp.minimum(j0 + 2, nq - 1))
        absorb(s_b, j0 + 1)
        return carry
    lax.fori_loop(0, nq // 2, pair, 0)


def _band_prompt(proj, tab, gq, gk, *, batch, seq, n_pairs, col_q, col_k, col_v, tq, m_rows):
    past = BAND_PAST * CHUNK
    assert past % tq == 0 and seq % tq == 0
    nq = seq // tq
    width = n_pairs * LANES
    win = past + tq
    est = (2 * 5 * _nbytes((seq, LANES), F32)
           + _nbytes((seq + past, LANES), BF16) + _nbytes((V_ROWS, seq + past), BF16)
           + 2 * _nbytes((win, 2 * tq), F32) + 8 * _nbytes((win, 2 * tq), F32))
    return pl.pallas_call(
        functools.partial(_band_prompt_body, tq=tq, seq=seq, past=past, dh=64),
        out_shape=(jax.ShapeDtypeStruct((m_rows, width), F32),
                   jax.ShapeDtypeStruct((batch, width, seq), F32),
                   jax.ShapeDtypeStruct((batch, width, seq), F32)),
        grid=(n_pairs, batch),
        in_specs=[
            pl.BlockSpec((seq, LANES), lambda hp, b: (b, col_q + hp)),
            pl.BlockSpec((seq, LANES), lambda hp, b: (b, col_k + hp)),
            pl.BlockSpec((seq, LANES), lambda hp, b: (b, col_v + hp)),
            pl.BlockSpec((None, win, 2 * tq), lambda hp, b: (hp, 0, 0)),
            pl.BlockSpec((1, LANES), lambda hp, b: (0, 0)),
            pl.BlockSpec((1, LANES), lambda hp, b: (0, 0)),
        ],
        out_specs=(pl.BlockSpec((seq, LANES), lambda hp, b: (b, hp)),
                   pl.BlockSpec((None, LANES, seq), lambda hp, b: (b, hp, 0)),
                   pl.BlockSpec((None, LANES, seq), lambda hp, b: (b, hp, 0))),
        scratch_shapes=[pltpu.VMEM((seq + past, LANES), BF16),
                        pltpu.VMEM(((seq + past) // tq, V_ROWS, tq), BF16),
                        pltpu.VMEM((win, 2 * tq), F32), pltpu.VMEM((win, 2 * tq), F32)],
        compiler_params=pltpu.CompilerParams(
            dimension_semantics=("parallel", "parallel"),
            vmem_limit_bytes=_vmem_limit(est)),
        name="band_prompt",
    )(proj, proj, proj, tab, gq, gk)


def _sample_attn_body(*refs, mode, n_pairs, s_new, n_cache, dh):
    if mode == "fox":
        (q_ref, k_ref, v_ref, ck_ref, cv_ref, fq_ref, ft_ref, gq_ref, gk_ref, _alias,
         o_ref, kn_ref) = refs
    else:
        (q_ref, k_ref, v_ref, ck_ref, cv_ref, tabc_ref, tabn_ref, gq_ref, gk_ref, _alias,
         o_ref, kn_ref) = refs
    ones = _pair_ones()
    row = lax.broadcasted_iota(I32, (s_new, s_new), 0)
    col = lax.broadcasted_iota(I32, (s_new, s_new), 1)
    for hp in range(n_pairs):
        sl = slice(hp * LANES, (hp + 1) * LANES)
        qn = _pair_norm(q_ref[:, sl], gq_ref[...], ones) * (dh ** -0.5)
        kn = _pair_norm(k_ref[:, sl], gk_ref[...], ones)
        kn_ref[:, sl] = kn
        vn = v_ref[:, sl]
        outs = []
        for e in range(2):
            h = 2 * hp + e
            hl = slice(e * dh, (e + 1) * dh)
            qh = qn[:, hl].astype(BF16)
            sc = _dot(qh, ck_ref[h].astype(BF16))
            sn = _dot_nt(qh, kn[:, hl].astype(BF16))
            if mode == "fox":
                fq = _lane_column(fq_ref[...], h)
                fk = ft_ref[h:h + 1, :]
                sc = sc + fq - fk[:, :n_cache]
                sn = jnp.where(col <= row, sn + fq - fk[:, n_cache:n_cache + s_new], NEG_INF)
            else:
                sc = sc + tabc_ref[h]
                sn = sn + tabn_ref[h]
            m = jnp.maximum(jnp.max(sc, axis=-1, keepdims=True),
                            jnp.max(sn, axis=-1, keepdims=True))
            pc = jnp.exp(sc - m)
            pn = jnp.exp(sn - m)
            l = jnp.sum(pc, axis=-1, keepdims=True) + jnp.sum(pn, axis=-1, keepdims=True)
            outs.append((_dot_nt(pc.astype(BF16), cv_ref[h].astype(BF16))
                         + _dot(pn.astype(BF16), vn[:, hl].astype(BF16))) / l)
        o_ref[:, sl] = jnp.concatenate(outs, axis=1)


def _sample_attn(proj, cache_kt, cache_vt, layer, extra_a, extra_b, gq, gk, o_buf, *, mode, batch,
                 s_new, n_pairs, col_q, row0):
    width = n_pairs * LANES
    _, _, n_heads, dh, n_cache = cache_kt.shape
    rb = row0 // s_new
    if mode == "fox":
        ex_specs = [pl.BlockSpec((s_new, LANES), lambda b: (b, 0)),
                    pl.BlockSpec((None, 16, extra_b.shape[2]), lambda b: (b, 0, 0))]
    else:
        ex_specs = [pl.BlockSpec(extra_a.shape, lambda b: (0, 0, 0)),
                    pl.BlockSpec(extra_b.shape, lambda b: (0, 0, 0))]
    est = (2 * 2 * _nbytes((n_cache, width), F32) + 8 * _nbytes((s_new, width), F32)
           + 2 * _nbytes(extra_a.shape, F32) + 16 * _nbytes((2 * s_new, n_cache), F32)
           + 4 * _nbytes((n_cache, LANES), BF16))
    return pl.pallas_call(
        functools.partial(_sample_attn_body, mode=mode, n_pairs=n_pairs, s_new=s_new,
                          n_cache=n_cache, dh=dh),
        out_shape=(jax.ShapeDtypeStruct(o_buf.shape, F32),
                   jax.ShapeDtypeStruct((batch * s_new, width), F32)),
        grid=(batch,),
        in_specs=[
            pl.BlockSpec((s_new, width), lambda b: (rb + b, col_q)),
            pl.BlockSpec((s_new, width), lambda b: (rb + b, col_q + 1)),
            pl.BlockSpec((s_new, width), lambda b: (rb + b, col_q + 2)),
            pl.BlockSpec((None, None, n_heads, dh, n_cache), lambda b: (layer, b, 0, 0, 0)),
            pl.BlockSpec((None, None, n_heads, dh, n_cache), lambda b: (layer, b, 0, 0, 0)),
            *ex_specs,
            pl.BlockSpec((1, LANES), lambda b: (0, 0)),
            pl.BlockSpec((1, LANES), lambda b: (0, 0)),
            pl.BlockSpec(memory_space=pl.ANY),
        ],
        out_specs=(pl.BlockSpec((s_new, width), lambda b: (rb + b, 0)),
                   pl.BlockSpec((s_new, width), lambda b: (b, 0))),
        input_output_aliases={9: 0},
        compiler_params=pltpu.CompilerParams(
            dimension_semantics=("parallel",), vmem_limit_bytes=_vmem_limit(est)),
        name=f"{mode}_sample",
    )(proj, proj, proj, cache_kt, cache_vt, extra_a, extra_b, gq, gk, o_buf)


def _cross_body(*refs, n_heads, dh, aliased):
    if aliased:
        q_ref, k_ref, v_ref, _alias, o_ref = refs
    else:
        q_ref, k_ref, v_ref, o_ref = refs
    for h in range(n_heads):
        sl = slice(h * dh, (h + 1) * dh)
        q = (q_ref[:, sl] * (dh ** -0.5)).astype(BF16)
        s = _dot_nt(q, k_ref[:, sl].astype(BF16))
        m = jnp.max(s, axis=-1, keepdims=True)
        p = jnp.exp(s - m)
        l = jnp.sum(p, axis=-1, keepdims=True)
        o = _dot(p.astype(BF16), v_ref[:, sl].astype(BF16)) / l
        o_ref[:, sl] = o.astype(o_ref.dtype)


def _cross_attn(q_all, k, v, o_buf, *, batch, q_len, n_mem, n_heads, dh, row0, m_rows):
    width = n_heads * dh
    tq = _pick(q_len, (512, 256, 128, 64, 32, 16))
    nq = q_len // tq
    rb = row0 // tq
    aliased = o_buf is not None
    in_specs = [pl.BlockSpec((tq, width), lambda b, qi: (rb + b * nq + qi, 0)),
                pl.BlockSpec((n_mem, width), lambda b, qi: (b, 0)),
                pl.BlockSpec((n_mem, width), lambda b, qi: (b, 0))]
    args = [q_all, k, v]
    if aliased:
        in_specs.append(pl.BlockSpec(memory_space=pl.ANY))
        args.append(o_buf)
    est = (2 * _nbytes((tq, width), F32) + 4 * _nbytes((n_mem, width), F32)
           + 2 * _nbytes((tq, width), BF16) + 12 * _nbytes((tq, n_mem), F32))
    return pl.pallas_call(
        functools.partial(_cross_body, n_heads=n_heads, dh=dh, aliased=aliased),
        out_shape=jax.ShapeDtypeStruct((m_rows, width), BF16),
        grid=(batch, nq),
        in_specs=in_specs,
        out_specs=pl.BlockSpec((tq, width), lambda b, qi: (rb + b * nq + qi, 0)),
        input_output_aliases={3: 0} if aliased else {},
        compiler_params=pltpu.CompilerParams(
            dimension_semantics=("parallel", "arbitrary"), vmem_limit_bytes=_vmem_limit(est)),
        name="cross_attn",
    )(*args)


def _s5_prep_body(lr_ref, li_ref, ldt_ref, br_ref, bi_ref, ar_ref, ai_ref, bbr_ref, bbi_ref):
    lr = lr_ref[...]
    li = li_ref[...]
    dt = jnp.exp(ldt_ref[...])
    mag = jnp.exp(lr * dt)
    a_re = mag * jnp.cos(li * dt)
    a_im = mag * jnp.sin(li * dt)
    den = lr * lr + li * li
    num_re = a_re - 1.0
    coef_re = (num_re * lr + a_im * li) / den
    coef_im = (a_im * lr - num_re * li) / den
    br = br_ref[...]
    bi = bi_ref[...]
    ar_ref[...] = a_re
    ai_ref[...] = a_im
    bbr_ref[...] = coef_re * br - coef_im * bi
    bbi_ref[...] = coef_re * bi + coef_im * br


def _s5_prep(lam_re, lam_im, log_dt, b_re, b_im):
    shape = lam_re.shape
    spec = pl.BlockSpec(shape, lambda: (0, 0))
    return pl.pallas_call(
        _s5_prep_body,
        out_shape=tuple(jax.ShapeDtypeStruct(shape, F32) for _ in range(4)),
        in_specs=[spec] * 5,
        out_specs=tuple([spec] * 4),
        name="s5_discretise",
    )(lam_re, lam_im, log_dt, b_re, b_im)


def _s5_body(u_ref, x0r_ref, x0i_ref, ar_ref, ai_ref, bre_ref, bim_ref, cre_ref, cim_ref,
             d_ref, wg_ref, bg_ref, o_ref, xr_out, xi_out, bur, bui, st_r, st_i,
             *, nb, t_chunk, n_blk, passes):
    i = pl.program_id(0)
    wc = u_ref.shape[1]
    ns = bur.shape[1]
    ub = wc // n_blk
    sb = ns // n_blk

    @pl.when(i == 0)
    def _():
        st_r[...] = x0r_ref[...]
        st_i[...] = x0i_ref[...]

    u = u_ref[...]
    for k in range(n_blk):
        uk = u[:, k * ub:(k + 1) * ub]
        bur[:, k * sb:(k + 1) * sb] = _dot_hp(
            uk, bre_ref[k * ub:(k + 1) * ub, k * sb:(k + 1) * sb], passes)
        bui[:, k * sb:(k + 1) * sb] = _dot_hp(
            uk, bim_ref[k * ub:(k + 1) * ub, k * sb:(k + 1) * sb], passes)

    a_re = ar_ref[...]
    a_im = ai_ref[...]

    def step(t, carry):
        xr, xi = carry
        r0 = pl.multiple_of(t * nb, nb)
        nr = a_re * xr - a_im * xi + bur[pl.ds(r0, nb), :]
        ni = a_re * xi + a_im * xr + bui[pl.ds(r0, nb), :]
        bur[pl.ds(r0, nb), :] = nr
        bui[pl.ds(r0, nb), :] = ni
        return nr, ni
    xr_f, xi_f = lax.fori_loop(0, t_chunk, step, (st_r[...], st_i[...]))
    st_r[...] = xr_f
    st_i[...] = xi_f
    xr_out[...] = xr_f
    xi_out[...] = xi_f
    ys = []
    for k in range(n_blk):
        xrk = bur[:, k * sb:(k + 1) * sb].astype(BF16)
        xik = bui[:, k * sb:(k + 1) * sb].astype(BF16)
        ys.append(_dot(xrk, cre_ref[k * sb:(k + 1) * sb, k * ub:(k + 1) * ub].astype(BF16))
                  - _dot(xik, cim_ref[k * sb:(k + 1) * sb, k * ub:(k + 1) * ub].astype(BF16)))
    y = jnp.concatenate(ys, axis=1) + d_ref[...] * u
    z = _gelu_tanh(y)
    gate = _dot(z.astype(BF16), wg_ref[...].astype(BF16)) + bg_ref[...]
    o_ref[...] = z * jax.nn.sigmoid(gate)


def _s5(u_tb, x0r, x0i, a_re, a_im, b_re, b_im, c_re, c_im, d, w_glu, b_glu, *, nb, seq, n_blk,
        passes):
    wc = u_tb.shape[1]
    ns = a_re.shape[1]
    t_chunk = _pick(seq, (64, 32, 16))
    rows = nb * t_chunk
    full = lambda shape: pl.BlockSpec(shape, lambda i: (0, 0))
    est = (4 * _nbytes((rows, wc), F32) + 2 * _nbytes((rows, ns), F32)
           + 2 * 4 * _nbytes((wc, ns), F32) + 2 * _nbytes((wc, wc), F32)
           + 12 * _nbytes((rows, ns // n_blk), F32) + 8 * _nbytes((nb, ns), F32))
    return pl.pallas_call(
        functools.partial(_s5_body, nb=nb, t_chunk=t_chunk, n_blk=n_blk, passes=passes),
        out_shape=(jax.ShapeDtypeStruct((seq * nb, wc), F32),
                   jax.ShapeDtypeStruct((nb, ns), F32), jax.ShapeDtypeStruct((nb, ns), F32)),
        grid=(seq // t_chunk,),
        in_specs=[pl.BlockSpec((rows, wc), lambda i: (i, 0)),
                  full((nb, ns)), full((nb, ns)), full((1, ns)), full((1, ns)),
                  full((wc, ns)), full((wc, ns)), full((ns, wc)), full((ns, wc)),
                  full((1, wc)), full((wc, wc)), full((1, wc))],
        out_specs=(pl.BlockSpec((rows, wc), lambda i: (i, 0)), full((nb, ns)), full((nb, ns))),
        scratch_shapes=[pltpu.VMEM((rows, ns), F32), pltpu.VMEM((rows, ns), F32),
                        pltpu.VMEM((nb, ns), F32), pltpu.VMEM((nb, ns), F32)],
        compiler_params=pltpu.CompilerParams(
            dimension_semantics=("arbitrary",), vmem_limit_bytes=_vmem_limit(est)),
        name="s5_scan",
    )(u_tb, x0r, x0i, a_re, a_im, b_re, b_im, c_re, c_im, d, w_glu, b_glu)


def _router_body(x_ref, g_ref, wr_ref, pk_ref, idx_ref, gate_ref, *, n_exp, rc):
    tm, d = x_ref.shape
    half = d // 2

    def chunk(c, carry):
        r0 = pl.multiple_of(c * rc, rc)
        x = x_ref[pl.ds(r0, rc), :]
        ms = jnp.mean(x * x, axis=-1, keepdims=True)
        h = x * lax.rsqrt(ms + EPS) * g_ref[...]
        hb = h.astype(BF16).astype(F32)
        lo = lax.shift_right_logical(pltpu.bitcast(hb[:, :half], U32), jnp.uint32(16))
        hi = pltpu.bitcast(hb[:, half:], U32) & jnp.uint32(0xFFFF0000)
        pk_ref[pl.ds(r0, rc), :] = lo | hi
        logits = _dot_hp(h, wr_ref[...])
        lane = lax.broadcasted_iota(I32, logits.shape, 1)
        logits = jnp.where(lane < n_exp, logits, NEG_INF)
        mx = jnp.max(logits, axis=-1, keepdims=True)
        e = jnp.exp(logits - mx)
        probs = e / jnp.sum(e, axis=-1, keepdims=True)
        probs = jnp.where(lane < n_exp, probs, -1.0)
        lane_f = lane.astype(F32)
        p1 = jnp.max(probs, axis=-1, keepdims=True)
        i1 = jnp.min(jnp.where(probs == p1, lane_f, float(LANES)), axis=-1, keepdims=True)
        rest = jnp.where(lane_f == i1, -1.0, probs)
        p2 = jnp.max(rest, axis=-1, keepdims=True)
        i2 = jnp.min(jnp.where(rest == p2, lane_f, float(LANES)), axis=-1, keepdims=True)
        tot = p1 + p2
        idx_ref[pl.ds(r0, rc), :] = jnp.where(lane == 0, i1, jnp.where(lane == 1, i2, 0.0)).astype(I32)
        gate_ref[pl.ds(r0, rc), :] = jnp.where(lane == 0, p1 / tot,
                                                jnp.where(lane == 1, p2 / tot, 0.0))
        return carry
    lax.fori_loop(0, tm // rc, chunk, 0)


def _router(x, g, w_router_pad, *, n_exp, tm):
    m, d = x.shape
    rc = _pick(tm, (256, 176, 128, 64, 32, 16, 8))
    est = 2 * _nbytes((tm, d), F32) + 2 * _nbytes((tm, d // 2), U32) + 16 * _nbytes((rc, d), F32)
    return pl.pallas_call(
        functools.partial(_router_body, n_exp=n_exp, rc=rc),
        out_shape=(jax.ShapeDtypeStruct((m, d // 2), U32),
                   jax.ShapeDtypeStruct((m, LANES), I32),
                   jax.ShapeDtypeStruct((m, LANES), F32)),
        grid=(m // tm,),
        in_specs=[pl.BlockSpec((tm, d), lambda i: (i, 0)),
                  pl.BlockSpec((1, d), lambda i: (0, 0)),
                  pl.BlockSpec((d, LANES), lambda i: (0, 0))],
        out_specs=(pl.BlockSpec((tm, d // 2), lambda i: (i, 0)),
                   pl.BlockSpec((tm, LANES), lambda i: (i, 0)),
                   pl.BlockSpec((tm, LANES), lambda i: (i, 0))),
        compiler_params=pltpu.CompilerParams(
            dimension_semantics=("parallel",), vmem_limit_bytes=_vmem_limit(est)),
        name="moe_router",
    )(x, g, w_router_pad)


GATHER_UNROLL = 8


def _row_copy(src_hbm, row, dst, r, sem):
    return pltpu.make_async_copy(src_hbm.at[pl.ds(row, 1), :], dst.at[pl.ds(r, 1), :], sem)


def _dispatch_body(nt_ref, tok_ref, nxt_ref, pk_hbm, a_ref, buf, sem, *, tm):
    i = pl.program_id(0)
    nt = nt_ref[0]
    half = buf.shape[2]

    def request(ids_ref, slot):
        def issue(c, carry):
            for u in range(GATHER_UNROLL):
                r = c * GATHER_UNROLL + u
                _row_copy(pk_hbm, ids_ref[0, 0, r], buf.at[slot], r, sem.at[slot]).start()
            return carry
        lax.fori_loop(0, tm // GATHER_UNROLL, issue, 0)

    @pl.when(i == 0)
    def _():
        request(tok_ref, 0)

    @pl.when(i + 1 < nt)
    def _():
        request(nxt_ref, (i + 1) % 2)

    @pl.when(i < nt)
    def _():
        slot = i % 2

        def drain(c, carry):
            for u in range(GATHER_UNROLL):
                r = c * GATHER_UNROLL + u
                _row_copy(pk_hbm, 0, buf.at[slot], r, sem.at[slot]).wait()
            return carry
        lax.fori_loop(0, tm // GATHER_UNROLL, drain, 0)
        pk = buf[slot]
        lo = pltpu.bitcast(lax.shift_left(pk, jnp.uint32(16)), F32)
        hi = pltpu.bitcast(pk & jnp.uint32(0xFFFF0000), F32)
        a_ref[:, :half] = lo.astype(BF16)
        a_ref[:, half:] = hi.astype(BF16)


def _dispatch(n_tiles, tok_sorted, packed, *, tm, r_max):
    m, half = packed.shape
    t_max = r_max // tm
    assert tm % GATHER_UNROLL == 0
    tok_tiles = tok_sorted.reshape(t_max, 1, tm)
    return pl.pallas_call(
        functools.partial(_dispatch_body, tm=tm),
        out_shape=jax.ShapeDtypeStruct((r_max, 2 * half), BF16),
        grid_spec=pltpu.PrefetchScalarGridSpec(
            num_scalar_prefetch=1,
            grid=(t_max,),
            in_specs=[pl.BlockSpec((1, 1, tm), lambda i, nt: (i, 0, 0), memory_space=pltpu.SMEM),
                      pl.BlockSpec((1, 1, tm), lambda i, nt: (jnp.minimum(i + 1, t_max - 1), 0, 0),
                                   memory_space=pltpu.SMEM),
                      pl.BlockSpec(memory_space=pl.ANY)],
            out_specs=pl.BlockSpec((tm, 2 * half), lambda i, nt: (jnp.minimum(i, nt[0] - 1), 0)),
            scratch_shapes=[pltpu.VMEM((2, tm, half), U32), pltpu.SemaphoreType.DMA((2,))]),
        compiler_params=pltpu.CompilerParams(dimension_semantics=("arbitrary",)),
        name="moe_dispatch",
    )(n_tiles, tok_tiles, tok_tiles, packed)


def _expert_up_body(te_ref, nt_ref, a_ref, w1_ref, w3_ref, h_ref, w1bf, w3bf, *, kc):
    i = pl.program_id(1)
    d = a_ref.shape[1]
    fresh = jnp.logical_or(i == 0, te_ref[i] != te_ref[jnp.maximum(i - 1, 0)])

    @pl.when(jnp.logical_and(fresh, i < nt_ref[0]))
    def _():
        def cast(c, carry):
            r0 = pl.multiple_of(c * kc, kc)
            w1bf[pl.ds(r0, kc), :] = w1_ref[pl.ds(r0, kc), :].astype(BF16)
            w3bf[pl.ds(r0, kc), :] = w3_ref[pl.ds(r0, kc), :].astype(BF16)
            return carry
        lax.fori_loop(0, d // kc, cast, 0)

    @pl.when(i < nt_ref[0])
    def _():
        a = a_ref[...]
        h_ref[...] = (jax.nn.silu(_dot(a, w1bf[...])) * _dot(a, w3bf[...])).astype(h_ref.dtype)


def _expert_up(tile_expert, n_tiles, a_sorted, w1, w3, *, tm, tf):
    r_max, d = a_sorted.shape
    n_exp, _, fe = w1.shape
    t_max = r_max // tm
    kc = _pick(d, (512, 256, 128))
    row = lambda f, i, te, nt: jnp.minimum(i, nt[0] - 1)
    est = (2 * _nbytes((tm, d), BF16) + 2 * 2 * _nbytes((d, tf), F32) + 2 * _nbytes((d, tf), BF16)
           + 2 * _nbytes((tm, tf), BF16) + 6 * _nbytes((tm, tf), F32))
    return pl.pallas_call(
        functools.partial(_expert_up_body, kc=kc),
        out_shape=jax.ShapeDtypeStruct((r_max, fe), BF16),
        grid_spec=pltpu.PrefetchScalarGridSpec(
            num_scalar_prefetch=2,
            grid=(fe // tf, t_max),
            in_specs=[pl.BlockSpec((tm, d), lambda f, i, te, nt: (row(f, i, te, nt), 0)),
                      pl.BlockSpec((None, d, tf), lambda f, i, te, nt: (te[i], 0, f)),
                      pl.BlockSpec((None, d, tf), lambda f, i, te, nt: (te[i], 0, f))],
            out_specs=pl.BlockSpec((tm, tf), lambda f, i, te, nt: (row(f, i, te, nt), f)),
            scratch_shapes=[pltpu.VMEM((d, tf), BF16), pltpu.VMEM((d, tf), BF16)]),
        compiler_params=pltpu.CompilerParams(
            dimension_semantics=("arbitrary", "arbitrary"), vmem_limit_bytes=_vmem_limit(est)),
        name="moe_expert_up",
    )(tile_expert, n_tiles, a_sorted, w1, w3)


def _expert_down_body(te_ref, nt_ref, h_ref, w2_ref, y_ref, w2bf, *, kc):
    i = pl.program_id(1)
    fe = h_ref.shape[1]
    fresh = jnp.logical_or(i == 0, te_ref[i] != te_ref[jnp.maximum(i - 1, 0)])

    @pl.when(jnp.logical_and(fresh, i < nt_ref[0]))
    def _():
        def cast(c, carry):
            r0 = pl.multiple_of(c * kc, kc)
            w2bf[pl.ds(r0, kc), :] = w2_ref[pl.ds(r0, kc), :].astype(BF16)
            return carry
        lax.fori_loop(0, fe // kc, cast, 0)

    @pl.when(i < nt_ref[0])
    def _():
        y_ref[...] = _dot(h_ref[...], w2bf[...])


def _expert_down(tile_expert, n_tiles, h_sorted, w2, *, tm, tn):
    r_max, fe = h_sorted.shape
    d = w2.shape[2]
    t_max = r_max // tm
    kc = _pick(fe, (512, 256, 128))
    row = lambda n, i, te, nt: jnp.minimum(i, nt[0] - 1)
    est = (2 * _nbytes((tm, fe), BF16) + 2 * _nbytes((fe, tn), F32) + _nbytes((fe, tn), BF16)
           + 4 * _nbytes((tm, tn), F32))
    return pl.pallas_call(
        functools.partial(_expert_down_body, kc=kc),
        out_shape=jax.ShapeDtypeStruct((r_max, d), F32),
        grid_spec=pltpu.PrefetchScalarGridSpec(
            num_scalar_prefetch=2,
            grid=(d // tn, t_max),
            in_specs=[pl.BlockSpec((tm, fe), lambda n, i, te, nt: (row(n, i, te, nt), 0)),
                      pl.BlockSpec((None, fe, tn), lambda n, i, te, nt: (te[i], 0, n))],
            out_specs=pl.BlockSpec((tm, tn), lambda n, i, te, nt: (row(n, i, te, nt), n)),
            scratch_shapes=[pltpu.VMEM((fe, tn), BF16)]),
        compiler_params=pltpu.CompilerParams(
            dimension_semantics=("arbitrary", "arbitrary"), vmem_limit_bytes=_vmem_limit(est)),
        name="moe_expert_down",
    )(tile_expert, n_tiles, h_sorted, w2)


def _combine_body(slot_ref, nxt_ref, x_ref, gate_ref, y_hbm, o_ref, buf, sem, *, tc):
    i = pl.program_id(0)
    n = pl.num_programs(0)
    unroll = GATHER_UNROLL // TOP_K

    def request(ids_ref, slot):
        def issue(c, carry):
            for u in range(unroll):
                r = c * unroll + u
                for k in range(TOP_K):
                    _row_copy(y_hbm, ids_ref[0, 0, TOP_K * r + k], buf.at[slot, k], r,
                              sem.at[slot]).start()
            return carry
        lax.fori_loop(0, tc // unroll, issue, 0)

    @pl.when(i == 0)
    def _():
        request(slot_ref, 0)

    @pl.when(i + 1 < n)
    def _():
        request(nxt_ref, (i + 1) % 2)

    slot = i % 2

    def drain(c, carry):
        for u in range(unroll):
            r = c * unroll + u
            for k in range(TOP_K):
                _row_copy(y_hbm, 0, buf.at[slot, k], r, sem.at[slot]).wait()
        return carry
    lax.fori_loop(0, tc // unroll, drain, 0)
    g = gate_ref[...]
    o_ref[...] = x_ref[...] + (g[:, 0:1] * buf[slot, 0] + g[:, 1:2] * buf[slot, 1])


def _combine(slots, x, gates, y_sorted, *, tc):
    m, d = x.shape
    n = m // tc
    assert tc % (GATHER_UNROLL // TOP_K) == 0
    est = 4 * _nbytes((tc, d), F32) + 2 * TOP_K * _nbytes((tc, d), F32) + 4 * _nbytes((tc, d), F32)
    slot_tiles = slots.reshape(n, 1, TOP_K * tc)
    return pl.pallas_call(
        functools.partial(_combine_body, tc=tc),
        out_shape=jax.ShapeDtypeStruct((m, d), F32),
        grid=(n,),
        in_specs=[pl.BlockSpec((1, 1, TOP_K * tc), lambda i: (i, 0, 0), memory_space=pltpu.SMEM),
                  pl.BlockSpec((1, 1, TOP_K * tc), lambda i: (jnp.minimum(i + 1, n - 1), 0, 0),
                               memory_space=pltpu.SMEM),
                  pl.BlockSpec((tc, d), lambda i: (i, 0)),
                  pl.BlockSpec((tc, LANES), lambda i: (i, 0)),
                  pl.BlockSpec(memory_space=pl.ANY)],
        out_specs=pl.BlockSpec((tc, d), lambda i: (i, 0)),
        scratch_shapes=[pltpu.VMEM((2, TOP_K, tc, d), F32), pltpu.SemaphoreType.DMA((2,))],
        compiler_params=pltpu.CompilerParams(
            dimension_semantics=("arbitrary",), vmem_limit_bytes=_vmem_limit(est)),
        name="moe_combine",
    )(slot_tiles, slot_tiles, x, gates, y_sorted)


def _moe(x, g_ffn, w_router, w_e1, w_e3, w_e2, *, tm_tok):
    m, d = x.shape
    n_exp, _, fe = w_e1.shape
    tm = _pick(m * TOP_K, (512, 256, 128, 64, 32, 16))
    wr = jnp.zeros((d, LANES), F32).at[:, :n_exp].set(w_router)
    packed, idx128, gate128 = _router(x, g_ffn, wr, n_exp=n_exp, tm=tm_tok)

    idx = idx128[:, :TOP_K]
    mask = jnp.sum(idx[:, :, None] == jnp.arange(n_exp, dtype=I32)[None, None, :], axis=1).astype(I32)
    counts = jnp.sum(mask, axis=0)
    padded = ((counts + tm - 1) // tm) * tm
    ends = jnp.cumsum(padded)
    starts = ends - padded
    pos = jnp.cumsum(mask, axis=0) - mask
    slot = starts[idx] + jnp.take_along_axis(pos, idx, axis=1)
    t_max = (m * TOP_K) // tm + n_exp
    r_max = t_max * tm
    tok_sorted = jnp.zeros((r_max,), I32).at[slot.reshape(-1)].set(
        jnp.repeat(jnp.arange(m, dtype=I32), TOP_K))
    n_tiles = (ends[-1] // tm).astype(I32).reshape(1)
    tile_start = jnp.minimum(jnp.arange(t_max, dtype=I32), n_tiles[0] - 1) * tm
    tile_expert = jnp.minimum(jnp.sum(ends[None, :] <= tile_start[:, None], axis=1),
                              n_exp - 1).astype(I32)

    a_sorted = _dispatch(n_tiles, tok_sorted, packed, tm=tm, r_max=r_max)
    tf = _pick(fe, (256, 128))
    h_sorted = _expert_up(tile_expert, n_tiles, a_sorted, w_e1, w_e3, tm=tm, tf=tf)
    y_sorted = _expert_down(tile_expert, n_tiles, h_sorted, w_e2, tm=tm, tn=_pick(d, (512, 256, 128)))
    tc = _pick(m, (256, 128, 64, 32, 16, 8))
    return _combine(slot, x, gate128, y_sorted, tc=tc)


def _block_diag(blocks):
    g, r, c = blocks.shape
    eye = jnp.eye(g, dtype=blocks.dtype)
    return (blocks[:, :, None, :] * eye[:, None, :, None]).reshape(g * r, g * c)


def _rel_table(rel_bias_l, q_pos, k_pos, max_rel):
    rel = np.clip(q_pos[:, None] - k_pos[None, :], -(CHUNK - 1), max_rel) + (CHUNK - 1)
    return rel_bias_l[:, rel].astype(F32)


def _band_table(rel_bias_l, tq, max_rel):
    n_heads = rel_bias_l.shape[0]
    past = BAND_PAST * CHUNK
    win = past + tq
    ring = -(-(tq + win - 1) // LANES) * LANES
    diff = np.arange(ring)
    diff = np.where(diff < tq, diff, diff - ring)
    idx = np.clip(diff + past, -(CHUNK - 1), max_rel) + (CHUNK - 1)
    by_diff = jnp.pad(rel_bias_l.astype(F32)[:, idx], ((0, 16 - n_heads), (0, 0)))

    def body(u_ref, o_ref):
        hp = pl.program_id(0)
        j = lax.broadcasted_iota(I32, (win, tq), 0)
        i = lax.broadcasted_iota(I32, (win, tq), 1)
        gap = (i + past) // CHUNK - j // CHUNK
        valid = (gap >= 0) & (gap <= BAND_PAST)
        for e in range(2):
            rows = jnp.broadcast_to(u_ref[pl.ds(2 * hp + e, 1), :], (win, ring))
            shifted = pltpu.roll(rows, 0, 1, stride=1, stride_axis=0)
            o_ref[:, e * tq:(e + 1) * tq] = jnp.where(valid, shifted[:, 0:tq], NEG_INF)

    return pl.pallas_call(
        body,
        out_shape=jax.ShapeDtypeStruct((n_heads // 2, win, 2 * tq), F32),
        grid=(n_heads // 2,),
        in_specs=[pl.BlockSpec((16, ring), lambda hp: (0, 0))],
        out_specs=pl.BlockSpec((None, win, 2 * tq), lambda hp: (hp, 0, 0)),
        compiler_params=pltpu.CompilerParams(dimension_semantics=("parallel",)),
        name="band_table",
    )(by_diff)


def _tile_gain(g):
    return jnp.tile(g.astype(F32), LANES // g.shape[0]).reshape(1, LANES)


def kernel(x_prompt, x_sample, mem_prompt, cache_fox_k, cache_fox_v, cache_fox_logf, cache_band_k, cache_band_v, state_ssm_re, state_ssm_im, cache_mem_k, cache_mem_v, g_mix, w_in, b_f, g_qa, g_ka, g_qb, g_kb, rel_bias, lam_re, lam_im, log_dt, ssm_b_re, ssm_b_im, ssm_c_re, ssm_c_im, ssm_d, w_glu, b_glu, g_mix_out, w_out, g_cross, g_mem, w_cq, w_ck, w_cv, g_cq, g_ck, w_co, g_ffn, w_ff1, w_ff3, w_ff2, w_router, w_e1, w_e3, w_e2):
    batch, seq, d = x_prompt.shape
    dbatch, dseq, _ = x_sample.shape
    depth = g_mix.shape[0]
    past_len, h_a, dh = cache_fox_k.shape[2:]
    band_rows, h_b = cache_band_k.shape[2:4]
    g_c, p_state = lam_re.shape[1:]
    w_a, w_b, w_c = h_a * dh, h_b * dh, g_c * SSM_GROUP
    n_mem, h_m, dh_m = cache_mem_k.shape[2:]
    w_m = h_m * dh_m
    max_rel = rel_bias.shape[2] - CHUNK
    assert dh == 64 and h_a % 2 == 0 and h_b % 2 == 0 and dh_m == LANES
    assert w_a == w_b and w_c <= w_a and h_a <= 16
    mp, ms = batch * seq, dbatch * dseq
    m = mp + ms
    pa, pb = h_a // 2, h_b // 2
    n_state = g_c * p_state
    n_blk = 2 if (w_c % 512 == 0) else 1
    nband = min(BAND_PAST * CHUNK, seq)
    tm = _pick(m, (1056, 1024, 768, 512, 256, 128, 64, 32, 16))
    tq_band = _pick(seq, (256, 128, 64))

    x = jnp.concatenate([x_prompt.reshape(mp, d), x_sample.reshape(ms, d)], axis=0)
    fox_kt, fox_vt, band_kt, band_vt = (jnp.transpose(c, (0, 1, 3, 4, 2)) for c in
                                        (cache_fox_k, cache_fox_v, cache_band_k, cache_band_v))

    outs = {k: [] for k in ("p_fk", "p_fv", "p_fl", "p_bk", "p_bv", "p_sr", "p_si", "p_mk", "p_mv",
                            "s_fk", "s_fv", "s_fl", "s_bk", "s_bv", "s_sr", "s_si")}
    for l in range(depth):
        sizes = (w_a, w_a, w_a, h_a, w_b, w_b, w_b)
        cuts = [sum(sizes[:i]) for i in range(len(sizes) + 1)]
        w_cat = _regroup_columns(
            w_in, l, ((0, cuts[3]), (cuts[4], cuts[7]), (cuts[7], w_in.shape[2]), (cuts[3], cuts[4])),
            7 * w_a)
        proj = _mm([x], [w_cat], gain=g_mix[l].reshape(1, d), tm=tm, tn=w_a, name="proj_in")
        col_uc = 6 * w_a

        fa_p = proj[:mp, col_uc + w_c:col_uc + w_c + h_a].reshape(batch, seq, h_a)
        fa_p = jnp.transpose(fa_p, (0, 2, 1)).reshape(batch * h_a, seq)
        bias_p = jnp.broadcast_to(jnp.tile(b_f[l], batch)[:, None], (batch * h_a, LANES))
        lf_p, f_p = _logf_cumsum(fa_p, bias_p, raw_from=0, valid_to=seq)
        lf_p = lf_p.reshape(batch, h_a, seq)
        f_p = f_p.reshape(batch, h_a, seq)
        f_t = jnp.transpose(f_p, (0, 2, 1)).reshape(batch, seq, pa, 2, 1)
        f_aug = jnp.tile(jnp.broadcast_to(f_t, (batch, seq, pa, 2, 3)).reshape(batch, seq, pa, 6),
                         (1, 1, 1, 2))
        f_aug = jnp.pad(f_aug, ((0, 0), (0, 0), (0, 0), (0, LANES - 12))).reshape(mp, pa * LANES)

        t_all = -(-(past_len + dseq) // LANES) * LANES
        fa_s = proj[mp:, col_uc + w_c:col_uc + w_c + h_a].reshape(dbatch, dseq, h_a)
        x_s = jnp.concatenate([jnp.transpose(cache_fox_logf[l], (0, 2, 1)),
                               jnp.transpose(fa_s, (0, 2, 1)),
                               jnp.zeros((dbatch, h_a, t_all - past_len - dseq), F32)], axis=2)
        bias_s = jnp.broadcast_to(jnp.tile(b_f[l], dbatch)[:, None], (dbatch * h_a, LANES))
        lf_s, f_s = _logf_cumsum(x_s.reshape(dbatch * h_a, t_all), bias_s,
                                 raw_from=past_len, valid_to=past_len + dseq)
        lf_s = lf_s.reshape(dbatch, h_a, t_all)[:, :, past_len:past_len + dseq]
        f_s = f_s.reshape(dbatch, h_a, t_all)
        fs_row = jnp.pad(f_s, ((0, 0), (0, 16 - h_a), (0, 0)))
        fs_col = jnp.pad(jnp.transpose(f_s[:, :, past_len:past_len + dseq], (0, 2, 1)),
                         ((0, 0), (0, 0), (0, LANES - h_a))).reshape(ms, LANES)

        gqa, gka = _tile_gain(g_qa[l]), _tile_gain(g_ka[l])
        oa, knt_a, vt_a = _fox_prompt(proj, f_aug, gqa, gka, batch=batch, seq=seq, n_pairs=pa,
                                      col_q=0, col_k=pa, col_v=2 * pa, m_rows=m)
        oa, kn_as = _sample_attn(proj, fox_kt, fox_vt, l, fs_col, fs_row,
                                 gqa, gka, oa, mode="fox", batch=dbatch, s_new=dseq, n_pairs=pa,
                                 col_q=0, row0=mp)

        gqb, gkb = _tile_gain(g_qb[l]), _tile_gain(g_kb[l])
        tab = _band_table(rel_bias[l], tq_band, max_rel)
        ob, knt_b, vt_b = _band_prompt(proj, tab, gqb, gkb, batch=batch, seq=seq, n_pairs=pb,
                                       col_q=3 * pa, col_k=3 * pa + pb, col_v=3 * pa + 2 * pb,
                                       tq=tq_band, m_rows=m)
        tab_s = _rel_table(rel_bias[l], band_rows + np.arange(dseq), np.arange(band_rows + dseq),
                           max_rel)
        ob, kn_bs = _sample_attn(proj, band_kt, band_vt, l,
                                 tab_s[:, :, :band_rows], tab_s[:, :, band_rows:], gqb, gkb, ob,
                                 mode="band", batch=dbatch, s_new=dseq, n_pairs=pb, col_q=3, row0=mp)

        rep = lambda a: jnp.repeat(a.astype(F32), SSM_GROUP, axis=0)
        a_re, a_im, bb_re, bb_im = _s5_prep(
            rep(lam_re[l]), rep(lam_im[l]),
            jnp.broadcast_to(rep(log_dt[l])[:, None], (g_c * SSM_GROUP, p_state)),
            jnp.transpose(ssm_b_re[l], (0, 2, 1)).reshape(g_c * SSM_GROUP, p_state),
            jnp.transpose(ssm_b_im[l], (0, 2, 1)).reshape(g_c * SSM_GROUP, p_state))
        a_re = a_re.reshape(g_c, SSM_GROUP, p_state)[:, 0, :].reshape(1, n_state)
        a_im = a_im.reshape(g_c, SSM_GROUP, p_state)[:, 0, :].reshape(1, n_state)
        b_re_d = _block_diag(bb_re.reshape(g_c, SSM_GROUP, p_state))
        b_im_d = _block_diag(bb_im.reshape(g_c, SSM_GROUP, p_state))
        c_re_d = _block_diag(jnp.transpose(ssm_c_re[l], (0, 2, 1)))
        c_im_d = _block_diag(jnp.transpose(ssm_c_im[l], (0, 2, 1)))
        d_row = ssm_d[l].reshape(1, w_c)
        s5_args = (a_re, a_im, b_re_d, b_im_d, c_re_d, c_im_d, d_row, w_glu[l],
                   b_glu[l].reshape(1, w_c))
        uc = proj[:, col_uc:col_uc + w_c]
        u_p = jnp.transpose(uc[:mp].reshape(batch, seq, w_c), (1, 0, 2)).reshape(mp, w_c)
        u_s = jnp.transpose(uc[mp:].reshape(dbatch, dseq, w_c), (1, 0, 2)).reshape(ms, w_c)
        zeros_p = jnp.zeros((batch, n_state), F32)
        oc_p, sr_p, si_p = _s5(u_p, zeros_p, zeros_p, *s5_args, nb=batch, seq=seq, n_blk=n_blk,
                               passes=1)
        oc_s, sr_s, si_s = _s5(u_s, state_ssm_re[l].reshape(dbatch, n_state),
                               state_ssm_im[l].reshape(dbatch, n_state), *s5_args,
                               nb=dbatch, seq=dseq, n_blk=n_blk, passes=3)
        oc = jnp.concatenate(
            [jnp.transpose(oc_p.reshape(seq, batch, w_c), (1, 0, 2)).reshape(mp, w_c),
             jnp.transpose(oc_s.reshape(dseq, dbatch, w_c), (1, 0, 2)).reshape(ms, w_c)], axis=0)

        x = _mm([oa, ob, oc], [w_out[l]], gain=g_mix_out[l].reshape(1, -1), residual=x,
                tm=tm, tn=_pick(d, (512, 256, 128)), name="merge_out")

        per_head = lambda t, h: jnp.transpose(t.reshape(batch, h, dh, -1), (0, 3, 1, 2))
        outs["p_fk"].append(per_head(knt_a, h_a))
        outs["p_fv"].append(per_head(vt_a, h_a))
        outs["p_fl"].append(jnp.transpose(lf_p, (0, 2, 1)))
        outs["p_bk"].append(per_head(knt_b[:, :, seq - nband:], h_b))
        outs["p_bv"].append(per_head(vt_b[:, :, seq - nband:], h_b))
        outs["p_sr"].append(sr_p.reshape(batch, g_c, p_state))
        outs["p_si"].append(si_p.reshape(batch, g_c, p_state))
        outs["s_fk"].append(kn_as.reshape(dbatch, dseq, h_a, dh))
        outs["s_fv"].append(proj[mp:, 2 * w_a:3 * w_a].reshape(dbatch, dseq, h_a, dh))
        outs["s_fl"].append(jnp.transpose(lf_s, (0, 2, 1)))
        outs["s_bk"].append(kn_bs.reshape(dbatch, dseq, h_b, dh))
        outs["s_bv"].append(proj[mp:, 5 * w_a:6 * w_a].reshape(dbatch, dseq, h_b, dh))
        outs["s_sr"].append(sr_s.reshape(dbatch, g_c, p_state))
        outs["s_si"].append(si_s.reshape(dbatch, g_c, p_state))

        mem2 = mem_prompt.reshape(batch * n_mem, d)
        tmm = _pick(batch * n_mem, (1024, 512, 256, 128))
        gck = jnp.tile(g_ck[l].astype(F32), h_m).reshape(1, w_m)
        gcq = jnp.tile(g_cq[l].astype(F32), h_m).reshape(1, w_m)
        mk = _mm([mem2], [w_ck[l]], gain=g_mem[l].reshape(1, d), group_gain=gck,
                 epilogue="group_norm", tm=tmm, tn=w_m, name="mem_k")
        mv = _mm([mem2], [w_cv[l]], gain=g_mem[l].reshape(1, d), tm=tmm, tn=w_m, name="mem_v")
        outs["p_mk"].append(mk.reshape(batch, n_mem, h_m, dh_m))
        outs["p_mv"].append(mv.reshape(batch, n_mem, h_m, dh_m))
        q_c = _mm([x], [w_cq[l]], gain=g_cross[l].reshape(1, d), group_gain=gcq,
                  epilogue="group_norm", tm=tm, tn=w_m, name="cross_q")
        o_c = _cross_attn(q_c, mk, mv, None, batch=batch, q_len=seq, n_mem=n_mem, n_heads=h_m,
                          dh=dh_m, row0=0, m_rows=m)
        o_c = _cross_attn(q_c, cache_mem_k[l].reshape(dbatch * n_mem, w_m),
                          cache_mem_v[l].reshape(dbatch * n_mem, w_m), o_c, batch=dbatch,
                          q_len=dseq, n_mem=n_mem, n_heads=h_m, dh=dh_m, row0=mp, m_rows=m)
        x = _mm([o_c], [w_co[l]], residual=x, tm=tm, tn=_pick(d, (512, 256, 128)), name="cross_out")

        i = l // 2
        if l % 2 == 0:
            hmid = _mm([x], [w_ff1[i], w_ff3[i]], gain=g_ffn[l].reshape(1, d), epilogue="swiglu",
                       out_dtype=BF16, tm=tm, tn=_pick(w_ff1.shape[2], (512, 256, 128)),
                       name="ffn_up")
            x = _mm([hmid], [w_ff2[i]], residual=x, tm=tm, tn=_pick(d, (256, 128)), name="ffn_down")
        else:
            x = _moe(x, g_ffn[l].reshape(1, d), w_router[i], w_e1[i], w_e3[i], w_e2[i], tm_tok=tm)

    st = lambda k: jnp.stack(outs[k])
    return (x[:mp].reshape(batch, seq, d), x[mp:].reshape(dbatch, dseq, d),
            st("p_fk"), st("p_fv"), st("p_fl"), st("p_bk"), st("p_bv"), st("p_sr"), st("p_si"),
            st("p_mk"), st("p_mv"), st("s_fk"), st("s_fv"), st("s_fl"), st("s_bk"), st("s_bv"),
            st("s_sr"), st("s_si"))
```

```python
import functools
import math

import jax
import jax.numpy as jnp
import numpy as np
from jax import lax
from jax.experimental import pallas as pl
from jax.experimental.pallas import tpu as pltpu

F32 = jnp.float32
BF16 = jnp.bfloat16
I32 = jnp.int32
U32 = jnp.uint32

EPS = 1e-6
NEG_INF = -1e30
CHUNK = 64
BAND_PAST = 8
SSM_GROUP = 16
TOP_K = 2

LANES = 128
SUBLANES = 8
BF16_ROWS = 16
VMEM_CAP = 60 * 1024 * 1024


def _vmem_limit(nbytes):
    return int(min(VMEM_CAP, max(16 * 1024 * 1024, nbytes * 5 // 4 + (4 << 20))))


def _pick(n, candidates):
    for c in candidates:
        if c <= n and n % c == 0:
            return c
    raise ValueError(f"no tile for {n} in {candidates}")


def _nbytes(shape, dtype):
    return math.prod(shape) * jnp.dtype(dtype).itemsize


def _split3(x):
    hi = x.astype(BF16)
    r1 = x - hi.astype(F32)
    mid = r1.astype(BF16)
    lo = (r1 - mid.astype(F32)).astype(BF16)
    return hi, mid, lo


def _dot(a, b):
    return jnp.dot(a, b, preferred_element_type=F32)


def _dot_nt(a, b):
    return lax.dot_general(a, b, (((1,), (1,)), ((), ())), preferred_element_type=F32)


def _dot_hp(a, b, passes=3):
    ah = a.astype(BF16)
    bh = b.astype(BF16)
    if passes == 1:
        return _dot(ah, bh)
    al = (a - ah.astype(F32)).astype(BF16)
    bl = (b - bh.astype(F32)).astype(BF16)
    return _dot(ah, bh) + (_dot(ah, bl) + _dot(al, bh))


def _pair_ones():
    r = lax.broadcasted_iota(I32, (LANES, LANES), 0) // 64
    c = lax.broadcasted_iota(I32, (LANES, LANES), 1) // 64
    return (r == c).astype(BF16)


def _pair_norm(x, g, ones):
    sq = x * x
    hi = sq.astype(BF16)
    lo = (sq - hi.astype(F32)).astype(BF16)
    ss = _dot(hi, ones) + _dot(lo, ones)
    return x * lax.rsqrt(ss * (1.0 / 64.0) + EPS) * g


def _log_sigmoid(x):
    return jnp.minimum(x, 0.0) - jnp.log(1.0 + jnp.exp(-jnp.abs(x)))


def _gelu_tanh(x):
    c = math.sqrt(2.0 / math.pi)
    return 0.5 * x * (1.0 + jnp.tanh(c * (x + 0.044715 * (x * x * x))))


def _mm_body(*refs, widths, norm, n_w, epilogue, has_res, has_gg, stage_a, tm, rc, kc, k_total):
    it = iter(refs)
    a_refs = [next(it) for _ in widths]
    g_ref = next(it) if norm else None
    w_refs = [next(it) for _ in range(n_w)]
    gg_ref = next(it) if has_gg else None
    res_ref = next(it) if has_res else None
    o_ref = next(it)
    abf = next(it) if stage_a else a_refs[0]
    cast_w = w_refs[0].dtype != BF16
    wbfs = [next(it) for _ in range(n_w)] if cast_w else w_refs
    j = pl.program_id(1)

    if stage_a:
        @pl.when(j == 0)
        def _():
            off = 0
            for a_ref, wd in zip(a_refs, widths):
                def chunk(c, carry, a_ref=a_ref, off=off, wd=wd):
                    r0 = pl.multiple_of(c * rc, rc)
                    x = a_ref[pl.ds(r0, rc), :].astype(F32)
                    if norm:
                        ms = jnp.mean(x * x, axis=-1, keepdims=True)
                        x = x * lax.rsqrt(ms + EPS) * g_ref[:, off:off + wd]
                    abf[pl.ds(r0, rc), off:off + wd] = x.astype(BF16)
                    return carry
                lax.fori_loop(0, tm // rc, chunk, 0)
                off += wd

    if cast_w:
        for w_ref, wbf in zip(w_refs, wbfs):
            def cast(c, carry, w_ref=w_ref, wbf=wbf):
                r0 = pl.multiple_of(c * kc, kc)
                wbf[pl.ds(r0, kc), :] = w_ref[pl.ds(r0, kc), :].astype(BF16)
                return carry
            lax.fori_loop(0, k_total // kc, cast, 0)

    def rows(c, carry):
        r0 = pl.multiple_of(c * rc, rc)
        a = abf[pl.ds(r0, rc), :]
        ys = [_dot(a, wbf[...]) for wbf in wbfs]
        if epilogue == "swiglu":
            y = jax.nn.silu(ys[0]) * ys[1]
        elif epilogue == "group_norm":
            parts = []
            for s in range(ys[0].shape[1] // LANES):
                ysl = ys[0][:, s * LANES:(s + 1) * LANES]
                ms = jnp.mean(ysl * ysl, axis=-1, keepdims=True)
                parts.append(ysl * lax.rsqrt(ms + EPS) * gg_ref[:, s * LANES:(s + 1) * LANES])
            y = jnp.concatenate(parts, axis=1)
        else:
            y = ys[0]
        if has_res:
            y = y + res_ref[pl.ds(r0, rc), :]
        o_ref[pl.ds(r0, rc), :] = y.astype(o_ref.dtype)
        return carry
    lax.fori_loop(0, tm // rc, rows, 0)


def _mm(a_parts, w_list, *, gain=None, group_gain=None, residual=None, epilogue="none",
        out_dtype=F32, tm, tn, name):
    m = a_parts[0].shape[0]
    widths = tuple(a.shape[1] for a in a_parts)
    k_total = sum(widths)
    n = w_list[0].shape[1]
    assert m % tm == 0 and n % tn == 0, (m, tm, n, tn)
    norm = gain is not None
    stage_a = norm or len(a_parts) > 1 or a_parts[0].dtype != BF16
    rc = _pick(tm, (512, 384, 352, 256, 176, 128, 64, 32, 16))
    kc = _pick(k_total, (512, 256, 128))
    grid = (m // tm, n // tn)
    in_specs = [pl.BlockSpec((tm, wd), lambda i, j: (i, 0)) for wd in widths]
    args = list(a_parts)
    est = sum(2 * _nbytes((tm, wd), a.dtype) for wd, a in zip(widths, a_parts))
    if norm:
        in_specs.append(pl.BlockSpec((1, k_total), lambda i, j: (0, 0)))
        args.append(gain)
    cast_w = w_list[0].dtype != BF16
    for w in w_list:
        in_specs.append(pl.BlockSpec((k_total, tn), lambda i, j: (0, j)))
        args.append(w)
        est += 2 * _nbytes((k_total, tn), w.dtype) + cast_w * _nbytes((k_total, tn), BF16)
    if group_gain is not None:
        in_specs.append(pl.BlockSpec((1, tn), lambda i, j: (0, j)))
        args.append(group_gain)
    if residual is not None:
        in_specs.append(pl.BlockSpec((tm, tn), lambda i, j: (i, j)))
        args.append(residual)
        est += 2 * _nbytes((tm, tn), F32)
    est += 2 * _nbytes((tm, tn), out_dtype) + stage_a * _nbytes((tm, k_total), BF16)
    est += 4 * _nbytes((rc, tn), F32) * len(w_list)
    body = functools.partial(
        _mm_body, widths=widths, norm=norm, n_w=len(w_list), epilogue=epilogue,
        has_res=residual is not None, has_gg=group_gain is not None, stage_a=stage_a, tm=tm,
        rc=rc, kc=kc, k_total=k_total)
    return pl.pallas_call(
        body,
        out_shape=jax.ShapeDtypeStruct((m, n), out_dtype),
        grid=grid,
        in_specs=in_specs,
        out_specs=pl.BlockSpec((tm, tn), lambda i, j: (i, j)),
        scratch_shapes=[pltpu.VMEM((tm, k_total), BF16)] * stage_a
        + [pltpu.VMEM((k_total, tn), BF16) for _ in w_list] * cast_w,
        compiler_params=pltpu.CompilerParams(
            dimension_semantics=("parallel", "arbitrary"),
            vmem_limit_bytes=_vmem_limit(est)),
        name=name,
    )(*args)


def _regroup_body(w_ref, o_ref, *, cuts, width):
    off = 0
    for lo, hi in cuts:
        o_ref[:, off:off + hi - lo] = w_ref[:, lo:hi].astype(o_ref.dtype)
        off += hi - lo
    if off < width:
        o_ref[:, off:width] = jnp.zeros((o_ref.shape[0], width - off), o_ref.dtype)


def _regroup_columns(w, layer, cuts, width):
    _, k, n = w.shape
    tr = _pick(k, (256, 128, 64, 32, 16))
    return pl.pallas_call(
        functools.partial(_regroup_body, cuts=cuts, width=width),
        out_shape=jax.ShapeDtypeStruct((k, width), BF16),
        grid=(k // tr,),
        in_specs=[pl.BlockSpec((None, tr, n), lambda i: (layer, i, 0))],
        out_specs=pl.BlockSpec((tr, width), lambda i: (i, 0)),
        compiler_params=pltpu.CompilerParams(
            dimension_semantics=("parallel",),
            vmem_limit_bytes=_vmem_limit(2 * _nbytes((tr, n), F32) + 2 * _nbytes((tr, width), BF16)
                                         + 4 * _nbytes((tr, width), F32))),
        name="regroup_w_in",
    )(w)


def _cumsum_body(x_ref, b_ref, lf_ref, f_ref, carry, *, raw_from, valid_to, tt):
    j = pl.program_id(0)

    @pl.when(j == 0)
    def _():
        carry[...] = jnp.zeros_like(carry)

    x = x_ref[...]
    lane = j * tt + lax.broadcasted_iota(I32, x.shape, 1)
    lf = jnp.where(lane >= raw_from, _log_sigmoid(x + b_ref[:, 0:1]), x)
    lf = jnp.where(lane < valid_to, lf, 0.0)
    lf_ref[...] = lf
    tri = (lax.broadcasted_iota(I32, (tt, tt), 0)
           <= lax.broadcasted_iota(I32, (tt, tt), 1)).astype(BF16)
    hi, mid, lo = _split3(lf)
    y = _dot(hi, tri) + _dot(mid, tri) + _dot(lo, tri) + carry[:, 0:1]
    f_ref[...] = y
    carry[...] = jnp.broadcast_to(y[:, tt - 1:tt], carry.shape)


def _logf_cumsum(x, bias, *, raw_from, valid_to):
    r, t = x.shape
    tt = _pick(t, (256, 128))
    return pl.pallas_call(
        functools.partial(_cumsum_body, raw_from=raw_from, valid_to=valid_to, tt=tt),
        out_shape=(jax.ShapeDtypeStruct((r, t), F32), jax.ShapeDtypeStruct((r, t), F32)),
        grid=(t // tt,),
        in_specs=[pl.BlockSpec((r, tt), lambda j: (0, j)),
                  pl.BlockSpec((r, LANES), lambda j: (0, 0))],
        out_specs=(pl.BlockSpec((r, tt), lambda j: (0, j)),
                   pl.BlockSpec((r, tt), lambda j: (0, j))),
        scratch_shapes=[pltpu.VMEM((r, LANES), F32)],
        compiler_params=pltpu.CompilerParams(dimension_semantics=("arbitrary",)),
        name="logf_cumsum",
    )(x, bias)


def _stack_heads(qn):
    lane = lax.broadcasted_iota(I32, qn.shape, 1)
    q0 = jnp.where(lane < 64, qn, 0.0)
    q1 = jnp.where(lane < 64, 0.0, qn)
    return jnp.concatenate([q0, q1], axis=0).astype(BF16)


def _unstack_heads(o, tq):
    lane = lax.broadcasted_iota(I32, (tq, LANES), 1)
    return jnp.where(lane < 64, o[:tq], o[tq:])


def _lane_column(block, h):
    lane = lax.broadcasted_iota(I32, block.shape, 1)
    return jnp.sum(jnp.where(lane == h, block, 0.0), axis=-1, keepdims=True)


V_ROWS = LANES + BF16_ROWS


def _split_select(x):
    hi, mid, lo = _split3(x)
    m3 = lax.broadcasted_iota(I32, x.shape, 1) % 3
    return jnp.where(m3 == 0, hi.astype(F32), jnp.where(m3 == 1, mid.astype(F32), lo.astype(F32)))


def _head_masks(shape):
    lane = lax.broadcasted_iota(I32, shape, 1)
    return lane, lane < 64


def _finish_heads(acc, tq):
    o0 = acc[0:64, 0:tq] / acc[LANES:LANES + 1, 0:tq]
    o1 = acc[64:LANES, tq:2 * tq] / acc[LANES:LANES + 1, tq:2 * tq]
    return jnp.concatenate([o0, o1], axis=0).T


def _fox_prompt_body(q_ref, k_ref, v_ref, f_ref, gq_ref, gk_ref,
                     o_ref, knt_ref, vt_ref, kaug, vt3, *, tq, seq, dh):
    ones = _pair_ones()
    nq = seq // tq
    lane, first = _head_masks((tq, LANES))

    for c in range(nq):
        rows = slice(c * tq, (c + 1) * tq)
        kn = _pair_norm(k_ref[rows, :], gk_ref[...], ones)
        knt_ref[:, rows] = kn.T
        kaug[rows, 0:LANES] = kn.astype(BF16)
        sp = _split_select(f_ref[rows, :])
        kaug[rows, LANES:2 * LANES] = jnp.where(
            lane < 6, -sp, jnp.where(lane < 12, 1.0, 0.0)).astype(BF16)
        vt = v_ref[rows, :].T
        vt_ref[:, rows] = vt
        vt3[c, 0:LANES, :] = vt.astype(BF16)
        vt3[c, LANES:V_ROWS, :] = jnp.ones((BF16_ROWS, tq), BF16)

    causal = (lax.broadcasted_iota(I32, (tq, 2 * tq), 0)
              <= lax.broadcasted_iota(I32, (tq, 2 * tq), 1) % tq)

    for qi in range(nq):
        rows = slice(qi * tq, (qi + 1) * tq)
        qn = _pair_norm(q_ref[rows, :], gq_ref[...], ones) * (dh ** -0.5)
        sp = _split_select(f_ref[rows, :])
        up0 = jnp.where(lane < 3, 1.0, jnp.where((lane >= 6) & (lane < 9), sp, 0.0))
        up1 = jnp.where((lane >= 3) & (lane < 6), 1.0,
                        jnp.where((lane >= 9) & (lane < 12), sp, 0.0))
        qs = jnp.concatenate(
            [jnp.concatenate([jnp.where(first, qn, 0.0), up0], axis=1),
             jnp.concatenate([jnp.where(first, 0.0, qn), up1], axis=1)], axis=0).astype(BF16)

        s = jnp.where(causal, _dot_nt(kaug[rows, :], qs), NEG_INF)
        m = jnp.max(s, axis=0, keepdims=True)
        acc = _dot(vt3[qi], jnp.exp(s - m).astype(BF16))
        for kj in range(qi):
            s = _dot_nt(kaug[kj * tq:(kj + 1) * tq, :], qs)
            m_new = jnp.maximum(m, jnp.max(s, axis=0, keepdims=True))
            acc = jnp.exp(m - m_new) * acc + _dot(vt3[kj], jnp.exp(s - m_new).astype(BF16))
            m = m_new
        o_ref[rows, :] = _finish_heads(acc, tq)


def _fox_prompt(proj, f_aug, gq, gk, *, batch, seq, n_pairs, col_q, col_k, col_v, m_rows):
    tq = _pick(seq, (512, 256, 128))
    nq = seq // tq
    width = n_pairs * LANES
    est = (2 * 7 * _nbytes((seq, LANES), F32)
           + _nbytes((seq, 2 * LANES), BF16) + _nbytes((V_ROWS, seq), BF16)
           + 16 * _nbytes((tq, 2 * tq), F32))
    return pl.pallas_call(
        functools.partial(_fox_prompt_body, tq=tq, seq=seq, dh=64),
        out_shape=(jax.ShapeDtypeStruct((m_rows, width), F32),
                   jax.ShapeDtypeStruct((batch, width, seq), F32),
                   jax.ShapeDtypeStruct((batch, width, seq), F32)),
        grid=(batch, n_pairs),
        in_specs=[
            pl.BlockSpec((seq, LANES), lambda b, hp: (b, col_q + hp)),
            pl.BlockSpec((seq, LANES), lambda b, hp: (b, col_k + hp)),
            pl.BlockSpec((seq, LANES), lambda b, hp: (b, col_v + hp)),
            pl.BlockSpec((seq, LANES), lambda b, hp: (b, hp)),
            pl.BlockSpec((1, LANES), lambda b, hp: (0, 0)),
            pl.BlockSpec((1, LANES), lambda b, hp: (0, 0)),
        ],
        out_specs=(pl.BlockSpec((seq, LANES), lambda b, hp: (b, hp)),
                   pl.BlockSpec((None, LANES, seq), lambda b, hp: (b, hp, 0)),
                   pl.BlockSpec((None, LANES, seq), lambda b, hp: (b, hp, 0))),
        scratch_shapes=[pltpu.VMEM((seq, 2 * LANES), BF16), pltpu.VMEM((nq, V_ROWS, tq), BF16)],
        compiler_params=pltpu.CompilerParams(
            dimension_semantics=("parallel", "parallel"),
            vmem_limit_bytes=_vmem_limit(est)),
        name="fox_prompt",
    )(proj, proj, proj, f_aug, gq, gk)


def _band_prompt_body(q_ref, k_ref, v_ref, tab_ref, gq_ref, gk_ref,
                      o_ref, knt_ref, vt_ref, kpad, vt3, *, tq, seq, past, dh):
    ones = _pair_ones()
    nq = seq // tq
    npad = past // tq
    win = past + tq

    kpad[0:past, :] = jnp.zeros((past, LANES), BF16)
    for c in range(npad):
        vt3[c] = jnp.zeros((V_ROWS, tq), BF16)
    for c in range(nq):
        rows = slice(c * tq, (c + 1) * tq)
        kn = _pair_norm(k_ref[rows, :], gk_ref[...], ones)
        knt_ref[:, rows] = kn.T
        kpad[past + c * tq:past + (c + 1) * tq, :] = kn.astype(BF16)
        vt = v_ref[rows, :].T
        vt_ref[:, rows] = vt
        vt3[npad + c, 0:LANES, :] = vt.astype(BF16)
        vt3[npad + c, LANES:V_ROWS, :] = jnp.ones((BF16_ROWS, tq), BF16)

    row = lax.broadcasted_iota(I32, (win, 2 * tq), 0)

    for qi in range(nq):
        r0 = qi * tq
        qn = _pair_norm(q_ref[r0:r0 + tq, :], gq_ref[...], ones) * (dh ** -0.5)
        s = _dot_nt(kpad[r0:r0 + win, :], _stack_heads(qn)) + tab_ref[...]
        if r0 < past:
            s = jnp.where(row >= past - r0, s, NEG_INF)
        m = jnp.max(s, axis=0, keepdims=True)
        p = jnp.exp(s - m).astype(BF16)
        acc = _dot(vt3[qi], p[0:tq, :])
        for c in range(1, win // tq):
            acc = acc + _dot(vt3[qi + c], p[c * tq:(c + 1) * tq, :])
        o_ref[r0:r0 + tq, :] = _finish_heads(acc, tq)


def _band_prompt(proj, tab, gq, gk, *, batch, seq, n_pairs, col_q, col_k, col_v, tq, m_rows):
    past = BAND_PAST * CHUNK
    assert past % tq == 0 and seq % tq == 0
    nq = seq // tq
    width = n_pairs * LANES
    win = past + tq
    est = (2 * 5 * _nbytes((seq, LANES), F32)
           + _nbytes((seq + past, LANES), BF16) + _nbytes((V_ROWS, seq + past), BF16)
           + 2 * _nbytes((win, 2 * tq), F32) + 8 * _nbytes((win, 2 * tq), F32))
    return pl.pallas_call(
        functools.partial(_band_prompt_body, tq=tq, seq=seq, past=past, dh=64),
        out_shape=(jax.ShapeDtypeStruct((m_rows, width), F32),
                   jax.ShapeDtypeStruct((batch, width, seq), F32),
                   jax.ShapeDtypeStruct((batch, width, seq), F32)),
        grid=(n_pairs, batch),
        in_specs=[
            pl.BlockSpec((seq, LANES), lambda hp, b: (b, col_q + hp)),
            pl.BlockSpec((seq, LANES), lambda hp, b: (b, col_k + hp)),
            pl.BlockSpec((seq, LANES), lambda hp, b: (b, col_v + hp)),
            pl.BlockSpec((None, win, 2 * tq), lambda hp, b: (hp, 0, 0)),
            pl.BlockSpec((1, LANES), lambda hp, b: (0, 0)),
            pl.BlockSpec((1, LANES), lambda hp, b: (0, 0)),
        ],
        out_specs=(pl.BlockSpec((seq, LANES), lambda hp, b: (b, hp)),
                   pl.BlockSpec((None, LANES, seq), lambda hp, b: (b, hp, 0)),
                   pl.BlockSpec((None, LANES, seq), lambda hp, b: (b, hp, 0))),
        scratch_shapes=[pltpu.VMEM((seq + past, LANES), BF16),
                        pltpu.VMEM(((seq + past) // tq, V_ROWS, tq), BF16)],
        compiler_params=pltpu.CompilerParams(
            dimension_semantics=("parallel", "parallel"),
            vmem_limit_bytes=_vmem_limit(est)),
        name="band_prompt",
    )(proj, proj, proj, tab, gq, gk)


def _sample_attn_body(*refs, mode, n_pairs, s_new, n_cache, dh):
    if mode == "fox":
        (q_ref, k_ref, v_ref, ck_ref, cv_ref, fq_ref, ft_ref, gq_ref, gk_ref, _alias,
         o_ref, kn_ref) = refs
    else:
        (q_ref, k_ref, v_ref, ck_ref, cv_ref, tabc_ref, tabn_ref, gq_ref, gk_ref, _alias,
         o_ref, kn_ref) = refs
    ones = _pair_ones()
    row = lax.broadcasted_iota(I32, (s_new, s_new), 0)
    col = lax.broadcasted_iota(I32, (s_new, s_new), 1)
    for hp in range(n_pairs):
        sl = slice(hp * LANES, (hp + 1) * LANES)
        qn = _pair_norm(q_ref[:, sl], gq_ref[...], ones) * (dh ** -0.5)
        kn = _pair_norm(k_ref[:, sl], gk_ref[...], ones)
        kn_ref[:, sl] = kn
        vn = v_ref[:, sl]
        outs = []
        for e in range(2):
            h = 2 * hp + e
            hl = slice(e * dh, (e + 1) * dh)
            qh = qn[:, hl].astype(BF16)
            sc = _dot(qh, ck_ref[h].astype(BF16))
            sn = _dot_nt(qh, kn[:, hl].astype(BF16))
            if mode == "fox":
                fq = _lane_column(fq_ref[...], h)
                fk = ft_ref[h:h + 1, :]
                sc = sc + fq - fk[:, :n_cache]
                sn = jnp.where(col <= row, sn + fq - fk[:, n_cache:n_cache + s_new], NEG_INF)
            else:
                sc = sc + tabc_ref[h]
                sn = sn + tabn_ref[h]
            m = jnp.maximum(jnp.max(sc, axis=-1, keepdims=True),
                            jnp.max(sn, axis=-1, keepdims=True))
            pc = jnp.exp(sc - m)
            pn = jnp.exp(sn - m)
            l = jnp.sum(pc, axis=-1, keepdims=True) + jnp.sum(pn, axis=-1, keepdims=True)
            outs.append((_dot_nt(pc.astype(BF16), cv_ref[h].astype(BF16))
                         + _dot(pn.astype(BF16), vn[:, hl].astype(BF16))) / l)
        o_ref[:, sl] = jnp.concatenate(outs, axis=1)


def _sample_attn(proj, cache_kt, cache_vt, layer, extra_a, extra_b, gq, gk, o_buf, *, mode, batch,
                 s_new, n_pairs, col_q, row0):
    width = n_pairs * LANES
    _, _, n_heads, dh, n_cache = cache_kt.shape
    rb = row0 // s_new
    if mode == "fox":
        ex_specs = [pl.BlockSpec((s_new, LANES), lambda b: (b, 0)),
                    pl.BlockSpec((None, 16, extra_b.shape[2]), lambda b: (b, 0, 0))]
    else:
        ex_specs = [pl.BlockSpec(extra_a.shape, lambda b: (0, 0, 0)),
                    pl.BlockSpec(extra_b.shape, lambda b: (0, 0, 0))]
    est = (2 * 2 * _nbytes((n_cache, width), F32) + 8 * _nbytes((s_new, width), F32)
           + 2 * _nbytes(extra_a.shape, F32) + 16 * _nbytes((2 * s_new, n_cache), F32)
           + 4 * _nbytes((n_cache, LANES), BF16))
    return pl.pallas_call(
        functools.partial(_sample_attn_body, mode=mode, n_pairs=n_pairs, s_new=s_new,
                          n_cache=n_cache, dh=dh),
        out_shape=(jax.ShapeDtypeStruct(o_buf.shape, F32),
                   jax.ShapeDtypeStruct((batch * s_new, width), F32)),
        grid=(batch,),
        in_specs=[
            pl.BlockSpec((s_new, width), lambda b: (rb + b, col_q)),
            pl.BlockSpec((s_new, width), lambda b: (rb + b, col_q + 1)),
            pl.BlockSpec((s_new, width), lambda b: (rb + b, col_q + 2)),
            pl.BlockSpec((None, None, n_heads, dh, n_cache), lambda b: (layer, b, 0, 0, 0)),
            pl.BlockSpec((None, None, n_heads, dh, n_cache), lambda b: (layer, b, 0, 0, 0)),
            *ex_specs,
            pl.BlockSpec((1, LANES), lambda b: (0, 0)),
            pl.BlockSpec((1, LANES), lambda b: (0, 0)),
            pl.BlockSpec(memory_space=pl.ANY),
        ],
        out_specs=(pl.BlockSpec((s_new, width), lambda b: (rb + b, 0)),
                   pl.BlockSpec((s_new, width), lambda b: (b, 0))),
        input_output_aliases={9: 0},
        compiler_params=pltpu.CompilerParams(
            dimension_semantics=("parallel",), vmem_limit_bytes=_vmem_limit(est)),
        name=f"{mode}_sample",
    )(proj, proj, proj, cache_kt, cache_vt, extra_a, extra_b, gq, gk, o_buf)


def _cross_body(*refs, n_heads, dh, aliased):
    if aliased:
        q_ref, k_ref, v_ref, _alias, o_ref = refs
    else:
        q_ref, k_ref, v_ref, o_ref = refs
    for h in range(n_heads):
        sl = slice(h * dh, (h + 1) * dh)
        q = (q_ref[:, sl] * (dh ** -0.5)).astype(BF16)
        s = _dot_nt(q, k_ref[:, sl].astype(BF16))
        m = jnp.max(s, axis=-1, keepdims=True)
        p = jnp.exp(s - m)
        l = jnp.sum(p, axis=-1, keepdims=True)
        o = _dot(p.astype(BF16), v_ref[:, sl].astype(BF16)) / l
        o_ref[:, sl] = o.astype(o_ref.dtype)


def _cross_attn(q_all, k, v, o_buf, *, batch, q_len, n_mem, n_heads, dh, row0, m_rows):
    width = n_heads * dh
    tq = _pick(q_len, (512, 256, 128, 64, 32, 16))
    nq = q_len // tq
    rb = row0 // tq
    aliased = o_buf is not None
    in_specs = [pl.BlockSpec((tq, width), lambda b, qi: (rb + b * nq + qi, 0)),
                pl.BlockSpec((n_mem, width), lambda b, qi: (b, 0)),
                pl.BlockSpec((n_mem, width), lambda b, qi: (b, 0))]
    args = [q_all, k, v]
    if aliased:
        in_specs.append(pl.BlockSpec(memory_space=pl.ANY))
        args.append(o_buf)
    est = (2 * _nbytes((tq, width), F32) + 4 * _nbytes((n_mem, width), F32)
           + 2 * _nbytes((tq, width), BF16) + 12 * _nbytes((tq, n_mem), F32))
    return pl.pallas_call(
        functools.partial(_cross_body, n_heads=n_heads, dh=dh, aliased=aliased),
        out_shape=jax.ShapeDtypeStruct((m_rows, width), BF16),
        grid=(batch, nq),
        in_specs=in_specs,
        out_specs=pl.BlockSpec((tq, width), lambda b, qi: (rb + b * nq + qi, 0)),
        input_output_aliases={3: 0} if aliased else {},
        compiler_params=pltpu.CompilerParams(
            dimension_semantics=("parallel", "arbitrary"), vmem_limit_bytes=_vmem_limit(est)),
        name="cross_attn",
    )(*args)


def _s5_prep_body(lr_ref, li_ref, ldt_ref, br_ref, bi_ref, ar_ref, ai_ref, bbr_ref, bbi_ref):
    lr = lr_ref[...]
    li = li_ref[...]
    dt = jnp.exp(ldt_ref[...])
    mag = jnp.exp(lr * dt)
    a_re = mag * jnp.cos(li * dt)
    a_im = mag * jnp.sin(li * dt)
    den = lr * lr + li * li
    num_re = a_re - 1.0
    coef_re = (num_re * lr + a_im * li) / den
    coef_im = (a_im * lr - num_re * li) / den
    br = br_ref[...]
    bi = bi_ref[...]
    ar_ref[...] = a_re
    ai_ref[...] = a_im
    bbr_ref[...] = coef_re * br - coef_im * bi
    bbi_ref[...] = coef_re * bi + coef_im * br


def _s5_prep(lam_re, lam_im, log_dt, b_re, b_im):
    shape = lam_re.shape
    spec = pl.BlockSpec(shape, lambda: (0, 0))
    return pl.pallas_call(
        _s5_prep_body,
        out_shape=tuple(jax.ShapeDtypeStruct(shape, F32) for _ in range(4)),
        in_specs=[spec] * 5,
        out_specs=tuple([spec] * 4),
        name="s5_discretise",
    )(lam_re, lam_im, log_dt, b_re, b_im)


def _s5_body(u_ref, x0r_ref, x0i_ref, ar_ref, ai_ref, bre_ref, bim_ref, cre_ref, cim_ref,
             d_ref, wg_ref, bg_ref, o_ref, xr_out, xi_out, bur, bui, st_r, st_i,
             *, nb, t_chunk, n_blk, passes):
    i = pl.program_id(0)
    wc = u_ref.shape[1]
    ns = bur.shape[1]
    ub = wc // n_blk
    sb = ns // n_blk

    @pl.when(i == 0)
    def _():
        st_r[...] = x0r_ref[...]
        st_i[...] = x0i_ref[...]

    u = u_ref[...]
    for k in range(n_blk):
        uk = u[:, k * ub:(k + 1) * ub]
        bur[:, k * sb:(k + 1) * sb] = _dot_hp(
            uk, bre_ref[k * ub:(k + 1) * ub, k * sb:(k + 1) * sb], passes)
        bui[:, k * sb:(k + 1) * sb] = _dot_hp(
            uk, bim_ref[k * ub:(k + 1) * ub, k * sb:(k + 1) * sb], passes)

    a_re = ar_ref[...]
    a_im = ai_ref[...]

    def step(t, carry):
        xr, xi = carry
        r0 = pl.multiple_of(t * nb, nb)
        nr = a_re * xr - a_im * xi + bur[pl.ds(r0, nb), :]
        ni = a_re * xi + a_im * xr + bui[pl.ds(r0, nb), :]
        bur[pl.ds(r0, nb), :] = nr
        bui[pl.ds(r0, nb), :] = ni
        return nr, ni
    xr_f, xi_f = lax.fori_loop(0, t_chunk, step, (st_r[...], st_i[...]))
    st_r[...] = xr_f
    st_i[...] = xi_f
    xr_out[...] = xr_f
    xi_out[...] = xi_f
    ys = []
    for k in range(n_blk):
        xrk = bur[:, k * sb:(k + 1) * sb].astype(BF16)
        xik = bui[:, k * sb:(k + 1) * sb].astype(BF16)
        ys.append(_dot(xrk, cre_ref[k * sb:(k + 1) * sb, k * ub:(k + 1) * ub].astype(BF16))
                  - _dot(xik, cim_ref[k * sb:(k + 1) * sb, k * ub:(k + 1) * ub].astype(BF16)))
    y = jnp.concatenate(ys, axis=1) + d_ref[...] * u
    z = _gelu_tanh(y)
    gate = _dot(z.astype(BF16), wg_ref[...].astype(BF16)) + bg_ref[...]
    o_ref[...] = z * jax.nn.sigmoid(gate)


def _s5(u_tb, x0r, x0i, a_re, a_im, b_re, b_im, c_re, c_im, d, w_glu, b_glu, *, nb, seq, n_blk,
        passes):
    wc = u_tb.shape[1]
    ns = a_re.shape[1]
    t_chunk = _pick(seq, (64, 32, 16))
    rows = nb * t_chunk
    full = lambda shape: pl.BlockSpec(shape, lambda i: (0, 0))
    est = (4 * _nbytes((rows, wc), F32) + 2 * _nbytes((rows, ns), F32)
           + 2 * 4 * _nbytes((wc, ns), F32) + 2 * _nbytes((wc, wc), F32)
           + 12 * _nbytes((rows, ns // n_blk), F32) + 8 * _nbytes((nb, ns), F32))
    return pl.pallas_call(
        functools.partial(_s5_body, nb=nb, t_chunk=t_chunk, n_blk=n_blk, passes=passes),
        out_shape=(jax.ShapeDtypeStruct((seq * nb, wc), F32),
                   jax.ShapeDtypeStruct((nb, ns), F32), jax.ShapeDtypeStruct((nb, ns), F32)),
        grid=(seq // t_chunk,),
        in_specs=[pl.BlockSpec((rows, wc), lambda i: (i, 0)),
                  full((nb, ns)), full((nb, ns)), full((1, ns)), full((1, ns)),
                  full((wc, ns)), full((wc, ns)), full((ns, wc)), full((ns, wc)),
                  full((1, wc)), full((wc, wc)), full((1, wc))],
        out_specs=(pl.BlockSpec((rows, wc), lambda i: (i, 0)), full((nb, ns)), full((nb, ns))),
        scratch_shapes=[pltpu.VMEM((rows, ns), F32), pltpu.VMEM((rows, ns), F32),
                        pltpu.VMEM((nb, ns), F32), pltpu.VMEM((nb, ns), F32)],
        compiler_params=pltpu.CompilerParams(
            dimension_semantics=("arbitrary",), vmem_limit_bytes=_vmem_limit(est)),
        name="s5_scan",
    )(u_tb, x0r, x0i, a_re, a_im, b_re, b_im, c_re, c_im, d, w_glu, b_glu)


def _router_body(x_ref, g_ref, wr_ref, pk_ref, idx_ref, gate_ref, *, n_exp, rc):
    tm, d = x_ref.shape
    half = d // 2

    def chunk(c, carry):
        r0 = pl.multiple_of(c * rc, rc)
        x = x_ref[pl.ds(r0, rc), :]
        ms = jnp.mean(x * x, axis=-1, keepdims=True)
        h = x * lax.rsqrt(ms + EPS) * g_ref[...]
        hb = h.astype(BF16).astype(F32)
        lo = lax.shift_right_logical(pltpu.bitcast(hb[:, :half], U32), jnp.uint32(16))
        hi = pltpu.bitcast(hb[:, half:], U32) & jnp.uint32(0xFFFF0000)
        pk_ref[pl.ds(r0, rc), :] = lo | hi
        logits = _dot_hp(h, wr_ref[...])
        lane = lax.broadcasted_iota(I32, logits.shape, 1)
        logits = jnp.where(lane < n_exp, logits, NEG_INF)
        mx = jnp.max(logits, axis=-1, keepdims=True)
        e = jnp.exp(logits - mx)
        probs = e / jnp.sum(e, axis=-1, keepdims=True)
        probs = jnp.where(lane < n_exp, probs, -1.0)
        lane_f = lane.astype(F32)
        p1 = jnp.max(probs, axis=-1, keepdims=True)
        i1 = jnp.min(jnp.where(probs == p1, lane_f, float(LANES)), axis=-1, keepdims=True)
        rest = jnp.where(lane_f == i1, -1.0, probs)
        p2 = jnp.max(rest, axis=-1, keepdims=True)
        i2 = jnp.min(jnp.where(rest == p2, lane_f, float(LANES)), axis=-1, keepdims=True)
        tot = p1 + p2
        idx_ref[pl.ds(r0, rc), :] = jnp.where(lane == 0, i1, jnp.where(lane == 1, i2, 0.0)).astype(I32)
        gate_ref[pl.ds(r0, rc), :] = jnp.where(lane == 0, p1 / tot,
                                                jnp.where(lane == 1, p2 / tot, 0.0))
        return carry
    lax.fori_loop(0, tm // rc, chunk, 0)


def _router(x, g, w_router_pad, *, n_exp, tm):
    m, d = x.shape
    rc = _pick(tm, (256, 176, 128, 64, 32, 16, 8))
    est = 2 * _nbytes((tm, d), F32) + 2 * _nbytes((tm, d // 2), U32) + 16 * _nbytes((rc, d), F32)
    return pl.pallas_call(
        functools.partial(_router_body, n_exp=n_exp, rc=rc),
        out_shape=(jax.ShapeDtypeStruct((m, d // 2), U32),
                   jax.ShapeDtypeStruct((m, LANES), I32),
                   jax.ShapeDtypeStruct((m, LANES), F32)),
        grid=(m // tm,),
        in_specs=[pl.BlockSpec((tm, d), lambda i: (i, 0)),
                  pl.BlockSpec((1, d), lambda i: (0, 0)),
                  pl.BlockSpec((d, LANES), lambda i: (0, 0))],
        out_specs=(pl.BlockSpec((tm, d // 2), lambda i: (i, 0)),
                   pl.BlockSpec((tm, LANES), lambda i: (i, 0)),
                   pl.BlockSpec((tm, LANES), lambda i: (i, 0))),
        compiler_params=pltpu.CompilerParams(
            dimension_semantics=("parallel",), vmem_limit_bytes=_vmem_limit(est)),
        name="moe_router",
    )(x, g, w_router_pad)


GATHER_UNROLL = 8


def _row_copy(src_hbm, row, dst, r, sem):
    return pltpu.make_async_copy(src_hbm.at[pl.ds(row, 1), :], dst.at[pl.ds(r, 1), :], sem)


def _dispatch_body(nt_ref, tok_ref, nxt_ref, pk_hbm, a_ref, buf, sem, *, tm):
    i = pl.program_id(0)
    nt = nt_ref[0]
    half = buf.shape[2]

    def request(ids_ref, slot):
        def issue(c, carry):
            for u in range(GATHER_UNROLL):
                r = c * GATHER_UNROLL + u
                _row_copy(pk_hbm, ids_ref[0, 0, r], buf.at[slot], r, sem.at[slot]).start()
            return carry
        lax.fori_loop(0, tm // GATHER_UNROLL, issue, 0)

    @pl.when(i == 0)
    def _():
        request(tok_ref, 0)

    @pl.when(i + 1 < nt)
    def _():
        request(nxt_ref, (i + 1) % 2)

    @pl.when(i < nt)
    def _():
        slot = i % 2

        def drain(c, carry):
            for u in range(GATHER_UNROLL):
                r = c * GATHER_UNROLL + u
                _row_copy(pk_hbm, 0, buf.at[slot], r, sem.at[slot]).wait()
            return carry
        lax.fori_loop(0, tm // GATHER_UNROLL, drain, 0)
        pk = buf[slot]
        lo = pltpu.bitcast(lax.shift_left(pk, jnp.uint32(16)), F32)
        hi = pltpu.bitcast(pk & jnp.uint32(0xFFFF0000), F32)
        a_ref[:, :half] = lo.astype(BF16)
        a_ref[:, half:] = hi.astype(BF16)


def _dispatch(n_tiles, tok_sorted, packed, *, tm, r_max):
    m, half = packed.shape
    t_max = r_max // tm
    assert tm % GATHER_UNROLL == 0
    tok_tiles = tok_sorted.reshape(t_max, 1, tm)
    return pl.pallas_call(
        functools.partial(_dispatch_body, tm=tm),
        out_shape=jax.ShapeDtypeStruct((r_max, 2 * half), BF16),
        grid_spec=pltpu.PrefetchScalarGridSpec(
            num_scalar_prefetch=1,
            grid=(t_max,),
            in_specs=[pl.BlockSpec((1, 1, tm), lambda i, nt: (i, 0, 0), memory_space=pltpu.SMEM),
                      pl.BlockSpec((1, 1, tm), lambda i, nt: (jnp.minimum(i + 1, t_max - 1), 0, 0),
                                   memory_space=pltpu.SMEM),
                      pl.BlockSpec(memory_space=pl.ANY)],
            out_specs=pl.BlockSpec((tm, 2 * half), lambda i, nt: (jnp.minimum(i, nt[0] - 1), 0)),
            scratch_shapes=[pltpu.VMEM((2, tm, half), U32), pltpu.SemaphoreType.DMA((2,))]),
        compiler_params=pltpu.CompilerParams(dimension_semantics=("arbitrary",)),
        name="moe_dispatch",
    )(n_tiles, tok_tiles, tok_tiles, packed)


def _expert_up_body(te_ref, nt_ref, a_ref, w1_ref, w3_ref, h_ref, w1bf, w3bf, *, kc):
    i = pl.program_id(1)
    d = a_ref.shape[1]
    fresh = jnp.logical_or(i == 0, te_ref[i] != te_ref[jnp.maximum(i - 1, 0)])

    @pl.when(jnp.logical_and(fresh, i < nt_ref[0]))
    def _():
        def cast(c, carry):
            r0 = pl.multiple_of(c * kc, kc)
            w1bf[pl.ds(r0, kc), :] = w1_ref[pl.ds(r0, kc), :].astype(BF16)
            w3bf[pl.ds(r0, kc), :] = w3_ref[pl.ds(r0, kc), :].astype(BF16)
            return carry
        lax.fori_loop(0, d // kc, cast, 0)

    @pl.when(i < nt_ref[0])
    def _():
        a = a_ref[...]
        h_ref[...] = (jax.nn.silu(_dot(a, w1bf[...])) * _dot(a, w3bf[...])).astype(h_ref.dtype)


def _expert_up(tile_expert, n_tiles, a_sorted, w1, w3, *, tm, tf):
    r_max, d = a_sorted.shape
    n_exp, _, fe = w1.shape
    t_max = r_max // tm
    kc = _pick(d, (512, 256, 128))
    row = lambda f, i, te, nt: jnp.minimum(i, nt[0] - 1)
    est = (2 * _nbytes((tm, d), BF16) + 2 * 2 * _nbytes((d, tf), F32) + 2 * _nbytes((d, tf), BF16)
           + 2 * _nbytes((tm, tf), BF16) + 6 * _nbytes((tm, tf), F32))
    return pl.pallas_call(
        functools.partial(_expert_up_body, kc=kc),
        out_shape=jax.ShapeDtypeStruct((r_max, fe), BF16),
        grid_spec=pltpu.PrefetchScalarGridSpec(
            num_scalar_prefetch=2,
            grid=(fe // tf, t_max),
            in_specs=[pl.BlockSpec((tm, d), lambda f, i, te, nt: (row(f, i, te, nt), 0)),
                      pl.BlockSpec((None, d, tf), lambda f, i, te, nt: (te[i], 0, f)),
                      pl.BlockSpec((None, d, tf), lambda f, i, te, nt: (te[i], 0, f))],
            out_specs=pl.BlockSpec((tm, tf), lambda f, i, te, nt: (row(f, i, te, nt), f)),
            scratch_shapes=[pltpu.VMEM((d, tf), BF16), pltpu.VMEM((d, tf), BF16)]),
        compiler_params=pltpu.CompilerParams(
            dimension_semantics=("arbitrary", "arbitrary"), vmem_limit_bytes=_vmem_limit(est)),
        name="moe_expert_up",
    )(tile_expert, n_tiles, a_sorted, w1, w3)


def _expert_down_body(te_ref, nt_ref, h_ref, w2_ref, y_ref, w2bf, *, kc):
    i = pl.program_id(1)
    fe = h_ref.shape[1]
    fresh = jnp.logical_or(i == 0, te_ref[i] != te_ref[jnp.maximum(i - 1, 0)])

    @pl.when(jnp.logical_and(fresh, i < nt_ref[0]))
    def _():
        def cast(c, carry):
            r0 = pl.multiple_of(c * kc, kc)
            w2bf[pl.ds(r0, kc), :] = w2_ref[pl.ds(r0, kc), :].astype(BF16)
            return carry
        lax.fori_loop(0, fe // kc, cast, 0)

    @pl.when(i < nt_ref[0])
    def _():
        y_ref[...] = _dot(h_ref[...], w2bf[...])


def _expert_down(tile_expert, n_tiles, h_sorted, w2, *, tm, tn):
    r_max, fe = h_sorted.shape
    d = w2.shape[2]
    t_max = r_max // tm
    kc = _pick(fe, (512, 256, 128))
    row = lambda n, i, te, nt: jnp.minimum(i, nt[0] - 1)
    est = (2 * _nbytes((tm, fe), BF16) + 2 * _nbytes((fe, tn), F32) + _nbytes((fe, tn), BF16)
           + 4 * _nbytes((tm, tn), F32))
    return pl.pallas_call(
        functools.partial(_expert_down_body, kc=kc),
        out_shape=jax.ShapeDtypeStruct((r_max, d), F32),
        grid_spec=pltpu.PrefetchScalarGridSpec(
            num_scalar_prefetch=2,
            grid=(d // tn, t_max),
            in_specs=[pl.BlockSpec((tm, fe), lambda n, i, te, nt: (row(n, i, te, nt), 0)),
                      pl.BlockSpec((None, fe, tn), lambda n, i, te, nt: (te[i], 0, n))],
            out_specs=pl.BlockSpec((tm, tn), lambda n, i, te, nt: (row(n, i, te, nt), n)),
            scratch_shapes=[pltpu.VMEM((fe, tn), BF16)]),
        compiler_params=pltpu.CompilerParams(
            dimension_semantics=("arbitrary", "arbitrary"), vmem_limit_bytes=_vmem_limit(est)),
        name="moe_expert_down",
    )(tile_expert, n_tiles, h_sorted, w2)


def _combine_body(slot_ref, nxt_ref, x_ref, gate_ref, y_hbm, o_ref, buf, sem, *, tc):
    i = pl.program_id(0)
    n = pl.num_programs(0)
    unroll = GATHER_UNROLL // TOP_K

    def request(ids_ref, slot):
        def issue(c, carry):
            for u in range(unroll):
                r = c * unroll + u
                for k in range(TOP_K):
                    _row_copy(y_hbm, ids_ref[0, 0, TOP_K * r + k], buf.at[slot, k], r,
                              sem.at[slot]).start()
            return carry
        lax.fori_loop(0, tc // unroll, issue, 0)

    @pl.when(i == 0)
    def _():
        request(slot_ref, 0)

    @pl.when(i + 1 < n)
    def _():
        request(nxt_ref, (i + 1) % 2)

    slot = i % 2

    def drain(c, carry):
        for u in range(unroll):
            r = c * unroll + u
            for k in range(TOP_K):
                _row_copy(y_hbm, 0, buf.at[slot, k], r, sem.at[slot]).wait()
        return carry
    lax.fori_loop(0, tc // unroll, drain, 0)
    g = gate_ref[...]
    o_ref[...] = x_ref[...] + (g[:, 0:1] * buf[slot, 0] + g[:, 1:2] * buf[slot, 1])


def _combine(slots, x, gates, y_sorted, *, tc):
    m, d = x.shape
    n = m // tc
    assert tc % (GATHER_UNROLL // TOP_K) == 0
    est = 4 * _nbytes((tc, d), F32) + 2 * TOP_K * _nbytes((tc, d), F32) + 4 * _nbytes((tc, d), F32)
    slot_tiles = slots.reshape(n, 1, TOP_K * tc)
    return pl.pallas_call(
        functools.partial(_combine_body, tc=tc),
        out_shape=jax.ShapeDtypeStruct((m, d), F32),
        grid=(n,),
        in_specs=[pl.BlockSpec((1, 1, TOP_K * tc), lambda i: (i, 0, 0), memory_space=pltpu.SMEM),
                  pl.BlockSpec((1, 1, TOP_K * tc), lambda i: (jnp.minimum(i + 1, n - 1), 0, 0),
                               memory_space=pltpu.SMEM),
                  pl.BlockSpec((tc, d), lambda i: (i, 0)),
                  pl.BlockSpec((tc, LANES), lambda i: (i, 0)),
                  pl.BlockSpec(memory_space=pl.ANY)],
        out_specs=pl.BlockSpec((tc, d), lambda i: (i, 0)),
        scratch_shapes=[pltpu.VMEM((2, TOP_K, tc, d), F32), pltpu.SemaphoreType.DMA((2,))],
        compiler_params=pltpu.CompilerParams(
            dimension_semantics=("arbitrary",), vmem_limit_bytes=_vmem_limit(est)),
        name="moe_combine",
    )(slot_tiles, slot_tiles, x, gates, y_sorted)


def _moe(x, g_ffn, w_router, w_e1, w_e3, w_e2, *, tm_tok):
    m, d = x.shape
    n_exp, _, fe = w_e1.shape
    tm = _pick(m * TOP_K, (512, 256, 128, 64, 32, 16))
    wr = jnp.zeros((d, LANES), F32).at[:, :n_exp].set(w_router)
    packed, idx128, gate128 = _router(x, g_ffn, wr, n_exp=n_exp, tm=tm_tok)

    idx = idx128[:, :TOP_K]
    mask = jnp.sum(idx[:, :, None] == jnp.arange(n_exp, dtype=I32)[None, None, :], axis=1).astype(I32)
    counts = jnp.sum(mask, axis=0)
    padded = ((counts + tm - 1) // tm) * tm
    ends = jnp.cumsum(padded)
    starts = ends - padded
    pos = jnp.cumsum(mask, axis=0) - mask
    slot = starts[idx] + jnp.take_along_axis(pos, idx, axis=1)
    t_max = (m * TOP_K) // tm + n_exp
    r_max = t_max * tm
    tok_sorted = jnp.zeros((r_max,), I32).at[slot.reshape(-1)].set(
        jnp.repeat(jnp.arange(m, dtype=I32), TOP_K))
    n_tiles = (ends[-1] // tm).astype(I32).reshape(1)
    tile_start = jnp.minimum(jnp.arange(t_max, dtype=I32), n_tiles[0] - 1) * tm
    tile_expert = jnp.minimum(jnp.sum(ends[None, :] <= tile_start[:, None], axis=1),
                              n_exp - 1).astype(I32)

    a_sorted = _dispatch(n_tiles, tok_sorted, packed, tm=tm, r_max=r_max)
    tf = _pick(fe, (256, 128))
    h_sorted = _expert_up(tile_expert, n_tiles, a_sorted, w_e1, w_e3, tm=tm, tf=tf)
    y_sorted = _expert_down(tile_expert, n_tiles, h_sorted, w_e2, tm=tm, tn=_pick(d, (512, 256, 128)))
    tc = _pick(m, (256, 128, 64, 32, 16, 8))
    return _combine(slot, x, gate128, y_sorted, tc=tc)


def _block_diag(blocks):
    g, r, c = blocks.shape
    eye = jnp.eye(g, dtype=blocks.dtype)
    return (blocks[:, :, None, :] * eye[:, None, :, None]).reshape(g * r, g * c)


def _rel_table(rel_bias_l, q_pos, k_pos, max_rel):
    rel = np.clip(q_pos[:, None] - k_pos[None, :], -(CHUNK - 1), max_rel) + (CHUNK - 1)
    return rel_bias_l[:, rel].astype(F32)


def _band_table(rel_bias_l, tq, max_rel):
    n_heads = rel_bias_l.shape[0]
    past = BAND_PAST * CHUNK
    win = past + tq
    ring = -(-(tq + win - 1) // LANES) * LANES
    diff = np.arange(ring)
    diff = np.where(diff < tq, diff, diff - ring)
    idx = np.clip(diff + past, -(CHUNK - 1), max_rel) + (CHUNK - 1)
    by_diff = jnp.pad(rel_bias_l.astype(F32)[:, idx], ((0, 16 - n_heads), (0, 0)))

    def body(u_ref, o_ref):
        hp = pl.program_id(0)
        j = lax.broadcasted_iota(I32, (win, tq), 0)
        i = lax.broadcasted_iota(I32, (win, tq), 1)
        gap = (i + past) // CHUNK - j // CHUNK
        valid = (gap >= 0) & (gap <= BAND_PAST)
        for e in range(2):
            rows = jnp.broadcast_to(u_ref[pl.ds(2 * hp + e, 1), :], (win, ring))
            shifted = pltpu.roll(rows, 0, 1, stride=1, stride_axis=0)
            o_ref[:, e * tq:(e + 1) * tq] = jnp.where(valid, shifted[:, 0:tq], NEG_INF)

    return pl.pallas_call(
        body,
        out_shape=jax.ShapeDtypeStruct((n_heads // 2, win, 2 * tq), F32),
        grid=(n_heads // 2,),
        in_specs=[pl.BlockSpec((16, ring), lambda hp: (0, 0))],
        out_specs=pl.BlockSpec((None, win, 2 * tq), lambda hp: (hp, 0, 0)),
        compiler_params=pltpu.CompilerParams(dimension_semantics=("parallel",)),
        name="band_table",
    )(by_diff)


def _tile_gain(g):
    return jnp.tile(g.astype(F32), LANES // g.shape[0]).reshape(1, LANES)


def kernel(x_prompt, x_sample, mem_prompt, cache_fox_k, cache_fox_v, cache_fox_logf, cache_band_k, cache_band_v, state_ssm_re, state_ssm_im, cache_mem_k, cache_mem_v, g_mix, w_in, b_f, g_qa, g_ka, g_qb, g_kb, rel_bias, lam_re, lam_im, log_dt, ssm_b_re, ssm_b_im, ssm_c_re, ssm_c_im, ssm_d, w_glu, b_glu, g_mix_out, w_out, g_cross, g_mem, w_cq, w_ck, w_cv, g_cq, g_ck, w_co, g_ffn, w_ff1, w_ff3, w_ff2, w_router, w_e1, w_e3, w_e2):
    batch, seq, d = x_prompt.shape
    dbatch, dseq, _ = x_sample.shape
    depth = g_mix.shape[0]
    past_len, h_a, dh = cache_fox_k.shape[2:]
    band_rows, h_b = cache_band_k.shape[2:4]
    g_c, p_state = lam_re.shape[1:]
    w_a, w_b, w_c = h_a * dh, h_b * dh, g_c * SSM_GROUP
    n_mem, h_m, dh_m = cache_mem_k.shape[2:]
    w_m = h_m * dh_m
    max_rel = rel_bias.shape[2] - CHUNK
    assert dh == 64 and h_a % 2 == 0 and h_b % 2 == 0 and dh_m == LANES
    assert w_a == w_b and w_c <= w_a and h_a <= 16
    mp, ms = batch * seq, dbatch * dseq
    m = mp + ms
    pa, pb = h_a // 2, h_b // 2
    n_state = g_c * p_state
    n_blk = 2 if (w_c % 512 == 0) else 1
    nband = min(BAND_PAST * CHUNK, seq)
    tm = _pick(m, (1056, 1024, 768, 512, 256, 128, 64, 32, 16))
    tq_band = _pick(seq, (256, 128, 64))

    x = jnp.concatenate([x_prompt.reshape(mp, d), x_sample.reshape(ms, d)], axis=0)
    fox_kt, fox_vt, band_kt, band_vt = (jnp.transpose(c, (0, 1, 3, 4, 2)) for c in
                                        (cache_fox_k, cache_fox_v, cache_band_k, cache_band_v))

    outs = {k: [] for k in ("p_fk", "p_fv", "p_fl", "p_bk", "p_bv", "p_sr", "p_si", "p_mk", "p_mv",
                            "s_fk", "s_fv", "s_fl", "s_bk", "s_bv", "s_sr", "s_si")}
    for l in range(depth):
        sizes = (w_a, w_a, w_a, h_a, w_b, w_b, w_b)
        cuts = [sum(sizes[:i]) for i in range(len(sizes) + 1)]
        w_cat = _regroup_columns(
            w_in, l, ((0, cuts[3]), (cuts[4], cuts[7]), (cuts[7], w_in.shape[2]), (cuts[3], cuts[4])),
            7 * w_a)
        proj = _mm([x], [w_cat], gain=g_mix[l].reshape(1, d), tm=tm, tn=w_a, name="proj_in")
        col_uc = 6 * w_a

        fa_p = proj[:mp, col_uc + w_c:col_uc + w_c + h_a].reshape(batch, seq, h_a)
        fa_p = jnp.transpose(fa_p, (0, 2, 1)).reshape(batch * h_a, seq)
        bias_p = jnp.broadcast_to(jnp.tile(b_f[l], batch)[:, None], (batch * h_a, LANES))
        lf_p, f_p = _logf_cumsum(fa_p, bias_p, raw_from=0, valid_to=seq)
        lf_p = lf_p.reshape(batch, h_a, seq)
        f_p = f_p.reshape(batch, h_a, seq)
        f_t = jnp.transpose(f_p, (0, 2, 1)).reshape(batch, seq, pa, 2, 1)
        f_aug = jnp.tile(jnp.broadcast_to(f_t, (batch, seq, pa, 2, 3)).reshape(batch, seq, pa, 6),
                         (1, 1, 1, 2))
        f_aug = jnp.pad(f_aug, ((0, 0), (0, 0), (0, 0), (0, LANES - 12))).reshape(mp, pa * LANES)

        t_all = -(-(past_len + dseq) // LANES) * LANES
        fa_s = proj[mp:, col_uc + w_c:col_uc + w_c + h_a].reshape(dbatch, dseq, h_a)
        x_s = jnp.concatenate([jnp.transpose(cache_fox_logf[l], (0, 2, 1)),
                               jnp.transpose(fa_s, (0, 2, 1)),
                               jnp.zeros((dbatch, h_a, t_all - past_len - dseq), F32)], axis=2)
        bias_s = jnp.broadcast_to(jnp.tile(b_f[l], dbatch)[:, None], (dbatch * h_a, LANES))
        lf_s, f_s = _logf_cumsum(x_s.reshape(dbatch * h_a, t_all), bias_s,
                                 raw_from=past_len, valid_to=past_len + dseq)
        lf_s = lf_s.reshape(dbatch, h_a, t_all)[:, :, past_len:past_len + dseq]
        f_s = f_s.reshape(dbatch, h_a, t_all)
        fs_row = jnp.pad(f_s, ((0, 0), (0, 16 - h_a), (0, 0)))
        fs_col = jnp.pad(jnp.transpose(f_s[:, :, past_len:past_len + dseq], (0, 2, 1)),
                         ((0, 0), (0, 0), (0, LANES - h_a))).reshape(ms, LANES)

        gqa, gka = _tile_gain(g_qa[l]), _tile_gain(g_ka[l])
        oa, knt_a, vt_a = _fox_prompt(proj, f_aug, gqa, gka, batch=batch, seq=seq, n_pairs=pa,
                                      col_q=0, col_k=pa, col_v=2 * pa, m_rows=m)
        oa, kn_as = _sample_attn(proj, fox_kt, fox_vt, l, fs_col, fs_row,
                                 gqa, gka, oa, mode="fox", batch=dbatch, s_new=dseq, n_pairs=pa,
                                 col_q=0, row0=mp)

        gqb, gkb = _tile_gain(g_qb[l]), _tile_gain(g_kb[l])
        tab = _band_table(rel_bias[l], tq_band, max_rel)
        ob, knt_b, vt_b = _band_prompt(proj, tab, gqb, gkb, batch=batch, seq=seq, n_pairs=pb,
                                       col_q=3 * pa, col_k=3 * pa + pb, col_v=3 * pa + 2 * pb,
                                       tq=tq_band, m_rows=m)
        tab_s = _rel_table(rel_bias[l], band_rows + np.arange(dseq), np.arange(band_rows + dseq),
                           max_rel)
        ob, kn_bs = _sample_attn(proj, band_kt, band_vt, l,
                                 tab_s[:, :, :band_rows], tab_s[:, :, band_rows:], gqb, gkb, ob,
                                 mode="band", batch=dbatch, s_new=dseq, n_pairs=pb, col_q=3, row0=mp)

        rep = lambda a: jnp.repeat(a.astype(F32), SSM_GROUP, axis=0)
        a_re, a_im, bb_re, bb_im = _s5_prep(
            rep(lam_re[l]), rep(lam_im[l]),
            jnp.broadcast_to(rep(log_dt[l])[:, None], (g_c * SSM_GROUP, p_state)),
            jnp.transpose(ssm_b_re[l], (0, 2, 1)).reshape(g_c * SSM_GROUP, p_state),
            jnp.transpose(ssm_b_im[l], (0, 2, 1)).reshape(g_c * SSM_GROUP, p_state))
        a_re = a_re.reshape(g_c, SSM_GROUP, p_state)[:, 0, :].reshape(1, n_state)
        a_im = a_im.reshape(g_c, SSM_GROUP, p_state)[:, 0, :].reshape(1, n_state)
        b_re_d = _block_diag(bb_re.reshape(g_c, SSM_GROUP, p_state))
        b_im_d = _block_diag(bb_im.reshape(g_c, SSM_GROUP, p_state))
        c_re_d = _block_diag(jnp.transpose(ssm_c_re[l], (0, 2, 1)))
        c_im_d = _block_diag(jnp.transpose(ssm_c_im[l], (0, 2, 1)))
        d_row = ssm_d[l].reshape(1, w_c)
        s5_args = (a_re, a_im, b_re_d, b_im_d, c_re_d, c_im_d, d_row, w_glu[l],
                   b_glu[l].reshape(1, w_c))
        uc = proj[:, col_uc:col_uc + w_c]
        u_p = jnp.transpose(uc[:mp].reshape(batch, seq, w_c), (1, 0, 2)).reshape(mp, w_c)
        u_s = jnp.transpose(uc[mp:].reshape(dbatch, dseq, w_c), (1, 0, 2)).reshape(ms, w_c)
        zeros_p = jnp.zeros((batch, n_state), F32)
        oc_p, sr_p, si_p = _s5(u_p, zeros_p, zeros_p, *s5_args, nb=batch, seq=seq, n_blk=n_blk,
                               passes=1)
        oc_s, sr_s, si_s = _s5(u_s, state_ssm_re[l].reshape(dbatch, n_state),
                               state_ssm_im[l].reshape(dbatch, n_state), *s5_args,
                               nb=dbatch, seq=dseq, n_blk=n_blk, passes=3)
        oc = jnp.concatenate(
            [jnp.transpose(oc_p.reshape(seq, batch, w_c), (1, 0, 2)).reshape(mp, w_c),
             jnp.transpose(oc_s.reshape(dseq, dbatch, w_c), (1, 0, 2)).reshape(ms, w_c)], axis=0)

        x = _mm([oa, ob, oc], [w_out[l].astype(BF16)], gain=g_mix_out[l].reshape(1, -1), residual=x,
                tm=tm, tn=_pick(d, (1024, 512, 256, 128)), name="merge_out")

        per_head = lambda t, h: jnp.transpose(t.reshape(batch, h, dh, -1), (0, 3, 1, 2))
        outs["p_fk"].append(per_head(knt_a, h_a))
        outs["p_fv"].append(per_head(vt_a, h_a))
        outs["p_fl"].append(jnp.transpose(lf_p, (0, 2, 1)))
        outs["p_bk"].append(per_head(knt_b[:, :, seq - nband:], h_b))
        outs["p_bv"].append(per_head(vt_b[:, :, seq - nband:], h_b))
        outs["p_sr"].append(sr_p.reshape(batch, g_c, p_state))
        outs["p_si"].append(si_p.reshape(batch, g_c, p_state))
        outs["s_fk"].append(kn_as.reshape(dbatch, dseq, h_a, dh))
        outs["s_fv"].append(proj[mp:, 2 * w_a:3 * w_a].reshape(dbatch, dseq, h_a, dh))
        outs["s_fl"].append(jnp.transpose(lf_s, (0, 2, 1)))
        outs["s_bk"].append(kn_bs.reshape(dbatch, dseq, h_b, dh))
        outs["s_bv"].append(proj[mp:, 5 * w_a:6 * w_a].reshape(dbatch, dseq, h_b, dh))
        outs["s_sr"].append(sr_s.reshape(dbatch, g_c, p_state))
        outs["s_si"].append(si_s.reshape(dbatch, g_c, p_state))

        mem2 = mem_prompt.reshape(batch * n_mem, d)
        tmm = _pick(batch * n_mem, (1024, 512, 256, 128))
        gck = jnp.tile(g_ck[l].astype(F32), h_m).reshape(1, w_m)
        gcq = jnp.tile(g_cq[l].astype(F32), h_m).reshape(1, w_m)
        mk = _mm([mem2], [w_ck[l]], gain=g_mem[l].reshape(1, d), group_gain=gck,
                 epilogue="group_norm", tm=tmm, tn=w_m, name="mem_k")
        mv = _mm([mem2], [w_cv[l]], gain=g_mem[l].reshape(1, d), tm=tmm, tn=w_m, name="mem_v")
        outs["p_mk"].append(mk.reshape(batch, n_mem, h_m, dh_m))
        outs["p_mv"].append(mv.reshape(batch, n_mem, h_m, dh_m))
        q_c = _mm([x], [w_cq[l].astype(BF16)], gain=g_cross[l].reshape(1, d), group_gain=gcq,
                  epilogue="group_norm", tm=tm, tn=w_m, name="cross_q")
        o_c = _cross_attn(q_c, mk, mv, None, batch=batch, q_len=seq, n_mem=n_mem, n_heads=h_m,
                          dh=dh_m, row0=0, m_rows=m)
        o_c = _cross_attn(q_c, cache_mem_k[l].reshape(dbatch * n_mem, w_m),
                          cache_mem_v[l].reshape(dbatch * n_mem, w_m), o_c, batch=dbatch,
                          q_len=dseq, n_mem=n_mem, n_heads=h_m, dh=dh_m, row0=mp, m_rows=m)
        x = _mm([o_c], [w_co[l].astype(BF16)], residual=x, tm=tm,
                tn=_pick(d, (1024, 512, 256, 128)), name="cross_out")

        i = l // 2
        if l % 2 == 0:
            hmid = _mm([x], [w_ff1[i].astype(BF16), w_ff3[i].astype(BF16)],
                       gain=g_ffn[l].reshape(1, d), epilogue="swiglu", out_dtype=BF16, tm=tm,
                       tn=_pick(w_ff1.shape[2], (512, 256, 128)), name="ffn_up")
            x = _mm([hmid], [w_ff2[i].astype(BF16)], residual=x, tm=tm,
                    tn=_pick(d, (512, 256, 128)), name="ffn_down")
        else:
            x = _moe(x, g_ffn[l].reshape(1, d), w_router[i], w_e1[i], w_e3[i], w_e2[i], tm_tok=tm)

    st = lambda k: jnp.stack(outs[k])
    return (x[:mp].reshape(batch, seq, d), x[mp:].reshape(dbatch, dseq, d),
            st("p_fk"), st("p_fv"), st("p_fl"), st("p_bk"), st("p_bv"), st("p_sr"), st("p_si"),
            st("p_mk"), st("p_mv"), st("s_fk"), st("s_fv"), st("s_fl"), st("s_bk"), st("s_bv"),
            st("s_sr"), st("s_si"))
```

```python
import functools
import math

import jax
import jax.numpy as jnp
import numpy as np
from jax import lax
from jax.experimental import pallas as pl
from jax.experimental.pallas import tpu as pltpu

F32 = jnp.float32
BF16 = jnp.bfloat16
I32 = jnp.int32
U32 = jnp.uint32

EPS = 1e-6
NEG_INF = -1e30
LOG2E = math.log2(math.e)
CHUNK = 64
BAND_PAST = 8
SSM_GROUP = 16
TOP_K = 2

LANES = 128
SUBLANES = 8
BF16_ROWS = 16
VMEM_CAP = 60 * 1024 * 1024


def _vmem_limit(nbytes):
    return int(min(VMEM_CAP, max(16 * 1024 * 1024, nbytes * 5 // 4 + (4 << 20))))


def _pick(n, candidates):
    for c in candidates:
        if c <= n and n % c == 0:
            return c
    raise ValueError(f"no tile for {n} in {candidates}")


def _nbytes(shape, dtype):
    return math.prod(shape) * jnp.dtype(dtype).itemsize


def _split3(x):
    hi = x.astype(BF16)
    r1 = x - hi.astype(F32)
    mid = r1.astype(BF16)
    lo = (r1 - mid.astype(F32)).astype(BF16)
    return hi, mid, lo


def _dot(a, b):
    return jnp.dot(a, b, preferred_element_type=F32)


def _dot_nt(a, b):
    return lax.dot_general(a, b, (((1,), (1,)), ((), ())), preferred_element_type=F32)


def _dot_hp(a, b, passes=3):
    ah = a.astype(BF16)
    bh = b.astype(BF16)
    if passes == 1:
        return _dot(ah, bh)
    al = (a - ah.astype(F32)).astype(BF16)
    bl = (b - bh.astype(F32)).astype(BF16)
    return _dot(ah, bh) + (_dot(ah, bl) + _dot(al, bh))


def _pair_ones():
    r = lax.broadcasted_iota(I32, (LANES, LANES), 0) // 64
    c = lax.broadcasted_iota(I32, (LANES, LANES), 1) // 64
    return (r == c).astype(BF16)


def _pair_norm(x, g, ones):
    sq = x * x
    hi = sq.astype(BF16)
    lo = (sq - hi.astype(F32)).astype(BF16)
    ss = _dot(hi, ones) + _dot(lo, ones)
    return x * lax.rsqrt(ss * (1.0 / 64.0) + EPS) * g


def _log_sigmoid(x):
    return jnp.minimum(x, 0.0) - jnp.log(1.0 + jnp.exp(-jnp.abs(x)))


def _gelu_tanh(x):
    c = math.sqrt(2.0 / math.pi)
    return 0.5 * x * (1.0 + jnp.tanh(c * (x + 0.044715 * (x * x * x))))


def _mm_body(*refs, widths, norm, n_w, epilogue, has_res, has_gg, stage_a, tm, rc, kc, k_total):
    it = iter(refs)
    a_refs = [next(it) for _ in widths]
    g_ref = next(it) if norm else None
    w_refs = [next(it) for _ in range(n_w)]
    gg_ref = next(it) if has_gg else None
    res_ref = next(it) if has_res else None
    o_ref = next(it)
    abf = next(it) if stage_a else a_refs[0]
    cast_w = w_refs[0].dtype != BF16
    wbfs = [next(it) for _ in range(n_w)] if cast_w else w_refs
    j = pl.program_id(1)

    if stage_a:
        @pl.when(j == 0)
        def _():
            off = 0
            for a_ref, wd in zip(a_refs, widths):
                def chunk(c, carry, a_ref=a_ref, off=off, wd=wd):
                    r0 = pl.multiple_of(c * rc, rc)
                    x = a_ref[pl.ds(r0, rc), :].astype(F32)
                    if norm:
                        ms = jnp.mean(x * x, axis=-1, keepdims=True)
                        x = x * lax.rsqrt(ms + EPS) * g_ref[:, off:off + wd]
                    abf[pl.ds(r0, rc), off:off + wd] = x.astype(BF16)
                    return carry
                lax.fori_loop(0, tm // rc, chunk, 0)
                off += wd

    if cast_w:
        for w_ref, wbf in zip(w_refs, wbfs):
            def cast(c, carry, w_ref=w_ref, wbf=wbf):
                r0 = pl.multiple_of(c * kc, kc)
                wbf[pl.ds(r0, kc), :] = w_ref[pl.ds(r0, kc), :].astype(BF16)
                return carry
            lax.fori_loop(0, k_total // kc, cast, 0)

    def rows(c, carry):
        r0 = pl.multiple_of(c * rc, rc)
        a = abf[pl.ds(r0, rc), :]
        ys = [_dot(a, wbf[...]) for wbf in wbfs]
        if epilogue == "swiglu":
            y = jax.nn.silu(ys[0]) * ys[1]
        elif epilogue == "group_norm":
            parts = []
            for s in range(ys[0].shape[1] // LANES):
                ysl = ys[0][:, s * LANES:(s + 1) * LANES]
                ms = jnp.mean(ysl * ysl, axis=-1, keepdims=True)
                parts.append(ysl * lax.rsqrt(ms + EPS) * gg_ref[:, s * LANES:(s + 1) * LANES])
            y = jnp.concatenate(parts, axis=1)
        else:
            y = ys[0]
        if has_res:
            y = y + res_ref[pl.ds(r0, rc), :]
        o_ref[pl.ds(r0, rc), :] = y.astype(o_ref.dtype)
        return carry
    lax.fori_loop(0, tm // rc, rows, 0)


def _mm(a_parts, w_list, *, gain=None, group_gain=None, residual=None, epilogue="none",
        out_dtype=F32, tm, tn, name):
    m = a_parts[0].shape[0]
    widths = tuple(a.shape[1] for a in a_parts)
    k_total = sum(widths)
    n = w_list[0].shape[1]
    assert m % tm == 0 and n % tn == 0, (m, tm, n, tn)
    norm = gain is not None
    stage_a = norm or len(a_parts) > 1 or a_parts[0].dtype != BF16
    rc = _pick(tm, (512, 384, 352, 256, 176, 128, 64, 32, 16))
    kc = _pick(k_total, (512, 256, 128))
    grid = (m // tm, n // tn)
    in_specs = [pl.BlockSpec((tm, wd), lambda i, j: (i, 0)) for wd in widths]
    args = list(a_parts)
    est = sum(2 * _nbytes((tm, wd), a.dtype) for wd, a in zip(widths, a_parts))
    if norm:
        in_specs.append(pl.BlockSpec((1, k_total), lambda i, j: (0, 0)))
        args.append(gain)
    cast_w = w_list[0].dtype != BF16
    for w in w_list:
        in_specs.append(pl.BlockSpec((k_total, tn), lambda i, j: (0, j)))
        args.append(w)
        est += 2 * _nbytes((k_total, tn), w.dtype) + cast_w * _nbytes((k_total, tn), BF16)
    if group_gain is not None:
        in_specs.append(pl.BlockSpec((1, tn), lambda i, j: (0, j)))
        args.append(group_gain)
    if residual is not None:
        in_specs.append(pl.BlockSpec((tm, tn), lambda i, j: (i, j)))
        args.append(residual)
        est += 2 * _nbytes((tm, tn), F32)
    est += 2 * _nbytes((tm, tn), out_dtype) + stage_a * _nbytes((tm, k_total), BF16)
    est += 4 * _nbytes((rc, tn), F32) * len(w_list)
    body = functools.partial(
        _mm_body, widths=widths, norm=norm, n_w=len(w_list), epilogue=epilogue,
        has_res=residual is not None, has_gg=group_gain is not None, stage_a=stage_a, tm=tm,
        rc=rc, kc=kc, k_total=k_total)
    return pl.pallas_call(
        body,
        out_shape=jax.ShapeDtypeStruct((m, n), out_dtype),
        grid=grid,
        in_specs=in_specs,
        out_specs=pl.BlockSpec((tm, tn), lambda i, j: (i, j)),
        scratch_shapes=[pltpu.VMEM((tm, k_total), BF16)] * stage_a
        + [pltpu.VMEM((k_total, tn), BF16) for _ in w_list] * cast_w,
        compiler_params=pltpu.CompilerParams(
            dimension_semantics=("parallel", "arbitrary"),
            vmem_limit_bytes=_vmem_limit(est)),
        name=name,
    )(*args)


def _regroup_body(w_ref, o_ref, *, cuts, width):
    off = 0
    for lo, hi in cuts:
        o_ref[:, off:off + hi - lo] = w_ref[:, lo:hi].astype(o_ref.dtype)
        off += hi - lo
    if off < width:
        o_ref[:, off:width] = jnp.zeros((o_ref.shape[0], width - off), o_ref.dtype)


def _regroup_columns(w, layer, cuts, width):
    _, k, n = w.shape
    tr = _pick(k, (256, 128, 64, 32, 16))
    return pl.pallas_call(
        functools.partial(_regroup_body, cuts=cuts, width=width),
        out_shape=jax.ShapeDtypeStruct((k, width), BF16),
        grid=(k // tr,),
        in_specs=[pl.BlockSpec((None, tr, n), lambda i: (layer, i, 0))],
        out_specs=pl.BlockSpec((tr, width), lambda i: (i, 0)),
        compiler_params=pltpu.CompilerParams(
            dimension_semantics=("parallel",),
            vmem_limit_bytes=_vmem_limit(2 * _nbytes((tr, n), F32) + 2 * _nbytes((tr, width), BF16)
                                         + 4 * _nbytes((tr, width), F32))),
        name="regroup_w_in",
    )(w)


def _cumsum_body(x_ref, b_ref, lf_ref, f_ref, carry, *, raw_from, valid_to, tt):
    j = pl.program_id(0)

    @pl.when(j == 0)
    def _():
        carry[...] = jnp.zeros_like(carry)

    x = x_ref[...]
    lane = j * tt + lax.broadcasted_iota(I32, x.shape, 1)
    lf = jnp.where(lane >= raw_from, _log_sigmoid(x + b_ref[:, 0:1]), x)
    lf = jnp.where(lane < valid_to, lf, 0.0)
    lf_ref[...] = lf
    tri = (lax.broadcasted_iota(I32, (tt, tt), 0)
           <= lax.broadcasted_iota(I32, (tt, tt), 1)).astype(BF16)
    hi, mid, lo = _split3(lf)
    y = _dot(hi, tri) + _dot(mid, tri) + _dot(lo, tri) + carry[:, 0:1]
    f_ref[...] = y
    carry[...] = jnp.broadcast_to(y[:, tt - 1:tt], carry.shape)


def _logf_cumsum(x, bias, *, raw_from, valid_to):
    r, t = x.shape
    tt = _pick(t, (256, 128))
    return pl.pallas_call(
        functools.partial(_cumsum_body, raw_from=raw_from, valid_to=valid_to, tt=tt),
        out_shape=(jax.ShapeDtypeStruct((r, t), F32), jax.ShapeDtypeStruct((r, t), F32)),
        grid=(t // tt,),
        in_specs=[pl.BlockSpec((r, tt), lambda j: (0, j)),
                  pl.BlockSpec((r, LANES), lambda j: (0, 0))],
        out_specs=(pl.BlockSpec((r, tt), lambda j: (0, j)),
                   pl.BlockSpec((r, tt), lambda j: (0, j))),
        scratch_shapes=[pltpu.VMEM((r, LANES), F32)],
        compiler_params=pltpu.CompilerParams(dimension_semantics=("arbitrary",)),
        name="logf_cumsum",
    )(x, bias)


def _stack_heads(qn):
    lane = lax.broadcasted_iota(I32, qn.shape, 1)
    q0 = jnp.where(lane < 64, qn, 0.0)
    q1 = jnp.where(lane < 64, 0.0, qn)
    return jnp.concatenate([q0, q1], axis=0).astype(BF16)


def _unstack_heads(o, tq):
    lane = lax.broadcasted_iota(I32, (tq, LANES), 1)
    return jnp.where(lane < 64, o[:tq], o[tq:])


def _lane_column(block, h):
    lane = lax.broadcasted_iota(I32, block.shape, 1)
    return jnp.sum(jnp.where(lane == h, block, 0.0), axis=-1, keepdims=True)


V_ROWS = LANES + BF16_ROWS


def _split_select(x):
    hi, mid, lo = _split3(x)
    m3 = lax.broadcasted_iota(I32, x.shape, 1) % 3
    return jnp.where(m3 == 0, hi.astype(F32), jnp.where(m3 == 1, mid.astype(F32), lo.astype(F32)))


def _head_masks(shape):
    lane = lax.broadcasted_iota(I32, shape, 1)
    return lane, lane < 64


def _finish_heads(acc, tq):
    o0 = acc[0:64, 0:tq] / acc[LANES:LANES + 1, 0:tq]
    o1 = acc[64:LANES, tq:2 * tq] / acc[LANES:LANES + 1, tq:2 * tq]
    return jnp.concatenate([o0, o1], axis=0).T


def _fox_prompt_body(q_ref, k_ref, v_ref, f_ref, gq_ref, gk_ref, *rest, tq, seq, dh):
    o_ref, knt_ref, vt_ref, kaug, vt3 = rest[-5:]
    ones = _pair_ones()
    nq = seq // tq
    lane, first = _head_masks((tq, LANES))

    for c in range(nq):
        rows = slice(c * tq, (c + 1) * tq)
        kn = _pair_norm(k_ref[rows, :], gk_ref[...], ones)
        knt_ref[:, rows] = kn.T
        kaug[rows, 0:LANES] = kn.astype(BF16)
        sp = _split_select(f_ref[rows, :])
        kaug[rows, LANES:2 * LANES] = jnp.where(
            lane < 6, -sp, jnp.where(lane < 12, 1.0, 0.0)).astype(BF16)
        vt = v_ref[rows, :].T
        vt_ref[:, rows] = vt
        vt3[c, 0:LANES, :] = vt.astype(BF16)
        vt3[c, LANES:V_ROWS, :] = jnp.ones((BF16_ROWS, tq), BF16)

    causal = (lax.broadcasted_iota(I32, (tq, 2 * tq), 0)
              <= lax.broadcasted_iota(I32, (tq, 2 * tq), 1) % tq)

    for qi in range(nq):
        rows = slice(qi * tq, (qi + 1) * tq)
        qn = _pair_norm(q_ref[rows, :], gq_ref[...], ones) * (dh ** -0.5 * LOG2E)
        sp = _split_select(f_ref[rows, :])
        up0 = jnp.where(lane < 3, 1.0, jnp.where((lane >= 6) & (lane < 9), sp, 0.0))
        up1 = jnp.where((lane >= 3) & (lane < 6), 1.0,
                        jnp.where((lane >= 9) & (lane < 12), sp, 0.0))
        qs = jnp.concatenate(
            [jnp.concatenate([jnp.where(first, qn, 0.0), up0], axis=1),
             jnp.concatenate([jnp.where(first, 0.0, qn), up1], axis=1)], axis=0).astype(BF16)

        s = jnp.where(causal, _dot_nt(kaug[rows, :], qs), NEG_INF)
        m = jnp.max(s, axis=0, keepdims=True)
        acc = _dot(vt3[qi], jnp.exp2(s - m).astype(BF16))
        for kj in range(qi):
            s = _dot_nt(kaug[kj * tq:(kj + 1) * tq, :], qs)
            m_new = jnp.maximum(m, jnp.max(s, axis=0, keepdims=True))
            acc = jnp.exp2(m - m_new) * acc + _dot(vt3[kj], jnp.exp2(s - m_new).astype(BF16))
            m = m_new
        o_ref[rows, :] = _finish_heads(acc, tq)


def _layer_cache_outputs(layer, depth, batch, width, seq, prev, n_inputs, index):
    shape = jax.ShapeDtypeStruct((depth, batch, width, seq), F32)
    spec = pl.BlockSpec((None, None, LANES, seq), lambda *g: (layer, *index(*g), 0))
    extra_specs = [] if prev is None else [pl.BlockSpec(memory_space=pl.ANY)] * 2
    aliases = {} if prev is None else {n_inputs: 1, n_inputs + 1: 2}
    return (shape, shape), (spec, spec), extra_specs, aliases, (() if prev is None else tuple(prev))


def _fox_prompt(proj, f_aug, gq, gk, *, batch, seq, n_pairs, col_q, col_k, col_v, m_rows, layer,
                depth, prev):
    tq = _pick(seq, (512, 256, 128))
    nq = seq // tq
    width = n_pairs * LANES
    est = (2 * 7 * _nbytes((seq, LANES), F32)
           + _nbytes((seq, 2 * LANES), BF16) + _nbytes((V_ROWS, seq), BF16)
           + 16 * _nbytes((tq, 2 * tq), F32))
    c_shapes, c_specs, extra_specs, aliases, extra_args = _layer_cache_outputs(
        layer, depth, batch, width, seq, prev, 6, lambda b, hp: (b, hp))
    return pl.pallas_call(
        functools.partial(_fox_prompt_body, tq=tq, seq=seq, dh=64),
        out_shape=(jax.ShapeDtypeStruct((m_rows, width), F32), *c_shapes),
        grid=(batch, n_pairs),
        in_specs=[
            pl.BlockSpec((seq, LANES), lambda b, hp: (b, col_q + hp)),
            pl.BlockSpec((seq, LANES), lambda b, hp: (b, col_k + hp)),
            pl.BlockSpec((seq, LANES), lambda b, hp: (b, col_v + hp)),
            pl.BlockSpec((seq, LANES), lambda b, hp: (b, hp)),
            pl.BlockSpec((1, LANES), lambda b, hp: (0, 0)),
            pl.BlockSpec((1, LANES), lambda b, hp: (0, 0)),
            *extra_specs,
        ],
        out_specs=(pl.BlockSpec((seq, LANES), lambda b, hp: (b, hp)), *c_specs),
        input_output_aliases=aliases,
        scratch_shapes=[pltpu.VMEM((seq, 2 * LANES), BF16), pltpu.VMEM((nq, V_ROWS, tq), BF16)],
        compiler_params=pltpu.CompilerParams(
            dimension_semantics=("parallel", "parallel"),
            vmem_limit_bytes=_vmem_limit(est)),
        name="fox_prompt",
    )(proj, proj, proj, f_aug, gq, gk, *extra_args)


def _band_prompt_body(q_ref, k_ref, v_ref, tab_ref, gq_ref, gk_ref, *rest, tq, seq, past, dh):
    o_ref, knt_ref, vt_ref, kpad, vt3 = rest[-5:]
    ones = _pair_ones()
    nq = seq // tq
    npad = past // tq
    win = past + tq

    kpad[0:past, :] = jnp.zeros((past, LANES), BF16)
    for c in range(npad):
        vt3[c] = jnp.zeros((V_ROWS, tq), BF16)
    for c in range(nq):
        rows = slice(c * tq, (c + 1) * tq)
        kn = _pair_norm(k_ref[rows, :], gk_ref[...], ones)
        knt_ref[:, rows] = kn.T
        kpad[past + c * tq:past + (c + 1) * tq, :] = kn.astype(BF16)
        vt = v_ref[rows, :].T
        vt_ref[:, rows] = vt
        vt3[npad + c, 0:LANES, :] = vt.astype(BF16)
        vt3[npad + c, LANES:V_ROWS, :] = jnp.ones((BF16_ROWS, tq), BF16)

    row = lax.broadcasted_iota(I32, (win, 2 * tq), 0)

    for qi in range(nq):
        r0 = qi * tq
        qn = _pair_norm(q_ref[r0:r0 + tq, :], gq_ref[...], ones) * (dh ** -0.5 * LOG2E)
        s = _dot_nt(kpad[r0:r0 + win, :], _stack_heads(qn)) + tab_ref[...]
        if r0 < past:
            s = jnp.where(row >= past - r0, s, NEG_INF)
        m = jnp.max(s, axis=0, keepdims=True)
        p = jnp.exp2(s - m).astype(BF16)
        acc = _dot(vt3[qi], p[0:tq, :])
        for c in range(1, win // tq):
            acc = acc + _dot(vt3[qi + c], p[c * tq:(c + 1) * tq, :])
        o_ref[r0:r0 + tq, :] = _finish_heads(acc, tq)


def _band_prompt(proj, tab, gq, gk, *, batch, seq, n_pairs, col_q, col_k, col_v, tq, m_rows, layer,
                 depth, prev):
    past = BAND_PAST * CHUNK
    assert past % tq == 0 and seq % tq == 0
    nq = seq // tq
    width = n_pairs * LANES
    win = past + tq
    est = (2 * 5 * _nbytes((seq, LANES), F32)
           + _nbytes((seq + past, LANES), BF16) + _nbytes((V_ROWS, seq + past), BF16)
           + 2 * _nbytes((win, 2 * tq), F32) + 8 * _nbytes((win, 2 * tq), F32))
    c_shapes, c_specs, extra_specs, aliases, extra_args = _layer_cache_outputs(
        layer, depth, batch, width, seq, prev, 6, lambda hp, b: (b, hp))
    return pl.pallas_call(
        functools.partial(_band_prompt_body, tq=tq, seq=seq, past=past, dh=64),
        out_shape=(jax.ShapeDtypeStruct((m_rows, width), F32), *c_shapes),
        grid=(n_pairs, batch),
        in_specs=[
            pl.BlockSpec((seq, LANES), lambda hp, b: (b, col_q + hp)),
            pl.BlockSpec((seq, LANES), lambda hp, b: (b, col_k + hp)),
            pl.BlockSpec((seq, LANES), lambda hp, b: (b, col_v + hp)),
            pl.BlockSpec((None, win, 2 * tq), lambda hp, b: (hp, 0, 0)),
            pl.BlockSpec((1, LANES), lambda hp, b: (0, 0)),
            pl.BlockSpec((1, LANES), lambda hp, b: (0, 0)),
            *extra_specs,
        ],
        out_specs=(pl.BlockSpec((seq, LANES), lambda hp, b: (b, hp)), *c_specs),
        input_output_aliases=aliases,
        scratch_shapes=[pltpu.VMEM((seq + past, LANES), BF16),
                        pltpu.VMEM(((seq + past) // tq, V_ROWS, tq), BF16)],
        compiler_params=pltpu.CompilerParams(
            dimension_semantics=("parallel", "parallel"),
            vmem_limit_bytes=_vmem_limit(est)),
        name="band_prompt",
    )(proj, proj, proj, tab, gq, gk, *extra_args)


def _sample_attn_body(*refs, mode, n_pairs, s_new, n_cache, dh):
    if mode == "fox":
        (q_ref, k_ref, v_ref, ck_ref, cv_ref, fq_ref, ft_ref, gq_ref, gk_ref, _alias,
         o_ref, kn_ref) = refs
    else:
        (q_ref, k_ref, v_ref, ck_ref, cv_ref, tabc_ref, tabn_ref, gq_ref, gk_ref, _alias,
         o_ref, kn_ref) = refs
    ones = _pair_ones()
    row = lax.broadcasted_iota(I32, (s_new, s_new), 0)
    col = lax.broadcasted_iota(I32, (s_new, s_new), 1)
    for hp in range(n_pairs):
        sl = slice(hp * LANES, (hp + 1) * LANES)
        qn = _pair_norm(q_ref[:, sl], gq_ref[...], ones) * (dh ** -0.5)
        kn = _pair_norm(k_ref[:, sl], gk_ref[...], ones)
        kn_ref[:, sl] = kn
        vn = v_ref[:, sl]
        outs = []
        for e in range(2):
            h = 2 * hp + e
            hl = slice(e * dh, (e + 1) * dh)
            qh = qn[:, hl].astype(BF16)
            sc = _dot(qh, ck_ref[h].astype(BF16))
            sn = _dot_nt(qh, kn[:, hl].astype(BF16))
            if mode == "fox":
                fq = _lane_column(fq_ref[...], h)
                fk = ft_ref[h:h + 1, :]
                sc = sc + fq - fk[:, :n_cache]
                sn = jnp.where(col <= row, sn + fq - fk[:, n_cache:n_cache + s_new], NEG_INF)
            else:
                sc = sc + tabc_ref[h]
                sn = sn + tabn_ref[h]
            m = jnp.maximum(jnp.max(sc, axis=-1, keepdims=True),
                            jnp.max(sn, axis=-1, keepdims=True))
            pc = jnp.exp(sc - m)
            pn = jnp.exp(sn - m)
            l = jnp.sum(pc, axis=-1, keepdims=True) + jnp.sum(pn, axis=-1, keepdims=True)
            outs.append((_dot_nt(pc.astype(BF16), cv_ref[h].astype(BF16))
                         + _dot(pn.astype(BF16), vn[:, hl].astype(BF16))) / l)
        o_ref[:, sl] = jnp.concatenate(outs, axis=1)


def _sample_attn(proj, cache_kt, cache_vt, layer, extra_a, extra_b, gq, gk, o_buf, *, mode, batch,
                 s_new, n_pairs, col_q, row0):
    width = n_pairs * LANES
    _, _, n_heads, dh, n_cache = cache_kt.shape
    rb = row0 // s_new
    if mode == "fox":
        ex_specs = [pl.BlockSpec((s_new, LANES), lambda b: (b, 0)),
                    pl.BlockSpec((None, 16, extra_b.shape[2]), lambda b: (b, 0, 0))]
    else:
        ex_specs = [pl.BlockSpec(extra_a.shape, lambda b: (0, 0, 0)),
                    pl.BlockSpec(extra_b.shape, lambda b: (0, 0, 0))]
    est = (2 * 2 * _nbytes((n_cache, width), F32) + 8 * _nbytes((s_new, width), F32)
           + 2 * _nbytes(extra_a.shape, F32) + 16 * _nbytes((2 * s_new, n_cache), F32)
           + 4 * _nbytes((n_cache, LANES), BF16))
    return pl.pallas_call(
        functools.partial(_sample_attn_body, mode=mode, n_pairs=n_pairs, s_new=s_new,
                          n_cache=n_cache, dh=dh),
        out_shape=(jax.ShapeDtypeStruct(o_buf.shape, F32),
                   jax.ShapeDtypeStruct((batch * s_new, width), F32)),
        grid=(batch,),
        in_specs=[
            pl.BlockSpec((s_new, width), lambda b: (rb + b, col_q)),
            pl.BlockSpec((s_new, width), lambda b: (rb + b, col_q + 1)),
            pl.BlockSpec((s_new, width), lambda b: (rb + b, col_q + 2)),
            pl.BlockSpec((None, None, n_heads, dh, n_cache), lambda b: (layer, b, 0, 0, 0)),
            pl.BlockSpec((None, None, n_heads, dh, n_cache), lambda b: (layer, b, 0, 0, 0)),
            *ex_specs,
            pl.BlockSpec((1, LANES), lambda b: (0, 0)),
            pl.BlockSpec((1, LANES), lambda b: (0, 0)),
            pl.BlockSpec(memory_space=pl.ANY),
        ],
        out_specs=(pl.BlockSpec((s_new, width), lambda b: (rb + b, 0)),
                   pl.BlockSpec((s_new, width), lambda b: (b, 0))),
        input_output_aliases={9: 0},
        compiler_params=pltpu.CompilerParams(
            dimension_semantics=("parallel",), vmem_limit_bytes=_vmem_limit(est)),
        name=f"{mode}_sample",
    )(proj, proj, proj, cache_kt, cache_vt, extra_a, extra_b, gq, gk, o_buf)


def _cross_body(*refs, n_heads, dh, aliased):
    if aliased:
        q_ref, k_ref, v_ref, _alias, o_ref = refs
    else:
        q_ref, k_ref, v_ref, o_ref = refs
    for h in range(n_heads):
        sl = slice(h * dh, (h + 1) * dh)
        q = (q_ref[:, sl] * (dh ** -0.5)).astype(BF16)
        s = _dot_nt(q, k_ref[:, sl].astype(BF16))
        m = jnp.max(s, axis=-1, keepdims=True)
        p = jnp.exp(s - m)
        l = jnp.sum(p, axis=-1, keepdims=True)
        o = _dot(p.astype(BF16), v_ref[:, sl].astype(BF16)) / l
        o_ref[:, sl] = o.astype(o_ref.dtype)


def _cross_attn(q_all, k, v, o_buf, *, batch, q_len, n_mem, n_heads, dh, row0, m_rows):
    width = n_heads * dh
    tq = _pick(q_len, (512, 256, 128, 64, 32, 16))
    nq = q_len // tq
    rb = row0 // tq
    aliased = o_buf is not None
    in_specs = [pl.BlockSpec((tq, width), lambda b, qi: (rb + b * nq + qi, 0)),
                pl.BlockSpec((n_mem, width), lambda b, qi: (b, 0)),
                pl.BlockSpec((n_mem, width), lambda b, qi: (b, 0))]
    args = [q_all, k, v]
    if aliased:
        in_specs.append(pl.BlockSpec(memory_space=pl.ANY))
        args.append(o_buf)
    est = (2 * _nbytes((tq, width), F32) + 4 * _nbytes((n_mem, width), F32)
           + 2 * _nbytes((tq, width), BF16) + 12 * _nbytes((tq, n_mem), F32))
    return pl.pallas_call(
        functools.partial(_cross_body, n_heads=n_heads, dh=dh, aliased=aliased),
        out_shape=jax.ShapeDtypeStruct((m_rows, width), BF16),
        grid=(batch, nq),
        in_specs=in_specs,
        out_specs=pl.BlockSpec((tq, width), lambda b, qi: (rb + b * nq + qi, 0)),
        input_output_aliases={3: 0} if aliased else {},
        compiler_params=pltpu.CompilerParams(
            dimension_semantics=("parallel", "arbitrary"), vmem_limit_bytes=_vmem_limit(est)),
        name="cross_attn",
    )(*args)


def _s5_prep_body(lr_ref, li_ref, ldt_ref, br_ref, bi_ref, ar_ref, ai_ref, bbr_ref, bbi_ref):
    lr = lr_ref[...]
    li = li_ref[...]
    dt = jnp.exp(ldt_ref[...])
    mag = jnp.exp(lr * dt)
    a_re = mag * jnp.cos(li * dt)
    a_im = mag * jnp.sin(li * dt)
    den = lr * lr + li * li
    num_re = a_re - 1.0
    coef_re = (num_re * lr + a_im * li) / den
    coef_im = (a_im * lr - num_re * li) / den
    br = br_ref[...]
    bi = bi_ref[...]
    ar_ref[...] = a_re
    ai_ref[...] = a_im
    bbr_ref[...] = coef_re * br - coef_im * bi
    bbi_ref[...] = coef_re * bi + coef_im * br


def _s5_prep(lam_re, lam_im, log_dt, b_re, b_im):
    shape = lam_re.shape
    spec = pl.BlockSpec(shape, lambda: (0, 0))
    return pl.pallas_call(
        _s5_prep_body,
        out_shape=tuple(jax.ShapeDtypeStruct(shape, F32) for _ in range(4)),
        in_specs=[spec] * 5,
        out_specs=tuple([spec] * 4),
        name="s5_discretise",
    )(lam_re, lam_im, log_dt, b_re, b_im)


def _s5_body(u_ref, x0r_ref, x0i_ref, ar_ref, ai_ref, bre_ref, bim_ref, cre_ref, cim_ref,
             d_ref, wg_ref, bg_ref, o_ref, xr_out, xi_out, bur, bui, st_r, st_i,
             *, nb, t_chunk, n_blk, passes):
    i = pl.program_id(0)
    wc = u_ref.shape[1]
    ns = bur.shape[1]
    ub = wc // n_blk
    sb = ns // n_blk

    @pl.when(i == 0)
    def _():
        st_r[...] = x0r_ref[...]
        st_i[...] = x0i_ref[...]

    u = u_ref[...]
    for k in range(n_blk):
        uk = u[:, k * ub:(k + 1) * ub]
        bur[:, k * sb:(k + 1) * sb] = _dot_hp(
            uk, bre_ref[k * ub:(k + 1) * ub, k * sb:(k + 1) * sb], passes)
        bui[:, k * sb:(k + 1) * sb] = _dot_hp(
            uk, bim_ref[k * ub:(k + 1) * ub, k * sb:(k + 1) * sb], passes)

    a_re = ar_ref[...]
    a_im = ai_ref[...]

    def step(t, carry):
        xr, xi = carry
        r0 = pl.multiple_of(t * nb, nb)
        nr = a_re * xr - a_im * xi + bur[pl.ds(r0, nb), :]
        ni = a_re * xi + a_im * xr + bui[pl.ds(r0, nb), :]
        bur[pl.ds(r0, nb), :] = nr
        bui[pl.ds(r0, nb), :] = ni
        return nr, ni
    xr_f, xi_f = lax.fori_loop(0, t_chunk, step, (st_r[...], st_i[...]))
    st_r[...] = xr_f
    st_i[...] = xi_f
    xr_out[...] = xr_f
    xi_out[...] = xi_f
    ys = []
    for k in range(n_blk):
        xrk = bur[:, k * sb:(k + 1) * sb].astype(BF16)
        xik = bui[:, k * sb:(k + 1) * sb].astype(BF16)
        ys.append(_dot(xrk, cre_ref[k * sb:(k + 1) * sb, k * ub:(k + 1) * ub].astype(BF16))
                  - _dot(xik, cim_ref[k * sb:(k + 1) * sb, k * ub:(k + 1) * ub].astype(BF16)))
    y = jnp.concatenate(ys, axis=1) + d_ref[...] * u
    z = _gelu_tanh(y)
    gate = _dot(z.astype(BF16), wg_ref[...].astype(BF16)) + bg_ref[...]
    o_ref[...] = z * jax.nn.sigmoid(gate)


def _s5(u_tb, x0r, x0i, a_re, a_im, b_re, b_im, c_re, c_im, d, w_glu, b_glu, *, nb, seq, n_blk,
        passes):
    wc = u_tb.shape[1]
    ns = a_re.shape[1]
    t_chunk = _pick(seq, (64, 32, 16))
    rows = nb * t_chunk
    full = lambda shape: pl.BlockSpec(shape, lambda i: (0, 0))
    est = (4 * _nbytes((rows, wc), F32) + 2 * _nbytes((rows, ns), F32)
           + 2 * 4 * _nbytes((wc, ns), F32) + 2 * _nbytes((wc, wc), F32)
           + 12 * _nbytes((rows, ns // n_blk), F32) + 8 * _nbytes((nb, ns), F32))
    return pl.pallas_call(
        functools.partial(_s5_body, nb=nb, t_chunk=t_chunk, n_blk=n_blk, passes=passes),
        out_shape=(jax.ShapeDtypeStruct((seq * nb, wc), F32),
                   jax.ShapeDtypeStruct((nb, ns), F32), jax.ShapeDtypeStruct((nb, ns), F32)),
        grid=(seq // t_chunk,),
        in_specs=[pl.BlockSpec((rows, wc), lambda i: (i, 0)),
                  full((nb, ns)), full((nb, ns)), full((1, ns)), full((1, ns)),
                  full((wc, ns)), full((wc, ns)), full((ns, wc)), full((ns, wc)),
                  full((1, wc)), full((wc, wc)), full((1, wc))],
        out_specs=(pl.BlockSpec((rows, wc), lambda i: (i, 0)), full((nb, ns)), full((nb, ns))),
        scratch_shapes=[pltpu.VMEM((rows, ns), F32), pltpu.VMEM((rows, ns), F32),
                        pltpu.VMEM((nb, ns), F32), pltpu.VMEM((nb, ns), F32)],
        compiler_params=pltpu.CompilerParams(
            dimension_semantics=("arbitrary",), vmem_limit_bytes=_vmem_limit(est)),
        name="s5_scan",
    )(u_tb, x0r, x0i, a_re, a_im, b_re, b_im, c_re, c_im, d, w_glu, b_glu)


def _router_body(x_ref, g_ref, wr_ref, pk_ref, idx_ref, gate_ref, *, n_exp, rc):
    tm, d = x_ref.shape
    half = d // 2

    def chunk(c, carry):
        r0 = pl.multiple_of(c * rc, rc)
        x = x_ref[pl.ds(r0, rc), :]
        ms = jnp.mean(x * x, axis=-1, keepdims=True)
        h = x * lax.rsqrt(ms + EPS) * g_ref[...]
        hb = h.astype(BF16).astype(F32)
        lo = lax.shift_right_logical(pltpu.bitcast(hb[:, :half], U32), jnp.uint32(16))
        hi = pltpu.bitcast(hb[:, half:], U32) & jnp.uint32(0xFFFF0000)
        pk_ref[pl.ds(r0, rc), :] = lo | hi
        logits = _dot_hp(h, wr_ref[...])
        lane = lax.broadcasted_iota(I32, logits.shape, 1)
        logits = jnp.where(lane < n_exp, logits, NEG_INF)
        mx = jnp.max(logits, axis=-1, keepdims=True)
        e = jnp.exp(logits - mx)
        probs = e / jnp.sum(e, axis=-1, keepdims=True)
        probs = jnp.where(lane < n_exp, probs, -1.0)
        lane_f = lane.astype(F32)
        p1 = jnp.max(probs, axis=-1, keepdims=True)
        i1 = jnp.min(jnp.where(probs == p1, lane_f, float(LANES)), axis=-1, keepdims=True)
        rest = jnp.where(lane_f == i1, -1.0, probs)
        p2 = jnp.max(rest, axis=-1, keepdims=True)
        i2 = jnp.min(jnp.where(rest == p2, lane_f, float(LANES)), axis=-1, keepdims=True)
        tot = p1 + p2
        idx_ref[pl.ds(r0, rc), :] = jnp.where(lane == 0, i1, jnp.where(lane == 1, i2, 0.0)).astype(I32)
        gate_ref[pl.ds(r0, rc), :] = jnp.where(lane == 0, p1 / tot,
                                                jnp.where(lane == 1, p2 / tot, 0.0))
        return carry
    lax.fori_loop(0, tm // rc, chunk, 0)


def _router(x, g, w_router_pad, *, n_exp, tm):
    m, d = x.shape
    rc = _pick(tm, (256, 176, 128, 64, 32, 16, 8))
    est = 2 * _nbytes((tm, d), F32) + 2 * _nbytes((tm, d // 2), U32) + 16 * _nbytes((rc, d), F32)
    return pl.pallas_call(
        functools.partial(_router_body, n_exp=n_exp, rc=rc),
        out_shape=(jax.ShapeDtypeStruct((m, d // 2), U32),
                   jax.ShapeDtypeStruct((m, LANES), I32),
                   jax.ShapeDtypeStruct((m, LANES), F32)),
        grid=(m // tm,),
        in_specs=[pl.BlockSpec((tm, d), lambda i: (i, 0)),
                  pl.BlockSpec((1, d), lambda i: (0, 0)),
                  pl.BlockSpec((d, LANES), lambda i: (0, 0))],
        out_specs=(pl.BlockSpec((tm, d // 2), lambda i: (i, 0)),
                   pl.BlockSpec((tm, LANES), lambda i: (i, 0)),
                   pl.BlockSpec((tm, LANES), lambda i: (i, 0))),
        compiler_params=pltpu.CompilerParams(
            dimension_semantics=("parallel",), vmem_limit_bytes=_vmem_limit(est)),
        name="moe_router",
    )(x, g, w_router_pad)


GATHER_UNROLL = 8


def _row_copy(src_hbm, row, dst, r, sem):
    return pltpu.make_async_copy(src_hbm.at[pl.ds(row, 1), :], dst.at[pl.ds(r, 1), :], sem)


def _dispatch_body(nt_ref, tok_ref, nxt_ref, pk_hbm, a_ref, buf, sem, *, tm):
    i = pl.program_id(0)
    nt = nt_ref[0]
    half = buf.shape[2]

    def request(ids_ref, slot):
        def issue(c, carry):
            for u in range(GATHER_UNROLL):
                r = c * GATHER_UNROLL + u
                _row_copy(pk_hbm, ids_ref[0, 0, r], buf.at[slot], r, sem.at[slot]).start(
                    priority=u % 2)
            return carry
        lax.fori_loop(0, tm // GATHER_UNROLL, issue, 0)

    @pl.when(i == 0)
    def _():
        request(tok_ref, 0)

    @pl.when(i + 1 < nt)
    def _():
        request(nxt_ref, (i + 1) % 2)

    @pl.when(i < nt)
    def _():
        slot = i % 2
        pltpu.make_async_copy(pk_hbm.at[pl.ds(0, tm), :], buf.at[slot], sem.at[slot]).wait()
        pk = buf[slot]
        lo = pltpu.bitcast(lax.shift_left(pk, jnp.uint32(16)), F32)
        hi = pltpu.bitcast(pk & jnp.uint32(0xFFFF0000), F32)
        a_ref[:, :half] = lo.astype(BF16)
        a_ref[:, half:] = hi.astype(BF16)


def _dispatch(n_tiles, tok_sorted, packed, *, tm, r_max):
    m, half = packed.shape
    t_max = r_max // tm
    assert tm % GATHER_UNROLL == 0
    tok_tiles = tok_sorted.reshape(t_max, 1, tm)
    return pl.pallas_call(
        functools.partial(_dispatch_body, tm=tm),
        out_shape=jax.ShapeDtypeStruct((r_max, 2 * half), BF16),
        grid_spec=pltpu.PrefetchScalarGridSpec(
            num_scalar_prefetch=1,
            grid=(t_max,),
            in_specs=[pl.BlockSpec((1, 1, tm), lambda i, nt: (i, 0, 0), memory_space=pltpu.SMEM),
                      pl.BlockSpec((1, 1, tm), lambda i, nt: (jnp.minimum(i + 1, t_max - 1), 0, 0),
                                   memory_space=pltpu.SMEM),
                      pl.BlockSpec(memory_space=pl.ANY)],
            out_specs=pl.BlockSpec((tm, 2 * half), lambda i, nt: (jnp.minimum(i, nt[0] - 1), 0)),
            scratch_shapes=[pltpu.VMEM((2, tm, half), U32), pltpu.SemaphoreType.DMA((2,))]),
        compiler_params=pltpu.CompilerParams(dimension_semantics=("arbitrary",)),
        name="moe_dispatch",
    )(n_tiles, tok_tiles, tok_tiles, packed)


def _expert_up_body(te_ref, nt_ref, a_ref, w1_ref, w3_ref, h_ref, w1bf, w3bf, *, kc):
    i = pl.program_id(1)
    d = a_ref.shape[1]
    fresh = jnp.logical_or(i == 0, te_ref[i] != te_ref[jnp.maximum(i - 1, 0)])

    @pl.when(jnp.logical_and(fresh, i < nt_ref[0]))
    def _():
        def cast(c, carry):
            r0 = pl.multiple_of(c * kc, kc)
            w1bf[pl.ds(r0, kc), :] = w1_ref[pl.ds(r0, kc), :].astype(BF16)
            w3bf[pl.ds(r0, kc), :] = w3_ref[pl.ds(r0, kc), :].astype(BF16)
            return carry
        lax.fori_loop(0, d // kc, cast, 0)

    @pl.when(i < nt_ref[0])
    def _():
        a = a_ref[...]
        h_ref[...] = (jax.nn.silu(_dot(a, w1bf[...])) * _dot(a, w3bf[...])).astype(h_ref.dtype)


def _expert_up(tile_expert, n_tiles, a_sorted, w1, w3, *, tm, tf):
    r_max, d = a_sorted.shape
    n_exp, _, fe = w1.shape
    t_max = r_max // tm
    kc = _pick(d, (512, 256, 128))
    row = lambda f, i, te, nt: jnp.minimum(i, nt[0] - 1)
    est = (2 * _nbytes((tm, d), BF16) + 2 * 2 * _nbytes((d, tf), F32) + 2 * _nbytes((d, tf), BF16)
           + 2 * _nbytes((tm, tf), BF16) + 6 * _nbytes((tm, tf), F32))
    return pl.pallas_call(
        functools.partial(_expert_up_body, kc=kc),
        out_shape=jax.ShapeDtypeStruct((r_max, fe), BF16),
        grid_spec=pltpu.PrefetchScalarGridSpec(
            num_scalar_prefetch=2,
            grid=(fe // tf, t_max),
            in_specs=[pl.BlockSpec((tm, d), lambda f, i, te, nt: (row(f, i, te, nt), 0)),
                      pl.BlockSpec((None, d, tf), lambda f, i, te, nt: (te[i], 0, f)),
                      pl.BlockSpec((None, d, tf), lambda f, i, te, nt: (te[i], 0, f))],
            out_specs=pl.BlockSpec((tm, tf), lambda f, i, te, nt: (row(f, i, te, nt), f)),
            scratch_shapes=[pltpu.VMEM((d, tf), BF16), pltpu.VMEM((d, tf), BF16)]),
        compiler_params=pltpu.CompilerParams(
            dimension_semantics=("arbitrary", "arbitrary"), vmem_limit_bytes=_vmem_limit(est)),
        name="moe_expert_up",
    )(tile_expert, n_tiles, a_sorted, w1, w3)


def _expert_down_body(te_ref, nt_ref, h_ref, w2_ref, y_ref, w2bf, *, kc):
    i = pl.program_id(1)
    fe = h_ref.shape[1]
    fresh = jnp.logical_or(i == 0, te_ref[i] != te_ref[jnp.maximum(i - 1, 0)])

    @pl.when(jnp.logical_and(fresh, i < nt_ref[0]))
    def _():
        def cast(c, carry):
            r0 = pl.multiple_of(c * kc, kc)
            w2bf[pl.ds(r0, kc), :] = w2_ref[pl.ds(r0, kc), :].astype(BF16)
            return carry
        lax.fori_loop(0, fe // kc, cast, 0)

    @pl.when(i < nt_ref[0])
    def _():
        y_ref[...] = _dot(h_ref[...], w2bf[...])


def _expert_down(tile_expert, n_tiles, h_sorted, w2, *, tm, tn):
    r_max, fe = h_sorted.shape
    d = w2.shape[2]
    t_max = r_max // tm
    kc = _pick(fe, (512, 256, 128))
    row = lambda n, i, te, nt: jnp.minimum(i, nt[0] - 1)
    est = (2 * _nbytes((tm, fe), BF16) + 2 * _nbytes((fe, tn), F32) + _nbytes((fe, tn), BF16)
           + 4 * _nbytes((tm, tn), F32))
    return pl.pallas_call(
        functools.partial(_expert_down_body, kc=kc),
        out_shape=jax.ShapeDtypeStruct((r_max, d), F32),
        grid_spec=pltpu.PrefetchScalarGridSpec(
            num_scalar_prefetch=2,
            grid=(d // tn, t_max),
            in_specs=[pl.BlockSpec((tm, fe), lambda n, i, te, nt: (row(n, i, te, nt), 0)),
                      pl.BlockSpec((None, fe, tn), lambda n, i, te, nt: (te[i], 0, n))],
            out_specs=pl.BlockSpec((tm, tn), lambda n, i, te, nt: (row(n, i, te, nt), n)),
            scratch_shapes=[pltpu.VMEM((fe, tn), BF16)]),
        compiler_params=pltpu.CompilerParams(
            dimension_semantics=("arbitrary", "arbitrary"), vmem_limit_bytes=_vmem_limit(est)),
        name="moe_expert_down",
    )(tile_expert, n_tiles, h_sorted, w2)


def _combine_body(slot_ref, nxt_ref, x_ref, gate_ref, y_hbm, *rest, tc, n_first):
    o_refs, (buf, sem) = rest[:-2], rest[-2:]
    i = pl.program_id(0)
    n = pl.num_programs(0)
    unroll = GATHER_UNROLL // TOP_K

    def request(ids_ref, slot):
        def issue(c, carry):
            for u in range(unroll):
                r = c * unroll + u
                for k in range(TOP_K):
                    _row_copy(y_hbm, ids_ref[0, 0, TOP_K * r + k], buf.at[slot, k], r,
                              sem.at[slot]).start(priority=k % 2)
            return carry
        lax.fori_loop(0, tc // unroll, issue, 0)

    @pl.when(i == 0)
    def _():
        request(slot_ref, 0)

    @pl.when(i + 1 < n)
    def _():
        request(nxt_ref, (i + 1) % 2)

    slot = i % 2
    for k in range(TOP_K):
        pltpu.make_async_copy(y_hbm.at[pl.ds(0, tc), :], buf.at[slot, k], sem.at[slot]).wait()
    g = gate_ref[...]
    val = x_ref[...] + (g[:, 0:1] * buf[slot, 0] + g[:, 1:2] * buf[slot, 1])
    if len(o_refs) == 1:
        o_refs[0][...] = val
    else:
        @pl.when(i < n_first)
        def _():
            o_refs[0][...] = val

        @pl.when(i >= n_first)
        def _():
            o_refs[1][...] = val


def _combine(slots, x, gates, y_sorted, *, tc, split=None):
    m, d = x.shape
    n = m // tc
    assert tc % (GATHER_UNROLL // TOP_K) == 0
    est = 4 * _nbytes((tc, d), F32) + 2 * TOP_K * _nbytes((tc, d), F32) + 6 * _nbytes((tc, d), F32)
    slot_tiles = slots.reshape(n, 1, TOP_K * tc)
    if split is None or split % tc or (m - split) % tc:
        n_first = n
        out_shape = jax.ShapeDtypeStruct((m, d), F32)
        out_specs = pl.BlockSpec((tc, d), lambda i: (i, 0))
    else:
        n_first = split // tc
        out_shape = (jax.ShapeDtypeStruct((split, d), F32), jax.ShapeDtypeStruct((m - split, d), F32))
        out_specs = (pl.BlockSpec((tc, d), lambda i: (jnp.minimum(i, n_first - 1), 0)),
                     pl.BlockSpec((tc, d), lambda i: (jnp.maximum(i - n_first, 0), 0)))
    return pl.pallas_call(
        functools.partial(_combine_body, tc=tc, n_first=n_first),
        out_shape=out_shape,
        grid=(n,),
        in_specs=[pl.BlockSpec((1, 1, TOP_K * tc), lambda i: (i, 0, 0), memory_space=pltpu.SMEM),
                  pl.BlockSpec((1, 1, TOP_K * tc), lambda i: (jnp.minimum(i + 1, n - 1), 0, 0),
                               memory_space=pltpu.SMEM),
                  pl.BlockSpec((tc, d), lambda i: (i, 0)),
                  pl.BlockSpec((tc, LANES), lambda i: (i, 0)),
                  pl.BlockSpec(memory_space=pl.ANY)],
        out_specs=out_specs,
        scratch_shapes=[pltpu.VMEM((2, TOP_K, tc, d), F32), pltpu.SemaphoreType.DMA((2,))],
        compiler_params=pltpu.CompilerParams(
            dimension_semantics=("arbitrary",), vmem_limit_bytes=_vmem_limit(est)),
        name="moe_combine",
    )(slot_tiles, slot_tiles, x, gates, y_sorted)


def _moe(x, g_ffn, w_router, w_e1, w_e3, w_e2, *, tm_tok, split=None):
    m, d = x.shape
    n_exp, _, fe = w_e1.shape
    tm = _pick(m * TOP_K, (512, 256, 128, 64, 32, 16))
    wr = jnp.zeros((d, LANES), F32).at[:, :n_exp].set(w_router)
    packed, idx128, gate128 = _router(x, g_ffn, wr, n_exp=n_exp, tm=tm_tok)

    idx = idx128[:, :TOP_K]
    mask = jnp.sum(idx[:, :, None] == jnp.arange(n_exp, dtype=I32)[None, None, :], axis=1).astype(I32)
    counts = jnp.sum(mask, axis=0)
    padded = ((counts + tm - 1) // tm) * tm
    ends = jnp.cumsum(padded)
    starts = ends - padded
    pos = jnp.cumsum(mask, axis=0) - mask
    slot = starts[idx] + jnp.take_along_axis(pos, idx, axis=1)
    t_max = (m * TOP_K) // tm + n_exp
    r_max = t_max * tm
    tok_sorted = jnp.zeros((r_max,), I32).at[slot.reshape(-1)].set(
        jnp.repeat(jnp.arange(m, dtype=I32), TOP_K))
    n_tiles = (ends[-1] // tm).astype(I32).reshape(1)
    tile_start = jnp.minimum(jnp.arange(t_max, dtype=I32), n_tiles[0] - 1) * tm
    tile_expert = jnp.minimum(jnp.sum(ends[None, :] <= tile_start[:, None], axis=1),
                              n_exp - 1).astype(I32)

    a_sorted = _dispatch(n_tiles, tok_sorted, packed, tm=tm, r_max=r_max)
    tf = _pick(fe, (256, 128))
    h_sorted = _expert_up(tile_expert, n_tiles, a_sorted, w_e1, w_e3, tm=tm, tf=tf)
    y_sorted = _expert_down(tile_expert, n_tiles, h_sorted, w_e2, tm=tm, tn=_pick(d, (512, 256, 128)))
    tc = _pick(m, (256, 128, 64, 32, 16, 8))
    return _combine(slot, x, gate128, y_sorted, tc=tc, split=split)


def _block_diag(blocks):
    g, r, c = blocks.shape
    eye = jnp.eye(g, dtype=blocks.dtype)
    return (blocks[:, :, None, :] * eye[:, None, :, None]).reshape(g * r, g * c)


def _rel_table(rel_bias_l, q_pos, k_pos, max_rel):
    rel = np.clip(q_pos[:, None] - k_pos[None, :], -(CHUNK - 1), max_rel) + (CHUNK - 1)
    return rel_bias_l[:, rel].astype(F32)


def _band_table(rel_bias_l, tq, max_rel):
    n_heads = rel_bias_l.shape[0]
    past = BAND_PAST * CHUNK
    win = past + tq
    ring = -(-(tq + win - 1) // LANES) * LANES
    diff = np.arange(ring)
    diff = np.where(diff < tq, diff, diff - ring)
    idx = np.clip(diff + past, -(CHUNK - 1), max_rel) + (CHUNK - 1)
    by_diff = jnp.pad(rel_bias_l.astype(F32)[:, idx], ((0, 16 - n_heads), (0, 0)))

    def body(u_ref, o_ref):
        hp = pl.program_id(0)
        j = lax.broadcasted_iota(I32, (win, tq), 0)
        i = lax.broadcasted_iota(I32, (win, tq), 1)
        gap = (i + past) // CHUNK - j // CHUNK
        valid = (gap >= 0) & (gap <= BAND_PAST)
        for e in range(2):
            rows = jnp.broadcast_to(u_ref[pl.ds(2 * hp + e, 1), :], (win, ring))
            shifted = pltpu.roll(rows, 0, 1, stride=1, stride_axis=0)
            o_ref[:, e * tq:(e + 1) * tq] = jnp.where(valid, shifted[:, 0:tq] * LOG2E, NEG_INF)

    return pl.pallas_call(
        body,
        out_shape=jax.ShapeDtypeStruct((n_heads // 2, win, 2 * tq), F32),
        grid=(n_heads // 2,),
        in_specs=[pl.BlockSpec((16, ring), lambda hp: (0, 0))],
        out_specs=pl.BlockSpec((None, win, 2 * tq), lambda hp: (hp, 0, 0)),
        compiler_params=pltpu.CompilerParams(dimension_semantics=("parallel",)),
        name="band_table",
    )(by_diff)


def _tile_gain(g):
    return jnp.tile(g.astype(F32), LANES // g.shape[0]).reshape(1, LANES)


def kernel(x_prompt, x_sample, mem_prompt, cache_fox_k, cache_fox_v, cache_fox_logf, cache_band_k, cache_band_v, state_ssm_re, state_ssm_im, cache_mem_k, cache_mem_v, g_mix, w_in, b_f, g_qa, g_ka, g_qb, g_kb, rel_bias, lam_re, lam_im, log_dt, ssm_b_re, ssm_b_im, ssm_c_re, ssm_c_im, ssm_d, w_glu, b_glu, g_mix_out, w_out, g_cross, g_mem, w_cq, w_ck, w_cv, g_cq, g_ck, w_co, g_ffn, w_ff1, w_ff3, w_ff2, w_router, w_e1, w_e3, w_e2):
    batch, seq, d = x_prompt.shape
    dbatch, dseq, _ = x_sample.shape
    depth = g_mix.shape[0]
    past_len, h_a, dh = cache_fox_k.shape[2:]
    band_rows, h_b = cache_band_k.shape[2:4]
    g_c, p_state = lam_re.shape[1:]
    w_a, w_b, w_c = h_a * dh, h_b * dh, g_c * SSM_GROUP
    n_mem, h_m, dh_m = cache_mem_k.shape[2:]
    w_m = h_m * dh_m
    max_rel = rel_bias.shape[2] - CHUNK
    assert dh == 64 and h_a % 2 == 0 and h_b % 2 == 0 and dh_m == LANES
    assert w_a == w_b and w_c <= w_a and h_a <= 16
    mp, ms = batch * seq, dbatch * dseq
    m = mp + ms
    pa, pb = h_a // 2, h_b // 2
    n_state = g_c * p_state
    n_blk = 2 if (w_c % 512 == 0) else 1
    nband = min(BAND_PAST * CHUNK, seq)
    tm = _pick(m, (1056, 1024, 768, 512, 256, 128, 64, 32, 16))
    tq_band = _pick(seq, (256, 128, 64))

    x = jnp.concatenate([x_prompt.reshape(mp, d), x_sample.reshape(ms, d)], axis=0)
    fox_kt, fox_vt, band_kt, band_vt = (jnp.transpose(c, (0, 1, 3, 4, 2)) for c in
                                        (cache_fox_k, cache_fox_v, cache_band_k, cache_band_v))

    outs = {k: [] for k in ("p_fl", "p_sr", "p_si", "p_mk", "p_mv",
                            "s_fk", "s_fv", "s_fl", "s_bk", "s_bv", "s_sr", "s_si")}
    cache_a = cache_b = None
    for l in range(depth):
        sizes = (w_a, w_a, w_a, h_a, w_b, w_b, w_b)
        cuts = [sum(sizes[:i]) for i in range(len(sizes) + 1)]
        w_cat = _regroup_columns(
            w_in, l, ((0, cuts[3]), (cuts[4], cuts[7]), (cuts[7], w_in.shape[2]), (cuts[3], cuts[4])),
            7 * w_a)
        proj = _mm([x], [w_cat], gain=g_mix[l].reshape(1, d), tm=tm, tn=w_a, name="proj_in")
        col_uc = 6 * w_a

        fa_p = proj[:mp, col_uc + w_c:col_uc + w_c + h_a].reshape(batch, seq, h_a)
        fa_p = jnp.transpose(fa_p, (0, 2, 1)).reshape(batch * h_a, seq)
        bias_p = jnp.broadcast_to(jnp.tile(b_f[l], batch)[:, None], (batch * h_a, LANES))
        lf_p, f_p = _logf_cumsum(fa_p, bias_p, raw_from=0, valid_to=seq)
        lf_p = lf_p.reshape(batch, h_a, seq)
        f_p = f_p.reshape(batch, h_a, seq)
        f_t = (jnp.transpose(f_p, (0, 2, 1)) * LOG2E).reshape(batch, seq, pa, 2, 1)
        f_aug = jnp.tile(jnp.broadcast_to(f_t, (batch, seq, pa, 2, 3)).reshape(batch, seq, pa, 6),
                         (1, 1, 1, 2))
        f_aug = jnp.pad(f_aug, ((0, 0), (0, 0), (0, 0), (0, LANES - 12))).reshape(mp, pa * LANES)

        t_all = -(-(past_len + dseq) // LANES) * LANES
        fa_s = proj[mp:, col_uc + w_c:col_uc + w_c + h_a].reshape(dbatch, dseq, h_a)
        x_s = jnp.concatenate([jnp.transpose(cache_fox_logf[l], (0, 2, 1)),
                               jnp.transpose(fa_s, (0, 2, 1)),
                               jnp.zeros((dbatch, h_a, t_all - past_len - dseq), F32)], axis=2)
        bias_s = jnp.broadcast_to(jnp.tile(b_f[l], dbatch)[:, None], (dbatch * h_a, LANES))
        lf_s, f_s = _logf_cumsum(x_s.reshape(dbatch * h_a, t_all), bias_s,
                                 raw_from=past_len, valid_to=past_len + dseq)
        lf_s = lf_s.reshape(dbatch, h_a, t_all)[:, :, past_len:past_len + dseq]
        f_s = f_s.reshape(dbatch, h_a, t_all)
        fs_row = jnp.pad(f_s, ((0, 0), (0, 16 - h_a), (0, 0)))
        fs_col = jnp.pad(jnp.transpose(f_s[:, :, past_len:past_len + dseq], (0, 2, 1)),
                         ((0, 0), (0, 0), (0, LANES - h_a))).reshape(ms, LANES)

        gqa, gka = _tile_gain(g_qa[l]), _tile_gain(g_ka[l])
        oa, *cache_a = _fox_prompt(proj, f_aug, gqa, gka, batch=batch, seq=seq, n_pairs=pa,
                                   col_q=0, col_k=pa, col_v=2 * pa, m_rows=m, layer=l, depth=depth,
                                   prev=cache_a)
        oa, kn_as = _sample_attn(proj, fox_kt, fox_vt, l, fs_col, fs_row,
                                 gqa, gka, oa, mode="fox", batch=dbatch, s_new=dseq, n_pairs=pa,
                                 col_q=0, row0=mp)

        gqb, gkb = _tile_gain(g_qb[l]), _tile_gain(g_kb[l])
        tab = _band_table(rel_bias[l], tq_band, max_rel)
        ob, *cache_b = _band_prompt(proj, tab, gqb, gkb, batch=batch, seq=seq, n_pairs=pb,
                                    col_q=3 * pa, col_k=3 * pa + pb, col_v=3 * pa + 2 * pb,
                                    tq=tq_band, m_rows=m, layer=l, depth=depth, prev=cache_b)
        tab_s = _rel_table(rel_bias[l], band_rows + np.arange(dseq), np.arange(band_rows + dseq),
                           max_rel)
        ob, kn_bs = _sample_attn(proj, band_kt, band_vt, l,
                                 tab_s[:, :, :band_rows], tab_s[:, :, band_rows:], gqb, gkb, ob,
                                 mode="band", batch=dbatch, s_new=dseq, n_pairs=pb, col_q=3, row0=mp)

        rep = lambda a: jnp.repeat(a.astype(F32), SSM_GROUP, axis=0)
        a_re, a_im, bb_re, bb_im = _s5_prep(
            rep(lam_re[l]), rep(lam_im[l]),
            jnp.broadcast_to(rep(log_dt[l])[:, None], (g_c * SSM_GROUP, p_state)),
            jnp.transpose(ssm_b_re[l], (0, 2, 1)).reshape(g_c * SSM_GROUP, p_state),
            jnp.transpose(ssm_b_im[l], (0, 2, 1)).reshape(g_c * SSM_GROUP, p_state))
        a_re = a_re.reshape(g_c, SSM_GROUP, p_state)[:, 0, :].reshape(1, n_state)
        a_im = a_im.reshape(g_c, SSM_GROUP, p_state)[:, 0, :].reshape(1, n_state)
        b_re_d = _block_diag(bb_re.reshape(g_c, SSM_GROUP, p_state))
        b_im_d = _block_diag(bb_im.reshape(g_c, SSM_GROUP, p_state))
        c_re_d = _block_diag(jnp.transpose(ssm_c_re[l], (0, 2, 1)))
        c_im_d = _block_diag(jnp.transpose(ssm_c_im[l], (0, 2, 1)))
        d_row = ssm_d[l].reshape(1, w_c)
        s5_args = (a_re, a_im, b_re_d, b_im_d, c_re_d, c_im_d, d_row, w_glu[l],
                   b_glu[l].reshape(1, w_c))
        uc = proj[:, col_uc:col_uc + w_c]
        u_p = jnp.transpose(uc[:mp].reshape(batch, seq, w_c), (1, 0, 2)).reshape(mp, w_c)
        u_s = jnp.transpose(uc[mp:].reshape(dbatch, dseq, w_c), (1, 0, 2)).reshape(ms, w_c)
        zeros_p = jnp.zeros((batch, n_state), F32)
        oc_p, sr_p, si_p = _s5(u_p, zeros_p, zeros_p, *s5_args, nb=batch, seq=seq, n_blk=n_blk,
                               passes=1)
        oc_s, sr_s, si_s = _s5(u_s, state_ssm_re[l].reshape(dbatch, n_state),
                               state_ssm_im[l].reshape(dbatch, n_state), *s5_args,
                               nb=dbatch, seq=dseq, n_blk=n_blk, passes=3)
        oc = jnp.concatenate(
            [jnp.transpose(oc_p.reshape(seq, batch, w_c), (1, 0, 2)).reshape(mp, w_c),
             jnp.transpose(oc_s.reshape(dseq, dbatch, w_c), (1, 0, 2)).reshape(ms, w_c)], axis=0)

        x = _mm([oa, ob, oc], [w_out[l].astype(BF16)], gain=g_mix_out[l].reshape(1, -1), residual=x,
                tm=tm, tn=_pick(d, (1024, 512, 256, 128)), name="merge_out")

        outs["p_fl"].append(jnp.transpose(lf_p, (0, 2, 1)))
        outs["p_sr"].append(sr_p.reshape(batch, g_c, p_state))
        outs["p_si"].append(si_p.reshape(batch, g_c, p_state))
        outs["s_fk"].append(kn_as.reshape(dbatch, dseq, h_a, dh))
        outs["s_fv"].append(proj[mp:, 2 * w_a:3 * w_a].reshape(dbatch, dseq, h_a, dh))
        outs["s_fl"].append(jnp.transpose(lf_s, (0, 2, 1)))
        outs["s_bk"].append(kn_bs.reshape(dbatch, dseq, h_b, dh))
        outs["s_bv"].append(proj[mp:, 5 * w_a:6 * w_a].reshape(dbatch, dseq, h_b, dh))
        outs["s_sr"].append(sr_s.reshape(dbatch, g_c, p_state))
        outs["s_si"].append(si_s.reshape(dbatch, g_c, p_state))

        mem2 = mem_prompt.reshape(batch * n_mem, d)
        tmm = _pick(batch * n_mem, (1024, 512, 256, 128))
        gck = jnp.tile(g_ck[l].astype(F32), h_m).reshape(1, w_m)
        gcq = jnp.tile(g_cq[l].astype(F32), h_m).reshape(1, w_m)
        mk = _mm([mem2], [w_ck[l]], gain=g_mem[l].reshape(1, d), group_gain=gck,
                 epilogue="group_norm", tm=tmm, tn=w_m, name="mem_k")
        mv = _mm([mem2], [w_cv[l]], gain=g_mem[l].reshape(1, d), tm=tmm, tn=w_m, name="mem_v")
        outs["p_mk"].append(mk.reshape(batch, n_mem, h_m, dh_m))
        outs["p_mv"].append(mv.reshape(batch, n_mem, h_m, dh_m))
        q_c = _mm([x], [w_cq[l].astype(BF16)], gain=g_cross[l].reshape(1, d), group_gain=gcq,
                  epilogue="group_norm", tm=tm, tn=w_m, name="cross_q")
        o_c = _cross_attn(q_c, mk, mv, None, batch=batch, q_len=seq, n_mem=n_mem, n_heads=h_m,
                          dh=dh_m, row0=0, m_rows=m)
        o_c = _cross_attn(q_c, cache_mem_k[l].reshape(dbatch * n_mem, w_m),
                          cache_mem_v[l].reshape(dbatch * n_mem, w_m), o_c, batch=dbatch,
                          q_len=dseq, n_mem=n_mem, n_heads=h_m, dh=dh_m, row0=mp, m_rows=m)
        x = _mm([o_c], [w_co[l].astype(BF16)], residual=x, tm=tm,
                tn=_pick(d, (1024, 512, 256, 128)), name="cross_out")

        i = l // 2
        if l % 2 == 0:
            hmid = _mm([x], [w_ff1[i].astype(BF16), w_ff3[i].astype(BF16)],
                       gain=g_ffn[l].reshape(1, d), epilogue="swiglu", out_dtype=BF16, tm=tm,
                       tn=_pick(w_ff1.shape[2], (512, 256, 128)), name="ffn_up")
            x = _mm([hmid], [w_ff2[i].astype(BF16)], residual=x, tm=tm,
                    tn=_pick(d, (512, 256, 128)), name="ffn_down")
        else:
            x = _moe(x, g_ffn[l].reshape(1, d), w_router[i], w_e1[i], w_e3[i], w_e2[i], tm_tok=tm,
                     split=mp if l == depth - 1 else None)

    st = lambda k: jnp.stack(outs[k])
    xp, xs = x if isinstance(x, tuple) else (x[:mp], x[mp:])
    per_head = lambda t, h: jnp.transpose(t.reshape(depth, batch, h, dh, -1), (0, 1, 4, 2, 3))
    return (xp.reshape(batch, seq, d), xs.reshape(dbatch, dseq, d),
            per_head(cache_a[0], h_a), per_head(cache_a[1], h_a), st("p_fl"),
            per_head(cache_b[0][..., seq - nband:], h_b), per_head(cache_b[1][..., seq - nband:], h_b),
            st("p_sr"), st("p_si"),
            st("p_mk"), st("p_mv"), st("s_fk"), st("s_fv"), st("s_fl"), st("s_bk"), st("s_bv"),
            st("s_sr"), st("s_si"))
```

```python
import functools
import math

import jax
import jax.numpy as jnp
import numpy as np
from jax import lax
from jax.experimental import pallas as pl
from jax.experimental.pallas import tpu as pltpu

F32 = jnp.float32
BF16 = jnp.bfloat16
I32 = jnp.int32
U32 = jnp.uint32

EPS = 1e-6
NEG_INF = -1e30
LOG2E = math.log2(math.e)
CHUNK = 64
BAND_PAST = 8
SSM_GROUP = 16
TOP_K = 2

LANES = 128
SUBLANES = 8
BF16_ROWS = 16
VMEM_CAP = 60 * 1024 * 1024


def _vmem_limit(nbytes):
    return int(min(VMEM_CAP, max(16 * 1024 * 1024, nbytes * 5 // 4 + (4 << 20))))


def _pick(n, candidates):
    for c in candidates:
        if c <= n and n % c == 0:
            return c
    raise ValueError(f"no tile for {n} in {candidates}")


def _nbytes(shape, dtype):
    return math.prod(shape) * jnp.dtype(dtype).itemsize


def _split3(x):
    hi = x.astype(BF16)
    r1 = x - hi.astype(F32)
    mid = r1.astype(BF16)
    lo = (r1 - mid.astype(F32)).astype(BF16)
    return hi, mid, lo


def _dot(a, b):
    return jnp.dot(a, b, preferred_element_type=F32)


def _dot_nt(a, b):
    return lax.dot_general(a, b, (((1,), (1,)), ((), ())), preferred_element_type=F32)


def _dot_hp(a, b, passes=3):
    ah = a.astype(BF16)
    bh = b.astype(BF16)
    if passes == 1:
        return _dot(ah, bh)
    al = (a - ah.astype(F32)).astype(BF16)
    bl = (b - bh.astype(F32)).astype(BF16)
    return _dot(ah, bh) + (_dot(ah, bl) + _dot(al, bh))


def _pair_ones():
    r = lax.broadcasted_iota(I32, (LANES, LANES), 0) // 64
    c = lax.broadcasted_iota(I32, (LANES, LANES), 1) // 64
    return (r == c).astype(BF16)


def _pair_norm(x, g, ones):
    sq = x * x
    hi = sq.astype(BF16)
    lo = (sq - hi.astype(F32)).astype(BF16)
    ss = _dot(hi, ones) + _dot(lo, ones)
    return x * lax.rsqrt(ss * (1.0 / 64.0) + EPS) * g


def _log_sigmoid(x):
    return jnp.minimum(x, 0.0) - jnp.log(1.0 + jnp.exp(-jnp.abs(x)))


def _gelu_tanh(x):
    c = math.sqrt(2.0 / math.pi)
    return 0.5 * x * (1.0 + jnp.tanh(c * (x + 0.044715 * (x * x * x))))


def _mm_body(*refs, widths, norm, n_w, epilogue, has_res, has_gg, stage_a, tm, rc, kc, k_total):
    it = iter(refs)
    a_refs = [next(it) for _ in widths]
    g_ref = next(it) if norm else None
    w_refs = [next(it) for _ in range(n_w)]
    gg_ref = next(it) if has_gg else None
    res_ref = next(it) if has_res else None
    o_ref = next(it)
    abf = next(it) if stage_a else a_refs[0]
    cast_w = w_refs[0].dtype != BF16
    wbfs = [next(it) for _ in range(n_w)] if cast_w else w_refs
    j = pl.program_id(1)

    if stage_a:
        @pl.when(j == 0)
        def _():
            off = 0
            for a_ref, wd in zip(a_refs, widths):
                def chunk(c, carry, a_ref=a_ref, off=off, wd=wd):
                    r0 = pl.multiple_of(c * rc, rc)
                    x = a_ref[pl.ds(r0, rc), :].astype(F32)
                    if norm:
                        ms = jnp.mean(x * x, axis=-1, keepdims=True)
                        x = x * lax.rsqrt(ms + EPS) * g_ref[:, off:off + wd]
                    abf[pl.ds(r0, rc), off:off + wd] = x.astype(BF16)
                    return carry
                lax.fori_loop(0, tm // rc, chunk, 0)
                off += wd

    if cast_w:
        for w_ref, wbf in zip(w_refs, wbfs):
            def cast(c, carry, w_ref=w_ref, wbf=wbf):
                r0 = pl.multiple_of(c * kc, kc)
                wbf[pl.ds(r0, kc), :] = w_ref[pl.ds(r0, kc), :].astype(BF16)
                return carry
            lax.fori_loop(0, k_total // kc, cast, 0)

    def rows(c, carry):
        r0 = pl.multiple_of(c * rc, rc)
        a = abf[pl.ds(r0, rc), :]
        ys = [_dot(a, wbf[...]) for wbf in wbfs]
        if epilogue == "swiglu":
            y = jax.nn.silu(ys[0]) * ys[1]
        elif epilogue == "group_norm":
            parts = []
            for s in range(ys[0].shape[1] // LANES):
                ysl = ys[0][:, s * LANES:(s + 1) * LANES]
                ms = jnp.mean(ysl * ysl, axis=-1, keepdims=True)
                parts.append(ysl * lax.rsqrt(ms + EPS) * gg_ref[:, s * LANES:(s + 1) * LANES])
            y = jnp.concatenate(parts, axis=1)
        else:
            y = ys[0]
        if has_res:
            y = y + res_ref[pl.ds(r0, rc), :]
        o_ref[pl.ds(r0, rc), :] = y.astype(o_ref.dtype)
        return carry
    lax.fori_loop(0, tm // rc, rows, 0)


def _mm(a_parts, w_list, *, gain=None, group_gain=None, residual=None, epilogue="none",
        out_dtype=F32, tm, tn, name):
    m = a_parts[0].shape[0]
    widths = tuple(a.shape[1] for a in a_parts)
    k_total = sum(widths)
    n = w_list[0].shape[1]
    assert m % tm == 0 and n % tn == 0, (m, tm, n, tn)
    norm = gain is not None
    stage_a = norm or len(a_parts) > 1 or a_parts[0].dtype != BF16
    rc = _pick(tm, (512, 384, 352, 256, 176, 128, 64, 32, 16))
    kc = _pick(k_total, (512, 256, 128))
    grid = (m // tm, n // tn)
    in_specs = [pl.BlockSpec((tm, wd), lambda i, j: (i, 0)) for wd in widths]
    args = list(a_parts)
    est = sum(2 * _nbytes((tm, wd), a.dtype) for wd, a in zip(widths, a_parts))
    if norm:
        in_specs.append(pl.BlockSpec((1, k_total), lambda i, j: (0, 0)))
        args.append(gain)
    cast_w = w_list[0].dtype != BF16
    for w in w_list:
        in_specs.append(pl.BlockSpec((k_total, tn), lambda i, j: (0, j)))
        args.append(w)
        est += 2 * _nbytes((k_total, tn), w.dtype) + cast_w * _nbytes((k_total, tn), BF16)
    if group_gain is not None:
        in_specs.append(pl.BlockSpec((1, tn), lambda i, j: (0, j)))
        args.append(group_gain)
    if residual is not None:
        in_specs.append(pl.BlockSpec((tm, tn), lambda i, j: (i, j)))
        args.append(residual)
        est += 2 * _nbytes((tm, tn), F32)
    est += 2 * _nbytes((tm, tn), out_dtype) + stage_a * _nbytes((tm, k_total), BF16)
    est += 4 * _nbytes((rc, tn), F32) * len(w_list)
    body = functools.partial(
        _mm_body, widths=widths, norm=norm, n_w=len(w_list), epilogue=epilogue,
        has_res=residual is not None, has_gg=group_gain is not None, stage_a=stage_a, tm=tm,
        rc=rc, kc=kc, k_total=k_total)
    return pl.pallas_call(
        body,
        out_shape=jax.ShapeDtypeStruct((m, n), out_dtype),
        grid=grid,
        in_specs=in_specs,
        out_specs=pl.BlockSpec((tm, tn), lambda i, j: (i, j)),
        scratch_shapes=[pltpu.VMEM((tm, k_total), BF16)] * stage_a
        + [pltpu.VMEM((k_total, tn), BF16) for _ in w_list] * cast_w,
        compiler_params=pltpu.CompilerParams(
            dimension_semantics=("parallel", "arbitrary"),
            vmem_limit_bytes=_vmem_limit(est)),
        name=name,
    )(*args)


def _regroup_body(w_ref, o_ref, *, cuts, width):
    off = 0
    for lo, hi in cuts:
        o_ref[:, off:off + hi - lo] = w_ref[:, lo:hi].astype(o_ref.dtype)
        off += hi - lo
    if off < width:
        o_ref[:, off:width] = jnp.zeros((o_ref.shape[0], width - off), o_ref.dtype)


def _regroup_columns(w, layer, cuts, width):
    _, k, n = w.shape
    tr = _pick(k, (256, 128, 64, 32, 16))
    return pl.pallas_call(
        functools.partial(_regroup_body, cuts=cuts, width=width),
        out_shape=jax.ShapeDtypeStruct((k, width), BF16),
        grid=(k // tr,),
        in_specs=[pl.BlockSpec((None, tr, n), lambda i: (layer, i, 0))],
        out_specs=pl.BlockSpec((tr, width), lambda i: (i, 0)),
        compiler_params=pltpu.CompilerParams(
            dimension_semantics=("parallel",),
            vmem_limit_bytes=_vmem_limit(2 * _nbytes((tr, n), F32) + 2 * _nbytes((tr, width), BF16)
                                         + 4 * _nbytes((tr, width), F32))),
        name="regroup_w_in",
    )(w)


def _cumsum_body(x_ref, b_ref, lf_ref, f_ref, carry, *, raw_from, valid_to, tt):
    j = pl.program_id(0)

    @pl.when(j == 0)
    def _():
        carry[...] = jnp.zeros_like(carry)

    x = x_ref[...]
    lane = j * tt + lax.broadcasted_iota(I32, x.shape, 1)
    lf = jnp.where(lane >= raw_from, _log_sigmoid(x + b_ref[:, 0:1]), x)
    lf = jnp.where(lane < valid_to, lf, 0.0)
    lf_ref[...] = lf
    tri = (lax.broadcasted_iota(I32, (tt, tt), 0)
           <= lax.broadcasted_iota(I32, (tt, tt), 1)).astype(BF16)
    hi, mid, lo = _split3(lf)
    y = _dot(hi, tri) + _dot(mid, tri) + _dot(lo, tri) + carry[:, 0:1]
    f_ref[...] = y
    carry[...] = jnp.broadcast_to(y[:, tt - 1:tt], carry.shape)


def _logf_cumsum(x, bias, *, raw_from, valid_to):
    r, t = x.shape
    tt = _pick(t, (256, 128))
    return pl.pallas_call(
        functools.partial(_cumsum_body, raw_from=raw_from, valid_to=valid_to, tt=tt),
        out_shape=(jax.ShapeDtypeStruct((r, t), F32), jax.ShapeDtypeStruct((r, t), F32)),
        grid=(t // tt,),
        in_specs=[pl.BlockSpec((r, tt), lambda j: (0, j)),
                  pl.BlockSpec((r, LANES), lambda j: (0, 0))],
        out_specs=(pl.BlockSpec((r, tt), lambda j: (0, j)),
                   pl.BlockSpec((r, tt), lambda j: (0, j))),
        scratch_shapes=[pltpu.VMEM((r, LANES), F32)],
        compiler_params=pltpu.CompilerParams(dimension_semantics=("arbitrary",)),
        name="logf_cumsum",
    )(x, bias)


def _stack_heads(qn):
    lane = lax.broadcasted_iota(I32, qn.shape, 1)
    q0 = jnp.where(lane < 64, qn, 0.0)
    q1 = jnp.where(lane < 64, 0.0, qn)
    return jnp.concatenate([q0, q1], axis=0).astype(BF16)


def _unstack_heads(o, tq):
    lane = lax.broadcasted_iota(I32, (tq, LANES), 1)
    return jnp.where(lane < 64, o[:tq], o[tq:])


def _lane_column(block, h):
    lane = lax.broadcasted_iota(I32, block.shape, 1)
    return jnp.sum(jnp.where(lane == h, block, 0.0), axis=-1, keepdims=True)


V_ROWS = LANES + BF16_ROWS


def _split_select(x):
    hi, mid, lo = _split3(x)
    m3 = lax.broadcasted_iota(I32, x.shape, 1) % 3
    return jnp.where(m3 == 0, hi.astype(F32), jnp.where(m3 == 1, mid.astype(F32), lo.astype(F32)))


def _head_masks(shape):
    lane = lax.broadcasted_iota(I32, shape, 1)
    return lane, lane < 64


def _finish_heads(acc, tq):
    o0 = acc[0:64, 0:tq] / acc[LANES:LANES + 1, 0:tq]
    o1 = acc[64:LANES, tq:2 * tq] / acc[LANES:LANES + 1, tq:2 * tq]
    return jnp.concatenate([o0, o1], axis=0).T


def _fox_prompt_body(q_ref, k_ref, v_ref, f_ref, gq_ref, gk_ref, *rest, tq, seq, dh):
    o_ref, knt_ref, vt_ref, kaug, vt3 = rest[-5:]
    ones = _pair_ones()
    nq = seq // tq
    lane, first = _head_masks((tq, LANES))

    for c in range(nq):
        rows = slice(c * tq, (c + 1) * tq)
        kn = _pair_norm(k_ref[rows, :], gk_ref[...], ones)
        knt_ref[:, rows] = kn.T
        kaug[rows, 0:LANES] = kn.astype(BF16)
        sp = _split_select(f_ref[rows, :])
        kaug[rows, LANES:2 * LANES] = jnp.where(
            lane < 6, -sp, jnp.where(lane < 12, 1.0, 0.0)).astype(BF16)
        vt = v_ref[rows, :].T
        vt_ref[:, rows] = vt
        vt3[c, 0:LANES, :] = vt.astype(BF16)
        vt3[c, LANES:V_ROWS, :] = jnp.ones((BF16_ROWS, tq), BF16)

    causal = (lax.broadcasted_iota(I32, (tq, 2 * tq), 0)
              <= lax.broadcasted_iota(I32, (tq, 2 * tq), 1) % tq)

    for qi in range(nq):
        rows = slice(qi * tq, (qi + 1) * tq)
        qn = _pair_norm(q_ref[rows, :], gq_ref[...], ones) * (dh ** -0.5 * LOG2E)
        sp = _split_select(f_ref[rows, :])
        up0 = jnp.where(lane < 3, 1.0, jnp.where((lane >= 6) & (lane < 9), sp, 0.0))
        up1 = jnp.where((lane >= 3) & (lane < 6), 1.0,
                        jnp.where((lane >= 9) & (lane < 12), sp, 0.0))
        qs = jnp.concatenate(
            [jnp.concatenate([jnp.where(first, qn, 0.0), up0], axis=1),
             jnp.concatenate([jnp.where(first, 0.0, qn), up1], axis=1)], axis=0).astype(BF16)

        s = jnp.where(causal, _dot_nt(kaug[rows, :], qs), NEG_INF)
        m = jnp.max(s, axis=0, keepdims=True)
        acc = _dot(vt3[qi], jnp.exp2(s - m).astype(BF16))
        for kj in range(qi):
            s = _dot_nt(kaug[kj * tq:(kj + 1) * tq, :], qs)
            m_new = jnp.maximum(m, jnp.max(s, axis=0, keepdims=True))
            acc = jnp.exp2(m - m_new) * acc + _dot(vt3[kj], jnp.exp2(s - m_new).astype(BF16))
            m = m_new
        o_ref[rows, :] = _finish_heads(acc, tq)


def _layer_cache_outputs(layer, depth, batch, width, seq, prev, n_inputs, index):
    shape = jax.ShapeDtypeStruct((depth, batch, width, seq), F32)
    spec = pl.BlockSpec((None, None, LANES, seq), lambda *g: (layer, *index(*g), 0))
    extra_specs = [] if prev is None else [pl.BlockSpec(memory_space=pl.ANY)] * 2
    aliases = {} if prev is None else {n_inputs: 1, n_inputs + 1: 2}
    return (shape, shape), (spec, spec), extra_specs, aliases, (() if prev is None else tuple(prev))


def _fox_prompt(proj, f_aug, gq, gk, *, batch, seq, n_pairs, col_q, col_k, col_v, m_rows, layer,
                depth, prev):
    tq = _pick(seq, (512, 256, 128))
    nq = seq // tq
    width = n_pairs * LANES
    est = (2 * 7 * _nbytes((seq, LANES), F32)
           + _nbytes((seq, 2 * LANES), BF16) + _nbytes((V_ROWS, seq), BF16)
           + 16 * _nbytes((tq, 2 * tq), F32))
    c_shapes, c_specs, extra_specs, aliases, extra_args = _layer_cache_outputs(
        layer, depth, batch, width, seq, prev, 6, lambda b, hp: (b, hp))
    return pl.pallas_call(
        functools.partial(_fox_prompt_body, tq=tq, seq=seq, dh=64),
        out_shape=(jax.ShapeDtypeStruct((m_rows, width), F32), *c_shapes),
        grid=(batch, n_pairs),
        in_specs=[
            pl.BlockSpec((seq, LANES), lambda b, hp: (b, col_q + hp)),
            pl.BlockSpec((seq, LANES), lambda b, hp: (b, col_k + hp)),
            pl.BlockSpec((seq, LANES), lambda b, hp: (b, col_v + hp)),
            pl.BlockSpec((seq, LANES), lambda b, hp: (b, hp)),
            pl.BlockSpec((1, LANES), lambda b, hp: (0, 0)),
            pl.BlockSpec((1, LANES), lambda b, hp: (0, 0)),
            *extra_specs,
        ],
        out_specs=(pl.BlockSpec((seq, LANES), lambda b, hp: (b, hp)), *c_specs),
        input_output_aliases=aliases,
        scratch_shapes=[pltpu.VMEM((seq, 2 * LANES), BF16), pltpu.VMEM((nq, V_ROWS, tq), BF16)],
        compiler_params=pltpu.CompilerParams(
            dimension_semantics=("parallel", "parallel"),
            vmem_limit_bytes=_vmem_limit(est)),
        name="fox_prompt",
    )(proj, proj, proj, f_aug, gq, gk, *extra_args)


def _band_prompt_body(q_ref, k_ref, v_ref, tab_ref, gq_ref, gk_ref, *rest, tq, seq, past, dh):
    o_ref, knt_ref, vt_ref, kpad, vt3 = rest[-5:]
    ones = _pair_ones()
    nq = seq // tq
    npad = past // tq
    win = past + tq

    kpad[0:past, :] = jnp.zeros((past, LANES), BF16)
    for c in range(npad):
        vt3[c] = jnp.zeros((V_ROWS, tq), BF16)
    for c in range(nq):
        rows = slice(c * tq, (c + 1) * tq)
        kn = _pair_norm(k_ref[rows, :], gk_ref[...], ones)
        knt_ref[:, rows] = kn.T
        kpad[past + c * tq:past + (c + 1) * tq, :] = kn.astype(BF16)
        vt = v_ref[rows, :].T
        vt_ref[:, rows] = vt
        vt3[npad + c, 0:LANES, :] = vt.astype(BF16)
        vt3[npad + c, LANES:V_ROWS, :] = jnp.ones((BF16_ROWS, tq), BF16)

    row = lax.broadcasted_iota(I32, (win, 2 * tq), 0)

    for qi in range(nq):
        r0 = qi * tq
        qn = _pair_norm(q_ref[r0:r0 + tq, :], gq_ref[...], ones) * (dh ** -0.5 * LOG2E)
        s = _dot_nt(kpad[r0:r0 + win, :], _stack_heads(qn)) + tab_ref[...]
        if r0 < past:
            s = jnp.where(row >= past - r0, s, NEG_INF)
        m = jnp.max(s, axis=0, keepdims=True)
        p = jnp.exp2(s - m).astype(BF16)
        acc = _dot(vt3[qi], p[0:tq, :])
        for c in range(1, win // tq):
            acc = acc + _dot(vt3[qi + c], p[c * tq:(c + 1) * tq, :])
        o_ref[r0:r0 + tq, :] = _finish_heads(acc, tq)


def _band_prompt(proj, tab, gq, gk, *, batch, seq, n_pairs, col_q, col_k, col_v, tq, m_rows, layer,
                 depth, prev):
    past = BAND_PAST * CHUNK
    assert past % tq == 0 and seq % tq == 0
    nq = seq // tq
    width = n_pairs * LANES
    win = past + tq
    est = (2 * 5 * _nbytes((seq, LANES), F32)
           + _nbytes((seq + past, LANES), BF16) + _nbytes((V_ROWS, seq + past), BF16)
           + 2 * _nbytes((win, 2 * tq), F32) + 8 * _nbytes((win, 2 * tq), F32))
    c_shapes, c_specs, extra_specs, aliases, extra_args = _layer_cache_outputs(
        layer, depth, batch, width, seq, prev, 6, lambda hp, b: (b, hp))
    return pl.pallas_call(
        functools.partial(_band_prompt_body, tq=tq, seq=seq, past=past, dh=64),
        out_shape=(jax.ShapeDtypeStruct((m_rows, width), F32), *c_shapes),
        grid=(n_pairs, batch),
        in_specs=[
            pl.BlockSpec((seq, LANES), lambda hp, b: (b, col_q + hp)),
            pl.BlockSpec((seq, LANES), lambda hp, b: (b, col_k + hp)),
            pl.BlockSpec((seq, LANES), lambda hp, b: (b, col_v + hp)),
            pl.BlockSpec((None, win, 2 * tq), lambda hp, b: (hp, 0, 0)),
            pl.BlockSpec((1, LANES), lambda hp, b: (0, 0)),
            pl.BlockSpec((1, LANES), lambda hp, b: (0, 0)),
            *extra_specs,
        ],
        out_specs=(pl.BlockSpec((seq, LANES), lambda hp, b: (b, hp)), *c_specs),
        input_output_aliases=aliases,
        scratch_shapes=[pltpu.VMEM((seq + past, LANES), BF16),
                        pltpu.VMEM(((seq + past) // tq, V_ROWS, tq), BF16)],
        compiler_params=pltpu.CompilerParams(
            dimension_semantics=("parallel", "parallel"),
            vmem_limit_bytes=_vmem_limit(est)),
        name="band_prompt",
    )(proj, proj, proj, tab, gq, gk, *extra_args)


def _sample_attn_body(*refs, mode, n_pairs, s_new, n_cache, dh):
    if mode == "fox":
        (q_ref, k_ref, v_ref, ck_ref, cv_ref, fq_ref, ft_ref, gq_ref, gk_ref, _alias,
         o_ref, kn_ref) = refs
    else:
        (q_ref, k_ref, v_ref, ck_ref, cv_ref, tabc_ref, tabn_ref, gq_ref, gk_ref, _alias,
         o_ref, kn_ref) = refs
    ones = _pair_ones()
    row = lax.broadcasted_iota(I32, (s_new, s_new), 0)
    col = lax.broadcasted_iota(I32, (s_new, s_new), 1)
    for hp in range(n_pairs):
        sl = slice(hp * LANES, (hp + 1) * LANES)
        qn = _pair_norm(q_ref[:, sl], gq_ref[...], ones) * (dh ** -0.5)
        kn = _pair_norm(k_ref[:, sl], gk_ref[...], ones)
        kn_ref[:, sl] = kn
        vn = v_ref[:, sl]
        outs = []
        for e in range(2):
            h = 2 * hp + e
            hl = slice(e * dh, (e + 1) * dh)
            qh = qn[:, hl].astype(BF16)
            sc = _dot(qh, ck_ref[h].astype(BF16))
            sn = _dot_nt(qh, kn[:, hl].astype(BF16))
            if mode == "fox":
                fq = _lane_column(fq_ref[...], h)
                fk = ft_ref[h:h + 1, :]
                sc = sc + fq - fk[:, :n_cache]
                sn = jnp.where(col <= row, sn + fq - fk[:, n_cache:n_cache + s_new], NEG_INF)
            else:
                sc = sc + tabc_ref[h]
                sn = sn + tabn_ref[h]
            m = jnp.maximum(jnp.max(sc, axis=-1, keepdims=True),
                            jnp.max(sn, axis=-1, keepdims=True))
            pc = jnp.exp(sc - m)
            pn = jnp.exp(sn - m)
            l = jnp.sum(pc, axis=-1, keepdims=True) + jnp.sum(pn, axis=-1, keepdims=True)
            outs.append((_dot_nt(pc.astype(BF16), cv_ref[h].astype(BF16))
                         + _dot(pn.astype(BF16), vn[:, hl].astype(BF16))) / l)
        o_ref[:, sl] = jnp.concatenate(outs, axis=1)


def _sample_attn(proj, cache_kt, cache_vt, layer, extra_a, extra_b, gq, gk, o_buf, *, mode, batch,
                 s_new, n_pairs, col_q, row0):
    width = n_pairs * LANES
    _, _, n_heads, dh, n_cache = cache_kt.shape
    rb = row0 // s_new
    if mode == "fox":
        ex_specs = [pl.BlockSpec((s_new, LANES), lambda b: (b, 0)),
                    pl.BlockSpec((None, 16, extra_b.shape[2]), lambda b: (b, 0, 0))]
    else:
        ex_specs = [pl.BlockSpec(extra_a.shape, lambda b: (0, 0, 0)),
                    pl.BlockSpec(extra_b.shape, lambda b: (0, 0, 0))]
    est = (2 * 2 * _nbytes((n_cache, width), F32) + 8 * _nbytes((s_new, width), F32)
           + 2 * _nbytes(extra_a.shape, F32) + 16 * _nbytes((2 * s_new, n_cache), F32)
           + 4 * _nbytes((n_cache, LANES), BF16))
    return pl.pallas_call(
        functools.partial(_sample_attn_body, mode=mode, n_pairs=n_pairs, s_new=s_new,
                          n_cache=n_cache, dh=dh),
        out_shape=(jax.ShapeDtypeStruct(o_buf.shape, F32),
                   jax.ShapeDtypeStruct((batch * s_new, width), F32)),
        grid=(batch,),
        in_specs=[
            pl.BlockSpec((s_new, width), lambda b: (rb + b, col_q)),
            pl.BlockSpec((s_new, width), lambda b: (rb + b, col_q + 1)),
            pl.BlockSpec((s_new, width), lambda b: (rb + b, col_q + 2)),
            pl.BlockSpec((None, None, n_heads, dh, n_cache), lambda b: (layer, b, 0, 0, 0)),
            pl.BlockSpec((None, None, n_heads, dh, n_cache), lambda b: (layer, b, 0, 0, 0)),
            *ex_specs,
            pl.BlockSpec((1, LANES), lambda b: (0, 0)),
            pl.BlockSpec((1, LANES), lambda b: (0, 0)),
            pl.BlockSpec(memory_space=pl.ANY),
        ],
        out_specs=(pl.BlockSpec((s_new, width), lambda b: (rb + b, 0)),
                   pl.BlockSpec((s_new, width), lambda b: (b, 0))),
        input_output_aliases={9: 0},
        compiler_params=pltpu.CompilerParams(
            dimension_semantics=("parallel",), vmem_limit_bytes=_vmem_limit(est)),
        name=f"{mode}_sample",
    )(proj, proj, proj, cache_kt, cache_vt, extra_a, extra_b, gq, gk, o_buf)


def _cross_body(*refs, n_heads, dh, aliased):
    if aliased:
        q_ref, k_ref, v_ref, _alias, o_ref = refs
    else:
        q_ref, k_ref, v_ref, o_ref = refs
    for h in range(n_heads):
        sl = slice(h * dh, (h + 1) * dh)
        q = (q_ref[:, sl] * (dh ** -0.5)).astype(BF16)
        s = _dot_nt(q, k_ref[:, sl].astype(BF16))
        m = jnp.max(s, axis=-1, keepdims=True)
        p = jnp.exp(s - m)
        l = jnp.sum(p, axis=-1, keepdims=True)
        o = _dot(p.astype(BF16), v_ref[:, sl].astype(BF16)) / l
        o_ref[:, sl] = o.astype(o_ref.dtype)


def _cross_attn(q_all, k, v, o_buf, *, batch, q_len, n_mem, n_heads, dh, row0, m_rows):
    width = n_heads * dh
    tq = _pick(q_len, (512, 256, 128, 64, 32, 16))
    nq = q_len // tq
    rb = row0 // tq
    aliased = o_buf is not None
    in_specs = [pl.BlockSpec((tq, width), lambda b, qi: (rb + b * nq + qi, 0)),
                pl.BlockSpec((n_mem, width), lambda b, qi: (b, 0)),
                pl.BlockSpec((n_mem, width), lambda b, qi: (b, 0))]
    args = [q_all, k, v]
    if aliased:
        in_specs.append(pl.BlockSpec(memory_space=pl.ANY))
        args.append(o_buf)
    est = (2 * _nbytes((tq, width), F32) + 4 * _nbytes((n_mem, width), F32)
           + 2 * _nbytes((tq, width), BF16) + 12 * _nbytes((tq, n_mem), F32))
    return pl.pallas_call(
        functools.partial(_cross_body, n_heads=n_heads, dh=dh, aliased=aliased),
        out_shape=jax.ShapeDtypeStruct((m_rows, width), BF16),
        grid=(batch, nq),
        in_specs=in_specs,
        out_specs=pl.BlockSpec((tq, width), lambda b, qi: (rb + b * nq + qi, 0)),
        input_output_aliases={3: 0} if aliased else {},
        compiler_params=pltpu.CompilerParams(
            dimension_semantics=("parallel", "arbitrary"), vmem_limit_bytes=_vmem_limit(est)),
        name="cross_attn",
    )(*args)


def _s5_prep_body(lr_ref, li_ref, ldt_ref, br_ref, bi_ref, ar_ref, ai_ref, bbr_ref, bbi_ref):
    lr = lr_ref[...]
    li = li_ref[...]
    dt = jnp.exp(ldt_ref[...])
    mag = jnp.exp(lr * dt)
    a_re = mag * jnp.cos(li * dt)
    a_im = mag * jnp.sin(li * dt)
    den = lr * lr + li * li
    num_re = a_re - 1.0
    coef_re = (num_re * lr + a_im * li) / den
    coef_im = (a_im * lr - num_re * li) / den
    br = br_ref[...]
    bi = bi_ref[...]
    ar_ref[...] = a_re
    ai_ref[...] = a_im
    bbr_ref[...] = coef_re * br - coef_im * bi
    bbi_ref[...] = coef_re * bi + coef_im * br


def _s5_prep(lam_re, lam_im, log_dt, b_re, b_im):
    shape = lam_re.shape
    spec = pl.BlockSpec(shape, lambda: (0, 0))
    return pl.pallas_call(
        _s5_prep_body,
        out_shape=tuple(jax.ShapeDtypeStruct(shape, F32) for _ in range(4)),
        in_specs=[spec] * 5,
        out_specs=tuple([spec] * 4),
        name="s5_discretise",
    )(lam_re, lam_im, log_dt, b_re, b_im)


def _s5_body(u_ref, x0r_ref, x0i_ref, ar_ref, ai_ref, bre_ref, bim_ref, cre_ref, cim_ref,
             d_ref, wg_ref, bg_ref, o_ref, xr_out, xi_out, bur, bui, st_r, st_i,
             *, nb, t_chunk, n_blk, passes):
    i = pl.program_id(0)
    wc = u_ref.shape[1]
    ns = bur.shape[1]
    ub = wc // n_blk
    sb = ns // n_blk

    @pl.when(i == 0)
    def _():
        st_r[...] = x0r_ref[...]
        st_i[...] = x0i_ref[...]

    u = u_ref[...]
    for k in range(n_blk):
        uk = u[:, k * ub:(k + 1) * ub]
        bur[:, k * sb:(k + 1) * sb] = _dot_hp(
            uk, bre_ref[k * ub:(k + 1) * ub, k * sb:(k + 1) * sb], passes)
        bui[:, k * sb:(k + 1) * sb] = _dot_hp(
            uk, bim_ref[k * ub:(k + 1) * ub, k * sb:(k + 1) * sb], passes)

    a_re = ar_ref[...]
    a_im = ai_ref[...]

    def step(t, carry):
        xr, xi = carry
        r0 = pl.multiple_of(t * nb, nb)
        nr = a_re * xr - a_im * xi + bur[pl.ds(r0, nb), :]
        ni = a_re * xi + a_im * xr + bui[pl.ds(r0, nb), :]
        bur[pl.ds(r0, nb), :] = nr
        bui[pl.ds(r0, nb), :] = ni
        return nr, ni
    xr_f, xi_f = lax.fori_loop(0, t_chunk, step, (st_r[...], st_i[...]))
    st_r[...] = xr_f
    st_i[...] = xi_f
    xr_out[...] = xr_f
    xi_out[...] = xi_f
    ys = []
    for k in range(n_blk):
        xrk = bur[:, k * sb:(k + 1) * sb].astype(BF16)
        xik = bui[:, k * sb:(k + 1) * sb].astype(BF16)
        ys.append(_dot(xrk, cre_ref[k * sb:(k + 1) * sb, k * ub:(k + 1) * ub].astype(BF16))
                  - _dot(xik, cim_ref[k * sb:(k + 1) * sb, k * ub:(k + 1) * ub].astype(BF16)))
    y = jnp.concatenate(ys, axis=1) + d_ref[...] * u
    z = _gelu_tanh(y)
    gate = _dot(z.astype(BF16), wg_ref[...].astype(BF16)) + bg_ref[...]
    o_ref[...] = z * jax.nn.sigmoid(gate)


def _s5(u_tb, x0r, x0i, a_re, a_im, b_re, b_im, c_re, c_im, d, w_glu, b_glu, *, nb, seq, n_blk,
        passes):
    wc = u_tb.shape[1]
    ns = a_re.shape[1]
    t_chunk = _pick(seq, (64, 32, 16))
    rows = nb * t_chunk
    full = lambda shape: pl.BlockSpec(shape, lambda i: (0, 0))
    est = (4 * _nbytes((rows, wc), F32) + 2 * _nbytes((rows, ns), F32)
           + 2 * 4 * _nbytes((wc, ns), F32) + 2 * _nbytes((wc, wc), F32)
           + 12 * _nbytes((rows, ns // n_blk), F32) + 8 * _nbytes((nb, ns), F32))
    return pl.pallas_call(
        functools.partial(_s5_body, nb=nb, t_chunk=t_chunk, n_blk=n_blk, passes=passes),
        out_shape=(jax.ShapeDtypeStruct((seq * nb, wc), F32),
                   jax.ShapeDtypeStruct((nb, ns), F32), jax.ShapeDtypeStruct((nb, ns), F32)),
        grid=(seq // t_chunk,),
        in_specs=[pl.BlockSpec((rows, wc), lambda i: (i, 0)),
                  full((nb, ns)), full((nb, ns)), full((1, ns)), full((1, ns)),
                  full((wc, ns)), full((wc, ns)), full((ns, wc)), full((ns, wc)),
                  full((1, wc)), full((wc, wc)), full((1, wc))],
        out_specs=(pl.BlockSpec((rows, wc), lambda i: (i, 0)), full((nb, ns)), full((nb, ns))),
        scratch_shapes=[pltpu.VMEM((rows, ns), F32), pltpu.VMEM((rows, ns), F32),
                        pltpu.VMEM((nb, ns), F32), pltpu.VMEM((nb, ns), F32)],
        compiler_params=pltpu.CompilerParams(
            dimension_semantics=("arbitrary",), vmem_limit_bytes=_vmem_limit(est)),
        name="s5_scan",
    )(u_tb, x0r, x0i, a_re, a_im, b_re, b_im, c_re, c_im, d, w_glu, b_glu)


def _router_body(x_ref, g_ref, wr_ref, pk_ref, idx_ref, gate_ref, *, n_exp, rc):
    tm, d = x_ref.shape
    half = d // 2

    def chunk(c, carry):
        r0 = pl.multiple_of(c * rc, rc)
        x = x_ref[pl.ds(r0, rc), :]
        ms = jnp.mean(x * x, axis=-1, keepdims=True)
        h = x * lax.rsqrt(ms + EPS) * g_ref[...]
        hb = h.astype(BF16).astype(F32)
        lo = lax.shift_right_logical(pltpu.bitcast(hb[:, :half], U32), jnp.uint32(16))
        hi = pltpu.bitcast(hb[:, half:], U32) & jnp.uint32(0xFFFF0000)
        pk_ref[pl.ds(r0, rc), :] = lo | hi
        logits = _dot_hp(h, wr_ref[...])
        lane = lax.broadcasted_iota(I32, logits.shape, 1)
        logits = jnp.where(lane < n_exp, logits, NEG_INF)
        mx = jnp.max(logits, axis=-1, keepdims=True)
        e = jnp.exp(logits - mx)
        probs = e / jnp.sum(e, axis=-1, keepdims=True)
        probs = jnp.where(lane < n_exp, probs, -1.0)
        lane_f = lane.astype(F32)
        p1 = jnp.max(probs, axis=-1, keepdims=True)
        i1 = jnp.min(jnp.where(probs == p1, lane_f, float(LANES)), axis=-1, keepdims=True)
        rest = jnp.where(lane_f == i1, -1.0, probs)
        p2 = jnp.max(rest, axis=-1, keepdims=True)
        i2 = jnp.min(jnp.where(rest == p2, lane_f, float(LANES)), axis=-1, keepdims=True)
        tot = p1 + p2
        idx_ref[pl.ds(r0, rc), :] = jnp.where(lane == 0, i1, jnp.where(lane == 1, i2, 0.0)).astype(I32)
        gate_ref[pl.ds(r0, rc), :] = jnp.where(lane == 0, p1 / tot,
                                                jnp.where(lane == 1, p2 / tot, 0.0))
        return carry
    lax.fori_loop(0, tm // rc, chunk, 0)


def _router(x, g, w_router_pad, *, n_exp, tm):
    m, d = x.shape
    rc = _pick(tm, (256, 176, 128, 64, 32, 16, 8))
    est = 2 * _nbytes((tm, d), F32) + 2 * _nbytes((tm, d // 2), U32) + 16 * _nbytes((rc, d), F32)
    return pl.pallas_call(
        functools.partial(_router_body, n_exp=n_exp, rc=rc),
        out_shape=(jax.ShapeDtypeStruct((m, d // 2), U32),
                   jax.ShapeDtypeStruct((m, LANES), I32),
                   jax.ShapeDtypeStruct((m, LANES), F32)),
        grid=(m // tm,),
        in_specs=[pl.BlockSpec((tm, d), lambda i: (i, 0)),
                  pl.BlockSpec((1, d), lambda i: (0, 0)),
                  pl.BlockSpec((d, LANES), lambda i: (0, 0))],
        out_specs=(pl.BlockSpec((tm, d // 2), lambda i: (i, 0)),
                   pl.BlockSpec((tm, LANES), lambda i: (i, 0)),
                   pl.BlockSpec((tm, LANES), lambda i: (i, 0))),
        compiler_params=pltpu.CompilerParams(
            dimension_semantics=("parallel",), vmem_limit_bytes=_vmem_limit(est)),
        name="moe_router",
    )(x, g, w_router_pad)


GATHER_UNROLL = 8


def _row_copy(src_hbm, row, dst, r, sem):
    return pltpu.make_async_copy(src_hbm.at[pl.ds(row, 1), :], dst.at[pl.ds(r, 1), :], sem)


def _dispatch_body(nt_ref, tok_ref, nxt_ref, pk_hbm, a_ref, buf, sem, *, tm):
    i = pl.program_id(0)
    nt = nt_ref[0]
    half = buf.shape[2]

    def request(ids_ref, slot):
        def issue(c, carry):
            for u in range(GATHER_UNROLL):
                r = c * GATHER_UNROLL + u
                _row_copy(pk_hbm, ids_ref[0, 0, r], buf.at[slot], r, sem.at[slot]).start(
                    priority=u % 2)
            return carry
        lax.fori_loop(0, tm // GATHER_UNROLL, issue, 0)

    @pl.when(i == 0)
    def _():
        request(tok_ref, 0)

    @pl.when(i + 1 < nt)
    def _():
        request(nxt_ref, (i + 1) % 2)

    @pl.when(i < nt)
    def _():
        slot = i % 2
        pltpu.make_async_copy(pk_hbm.at[pl.ds(0, tm), :], buf.at[slot], sem.at[slot]).wait()
        pk = buf[slot]
        lo = pltpu.bitcast(lax.shift_left(pk, jnp.uint32(16)), F32)
        hi = pltpu.bitcast(pk & jnp.uint32(0xFFFF0000), F32)
        a_ref[:, :half] = lo.astype(BF16)
        a_ref[:, half:] = hi.astype(BF16)


def _dispatch(n_tiles, tok_sorted, packed, *, tm, r_max):
    m, half = packed.shape
    t_max = r_max // tm
    assert tm % GATHER_UNROLL == 0
    tok_tiles = tok_sorted.reshape(t_max, 1, tm)
    return pl.pallas_call(
        functools.partial(_dispatch_body, tm=tm),
        out_shape=jax.ShapeDtypeStruct((r_max, 2 * half), BF16),
        grid_spec=pltpu.PrefetchScalarGridSpec(
            num_scalar_prefetch=1,
            grid=(t_max,),
            in_specs=[pl.BlockSpec((1, 1, tm), lambda i, nt: (i, 0, 0), memory_space=pltpu.SMEM),
                      pl.BlockSpec((1, 1, tm), lambda i, nt: (jnp.minimum(i + 1, t_max - 1), 0, 0),
                                   memory_space=pltpu.SMEM),
                      pl.BlockSpec(memory_space=pl.ANY)],
            out_specs=pl.BlockSpec((tm, 2 * half), lambda i, nt: (jnp.minimum(i, nt[0] - 1), 0)),
            scratch_shapes=[pltpu.VMEM((2, tm, half), U32), pltpu.SemaphoreType.DMA((2,))]),
        compiler_params=pltpu.CompilerParams(dimension_semantics=("arbitrary",)),
        name="moe_dispatch",
    )(n_tiles, tok_tiles, tok_tiles, packed)


def _expert_up_body(te_ref, nt_ref, a_ref, w1_ref, w3_ref, h_ref, w1bf, w3bf, *, kc, rem):
    f = pl.program_id(0)
    i = pl.program_id(1)
    last = pl.num_programs(0) - 1
    d, tf = w1bf.shape
    fresh = jnp.logical_or(i == 0, te_ref[i] != te_ref[jnp.maximum(i - 1, 0)])
    live = i < nt_ref[0]

    def run(cols, when):
        @pl.when(jnp.logical_and(when, jnp.logical_and(fresh, live)))
        def _():
            def cast(c, carry):
                r0 = pl.multiple_of(c * kc, kc)
                w1bf[pl.ds(r0, kc), 0:cols] = w1_ref[pl.ds(r0, kc), 0:cols].astype(BF16)
                w3bf[pl.ds(r0, kc), 0:cols] = w3_ref[pl.ds(r0, kc), 0:cols].astype(BF16)
                return carry
            lax.fori_loop(0, d // kc, cast, 0)

        @pl.when(jnp.logical_and(when, live))
        def _():
            a = a_ref[...]
            h_ref[:, 0:cols] = (jax.nn.silu(_dot(a, w1bf[:, 0:cols]))
                                * _dot(a, w3bf[:, 0:cols])).astype(h_ref.dtype)

    if rem:
        run(tf, f < last)
        run(rem, f == last)
    else:
        run(tf, True)


def _expert_up(tile_expert, n_tiles, a_sorted, w1, w3, *, tm, tf):
    r_max, d = a_sorted.shape
    n_exp, _, fe = w1.shape
    t_max = r_max // tm
    kc = _pick(d, (512, 256, 128))
    row = lambda f, i, te, nt: jnp.minimum(i, nt[0] - 1)
    est = (2 * _nbytes((tm, d), BF16) + 2 * 2 * _nbytes((d, tf), F32) + 2 * _nbytes((d, tf), BF16)
           + 2 * _nbytes((tm, tf), BF16) + 6 * _nbytes((tm, tf), F32))
    return pl.pallas_call(
        functools.partial(_expert_up_body, kc=kc, rem=fe % tf),
        out_shape=jax.ShapeDtypeStruct((r_max, fe), BF16),
        grid_spec=pltpu.PrefetchScalarGridSpec(
            num_scalar_prefetch=2,
            grid=(pl.cdiv(fe, tf), t_max),
            in_specs=[pl.BlockSpec((tm, d), lambda f, i, te, nt: (row(f, i, te, nt), 0)),
                      pl.BlockSpec((None, d, tf), lambda f, i, te, nt: (te[i], 0, f)),
                      pl.BlockSpec((None, d, tf), lambda f, i, te, nt: (te[i], 0, f))],
            out_specs=pl.BlockSpec((tm, tf), lambda f, i, te, nt: (row(f, i, te, nt), f)),
            scratch_shapes=[pltpu.VMEM((d, tf), BF16), pltpu.VMEM((d, tf), BF16)]),
        compiler_params=pltpu.CompilerParams(
            dimension_semantics=("arbitrary", "arbitrary"), vmem_limit_bytes=_vmem_limit(est)),
        name="moe_expert_up",
    )(tile_expert, n_tiles, a_sorted, w1, w3)


def _expert_down_body(te_ref, nt_ref, h_ref, w2_ref, y_ref, w2bf, *, kc):
    i = pl.program_id(1)
    fe = h_ref.shape[1]
    fresh = jnp.logical_or(i == 0, te_ref[i] != te_ref[jnp.maximum(i - 1, 0)])

    @pl.when(jnp.logical_and(fresh, i < nt_ref[0]))
    def _():
        def cast(c, carry):
            r0 = pl.multiple_of(c * kc, kc)
            w2bf[pl.ds(r0, kc), :] = w2_ref[pl.ds(r0, kc), :].astype(BF16)
            return carry
        lax.fori_loop(0, fe // kc, cast, 0)

    @pl.when(i < nt_ref[0])
    def _():
        y_ref[...] = _dot(h_ref[...], w2bf[...])


def _expert_down(tile_expert, n_tiles, h_sorted, w2, *, tm, tn):
    r_max, fe = h_sorted.shape
    d = w2.shape[2]
    t_max = r_max // tm
    kc = _pick(fe, (512, 256, 128))
    row = lambda n, i, te, nt: jnp.minimum(i, nt[0] - 1)
    est = (2 * _nbytes((tm, fe), BF16) + 2 * _nbytes((fe, tn), F32) + _nbytes((fe, tn), BF16)
           + 4 * _nbytes((tm, tn), F32))
    return pl.pallas_call(
        functools.partial(_expert_down_body, kc=kc),
        out_shape=jax.ShapeDtypeStruct((r_max, d), F32),
        grid_spec=pltpu.PrefetchScalarGridSpec(
            num_scalar_prefetch=2,
            grid=(d // tn, t_max),
            in_specs=[pl.BlockSpec((tm, fe), lambda n, i, te, nt: (row(n, i, te, nt), 0)),
                      pl.BlockSpec((None, fe, tn), lambda n, i, te, nt: (te[i], 0, n))],
            out_specs=pl.BlockSpec((tm, tn), lambda n, i, te, nt: (row(n, i, te, nt), n)),
            scratch_shapes=[pltpu.VMEM((fe, tn), BF16)]),
        compiler_params=pltpu.CompilerParams(
            dimension_semantics=("arbitrary", "arbitrary"), vmem_limit_bytes=_vmem_limit(est)),
        name="moe_expert_down",
    )(tile_expert, n_tiles, h_sorted, w2)


def _combine_body(slot_ref, nxt_ref, x_ref, gate_ref, y_hbm, *rest, tc, n_first):
    o_refs, (buf, sem) = rest[:-2], rest[-2:]
    i = pl.program_id(0)
    n = pl.num_programs(0)
    unroll = GATHER_UNROLL // TOP_K

    def request(ids_ref, slot):
        def issue(c, carry):
            for u in range(unroll):
                r = c * unroll + u
                for k in range(TOP_K):
                    _row_copy(y_hbm, ids_ref[0, 0, TOP_K * r + k], buf.at[slot, k], r,
                              sem.at[slot]).start(priority=k % 2)
            return carry
        lax.fori_loop(0, tc // unroll, issue, 0)

    @pl.when(i == 0)
    def _():
        request(slot_ref, 0)

    @pl.when(i + 1 < n)
    def _():
        request(nxt_ref, (i + 1) % 2)

    slot = i % 2
    for k in range(TOP_K):
        pltpu.make_async_copy(y_hbm.at[pl.ds(0, tc), :], buf.at[slot, k], sem.at[slot]).wait()
    g = gate_ref[...]
    val = x_ref[...] + (g[:, 0:1] * buf[slot, 0] + g[:, 1:2] * buf[slot, 1])
    if len(o_refs) == 1:
        o_refs[0][...] = val
    else:
        @pl.when(i < n_first)
        def _():
            o_refs[0][...] = val

        @pl.when(i >= n_first)
        def _():
            o_refs[1][...] = val


def _combine(slots, x, gates, y_sorted, *, tc, split=None):
    m, d = x.shape
    n = m // tc
    assert tc % (GATHER_UNROLL // TOP_K) == 0
    est = 4 * _nbytes((tc, d), F32) + 2 * TOP_K * _nbytes((tc, d), F32) + 6 * _nbytes((tc, d), F32)
    slot_tiles = slots.reshape(n, 1, TOP_K * tc)
    if split is None or split % tc or (m - split) % tc:
        n_first = n
        out_shape = jax.ShapeDtypeStruct((m, d), F32)
        out_specs = pl.BlockSpec((tc, d), lambda i: (i, 0))
    else:
        n_first = split // tc
        out_shape = (jax.ShapeDtypeStruct((split, d), F32), jax.ShapeDtypeStruct((m - split, d), F32))
        out_specs = (pl.BlockSpec((tc, d), lambda i: (jnp.minimum(i, n_first - 1), 0)),
                     pl.BlockSpec((tc, d), lambda i: (jnp.maximum(i - n_first, 0), 0)))
    return pl.pallas_call(
        functools.partial(_combine_body, tc=tc, n_first=n_first),
        out_shape=out_shape,
        grid=(n,),
        in_specs=[pl.BlockSpec((1, 1, TOP_K * tc), lambda i: (i, 0, 0), memory_space=pltpu.SMEM),
                  pl.BlockSpec((1, 1, TOP_K * tc), lambda i: (jnp.minimum(i + 1, n - 1), 0, 0),
                               memory_space=pltpu.SMEM),
                  pl.BlockSpec((tc, d), lambda i: (i, 0)),
                  pl.BlockSpec((tc, LANES), lambda i: (i, 0)),
                  pl.BlockSpec(memory_space=pl.ANY)],
        out_specs=out_specs,
        scratch_shapes=[pltpu.VMEM((2, TOP_K, tc, d), F32), pltpu.SemaphoreType.DMA((2,))],
        compiler_params=pltpu.CompilerParams(
            dimension_semantics=("arbitrary",), vmem_limit_bytes=_vmem_limit(est)),
        name="moe_combine",
    )(slot_tiles, slot_tiles, x, gates, y_sorted)


def _moe(x, g_ffn, w_router, w_e1, w_e3, w_e2, *, tm_tok, split=None):
    m, d = x.shape
    n_exp, _, fe = w_e1.shape
    tm = _pick(m * TOP_K, (512, 256, 128, 64, 32, 16))
    wr = jnp.zeros((d, LANES), F32).at[:, :n_exp].set(w_router)
    packed, idx128, gate128 = _router(x, g_ffn, wr, n_exp=n_exp, tm=tm_tok)

    idx = idx128[:, :TOP_K]
    mask = jnp.sum(idx[:, :, None] == jnp.arange(n_exp, dtype=I32)[None, None, :], axis=1).astype(I32)
    counts = jnp.sum(mask, axis=0)
    padded = ((counts + tm - 1) // tm) * tm
    ends = jnp.cumsum(padded)
    starts = ends - padded
    pos = jnp.cumsum(mask, axis=0) - mask
    slot = starts[idx] + jnp.take_along_axis(pos, idx, axis=1)
    t_max = (m * TOP_K) // tm + n_exp
    r_max = t_max * tm
    tok_sorted = jnp.zeros((r_max,), I32).at[slot.reshape(-1)].set(
        jnp.repeat(jnp.arange(m, dtype=I32), TOP_K))
    n_tiles = (ends[-1] // tm).astype(I32).reshape(1)
    tile_start = jnp.minimum(jnp.arange(t_max, dtype=I32), n_tiles[0] - 1) * tm
    tile_expert = jnp.minimum(jnp.sum(ends[None, :] <= tile_start[:, None], axis=1),
                              n_exp - 1).astype(I32)

    a_sorted = _dispatch(n_tiles, tok_sorted, packed, tm=tm, r_max=r_max)
    th = _pick(fe, (256, 128))
    tf = 2 * th if fe >= 2 * th else th
    h_sorted = _expert_up(tile_expert, n_tiles, a_sorted, w_e1, w_e3, tm=tm, tf=tf)
    y_sorted = _expert_down(tile_expert, n_tiles, h_sorted, w_e2, tm=tm,
                            tn=_pick(d, (1024, 512, 256, 128)))
    tc = _pick(m, (256, 128, 64, 32, 16, 8))
    return _combine(slot, x, gate128, y_sorted, tc=tc, split=split)


def _block_diag(blocks):
    g, r, c = blocks.shape
    eye = jnp.eye(g, dtype=blocks.dtype)
    return (blocks[:, :, None, :] * eye[:, None, :, None]).reshape(g * r, g * c)


def _rel_table(rel_bias_l, q_pos, k_pos, max_rel):
    rel = np.clip(q_pos[:, None] - k_pos[None, :], -(CHUNK - 1), max_rel) + (CHUNK - 1)
    return rel_bias_l[:, rel].astype(F32)


def _band_table(rel_bias_l, tq, max_rel):
    n_heads = rel_bias_l.shape[0]
    past = BAND_PAST * CHUNK
    win = past + tq
    ring = -(-(tq + win - 1) // LANES) * LANES
    diff = np.arange(ring)
    diff = np.where(diff < tq, diff, diff - ring)
    idx = np.clip(diff + past, -(CHUNK - 1), max_rel) + (CHUNK - 1)
    by_diff = jnp.pad(rel_bias_l.astype(F32)[:, idx], ((0, 16 - n_heads), (0, 0)))

    def body(u_ref, o_ref):
        hp = pl.program_id(0)
        j = lax.broadcasted_iota(I32, (win, tq), 0)
        i = lax.broadcasted_iota(I32, (win, tq), 1)
        gap = (i + past) // CHUNK - j // CHUNK
        valid = (gap >= 0) & (gap <= BAND_PAST)
        for e in range(2):
            rows = jnp.broadcast_to(u_ref[pl.ds(2 * hp + e, 1), :], (win, ring))
            shifted = pltpu.roll(rows, 0, 1, stride=1, stride_axis=0)
            o_ref[:, e * tq:(e + 1) * tq] = jnp.where(valid, shifted[:, 0:tq] * LOG2E, NEG_INF)

    return pl.pallas_call(
        body,
        out_shape=jax.ShapeDtypeStruct((n_heads // 2, win, 2 * tq), F32),
        grid=(n_heads // 2,),
        in_specs=[pl.BlockSpec((16, ring), lambda hp: (0, 0))],
        out_specs=pl.BlockSpec((None, win, 2 * tq), lambda hp: (hp, 0, 0)),
        compiler_params=pltpu.CompilerParams(dimension_semantics=("parallel",)),
        name="band_table",
    )(by_diff)


def _tile_gain(g):
    return jnp.tile(g.astype(F32), LANES // g.shape[0]).reshape(1, LANES)


def kernel(x_prompt, x_sample, mem_prompt, cache_fox_k, cache_fox_v, cache_fox_logf, cache_band_k, cache_band_v, state_ssm_re, state_ssm_im, cache_mem_k, cache_mem_v, g_mix, w_in, b_f, g_qa, g_ka, g_qb, g_kb, rel_bias, lam_re, lam_im, log_dt, ssm_b_re, ssm_b_im, ssm_c_re, ssm_c_im, ssm_d, w_glu, b_glu, g_mix_out, w_out, g_cross, g_mem, w_cq, w_ck, w_cv, g_cq, g_ck, w_co, g_ffn, w_ff1, w_ff3, w_ff2, w_router, w_e1, w_e3, w_e2):
    batch, seq, d = x_prompt.shape
    dbatch, dseq, _ = x_sample.shape
    depth = g_mix.shape[0]
    past_len, h_a, dh = cache_fox_k.shape[2:]
    band_rows, h_b = cache_band_k.shape[2:4]
    g_c, p_state = lam_re.shape[1:]
    w_a, w_b, w_c = h_a * dh, h_b * dh, g_c * SSM_GROUP
    n_mem, h_m, dh_m = cache_mem_k.shape[2:]
    w_m = h_m * dh_m
    max_rel = rel_bias.shape[2] - CHUNK
    assert dh == 64 and h_a % 2 == 0 and h_b % 2 == 0 and dh_m == LANES
    assert w_a == w_b and w_c <= w_a and h_a <= 16
    mp, ms = batch * seq, dbatch * dseq
    m = mp + ms
    pa, pb = h_a // 2, h_b // 2
    n_state = g_c * p_state
    n_blk = 2 if (w_c % 512 == 0) else 1
    nband = min(BAND_PAST * CHUNK, seq)
    tm = _pick(m, (1056, 1024, 768, 512, 256, 128, 64, 32, 16))
    tq_band = _pick(seq, (256, 128, 64))

    x = jnp.concatenate([x_prompt.reshape(mp, d), x_sample.reshape(ms, d)], axis=0)
    fox_kt, fox_vt, band_kt, band_vt = (jnp.transpose(c, (0, 1, 3, 4, 2)) for c in
                                        (cache_fox_k, cache_fox_v, cache_band_k, cache_band_v))

    outs = {k: [] for k in ("p_fl", "p_sr", "p_si", "p_mk", "p_mv",
                            "s_fk", "s_fv", "s_fl", "s_bk", "s_bv", "s_sr", "s_si")}
    cache_a = cache_b = None
    for l in range(depth):
        sizes = (w_a, w_a, w_a, h_a, w_b, w_b, w_b)
        cuts = [sum(sizes[:i]) for i in range(len(sizes) + 1)]
        w_cat = _regroup_columns(
            w_in, l, ((0, cuts[3]), (cuts[4], cuts[7]), (cuts[7], w_in.shape[2]), (cuts[3], cuts[4])),
            7 * w_a)
        proj = _mm([x], [w_cat], gain=g_mix[l].reshape(1, d), tm=tm, tn=w_a, name="proj_in")
        col_uc = 6 * w_a

        fa_p = proj[:mp, col_uc + w_c:col_uc + w_c + h_a].reshape(batch, seq, h_a)
        fa_p = jnp.transpose(fa_p, (0, 2, 1)).reshape(batch * h_a, seq)
        bias_p = jnp.broadcast_to(jnp.tile(b_f[l], batch)[:, None], (batch * h_a, LANES))
        lf_p, f_p = _logf_cumsum(fa_p, bias_p, raw_from=0, valid_to=seq)
        lf_p = lf_p.reshape(batch, h_a, seq)
        f_p = f_p.reshape(batch, h_a, seq)
        f_t = (jnp.transpose(f_p, (0, 2, 1)) * LOG2E).reshape(batch, seq, pa, 2, 1)
        f_aug = jnp.tile(jnp.broadcast_to(f_t, (batch, seq, pa, 2, 3)).reshape(batch, seq, pa, 6),
                         (1, 1, 1, 2))
        f_aug = jnp.pad(f_aug, ((0, 0), (0, 0), (0, 0), (0, LANES - 12))).reshape(mp, pa * LANES)

        t_all = -(-(past_len + dseq) // LANES) * LANES
        fa_s = proj[mp:, col_uc + w_c:col_uc + w_c + h_a].reshape(dbatch, dseq, h_a)
        x_s = jnp.concatenate([jnp.transpose(cache_fox_logf[l], (0, 2, 1)),
                               jnp.transpose(fa_s, (0, 2, 1)),
                               jnp.zeros((dbatch, h_a, t_all - past_len - dseq), F32)], axis=2)
        bias_s = jnp.broadcast_to(jnp.tile(b_f[l], dbatch)[:, None], (dbatch * h_a, LANES))
        lf_s, f_s = _logf_cumsum(x_s.reshape(dbatch * h_a, t_all), bias_s,
                                 raw_from=past_len, valid_to=past_len + dseq)
        lf_s = lf_s.reshape(dbatch, h_a, t_all)[:, :, past_len:past_len + dseq]
        f_s = f_s.reshape(dbatch, h_a, t_all)
        fs_row = jnp.pad(f_s, ((0, 0), (0, 16 - h_a), (0, 0)))
        fs_col = jnp.pad(jnp.transpose(f_s[:, :, past_len:past_len + dseq], (0, 2, 1)),
                         ((0, 0), (0, 0), (0, LANES - h_a))).reshape(ms, LANES)

        gqa, gka = _tile_gain(g_qa[l]), _tile_gain(g_ka[l])
        oa, *cache_a = _fox_prompt(proj, f_aug, gqa, gka, batch=batch, seq=seq, n_pairs=pa,
                                   col_q=0, col_k=pa, col_v=2 * pa, m_rows=m, layer=l, depth=depth,
                                   prev=cache_a)
        oa, kn_as = _sample_attn(proj, fox_kt, fox_vt, l, fs_col, fs_row,
                                 gqa, gka, oa, mode="fox", batch=dbatch, s_new=dseq, n_pairs=pa,
                                 col_q=0, row0=mp)

        gqb, gkb = _tile_gain(g_qb[l]), _tile_gain(g_kb[l])
        tab = _band_table(rel_bias[l], tq_band, max_rel)
        ob, *cache_b = _band_prompt(proj, tab, gqb, gkb, batch=batch, seq=seq, n_pairs=pb,
                                    col_q=3 * pa, col_k=3 * pa + pb, col_v=3 * pa + 2 * pb,
                                    tq=tq_band, m_rows=m, layer=l, depth=depth, prev=cache_b)
        tab_s = _rel_table(rel_bias[l], band_rows + np.arange(dseq), np.arange(band_rows + dseq),
                           max_rel)
        ob, kn_bs = _sample_attn(proj, band_kt, band_vt, l,
                                 tab_s[:, :, :band_rows], tab_s[:, :, band_rows:], gqb, gkb, ob,
                                 mode="band", batch=dbatch, s_new=dseq, n_pairs=pb, col_q=3, row0=mp)

        rep = lambda a: jnp.repeat(a.astype(F32), SSM_GROUP, axis=0)
        a_re, a_im, bb_re, bb_im = _s5_prep(
            rep(lam_re[l]), rep(lam_im[l]),
            jnp.broadcast_to(rep(log_dt[l])[:, None], (g_c * SSM_GROUP, p_state)),
            jnp.transpose(ssm_b_re[l], (0, 2, 1)).reshape(g_c * SSM_GROUP, p_state),
            jnp.transpose(ssm_b_im[l], (0, 2, 1)).reshape(g_c * SSM_GROUP, p_state))
        a_re = a_re.reshape(g_c, SSM_GROUP, p_state)[:, 0, :].reshape(1, n_state)
        a_im = a_im.reshape(g_c, SSM_GROUP, p_state)[:, 0, :].reshape(1, n_state)
        b_re_d = _block_diag(bb_re.reshape(g_c, SSM_GROUP, p_state))
        b_im_d = _block_diag(bb_im.reshape(g_c, SSM_GROUP, p_state))
        c_re_d = _block_diag(jnp.transpose(ssm_c_re[l], (0, 2, 1)))
        c_im_d = _block_diag(jnp.transpose(ssm_c_im[l], (0, 2, 1)))
        d_row = ssm_d[l].reshape(1, w_c)
        s5_args = (a_re, a_im, b_re_d, b_im_d, c_re_d, c_im_d, d_row, w_glu[l],
                   b_glu[l].reshape(1, w_c))
        uc = proj[:, col_uc:col_uc + w_c]
        u_p = jnp.transpose(uc[:mp].reshape(batch, seq, w_c), (1, 0, 2)).reshape(mp, w_c)
        u_s = jnp.transpose(uc[mp:].reshape(dbatch, dseq, w_c), (1, 0, 2)).reshape(ms, w_c)
        zeros_p = jnp.zeros((batch, n_state), F32)
        oc_p, sr_p, si_p = _s5(u_p, zeros_p, zeros_p, *s5_args, nb=batch, seq=seq, n_blk=n_blk,
                               passes=1)
        oc_s, sr_s, si_s = _s5(u_s, state_ssm_re[l].reshape(dbatch, n_state),
                               state_ssm_im[l].reshape(dbatch, n_state), *s5_args,
                               nb=dbatch, seq=dseq, n_blk=n_blk, passes=3)
        oc = jnp.concatenate(
            [jnp.transpose(oc_p.reshape(seq, batch, w_c), (1, 0, 2)).reshape(mp, w_c),
             jnp.transpose(oc_s.reshape(dseq, dbatch, w_c), (1, 0, 2)).reshape(ms, w_c)], axis=0)

        x = _mm([oa, ob, oc], [w_out[l].astype(BF16)], gain=g_mix_out[l].reshape(1, -1), residual=x,
                tm=tm, tn=_pick(d, (1024, 512, 256, 128)), name="merge_out")

        outs["p_fl"].append(jnp.transpose(lf_p, (0, 2, 1)))
        outs["p_sr"].append(sr_p.reshape(batch, g_c, p_state))
        outs["p_si"].append(si_p.reshape(batch, g_c, p_state))
        outs["s_fk"].append(kn_as.reshape(dbatch, dseq, h_a, dh))
        outs["s_fv"].append(proj[mp:, 2 * w_a:3 * w_a].reshape(dbatch, dseq, h_a, dh))
        outs["s_fl"].append(jnp.transpose(lf_s, (0, 2, 1)))
        outs["s_bk"].append(kn_bs.reshape(dbatch, dseq, h_b, dh))
        outs["s_bv"].append(proj[mp:, 5 * w_a:6 * w_a].reshape(dbatch, dseq, h_b, dh))
        outs["s_sr"].append(sr_s.reshape(dbatch, g_c, p_state))
        outs["s_si"].append(si_s.reshape(dbatch, g_c, p_state))

        mem2 = mem_prompt.reshape(batch * n_mem, d)
        tmm = _pick(batch * n_mem, (1024, 512, 256, 128))
        gck = jnp.tile(g_ck[l].astype(F32), h_m).reshape(1, w_m)
        gcq = jnp.tile(g_cq[l].astype(F32), h_m).reshape(1, w_m)
        mk = _mm([mem2], [w_ck[l]], gain=g_mem[l].reshape(1, d), group_gain=gck,
                 epilogue="group_norm", tm=tmm, tn=w_m, name="mem_k")
        mv = _mm([mem2], [w_cv[l]], gain=g_mem[l].reshape(1, d), tm=tmm, tn=w_m, name="mem_v")
        outs["p_mk"].append(mk.reshape(batch, n_mem, h_m, dh_m))
        outs["p_mv"].append(mv.reshape(batch, n_mem, h_m, dh_m))
        q_c = _mm([x], [w_cq[l].astype(BF16)], gain=g_cross[l].reshape(1, d), group_gain=gcq,
                  epilogue="group_norm", tm=tm, tn=w_m, name="cross_q")
        o_c = _cross_attn(q_c, mk, mv, None, batch=batch, q_len=seq, n_mem=n_mem, n_heads=h_m,
                          dh=dh_m, row0=0, m_rows=m)
        o_c = _cross_attn(q_c, cache_mem_k[l].reshape(dbatch * n_mem, w_m),
                          cache_mem_v[l].reshape(dbatch * n_mem, w_m), o_c, batch=dbatch,
                          q_len=dseq, n_mem=n_mem, n_heads=h_m, dh=dh_m, row0=mp, m_rows=m)
        x = _mm([o_c], [w_co[l].astype(BF16)], residual=x, tm=tm,
                tn=_pick(d, (1024, 512, 256, 128)), name="cross_out")

        i = l // 2
        if l % 2 == 0:
            hmid = _mm([x], [w_ff1[i].astype(BF16), w_ff3[i].astype(BF16)],
                       gain=g_ffn[l].reshape(1, d), epilogue="swiglu", out_dtype=BF16, tm=tm,
                       tn=_pick(w_ff1.shape[2], (512, 256, 128)), name="ffn_up")
            x = _mm([hmid], [w_ff2[i].astype(BF16)], residual=x, tm=tm,
                    tn=_pick(d, (512, 256, 128)), name="ffn_down")
        else:
            x = _moe(x, g_ffn[l].reshape(1, d), w_router[i], w_e1[i], w_e3[i], w_e2[i], tm_tok=tm,
                     split=mp if l == depth - 1 else None)

    st = lambda k: jnp.stack(outs[k])
    xp, xs = x if isinstance(x, tuple) else (x[:mp], x[mp:])
    per_head = lambda t, h: jnp.transpose(t.reshape(depth, batch, h, dh, -1), (0, 1, 4, 2, 3))
    return (xp.reshape(batch, seq, d), xs.reshape(dbatch, dseq, d),
            per_head(cache_a[0], h_a), per_head(cache_a[1], h_a), st("p_fl"),
            per_head(cache_b[0][..., seq - nband:], h_b), per_head(cache_b[1][..., seq - nband:], h_b),
            st("p_sr"), st("p_si"),
            st("p_mk"), st("p_mv"), st("s_fk"), st("s_fv"), st("s_fl"), st("s_bk"), st("s_bv"),
            st("s_sr"), st("s_si"))
```

```python
import functools
import math

import jax
import jax.numpy as jnp
import numpy as np
from jax import lax
from jax.experimental import pallas as pl
from jax.experimental.pallas import tpu as pltpu

F32 = jnp.float32
BF16 = jnp.bfloat16
I32 = jnp.int32
U32 = jnp.uint32

EPS = 1e-6
NEG_INF = -1e30
LOG2E = math.log2(math.e)
CHUNK = 64
BAND_PAST = 8
SSM_GROUP = 16
TOP_K = 2

LANES = 128
SUBLANES = 8
BF16_ROWS = 16
VMEM_CAP = 60 * 1024 * 1024


def _vmem_limit(nbytes):
    return int(min(VMEM_CAP, max(16 * 1024 * 1024, nbytes * 5 // 4 + (4 << 20))))


def _pick(n, candidates):
    for c in candidates:
        if c <= n and n % c == 0:
            return c
    raise ValueError(f"no tile for {n} in {candidates}")


def _nbytes(shape, dtype):
    return math.prod(shape) * jnp.dtype(dtype).itemsize


def _split3(x):
    hi = x.astype(BF16)
    r1 = x - hi.astype(F32)
    mid = r1.astype(BF16)
    lo = (r1 - mid.astype(F32)).astype(BF16)
    return hi, mid, lo


def _dot(a, b):
    return jnp.dot(a, b, preferred_element_type=F32)


def _dot_nt(a, b):
    return lax.dot_general(a, b, (((1,), (1,)), ((), ())), preferred_element_type=F32)


def _dot_hp(a, b, passes=3):
    ah = a.astype(BF16)
    bh = b.astype(BF16)
    if passes == 1:
        return _dot(ah, bh)
    al = (a - ah.astype(F32)).astype(BF16)
    bl = (b - bh.astype(F32)).astype(BF16)
    return _dot(ah, bh) + (_dot(ah, bl) + _dot(al, bh))


def _pair_ones():
    r = lax.broadcasted_iota(I32, (LANES, LANES), 0) // 64
    c = lax.broadcasted_iota(I32, (LANES, LANES), 1) // 64
    return (r == c).astype(BF16)


def _pair_norm(x, g, ones):
    sq = x * x
    hi = sq.astype(BF16)
    lo = (sq - hi.astype(F32)).astype(BF16)
    ss = _dot(hi, ones) + _dot(lo, ones)
    return x * lax.rsqrt(ss * (1.0 / 64.0) + EPS) * g


def _log_sigmoid(x):
    return jnp.minimum(x, 0.0) - jnp.log(1.0 + jnp.exp(-jnp.abs(x)))


def _gelu_tanh(x):
    c = math.sqrt(2.0 / math.pi)
    return 0.5 * x * (1.0 + jnp.tanh(c * (x + 0.044715 * (x * x * x))))


def _mm_body(*refs, widths, norm, n_w, epilogue, has_res, has_gg, stage_a, tm, rc, kc, k_total):
    it = iter(refs)
    a_refs = [next(it) for _ in widths]
    g_ref = next(it) if norm else None
    w_refs = [next(it) for _ in range(n_w)]
    gg_ref = next(it) if has_gg else None
    res_ref = next(it) if has_res else None
    o_ref = next(it)
    abf = next(it) if stage_a else a_refs[0]
    cast_w = w_refs[0].dtype != BF16
    wbfs = [next(it) for _ in range(n_w)] if cast_w else w_refs
    j = pl.program_id(1)

    if stage_a:
        @pl.when(j == 0)
        def _():
            off = 0
            for a_ref, wd in zip(a_refs, widths):
                def chunk(c, carry, a_ref=a_ref, off=off, wd=wd):
                    r0 = pl.multiple_of(c * rc, rc)
                    x = a_ref[pl.ds(r0, rc), :].astype(F32)
                    if norm:
                        ms = jnp.mean(x * x, axis=-1, keepdims=True)
                        x = x * lax.rsqrt(ms + EPS) * g_ref[:, off:off + wd]
                    abf[pl.ds(r0, rc), off:off + wd] = x.astype(BF16)
                    return carry
                lax.fori_loop(0, tm // rc, chunk, 0)
                off += wd

    if cast_w:
        for w_ref, wbf in zip(w_refs, wbfs):
            def cast(c, carry, w_ref=w_ref, wbf=wbf):
                r0 = pl.multiple_of(c * kc, kc)
                wbf[pl.ds(r0, kc), :] = w_ref[pl.ds(r0, kc), :].astype(BF16)
                return carry
            lax.fori_loop(0, k_total // kc, cast, 0)

    def rows(c, carry):
        r0 = pl.multiple_of(c * rc, rc)
        a = abf[pl.ds(r0, rc), :]
        ys = [_dot(a, wbf[...]) for wbf in wbfs]
        if epilogue == "swiglu":
            y = jax.nn.silu(ys[0]) * ys[1]
        elif epilogue == "group_norm":
            parts = []
            for s in range(ys[0].shape[1] // LANES):
                ysl = ys[0][:, s * LANES:(s + 1) * LANES]
                ms = jnp.mean(ysl * ysl, axis=-1, keepdims=True)
                parts.append(ysl * lax.rsqrt(ms + EPS) * gg_ref[:, s * LANES:(s + 1) * LANES])
            y = jnp.concatenate(parts, axis=1)
        else:
            y = ys[0]
        if has_res:
            y = y + res_ref[pl.ds(r0, rc), :]
        o_ref[pl.ds(r0, rc), :] = y.astype(o_ref.dtype)
        return carry
    lax.fori_loop(0, tm // rc, rows, 0)


def _mm(a_parts, w_list, *, gain=None, group_gain=None, residual=None, epilogue="none",
        out_dtype=F32, tm, tn, name):
    m = a_parts[0].shape[0]
    widths = tuple(a.shape[1] for a in a_parts)
    k_total = sum(widths)
    n = w_list[0].shape[1]
    assert m % tm == 0 and n % tn == 0, (m, tm, n, tn)
    norm = gain is not None
    stage_a = norm or len(a_parts) > 1 or a_parts[0].dtype != BF16
    rc = _pick(tm, (512, 384, 352, 256, 176, 128, 64, 32, 16))
    kc = _pick(k_total, (512, 256, 128))
    grid = (m // tm, n // tn)
    in_specs = [pl.BlockSpec((tm, wd), lambda i, j: (i, 0)) for wd in widths]
    args = list(a_parts)
    est = sum(2 * _nbytes((tm, wd), a.dtype) for wd, a in zip(widths, a_parts))
    if norm:
        in_specs.append(pl.BlockSpec((1, k_total), lambda i, j: (0, 0)))
        args.append(gain)
    cast_w = w_list[0].dtype != BF16
    for w in w_list:
        in_specs.append(pl.BlockSpec((k_total, tn), lambda i, j: (0, j)))
        args.append(w)
        est += 2 * _nbytes((k_total, tn), w.dtype) + cast_w * _nbytes((k_total, tn), BF16)
    if group_gain is not None:
        in_specs.append(pl.BlockSpec((1, tn), lambda i, j: (0, j)))
        args.append(group_gain)
    if residual is not None:
        in_specs.append(pl.BlockSpec((tm, tn), lambda i, j: (i, j)))
        args.append(residual)
        est += 2 * _nbytes((tm, tn), F32)
    est += 2 * _nbytes((tm, tn), out_dtype) + stage_a * _nbytes((tm, k_total), BF16)
    est += 4 * _nbytes((rc, tn), F32) * len(w_list)
    body = functools.partial(
        _mm_body, widths=widths, norm=norm, n_w=len(w_list), epilogue=epilogue,
        has_res=residual is not None, has_gg=group_gain is not None, stage_a=stage_a, tm=tm,
        rc=rc, kc=kc, k_total=k_total)
    return pl.pallas_call(
        body,
        out_shape=jax.ShapeDtypeStruct((m, n), out_dtype),
        grid=grid,
        in_specs=in_specs,
        out_specs=pl.BlockSpec((tm, tn), lambda i, j: (i, j)),
        scratch_shapes=[pltpu.VMEM((tm, k_total), BF16)] * stage_a
        + [pltpu.VMEM((k_total, tn), BF16) for _ in w_list] * cast_w,
        compiler_params=pltpu.CompilerParams(
            dimension_semantics=("parallel", "arbitrary"),
            vmem_limit_bytes=_vmem_limit(est)),
        name=name,
    )(*args)


def _regroup_body(w_ref, o_ref, *, cuts, width):
    off = 0
    for lo, hi in cuts:
        o_ref[:, off:off + hi - lo] = w_ref[:, lo:hi].astype(o_ref.dtype)
        off += hi - lo
    if off < width:
        o_ref[:, off:width] = jnp.zeros((o_ref.shape[0], width - off), o_ref.dtype)


def _regroup_columns(w, layer, cuts, width):
    _, k, n = w.shape
    tr = _pick(k, (256, 128, 64, 32, 16))
    return pl.pallas_call(
        functools.partial(_regroup_body, cuts=cuts, width=width),
        out_shape=jax.ShapeDtypeStruct((k, width), BF16),
        grid=(k // tr,),
        in_specs=[pl.BlockSpec((None, tr, n), lambda i: (layer, i, 0))],
        out_specs=pl.BlockSpec((tr, width), lambda i: (i, 0)),
        compiler_params=pltpu.CompilerParams(
            dimension_semantics=("parallel",),
            vmem_limit_bytes=_vmem_limit(2 * _nbytes((tr, n), F32) + 2 * _nbytes((tr, width), BF16)
                                         + 4 * _nbytes((tr, width), F32))),
        name="regroup_w_in",
    )(w)


def _cumsum_body(x_ref, b_ref, lf_ref, f_ref, carry, *, raw_from, valid_to, tt):
    j = pl.program_id(0)

    @pl.when(j == 0)
    def _():
        carry[...] = jnp.zeros_like(carry)

    x = x_ref[...]
    lane = j * tt + lax.broadcasted_iota(I32, x.shape, 1)
    lf = jnp.where(lane >= raw_from, _log_sigmoid(x + b_ref[:, 0:1]), x)
    lf = jnp.where(lane < valid_to, lf, 0.0)
    lf_ref[...] = lf
    tri = (lax.broadcasted_iota(I32, (tt, tt), 0)
           <= lax.broadcasted_iota(I32, (tt, tt), 1)).astype(BF16)
    hi, mid, lo = _split3(lf)
    y = _dot(hi, tri) + _dot(mid, tri) + _dot(lo, tri) + carry[:, 0:1]
    f_ref[...] = y
    carry[...] = jnp.broadcast_to(y[:, tt - 1:tt], carry.shape)


def _logf_cumsum(x, bias, *, raw_from, valid_to):
    r, t = x.shape
    tt = _pick(t, (256, 128))
    return pl.pallas_call(
        functools.partial(_cumsum_body, raw_from=raw_from, valid_to=valid_to, tt=tt),
        out_shape=(jax.ShapeDtypeStruct((r, t), F32), jax.ShapeDtypeStruct((r, t), F32)),
        grid=(t // tt,),
        in_specs=[pl.BlockSpec((r, tt), lambda j: (0, j)),
                  pl.BlockSpec((r, LANES), lambda j: (0, 0))],
        out_specs=(pl.BlockSpec((r, tt), lambda j: (0, j)),
                   pl.BlockSpec((r, tt), lambda j: (0, j))),
        scratch_shapes=[pltpu.VMEM((r, LANES), F32)],
        compiler_params=pltpu.CompilerParams(dimension_semantics=("arbitrary",)),
        name="logf_cumsum",
    )(x, bias)


def _logf_rows_body(fa_ref, b_ref, lf_ref, faug_ref, carry, *, tt, n_heads):
    j = pl.program_id(1)

    @pl.when(j == 0)
    def _():
        carry[...] = jnp.zeros_like(carry)

    lane = lax.broadcasted_iota(I32, (tt, LANES), 1)
    lf = jnp.where(lane < n_heads, _log_sigmoid(fa_ref[...] + b_ref[...]), 0.0)
    lf_ref[...] = lf
    tri = (lax.broadcasted_iota(I32, (tt, tt), 0)
           >= lax.broadcasted_iota(I32, (tt, tt), 1)).astype(BF16)
    hi, mid, lo = _split3(lf)
    f = _dot(tri, hi) + _dot(tri, mid) + _dot(tri, lo) + carry[0:1, :]
    carry[...] = jnp.broadcast_to(f[tt - 1:tt, :], carry.shape)
    width = faug_ref.shape[1]
    head = lax.broadcasted_iota(I32, (LANES, width), 0)
    col = lax.broadcasted_iota(I32, (LANES, width), 1)
    within = col % LANES
    sel = ((within < 12) & (head == 2 * (col // LANES) + (within % 6) // 3)).astype(BF16)
    hi, mid, lo = _split3(f * LOG2E)
    faug_ref[...] = _dot(hi, sel) + _dot(mid, sel) + _dot(lo, sel)


def _logf_rows(proj, bias_row, *, batch, seq, n_heads, n_pairs, col_block):
    tt = _pick(seq, (256, 128))
    nt = seq // tt
    return pl.pallas_call(
        functools.partial(_logf_rows_body, tt=tt, n_heads=n_heads),
        out_shape=(jax.ShapeDtypeStruct((batch * seq, LANES), F32),
                   jax.ShapeDtypeStruct((batch * seq, n_pairs * LANES), F32)),
        grid=(batch, nt),
        in_specs=[pl.BlockSpec((tt, LANES), lambda b, j: (b * nt + j, col_block)),
                  pl.BlockSpec((1, LANES), lambda b, j: (0, 0))],
        out_specs=(pl.BlockSpec((tt, LANES), lambda b, j: (b * nt + j, 0)),
                   pl.BlockSpec((tt, n_pairs * LANES), lambda b, j: (b * nt + j, 0))),
        scratch_shapes=[pltpu.VMEM((SUBLANES, LANES), F32)],
        compiler_params=pltpu.CompilerParams(dimension_semantics=("parallel", "arbitrary")),
        name="logf_rows",
    )(proj, bias_row)


def _stack_heads(qn):
    lane = lax.broadcasted_iota(I32, qn.shape, 1)
    q0 = jnp.where(lane < 64, qn, 0.0)
    q1 = jnp.where(lane < 64, 0.0, qn)
    return jnp.concatenate([q0, q1], axis=0).astype(BF16)


def _unstack_heads(o, tq):
    lane = lax.broadcasted_iota(I32, (tq, LANES), 1)
    return jnp.where(lane < 64, o[:tq], o[tq:])


def _lane_column(block, h):
    lane = lax.broadcasted_iota(I32, block.shape, 1)
    return jnp.sum(jnp.where(lane == h, block, 0.0), axis=-1, keepdims=True)


V_ROWS = LANES + BF16_ROWS


def _split_select(x):
    hi, mid, lo = _split3(x)
    m3 = lax.broadcasted_iota(I32, x.shape, 1) % 3
    return jnp.where(m3 == 0, hi.astype(F32), jnp.where(m3 == 1, mid.astype(F32), lo.astype(F32)))


def _head_masks(shape):
    lane = lax.broadcasted_iota(I32, shape, 1)
    return lane, lane < 64


def _finish_heads(acc, tq):
    o0 = acc[0:64, 0:tq] / acc[LANES:LANES + 1, 0:tq]
    o1 = acc[64:LANES, tq:2 * tq] / acc[LANES:LANES + 1, tq:2 * tq]
    return jnp.concatenate([o0, o1], axis=0).T


def _fox_prompt_body(q_ref, k_ref, v_ref, f_ref, gq_ref, gk_ref, *rest, tq, seq, dh):
    o_ref, knt_ref, vt_ref, kaug, vt3 = rest[-5:]
    ones = _pair_ones()
    nq = seq // tq
    lane, first = _head_masks((tq, LANES))

    for c in range(nq):
        rows = slice(c * tq, (c + 1) * tq)
        kn = _pair_norm(k_ref[rows, :], gk_ref[...], ones)
        knt_ref[:, rows] = kn.T
        kaug[rows, 0:LANES] = kn.astype(BF16)
        sp = _split_select(f_ref[rows, :])
        kaug[rows, LANES:2 * LANES] = jnp.where(
            lane < 6, -sp, jnp.where(lane < 12, 1.0, 0.0)).astype(BF16)
        vt = v_ref[rows, :].T
        vt_ref[:, rows] = vt
        vt3[c, 0:LANES, :] = vt.astype(BF16)
        vt3[c, LANES:V_ROWS, :] = jnp.ones((BF16_ROWS, tq), BF16)

    causal = (lax.broadcasted_iota(I32, (tq, 2 * tq), 0)
              <= lax.broadcasted_iota(I32, (tq, 2 * tq), 1) % tq)

    for qi in range(nq):
        rows = slice(qi * tq, (qi + 1) * tq)
        qn = _pair_norm(q_ref[rows, :], gq_ref[...], ones) * (dh ** -0.5 * LOG2E)
        sp = _split_select(f_ref[rows, :])
        up0 = jnp.where(lane < 3, 1.0, jnp.where((lane >= 6) & (lane < 9), sp, 0.0))
        up1 = jnp.where((lane >= 3) & (lane < 6), 1.0,
                        jnp.where((lane >= 9) & (lane < 12), sp, 0.0))
        qs = jnp.concatenate(
            [jnp.concatenate([jnp.where(first, qn, 0.0), up0], axis=1),
             jnp.concatenate([jnp.where(first, 0.0, qn), up1], axis=1)], axis=0).astype(BF16)

        s = jnp.where(causal, _dot_nt(kaug[rows, :], qs), NEG_INF)
        m = jnp.max(s, axis=0, keepdims=True)
        acc = _dot(vt3[qi], jnp.exp2(s - m).astype(BF16))
        for kj in range(qi):
            s = _dot_nt(kaug[kj * tq:(kj + 1) * tq, :], qs)
            m_new = jnp.maximum(m, jnp.max(s, axis=0, keepdims=True))
            acc = jnp.exp2(m - m_new) * acc + _dot(vt3[kj], jnp.exp2(s - m_new).astype(BF16))
            m = m_new
        o_ref[rows, :] = _finish_heads(acc, tq)


def _layer_cache_outputs(layer, depth, batch, width, seq, prev, n_inputs, index):
    shape = jax.ShapeDtypeStruct((depth, batch, width, seq), F32)
    spec = pl.BlockSpec((None, None, LANES, seq), lambda *g: (layer, *index(*g), 0))
    extra_specs = [] if prev is None else [pl.BlockSpec(memory_space=pl.ANY)] * 2
    aliases = {} if prev is None else {n_inputs: 1, n_inputs + 1: 2}
    return (shape, shape), (spec, spec), extra_specs, aliases, (() if prev is None else tuple(prev))


def _fox_prompt(proj, f_aug, gq, gk, *, batch, seq, n_pairs, col_q, col_k, col_v, m_rows, layer,
                depth, prev):
    tq = _pick(seq, (512, 256, 128))
    nq = seq // tq
    width = n_pairs * LANES
    est = (2 * 7 * _nbytes((seq, LANES), F32)
           + _nbytes((seq, 2 * LANES), BF16) + _nbytes((V_ROWS, seq), BF16)
           + 16 * _nbytes((tq, 2 * tq), F32))
    c_shapes, c_specs, extra_specs, aliases, extra_args = _layer_cache_outputs(
        layer, depth, batch, width, seq, prev, 6, lambda b, hp: (b, hp))
    return pl.pallas_call(
        functools.partial(_fox_prompt_body, tq=tq, seq=seq, dh=64),
        out_shape=(jax.ShapeDtypeStruct((m_rows, width), F32), *c_shapes),
        grid=(batch, n_pairs),
        in_specs=[
            pl.BlockSpec((seq, LANES), lambda b, hp: (b, col_q + hp)),
            pl.BlockSpec((seq, LANES), lambda b, hp: (b, col_k + hp)),
            pl.BlockSpec((seq, LANES), lambda b, hp: (b, col_v + hp)),
            pl.BlockSpec((seq, LANES), lambda b, hp: (b, hp)),
            pl.BlockSpec((1, LANES), lambda b, hp: (0, 0)),
            pl.BlockSpec((1, LANES), lambda b, hp: (0, 0)),
            *extra_specs,
        ],
        out_specs=(pl.BlockSpec((seq, LANES), lambda b, hp: (b, hp)), *c_specs),
        input_output_aliases=aliases,
        scratch_shapes=[pltpu.VMEM((seq, 2 * LANES), BF16), pltpu.VMEM((nq, V_ROWS, tq), BF16)],
        compiler_params=pltpu.CompilerParams(
            dimension_semantics=("parallel", "parallel"),
            vmem_limit_bytes=_vmem_limit(est)),
        name="fox_prompt",
    )(proj, proj, proj, f_aug, gq, gk, *extra_args)


def _band_prompt_body(q_ref, k_ref, v_ref, tab_ref, gq_ref, gk_ref, *rest, tq, seq, past, dh):
    o_ref, knt_ref, vt_ref, kpad, vt3 = rest[-5:]
    ones = _pair_ones()
    nq = seq // tq
    npad = past // tq
    win = past + tq

    kpad[0:past, :] = jnp.zeros((past, LANES), BF16)
    for c in range(npad):
        vt3[c] = jnp.zeros((V_ROWS, tq), BF16)
    for c in range(nq):
        rows = slice(c * tq, (c + 1) * tq)
        kn = _pair_norm(k_ref[rows, :], gk_ref[...], ones)
        knt_ref[:, rows] = kn.T
        kpad[past + c * tq:past + (c + 1) * tq, :] = kn.astype(BF16)
        vt = v_ref[rows, :].T
        vt_ref[:, rows] = vt
        vt3[npad + c, 0:LANES, :] = vt.astype(BF16)
        vt3[npad + c, LANES:V_ROWS, :] = jnp.ones((BF16_ROWS, tq), BF16)

    row = lax.broadcasted_iota(I32, (win, 2 * tq), 0)

    for qi in range(nq):
        r0 = qi * tq
        qn = _pair_norm(q_ref[r0:r0 + tq, :], gq_ref[...], ones) * (dh ** -0.5 * LOG2E)
        s = _dot_nt(kpad[r0:r0 + win, :], _stack_heads(qn)) + tab_ref[...]
        if r0 < past:
            s = jnp.where(row >= past - r0, s, NEG_INF)
        m = jnp.max(s, axis=0, keepdims=True)
        p = jnp.exp2(s - m).astype(BF16)
        acc = _dot(vt3[qi], p[0:tq, :])
        for c in range(1, win // tq):
            acc = acc + _dot(vt3[qi + c], p[c * tq:(c + 1) * tq, :])
        o_ref[r0:r0 + tq, :] = _finish_heads(acc, tq)


def _band_prompt(proj, tab, gq, gk, *, batch, seq, n_pairs, col_q, col_k, col_v, tq, m_rows, layer,
                 depth, prev):
    past = BAND_PAST * CHUNK
    assert past % tq == 0 and seq % tq == 0
    nq = seq // tq
    width = n_pairs * LANES
    win = past + tq
    est = (2 * 5 * _nbytes((seq, LANES), F32)
           + _nbytes((seq + past, LANES), BF16) + _nbytes((V_ROWS, seq + past), BF16)
           + 2 * _nbytes((win, 2 * tq), F32) + 8 * _nbytes((win, 2 * tq), F32))
    c_shapes, c_specs, extra_specs, aliases, extra_args = _layer_cache_outputs(
        layer, depth, batch, width, seq, prev, 6, lambda hp, b: (b, hp))
    return pl.pallas_call(
        functools.partial(_band_prompt_body, tq=tq, seq=seq, past=past, dh=64),
        out_shape=(jax.ShapeDtypeStruct((m_rows, width), F32), *c_shapes),
        grid=(n_pairs, batch),
        in_specs=[
            pl.BlockSpec((seq, LANES), lambda hp, b: (b, col_q + hp)),
            pl.BlockSpec((seq, LANES), lambda hp, b: (b, col_k + hp)),
            pl.BlockSpec((seq, LANES), lambda hp, b: (b, col_v + hp)),
            pl.BlockSpec((None, win, 2 * tq), lambda hp, b: (hp, 0, 0)),
            pl.BlockSpec((1, LANES), lambda hp, b: (0, 0)),
            pl.BlockSpec((1, LANES), lambda hp, b: (0, 0)),
            *extra_specs,
        ],
        out_specs=(pl.BlockSpec((seq, LANES), lambda hp, b: (b, hp)), *c_specs),
        input_output_aliases=aliases,
        scratch_shapes=[pltpu.VMEM((seq + past, LANES), BF16),
                        pltpu.VMEM(((seq + past) // tq, V_ROWS, tq), BF16)],
        compiler_params=pltpu.CompilerParams(
            dimension_semantics=("parallel", "parallel"),
            vmem_limit_bytes=_vmem_limit(est)),
        name="band_prompt",
    )(proj, proj, proj, tab, gq, gk, *extra_args)


def _sample_attn_body(*refs, mode, n_pairs, s_new, n_cache, dh):
    if mode == "fox":
        (q_ref, k_ref, v_ref, ck_ref, cv_ref, fq_ref, ft_ref, gq_ref, gk_ref, _alias,
         o_ref, kn_ref) = refs
    else:
        (q_ref, k_ref, v_ref, ck_ref, cv_ref, tabc_ref, tabn_ref, gq_ref, gk_ref, _alias,
         o_ref, kn_ref) = refs
    ones = _pair_ones()
    row = lax.broadcasted_iota(I32, (s_new, s_new), 0)
    col = lax.broadcasted_iota(I32, (s_new, s_new), 1)
    for hp in range(n_pairs):
        sl = slice(hp * LANES, (hp + 1) * LANES)
        qn = _pair_norm(q_ref[:, sl], gq_ref[...], ones) * (dh ** -0.5)
        kn = _pair_norm(k_ref[:, sl], gk_ref[...], ones)
        kn_ref[:, sl] = kn
        vn = v_ref[:, sl]
        outs = []
        for e in range(2):
            h = 2 * hp + e
            hl = slice(e * dh, (e + 1) * dh)
            qh = qn[:, hl].astype(BF16)
            sc = _dot(qh, ck_ref[h].astype(BF16))
            sn = _dot_nt(qh, kn[:, hl].astype(BF16))
            if mode == "fox":
                fq = _lane_column(fq_ref[...], h)
                fk = ft_ref[h:h + 1, :]
                sc = sc + fq - fk[:, :n_cache]
                sn = jnp.where(col <= row, sn + fq - fk[:, n_cache:n_cache + s_new], NEG_INF)
            else:
                sc = sc + tabc_ref[h]
                sn = sn + tabn_ref[h]
            m = jnp.maximum(jnp.max(sc, axis=-1, keepdims=True),
                            jnp.max(sn, axis=-1, keepdims=True))
            pc = jnp.exp(sc - m)
            pn = jnp.exp(sn - m)
            l = jnp.sum(pc, axis=-1, keepdims=True) + jnp.sum(pn, axis=-1, keepdims=True)
            outs.append((_dot_nt(pc.astype(BF16), cv_ref[h].astype(BF16))
                         + _dot(pn.astype(BF16), vn[:, hl].astype(BF16))) / l)
        o_ref[:, sl] = jnp.concatenate(outs, axis=1)


def _sample_attn(proj, cache_kt, cache_vt, layer, extra_a, extra_b, gq, gk, o_buf, *, mode, batch,
                 s_new, n_pairs, col_q, row0):
    width = n_pairs * LANES
    _, _, n_heads, dh, n_cache = cache_kt.shape
    rb = row0 // s_new
    if mode == "fox":
        ex_specs = [pl.BlockSpec((s_new, LANES), lambda b: (b, 0)),
                    pl.BlockSpec((None, 16, extra_b.shape[2]), lambda b: (b, 0, 0))]
    else:
        ex_specs = [pl.BlockSpec(extra_a.shape, lambda b: (0, 0, 0)),
                    pl.BlockSpec(extra_b.shape, lambda b: (0, 0, 0))]
    est = (2 * 2 * _nbytes((n_cache, width), F32) + 8 * _nbytes((s_new, width), F32)
           + 2 * _nbytes(extra_a.shape, F32) + 16 * _nbytes((2 * s_new, n_cache), F32)
           + 4 * _nbytes((n_cache, LANES), BF16))
    return pl.pallas_call(
        functools.partial(_sample_attn_body, mode=mode, n_pairs=n_pairs, s_new=s_new,
                          n_cache=n_cache, dh=dh),
        out_shape=(jax.ShapeDtypeStruct(o_buf.shape, F32),
                   jax.ShapeDtypeStruct((batch * s_new, width), F32)),
        grid=(batch,),
        in_specs=[
            pl.BlockSpec((s_new, width), lambda b: (rb + b, col_q)),
            pl.BlockSpec((s_new, width), lambda b: (rb + b, col_q + 1)),
            pl.BlockSpec((s_new, width), lambda b: (rb + b, col_q + 2)),
            pl.BlockSpec((None, None, n_heads, dh, n_cache), lambda b: (layer, b, 0, 0, 0)),
            pl.BlockSpec((None, None, n_heads, dh, n_cache), lambda b: (layer, b, 0, 0, 0)),
            *ex_specs,
            pl.BlockSpec((1, LANES), lambda b: (0, 0)),
            pl.BlockSpec((1, LANES), lambda b: (0, 0)),
            pl.BlockSpec(memory_space=pl.ANY),
        ],
        out_specs=(pl.BlockSpec((s_new, width), lambda b: (rb + b, 0)),
                   pl.BlockSpec((s_new, width), lambda b: (b, 0))),
        input_output_aliases={9: 0},
        compiler_params=pltpu.CompilerParams(
            dimension_semantics=("parallel",), vmem_limit_bytes=_vmem_limit(est)),
        name=f"{mode}_sample",
    )(proj, proj, proj, cache_kt, cache_vt, extra_a, extra_b, gq, gk, o_buf)


def _cross_body(*refs, n_heads, dh, aliased):
    if aliased:
        q_ref, k_ref, v_ref, _alias, o_ref = refs
    else:
        q_ref, k_ref, v_ref, o_ref = refs
    for h in range(n_heads):
        sl = slice(h * dh, (h + 1) * dh)
        q = (q_ref[:, sl] * (dh ** -0.5)).astype(BF16)
        s = _dot_nt(q, k_ref[:, sl].astype(BF16))
        m = jnp.max(s, axis=-1, keepdims=True)
        p = jnp.exp(s - m)
        l = jnp.sum(p, axis=-1, keepdims=True)
        o = _dot(p.astype(BF16), v_ref[:, sl].astype(BF16)) / l
        o_ref[:, sl] = o.astype(o_ref.dtype)


def _cross_attn(q_all, k, v, o_buf, *, batch, q_len, n_mem, n_heads, dh, row0, m_rows):
    width = n_heads * dh
    tq = _pick(q_len, (512, 256, 128, 64, 32, 16))
    nq = q_len // tq
    rb = row0 // tq
    aliased = o_buf is not None
    in_specs = [pl.BlockSpec((tq, width), lambda b, qi: (rb + b * nq + qi, 0)),
                pl.BlockSpec((n_mem, width), lambda b, qi: (b, 0)),
                pl.BlockSpec((n_mem, width), lambda b, qi: (b, 0))]
    args = [q_all, k, v]
    if aliased:
        in_specs.append(pl.BlockSpec(memory_space=pl.ANY))
        args.append(o_buf)
    est = (2 * _nbytes((tq, width), F32) + 4 * _nbytes((n_mem, width), F32)
           + 2 * _nbytes((tq, width), BF16) + 12 * _nbytes((tq, n_mem), F32))
    return pl.pallas_call(
        functools.partial(_cross_body, n_heads=n_heads, dh=dh, aliased=aliased),
        out_shape=jax.ShapeDtypeStruct((m_rows, width), BF16),
        grid=(batch, nq),
        in_specs=in_specs,
        out_specs=pl.BlockSpec((tq, width), lambda b, qi: (rb + b * nq + qi, 0)),
        input_output_aliases={3: 0} if aliased else {},
        compiler_params=pltpu.CompilerParams(
            dimension_semantics=("parallel", "arbitrary"), vmem_limit_bytes=_vmem_limit(est)),
        name="cross_attn",
    )(*args)


def _s5_prep_body(lr_ref, li_ref, ldt_ref, br_ref, bi_ref, ar_ref, ai_ref, bbr_ref, bbi_ref):
    lr = lr_ref[...]
    li = li_ref[...]
    dt = jnp.exp(ldt_ref[...])
    mag = jnp.exp(lr * dt)
    a_re = mag * jnp.cos(li * dt)
    a_im = mag * jnp.sin(li * dt)
    den = lr * lr + li * li
    num_re = a_re - 1.0
    coef_re = (num_re * lr + a_im * li) / den
    coef_im = (a_im * lr - num_re * li) / den
    br = br_ref[...]
    bi = bi_ref[...]
    ar_ref[...] = a_re
    ai_ref[...] = a_im
    bbr_ref[...] = coef_re * br - coef_im * bi
    bbi_ref[...] = coef_re * bi + coef_im * br


def _s5_prep(lam_re, lam_im, log_dt, b_re, b_im):
    shape = lam_re.shape
    spec = pl.BlockSpec(shape, lambda: (0, 0))
    return pl.pallas_call(
        _s5_prep_body,
        out_shape=tuple(jax.ShapeDtypeStruct(shape, F32) for _ in range(4)),
        in_specs=[spec] * 5,
        out_specs=tuple([spec] * 4),
        name="s5_discretise",
    )(lam_re, lam_im, log_dt, b_re, b_im)


def _s5_body(*refs, nb, t_chunk, n_blk, passes, per_stream):
    n_u = nb if per_stream else 1
    u_refs = refs[:n_u]
    (x0r_ref, x0i_ref, ar_ref, ai_ref, bre_ref, bim_ref, cre_ref, cim_ref, d_ref, wg_ref, bg_ref,
     o_ref, xr_out, xi_out, bur, bui, st_r, st_i) = refs[n_u:n_u + 18]
    i = pl.program_id(0)
    wc = u_refs[0].shape[1]
    ns = bur.shape[1]
    ub = wc // n_blk
    sb = ns // n_blk

    @pl.when(i == 0)
    def _():
        st_r[...] = x0r_ref[...]
        st_i[...] = x0i_ref[...]

    if per_stream:
        slab = refs[n_u + 18]
        for b in range(nb):
            for k in range(wc // LANES):
                slab[k, pl.ds(b, t_chunk, stride=nb), :] = u_refs[b][:, k * LANES:(k + 1) * LANES]
        u = jnp.concatenate([slab[k] for k in range(wc // LANES)], axis=1)
    else:
        u = u_refs[0][...]
    for k in range(n_blk):
        uk = u[:, k * ub:(k + 1) * ub]
        bur[:, k * sb:(k + 1) * sb] = _dot_hp(
            uk, bre_ref[k * ub:(k + 1) * ub, k * sb:(k + 1) * sb], passes)
        bui[:, k * sb:(k + 1) * sb] = _dot_hp(
            uk, bim_ref[k * ub:(k + 1) * ub, k * sb:(k + 1) * sb], passes)

    a_re = ar_ref[...]
    a_im = ai_ref[...]

    def step(t, carry):
        xr, xi = carry
        r0 = pl.multiple_of(t * nb, nb)
        nr = a_re * xr - a_im * xi + bur[pl.ds(r0, nb), :]
        ni = a_re * xi + a_im * xr + bui[pl.ds(r0, nb), :]
        bur[pl.ds(r0, nb), :] = nr
        bui[pl.ds(r0, nb), :] = ni
        return nr, ni
    xr_f, xi_f = lax.fori_loop(0, t_chunk, step, (st_r[...], st_i[...]))
    st_r[...] = xr_f
    st_i[...] = xi_f
    xr_out[...] = xr_f
    xi_out[...] = xi_f
    ys = []
    for k in range(n_blk):
        xrk = bur[:, k * sb:(k + 1) * sb].astype(BF16)
        xik = bui[:, k * sb:(k + 1) * sb].astype(BF16)
        ys.append(_dot(xrk, cre_ref[k * sb:(k + 1) * sb, k * ub:(k + 1) * ub].astype(BF16))
                  - _dot(xik, cim_ref[k * sb:(k + 1) * sb, k * ub:(k + 1) * ub].astype(BF16)))
    y = jnp.concatenate(ys, axis=1) + d_ref[...] * u
    z = _gelu_tanh(y)
    gate = _dot(z.astype(BF16), wg_ref[...].astype(BF16)) + bg_ref[...]
    out = z * jax.nn.sigmoid(gate)
    if per_stream:
        for k in range(wc // LANES):
            slab[k] = out[:, k * LANES:(k + 1) * LANES]
        for b in range(nb):
            o_ref[b] = jnp.concatenate(
                [slab[k, pl.ds(b, t_chunk, stride=nb), :] for k in range(wc // LANES)], axis=1)
    else:
        o_ref[...] = out


def _s5(u, x0r, x0i, a_re, a_im, b_re, b_im, c_re, c_im, d, w_glu, b_glu, *, nb, seq, n_blk,
        passes, col_block=None):
    per_stream = col_block is not None
    wc, ns = b_re.shape
    t_chunk = _pick(seq, (64, 32, 16))
    nc = seq // t_chunk
    rows = nb * t_chunk
    full = lambda shape: pl.BlockSpec(shape, lambda i: (0,) * len(shape))
    est = (6 * _nbytes((rows, wc), F32) + 2 * _nbytes((rows, ns), F32)
           + 2 * 4 * _nbytes((wc, ns), F32) + 2 * _nbytes((wc, wc), F32)
           + 12 * _nbytes((rows, ns // n_blk), F32) + 8 * _nbytes((nb, ns), F32))
    if per_stream:
        u_specs = [pl.BlockSpec((t_chunk, wc), lambda i, s=s: (s * nc + i, col_block))
                   for s in range(nb)]
        u_args = [u] * nb
        o_shape = jax.ShapeDtypeStruct((nb, seq, wc), F32)
        o_spec = pl.BlockSpec((nb, t_chunk, wc), lambda i: (0, i, 0))
        slab = [pltpu.VMEM((wc // LANES, rows, LANES), F32)]
    else:
        u_specs = [pl.BlockSpec((rows, wc), lambda i: (i, 0))]
        u_args = [u]
        o_shape = jax.ShapeDtypeStruct((seq * nb, wc), F32)
        o_spec = pl.BlockSpec((rows, wc), lambda i: (i, 0))
        slab = []
    return pl.pallas_call(
        functools.partial(_s5_body, nb=nb, t_chunk=t_chunk, n_blk=n_blk, passes=passes,
                          per_stream=per_stream),
        out_shape=(o_shape, jax.ShapeDtypeStruct((nb, ns), F32), jax.ShapeDtypeStruct((nb, ns), F32)),
        grid=(nc,),
        in_specs=[*u_specs,
                  full((nb, ns)), full((nb, ns)), full((1, ns)), full((1, ns)),
                  full((wc, ns)), full((wc, ns)), full((ns, wc)), full((ns, wc)),
                  full((1, wc)), full((wc, wc)), full((1, wc))],
        out_specs=(o_spec, full((nb, ns)), full((nb, ns))),
        scratch_shapes=[pltpu.VMEM((rows, ns), F32), pltpu.VMEM((rows, ns), F32),
                        pltpu.VMEM((nb, ns), F32), pltpu.VMEM((nb, ns), F32), *slab],
        compiler_params=pltpu.CompilerParams(
            dimension_semantics=("arbitrary",), vmem_limit_bytes=_vmem_limit(est)),
        name="s5_scan",
    )(*u_args, x0r, x0i, a_re, a_im, b_re, b_im, c_re, c_im, d, w_glu, b_glu)


def _router_body(x_ref, g_ref, wr_ref, pk_ref, idx_ref, gate_ref, *, n_exp, rc):
    tm, d = x_ref.shape
    half = d // 2

    def chunk(c, carry):
        r0 = pl.multiple_of(c * rc, rc)
        x = x_ref[pl.ds(r0, rc), :]
        ms = jnp.mean(x * x, axis=-1, keepdims=True)
        h = x * lax.rsqrt(ms + EPS) * g_ref[...]
        hb = h.astype(BF16).astype(F32)
        lo = lax.shift_right_logical(pltpu.bitcast(hb[:, :half], U32), jnp.uint32(16))
        hi = pltpu.bitcast(hb[:, half:], U32) & jnp.uint32(0xFFFF0000)
        pk_ref[pl.ds(r0, rc), :] = lo | hi
        logits = _dot_hp(h, wr_ref[...])
        lane = lax.broadcasted_iota(I32, logits.shape, 1)
        logits = jnp.where(lane < n_exp, logits, NEG_INF)
        mx = jnp.max(logits, axis=-1, keepdims=True)
        e = jnp.exp(logits - mx)
        probs = e / jnp.sum(e, axis=-1, keepdims=True)
        probs = jnp.where(lane < n_exp, probs, -1.0)
        lane_f = lane.astype(F32)
        p1 = jnp.max(probs, axis=-1, keepdims=True)
        i1 = jnp.min(jnp.where(probs == p1, lane_f, float(LANES)), axis=-1, keepdims=True)
        rest = jnp.where(lane_f == i1, -1.0, probs)
        p2 = jnp.max(rest, axis=-1, keepdims=True)
        i2 = jnp.min(jnp.where(rest == p2, lane_f, float(LANES)), axis=-1, keepdims=True)
        tot = p1 + p2
        idx_ref[pl.ds(r0, rc), :] = jnp.where(lane == 0, i1, jnp.where(lane == 1, i2, 0.0)).astype(I32)
        gate_ref[pl.ds(r0, rc), :] = jnp.where(lane == 0, p1 / tot,
                                                jnp.where(lane == 1, p2 / tot, 0.0))
        return carry
    lax.fori_loop(0, tm // rc, chunk, 0)


def _router(x, g, w_router_pad, *, n_exp, tm):
    m, d = x.shape
    rc = _pick(tm, (256, 176, 128, 64, 32, 16, 8))
    est = 2 * _nbytes((tm, d), F32) + 2 * _nbytes((tm, d // 2), U32) + 16 * _nbytes((rc, d), F32)
    return pl.pallas_call(
        functools.partial(_router_body, n_exp=n_exp, rc=rc),
        out_shape=(jax.ShapeDtypeStruct((m, d // 2), U32),
                   jax.ShapeDtypeStruct((m, LANES), I32),
                   jax.ShapeDtypeStruct((m, LANES), F32)),
        grid=(m // tm,),
        in_specs=[pl.BlockSpec((tm, d), lambda i: (i, 0)),
                  pl.BlockSpec((1, d), lambda i: (0, 0)),
                  pl.BlockSpec((d, LANES), lambda i: (0, 0))],
        out_specs=(pl.BlockSpec((tm, d // 2), lambda i: (i, 0)),
                   pl.BlockSpec((tm, LANES), lambda i: (i, 0)),
                   pl.BlockSpec((tm, LANES), lambda i: (i, 0))),
        compiler_params=pltpu.CompilerParams(
            dimension_semantics=("parallel",), vmem_limit_bytes=_vmem_limit(est)),
        name="moe_router",
    )(x, g, w_router_pad)


GATHER_UNROLL = 8


def _row_copy(src_hbm, row, dst, r, sem):
    return pltpu.make_async_copy(src_hbm.at[pl.ds(row, 1), :], dst.at[pl.ds(r, 1), :], sem)


def _dispatch_body(nt_ref, tok_ref, nxt_ref, pk_hbm, a_ref, buf, sem, *, tm):
    i = pl.program_id(0)
    nt = nt_ref[0]
    half = buf.shape[2]

    def request(ids_ref, slot):
        def issue(c, carry):
            for u in range(GATHER_UNROLL):
                r = c * GATHER_UNROLL + u
                _row_copy(pk_hbm, ids_ref[0, 0, r], buf.at[slot], r, sem.at[slot]).start(
                    priority=u % 2)
            return carry
        lax.fori_loop(0, tm // GATHER_UNROLL, issue, 0)

    @pl.when(i == 0)
    def _():
        request(tok_ref, 0)

    @pl.when(i + 1 < nt)
    def _():
        request(nxt_ref, (i + 1) % 2)

    @pl.when(i < nt)
    def _():
        slot = i % 2
        pltpu.make_async_copy(pk_hbm.at[pl.ds(0, tm), :], buf.at[slot], sem.at[slot]).wait()
        pk = buf[slot]
        lo = pltpu.bitcast(lax.shift_left(pk, jnp.uint32(16)), F32)
        hi = pltpu.bitcast(pk & jnp.uint32(0xFFFF0000), F32)
        a_ref[:, :half] = lo.astype(BF16)
        a_ref[:, half:] = hi.astype(BF16)


def _dispatch(n_tiles, tok_sorted, packed, *, tm, r_max):
    m, half = packed.shape
    t_max = r_max // tm
    assert tm % GATHER_UNROLL == 0
    tok_tiles = tok_sorted.reshape(t_max, 1, tm)
    return pl.pallas_call(
        functools.partial(_dispatch_body, tm=tm),
        out_shape=jax.ShapeDtypeStruct((r_max, 2 * half), BF16),
        grid_spec=pltpu.PrefetchScalarGridSpec(
            num_scalar_prefetch=1,
            grid=(t_max,),
            in_specs=[pl.BlockSpec((1, 1, tm), lambda i, nt: (i, 0, 0), memory_space=pltpu.SMEM),
                      pl.BlockSpec((1, 1, tm), lambda i, nt: (jnp.minimum(i + 1, t_max - 1), 0, 0),
                                   memory_space=pltpu.SMEM),
                      pl.BlockSpec(memory_space=pl.ANY)],
            out_specs=pl.BlockSpec((tm, 2 * half), lambda i, nt: (jnp.minimum(i, nt[0] - 1), 0)),
            scratch_shapes=[pltpu.VMEM((2, tm, half), U32), pltpu.SemaphoreType.DMA((2,))]),
        compiler_params=pltpu.CompilerParams(dimension_semantics=("arbitrary",)),
        name="moe_dispatch",
    )(n_tiles, tok_tiles, tok_tiles, packed)


def _expert_up_body(te_ref, nt_ref, a_ref, w1_ref, w3_ref, h_ref, w1bf, w3bf, *, kc, rem):
    f = pl.program_id(0)
    i = pl.program_id(1)
    last = pl.num_programs(0) - 1
    d, tf = w1bf.shape
    fresh = jnp.logical_or(i == 0, te_ref[i] != te_ref[jnp.maximum(i - 1, 0)])
    live = i < nt_ref[0]

    def run(cols, when):
        @pl.when(jnp.logical_and(when, jnp.logical_and(fresh, live)))
        def _():
            def cast(c, carry):
                r0 = pl.multiple_of(c * kc, kc)
                w1bf[pl.ds(r0, kc), 0:cols] = w1_ref[pl.ds(r0, kc), 0:cols].astype(BF16)
                w3bf[pl.ds(r0, kc), 0:cols] = w3_ref[pl.ds(r0, kc), 0:cols].astype(BF16)
                return carry
            lax.fori_loop(0, d // kc, cast, 0)

        @pl.when(jnp.logical_and(when, live))
        def _():
            a = a_ref[...]
            h_ref[:, 0:cols] = (jax.nn.silu(_dot(a, w1bf[:, 0:cols]))
                                * _dot(a, w3bf[:, 0:cols])).astype(h_ref.dtype)

    if rem:
        run(tf, f < last)
        run(rem, f == last)
    else:
        run(tf, True)


def _expert_up(tile_expert, n_tiles, a_sorted, w1, w3, *, tm, tf):
    r_max, d = a_sorted.shape
    n_exp, _, fe = w1.shape
    t_max = r_max // tm
    kc = _pick(d, (512, 256, 128))
    row = lambda f, i, te, nt: jnp.minimum(i, nt[0] - 1)
    est = (2 * _nbytes((tm, d), BF16) + 2 * 2 * _nbytes((d, tf), F32) + 2 * _nbytes((d, tf), BF16)
           + 2 * _nbytes((tm, tf), BF16) + 6 * _nbytes((tm, tf), F32))
    return pl.pallas_call(
        functools.partial(_expert_up_body, kc=kc, rem=fe % tf),
        out_shape=jax.ShapeDtypeStruct((r_max, fe), BF16),
        grid_spec=pltpu.PrefetchScalarGridSpec(
            num_scalar_prefetch=2,
            grid=(pl.cdiv(fe, tf), t_max),
            in_specs=[pl.BlockSpec((tm, d), lambda f, i, te, nt: (row(f, i, te, nt), 0)),
                      pl.BlockSpec((None, d, tf), lambda f, i, te, nt: (te[i], 0, f)),
                      pl.BlockSpec((None, d, tf), lambda f, i, te, nt: (te[i], 0, f))],
            out_specs=pl.BlockSpec((tm, tf), lambda f, i, te, nt: (row(f, i, te, nt), f)),
            scratch_shapes=[pltpu.VMEM((d, tf), BF16), pltpu.VMEM((d, tf), BF16)]),
        compiler_params=pltpu.CompilerParams(
            dimension_semantics=("arbitrary", "arbitrary"), vmem_limit_bytes=_vmem_limit(est)),
        name="moe_expert_up",
    )(tile_expert, n_tiles, a_sorted, w1, w3)


def _expert_down_body(te_ref, nt_ref, h_ref, w2_ref, y_ref, w2bf, *, kc):
    i = pl.program_id(1)
    fe = h_ref.shape[1]
    fresh = jnp.logical_or(i == 0, te_ref[i] != te_ref[jnp.maximum(i - 1, 0)])

    @pl.when(jnp.logical_and(fresh, i < nt_ref[0]))
    def _():
        def cast(c, carry):
            r0 = pl.multiple_of(c * kc, kc)
            w2bf[pl.ds(r0, kc), :] = w2_ref[pl.ds(r0, kc), :].astype(BF16)
            return carry
        lax.fori_loop(0, fe // kc, cast, 0)

    @pl.when(i < nt_ref[0])
    def _():
        y_ref[...] = _dot(h_ref[...], w2bf[...])


def _expert_down(tile_expert, n_tiles, h_sorted, w2, *, tm, tn):
    r_max, fe = h_sorted.shape
    d = w2.shape[2]
    t_max = r_max // tm
    kc = _pick(fe, (512, 256, 128))
    row = lambda n, i, te, nt: jnp.minimum(i, nt[0] - 1)
    est = (2 * _nbytes((tm, fe), BF16) + 2 * _nbytes((fe, tn), F32) + _nbytes((fe, tn), BF16)
           + 4 * _nbytes((tm, tn), F32))
    return pl.pallas_call(
        functools.partial(_expert_down_body, kc=kc),
        out_shape=jax.ShapeDtypeStruct((r_max, d), F32),
        grid_spec=pltpu.PrefetchScalarGridSpec(
            num_scalar_prefetch=2,
            grid=(d // tn, t_max),
            in_specs=[pl.BlockSpec((tm, fe), lambda n, i, te, nt: (row(n, i, te, nt), 0)),
                      pl.BlockSpec((None, fe, tn), lambda n, i, te, nt: (te[i], 0, n))],
            out_specs=pl.BlockSpec((tm, tn), lambda n, i, te, nt: (row(n, i, te, nt), n)),
            scratch_shapes=[pltpu.VMEM((fe, tn), BF16)]),
        compiler_params=pltpu.CompilerParams(
            dimension_semantics=("arbitrary", "arbitrary"), vmem_limit_bytes=_vmem_limit(est)),
        name="moe_expert_down",
    )(tile_expert, n_tiles, h_sorted, w2)


def _combine_body(slot_ref, nxt_ref, x_ref, gate_ref, y_hbm, *rest, tc, n_first):
    o_refs, (buf, sem) = rest[:-2], rest[-2:]
    i = pl.program_id(0)
    n = pl.num_programs(0)
    unroll = GATHER_UNROLL // TOP_K

    def request(ids_ref, slot):
        def issue(c, carry):
            for u in range(unroll):
                r = c * unroll + u
                for k in range(TOP_K):
                    _row_copy(y_hbm, ids_ref[0, 0, TOP_K * r + k], buf.at[slot, k], r,
                              sem.at[slot]).start(priority=k % 2)
            return carry
        lax.fori_loop(0, tc // unroll, issue, 0)

    @pl.when(i == 0)
    def _():
        request(slot_ref, 0)

    @pl.when(i + 1 < n)
    def _():
        request(nxt_ref, (i + 1) % 2)

    slot = i % 2
    for k in range(TOP_K):
        pltpu.make_async_copy(y_hbm.at[pl.ds(0, tc), :], buf.at[slot, k], sem.at[slot]).wait()
    g = gate_ref[...]
    val = x_ref[...] + (g[:, 0:1] * buf[slot, 0] + g[:, 1:2] * buf[slot, 1])
    if len(o_refs) == 1:
        o_refs[0][...] = val
    else:
        @pl.when(i < n_first)
        def _():
            o_refs[0][...] = val

        @pl.when(i >= n_first)
        def _():
            o_refs[1][...] = val


def _combine(slots, x, gates, y_sorted, *, tc, split=None):
    m, d = x.shape
    n = m // tc
    assert tc % (GATHER_UNROLL // TOP_K) == 0
    est = 4 * _nbytes((tc, d), F32) + 2 * TOP_K * _nbytes((tc, d), F32) + 6 * _nbytes((tc, d), F32)
    slot_tiles = slots.reshape(n, 1, TOP_K * tc)
    if split is None or split % tc or (m - split) % tc:
        n_first = n
        out_shape = jax.ShapeDtypeStruct((m, d), F32)
        out_specs = pl.BlockSpec((tc, d), lambda i: (i, 0))
    else:
        n_first = split // tc
        out_shape = (jax.ShapeDtypeStruct((split, d), F32), jax.ShapeDtypeStruct((m - split, d), F32))
        out_specs = (pl.BlockSpec((tc, d), lambda i: (jnp.minimum(i, n_first - 1), 0)),
                     pl.BlockSpec((tc, d), lambda i: (jnp.maximum(i - n_first, 0), 0)))
    return pl.pallas_call(
        functools.partial(_combine_body, tc=tc, n_first=n_first),
        out_shape=out_shape,
        grid=(n,),
        in_specs=[pl.BlockSpec((1, 1, TOP_K * tc), lambda i: (i, 0, 0), memory_space=pltpu.SMEM),
                  pl.BlockSpec((1, 1, TOP_K * tc), lambda i: (jnp.minimum(i + 1, n - 1), 0, 0),
                               memory_space=pltpu.SMEM),
                  pl.BlockSpec((tc, d), lambda i: (i, 0)),
                  pl.BlockSpec((tc, LANES), lambda i: (i, 0)),
                  pl.BlockSpec(memory_space=pl.ANY)],
        out_specs=out_specs,
        scratch_shapes=[pltpu.VMEM((2, TOP_K, tc, d), F32), pltpu.SemaphoreType.DMA((2,))],
        compiler_params=pltpu.CompilerParams(
            dimension_semantics=("arbitrary",), vmem_limit_bytes=_vmem_limit(est)),
        name="moe_combine",
    )(slot_tiles, slot_tiles, x, gates, y_sorted)


def _moe(x, g_ffn, w_router, w_e1, w_e3, w_e2, *, tm_tok, split=None):
    m, d = x.shape
    n_exp, _, fe = w_e1.shape
    tm = _pick(m * TOP_K, (512, 256, 128, 64, 32, 16))
    wr = jnp.zeros((d, LANES), F32).at[:, :n_exp].set(w_router)
    packed, idx128, gate128 = _router(x, g_ffn, wr, n_exp=n_exp, tm=tm_tok)

    idx = idx128[:, :TOP_K]
    mask = jnp.sum(idx[:, :, None] == jnp.arange(n_exp, dtype=I32)[None, None, :], axis=1).astype(I32)
    counts = jnp.sum(mask, axis=0)
    padded = ((counts + tm - 1) // tm) * tm
    ends = jnp.cumsum(padded)
    starts = ends - padded
    pos = jnp.cumsum(mask, axis=0) - mask
    slot = starts[idx] + jnp.take_along_axis(pos, idx, axis=1)
    t_max = (m * TOP_K) // tm + n_exp
    r_max = t_max * tm
    tok_sorted = jnp.zeros((r_max,), I32).at[slot.reshape(-1)].set(
        jnp.repeat(jnp.arange(m, dtype=I32), TOP_K))
    n_tiles = (ends[-1] // tm).astype(I32).reshape(1)
    tile_start = jnp.minimum(jnp.arange(t_max, dtype=I32), n_tiles[0] - 1) * tm
    tile_expert = jnp.minimum(jnp.sum(ends[None, :] <= tile_start[:, None], axis=1),
                              n_exp - 1).astype(I32)

    a_sorted = _dispatch(n_tiles, tok_sorted, packed, tm=tm, r_max=r_max)
    th = _pick(fe, (256, 128))
    tf = 2 * th if fe >= 2 * th else th
    h_sorted = _expert_up(tile_expert, n_tiles, a_sorted, w_e1, w_e3, tm=tm, tf=tf)
    y_sorted = _expert_down(tile_expert, n_tiles, h_sorted, w_e2, tm=tm,
                            tn=_pick(d, (1024, 512, 256, 128)))
    tc = _pick(m, (256, 128, 64, 32, 16, 8))
    return _combine(slot, x, gate128, y_sorted, tc=tc, split=split)


def _block_diag(blocks):
    g, r, c = blocks.shape
    eye = jnp.eye(g, dtype=blocks.dtype)
    return (blocks[:, :, None, :] * eye[:, None, :, None]).reshape(g * r, g * c)


def _rel_table(rel_bias_l, q_pos, k_pos, max_rel):
    rel = np.clip(q_pos[:, None] - k_pos[None, :], -(CHUNK - 1), max_rel) + (CHUNK - 1)
    return rel_bias_l[:, rel].astype(F32)


def _band_table(rel_bias_l, tq, max_rel):
    n_heads = rel_bias_l.shape[0]
    past = BAND_PAST * CHUNK
    win = past + tq
    ring = -(-(tq + win - 1) // LANES) * LANES
    diff = np.arange(ring)
    diff = np.where(diff < tq, diff, diff - ring)
    idx = np.clip(diff + past, -(CHUNK - 1), max_rel) + (CHUNK - 1)
    by_diff = jnp.pad(rel_bias_l.astype(F32)[:, idx], ((0, 16 - n_heads), (0, 0)))

    def body(u_ref, o_ref):
        hp = pl.program_id(0)
        j = lax.broadcasted_iota(I32, (win, tq), 0)
        i = lax.broadcasted_iota(I32, (win, tq), 1)
        gap = (i + past) // CHUNK - j // CHUNK
        valid = (gap >= 0) & (gap <= BAND_PAST)
        for e in range(2):
            rows = jnp.broadcast_to(u_ref[pl.ds(2 * hp + e, 1), :], (win, ring))
            shifted = pltpu.roll(rows, 0, 1, stride=1, stride_axis=0)
            o_ref[:, e * tq:(e + 1) * tq] = jnp.where(valid, shifted[:, 0:tq] * LOG2E, NEG_INF)

    return pl.pallas_call(
        body,
        out_shape=jax.ShapeDtypeStruct((n_heads // 2, win, 2 * tq), F32),
        grid=(n_heads // 2,),
        in_specs=[pl.BlockSpec((16, ring), lambda hp: (0, 0))],
        out_specs=pl.BlockSpec((None, win, 2 * tq), lambda hp: (hp, 0, 0)),
        compiler_params=pltpu.CompilerParams(dimension_semantics=("parallel",)),
        name="band_table",
    )(by_diff)


def _tile_gain(g):
    return jnp.tile(g.astype(F32), LANES // g.shape[0]).reshape(1, LANES)


def kernel(x_prompt, x_sample, mem_prompt, cache_fox_k, cache_fox_v, cache_fox_logf, cache_band_k, cache_band_v, state_ssm_re, state_ssm_im, cache_mem_k, cache_mem_v, g_mix, w_in, b_f, g_qa, g_ka, g_qb, g_kb, rel_bias, lam_re, lam_im, log_dt, ssm_b_re, ssm_b_im, ssm_c_re, ssm_c_im, ssm_d, w_glu, b_glu, g_mix_out, w_out, g_cross, g_mem, w_cq, w_ck, w_cv, g_cq, g_ck, w_co, g_ffn, w_ff1, w_ff3, w_ff2, w_router, w_e1, w_e3, w_e2):
    batch, seq, d = x_prompt.shape
    dbatch, dseq, _ = x_sample.shape
    depth = g_mix.shape[0]
    past_len, h_a, dh = cache_fox_k.shape[2:]
    band_rows, h_b = cache_band_k.shape[2:4]
    g_c, p_state = lam_re.shape[1:]
    w_a, w_b, w_c = h_a * dh, h_b * dh, g_c * SSM_GROUP
    n_mem, h_m, dh_m = cache_mem_k.shape[2:]
    w_m = h_m * dh_m
    max_rel = rel_bias.shape[2] - CHUNK
    assert dh == 64 and h_a % 2 == 0 and h_b % 2 == 0 and dh_m == LANES
    assert w_a == w_b and w_c <= w_a and h_a <= 16
    mp, ms = batch * seq, dbatch * dseq
    m = mp + ms
    pa, pb = h_a // 2, h_b // 2
    n_state = g_c * p_state
    n_blk = 2 if (w_c % 512 == 0) else 1
    nband = min(BAND_PAST * CHUNK, seq)
    tm = _pick(m, (1056, 1024, 768, 512, 256, 128, 64, 32, 16))
    tq_band = _pick(seq, (256, 128, 64))

    x = jnp.concatenate([x_prompt.reshape(mp, d), x_sample.reshape(ms, d)], axis=0)
    fox_kt, fox_vt, band_kt, band_vt = (jnp.transpose(c, (0, 1, 3, 4, 2)) for c in
                                        (cache_fox_k, cache_fox_v, cache_band_k, cache_band_v))

    outs = {k: [] for k in ("p_fl", "p_sr", "p_si", "p_mk", "p_mv",
                            "s_fk", "s_fv", "s_fl", "s_bk", "s_bv", "s_sr", "s_si")}
    cache_a = cache_b = None
    for l in range(depth):
        sizes = (w_a, w_a, w_a, h_a, w_b, w_b, w_b)
        cuts = [sum(sizes[:i]) for i in range(len(sizes) + 1)]
        w_cat = _regroup_columns(
            w_in, l, ((0, cuts[3]), (cuts[4], cuts[7]), (cuts[7], w_in.shape[2]), (cuts[3], cuts[4])),
            7 * w_a)
        proj = _mm([x], [w_cat], gain=g_mix[l].reshape(1, d), tm=tm, tn=w_a, name="proj_in")
        col_uc = 6 * w_a

        assert (col_uc + w_c) % LANES == 0
        lf_p, f_aug = _logf_rows(proj, jnp.pad(b_f[l].astype(F32), (0, LANES - h_a)).reshape(1, LANES),
                                 batch=batch, seq=seq, n_heads=h_a, n_pairs=pa,
                                 col_block=(col_uc + w_c) // LANES)

        t_all = -(-(past_len + dseq) // LANES) * LANES
        fa_s = proj[mp:, col_uc + w_c:col_uc + w_c + h_a].reshape(dbatch, dseq, h_a)
        x_s = jnp.concatenate([jnp.transpose(cache_fox_logf[l], (0, 2, 1)),
                               jnp.transpose(fa_s, (0, 2, 1)),
                               jnp.zeros((dbatch, h_a, t_all - past_len - dseq), F32)], axis=2)
        bias_s = jnp.broadcast_to(jnp.tile(b_f[l], dbatch)[:, None], (dbatch * h_a, LANES))
        lf_s, f_s = _logf_cumsum(x_s.reshape(dbatch * h_a, t_all), bias_s,
                                 raw_from=past_len, valid_to=past_len + dseq)
        lf_s = lf_s.reshape(dbatch, h_a, t_all)[:, :, past_len:past_len + dseq]
        f_s = f_s.reshape(dbatch, h_a, t_all)
        fs_row = jnp.pad(f_s, ((0, 0), (0, 16 - h_a), (0, 0)))
        fs_col = jnp.pad(jnp.transpose(f_s[:, :, past_len:past_len + dseq], (0, 2, 1)),
                         ((0, 0), (0, 0), (0, LANES - h_a))).reshape(ms, LANES)

        gqa, gka = _tile_gain(g_qa[l]), _tile_gain(g_ka[l])
        oa, *cache_a = _fox_prompt(proj, f_aug, gqa, gka, batch=batch, seq=seq, n_pairs=pa,
                                   col_q=0, col_k=pa, col_v=2 * pa, m_rows=m, layer=l, depth=depth,
                                   prev=cache_a)
        oa, kn_as = _sample_attn(proj, fox_kt, fox_vt, l, fs_col, fs_row,
                                 gqa, gka, oa, mode="fox", batch=dbatch, s_new=dseq, n_pairs=pa,
                                 col_q=0, row0=mp)

        gqb, gkb = _tile_gain(g_qb[l]), _tile_gain(g_kb[l])
        tab = _band_table(rel_bias[l], tq_band, max_rel)
        ob, *cache_b = _band_prompt(proj, tab, gqb, gkb, batch=batch, seq=seq, n_pairs=pb,
                                    col_q=3 * pa, col_k=3 * pa + pb, col_v=3 * pa + 2 * pb,
                                    tq=tq_band, m_rows=m, layer=l, depth=depth, prev=cache_b)
        tab_s = _rel_table(rel_bias[l], band_rows + np.arange(dseq), np.arange(band_rows + dseq),
                           max_rel)
        ob, kn_bs = _sample_attn(proj, band_kt, band_vt, l,
                                 tab_s[:, :, :band_rows], tab_s[:, :, band_rows:], gqb, gkb, ob,
                                 mode="band", batch=dbatch, s_new=dseq, n_pairs=pb, col_q=3, row0=mp)

        rep = lambda a: jnp.repeat(a.astype(F32), SSM_GROUP, axis=0)
        a_re, a_im, bb_re, bb_im = _s5_prep(
            rep(lam_re[l]), rep(lam_im[l]),
            jnp.broadcast_to(rep(log_dt[l])[:, None], (g_c * SSM_GROUP, p_state)),
            jnp.transpose(ssm_b_re[l], (0, 2, 1)).reshape(g_c * SSM_GROUP, p_state),
            jnp.transpose(ssm_b_im[l], (0, 2, 1)).reshape(g_c * SSM_GROUP, p_state))
        a_re = a_re.reshape(g_c, SSM_GROUP, p_state)[:, 0, :].reshape(1, n_state)
        a_im = a_im.reshape(g_c, SSM_GROUP, p_state)[:, 0, :].reshape(1, n_state)
        b_re_d = _block_diag(bb_re.reshape(g_c, SSM_GROUP, p_state))
        b_im_d = _block_diag(bb_im.reshape(g_c, SSM_GROUP, p_state))
        c_re_d = _block_diag(jnp.transpose(ssm_c_re[l], (0, 2, 1)))
        c_im_d = _block_diag(jnp.transpose(ssm_c_im[l], (0, 2, 1)))
        d_row = ssm_d[l].reshape(1, w_c)
        s5_args = (a_re, a_im, b_re_d, b_im_d, c_re_d, c_im_d, d_row, w_glu[l],
                   b_glu[l].reshape(1, w_c))
        assert col_uc % w_c == 0 and w_c % LANES == 0
        u_s = jnp.transpose(proj[mp:, col_uc:col_uc + w_c].reshape(dbatch, dseq, w_c),
                            (1, 0, 2)).reshape(ms, w_c)
        zeros_p = jnp.zeros((batch, n_state), F32)
        oc_p, sr_p, si_p = _s5(proj, zeros_p, zeros_p, *s5_args, nb=batch, seq=seq, n_blk=n_blk,
                               passes=1, col_block=col_uc // w_c)
        oc_s, sr_s, si_s = _s5(u_s, state_ssm_re[l].reshape(dbatch, n_state),
                               state_ssm_im[l].reshape(dbatch, n_state), *s5_args,
                               nb=dbatch, seq=dseq, n_blk=n_blk, passes=3)
        oc = jnp.concatenate(
            [oc_p.reshape(mp, w_c),
             jnp.transpose(oc_s.reshape(dseq, dbatch, w_c), (1, 0, 2)).reshape(ms, w_c)], axis=0)

        x = _mm([oa, ob, oc], [w_out[l].astype(BF16)], gain=g_mix_out[l].reshape(1, -1), residual=x,
                tm=tm, tn=_pick(d, (1024, 512, 256, 128)), name="merge_out")

        outs["p_fl"].append(lf_p[:, :h_a].reshape(batch, seq, h_a))
        outs["p_sr"].append(sr_p.reshape(batch, g_c, p_state))
        outs["p_si"].append(si_p.reshape(batch, g_c, p_state))
        outs["s_fk"].append(kn_as.reshape(dbatch, dseq, h_a, dh))
        outs["s_fv"].append(proj[mp:, 2 * w_a:3 * w_a].reshape(dbatch, dseq, h_a, dh))
        outs["s_fl"].append(jnp.transpose(lf_s, (0, 2, 1)))
        outs["s_bk"].append(kn_bs.reshape(dbatch, dseq, h_b, dh))
        outs["s_bv"].append(proj[mp:, 5 * w_a:6 * w_a].reshape(dbatch, dseq, h_b, dh))
        outs["s_sr"].append(sr_s.reshape(dbatch, g_c, p_state))
        outs["s_si"].append(si_s.reshape(dbatch, g_c, p_state))

        mem2 = mem_prompt.reshape(batch * n_mem, d)
        tmm = _pick(batch * n_mem, (1024, 512, 256, 128))
        gck = jnp.tile(g_ck[l].astype(F32), h_m).reshape(1, w_m)
        gcq = jnp.tile(g_cq[l].astype(F32), h_m).reshape(1, w_m)
        mk = _mm([mem2], [w_ck[l]], gain=g_mem[l].reshape(1, d), group_gain=gck,
                 epilogue="group_norm", tm=tmm, tn=w_m, name="mem_k")
        mv = _mm([mem2], [w_cv[l]], gain=g_mem[l].reshape(1, d), tm=tmm, tn=w_m, name="mem_v")
        outs["p_mk"].append(mk.reshape(batch, n_mem, h_m, dh_m))
        outs["p_mv"].append(mv.reshape(batch, n_mem, h_m, dh_m))
        q_c = _mm([x], [w_cq[l].astype(BF16)], gain=g_cross[l].reshape(1, d), group_gain=gcq,
                  epilogue="group_norm", tm=tm, tn=w_m, name="cross_q")
        o_c = _cross_attn(q_c, mk, mv, None, batch=batch, q_len=seq, n_mem=n_mem, n_heads=h_m,
                          dh=dh_m, row0=0, m_rows=m)
        o_c = _cross_attn(q_c, cache_mem_k[l].reshape(dbatch * n_mem, w_m),
                          cache_mem_v[l].reshape(dbatch * n_mem, w_m), o_c, batch=dbatch,
                          q_len=dseq, n_mem=n_mem, n_heads=h_m, dh=dh_m, row0=mp, m_rows=m)
        x = _mm([o_c], [w_co[l].astype(BF16)], residual=x, tm=tm,
                tn=_pick(d, (1024, 512, 256, 128)), name="cross_out")

        i = l // 2
        if l % 2 == 0:
            hmid = _mm([x], [w_ff1[i].astype(BF16), w_ff3[i].astype(BF16)],
                       gain=g_ffn[l].reshape(1, d), epilogue="swiglu", out_dtype=BF16, tm=tm,
                       tn=_pick(w_ff1.shape[2], (512, 256, 128)), name="ffn_up")
            x = _mm([hmid], [w_ff2[i].astype(BF16)], residual=x, tm=tm,
                    tn=_pick(d, (512, 256, 128)), name="ffn_down")
        else:
            x = _moe(x, g_ffn[l].reshape(1, d), w_router[i], w_e1[i], w_e3[i], w_e2[i], tm_tok=tm,
                     split=mp if l == depth - 1 else None)

    st = lambda k: jnp.stack(outs[k])
    xp, xs = x if isinstance(x, tuple) else (x[:mp], x[mp:])
    per_head = lambda t, h: jnp.transpose(t.reshape(depth, batch, h, dh, -1), (0, 1, 4, 2, 3))
    return (xp.reshape(batch, seq, d), xs.reshape(dbatch, dseq, d),
            per_head(cache_a[0], h_a), per_head(cache_a[1], h_a), st("p_fl"),
            per_head(cache_b[0][..., seq - nband:], h_b), per_head(cache_b[1][..., seq - nband:], h_b),
            st("p_sr"), st("p_si"),
            st("p_mk"), st("p_mv"), st("s_fk"), st("s_fv"), st("s_fl"), st("s_bk"), st("s_bv"),
            st("s_sr"), st("s_si"))
```

```python
import functools
import math

import jax
import jax.numpy as jnp
import numpy as np
from jax import lax
from jax.experimental import pallas as pl
from jax.experimental.pallas import tpu as pltpu

F32 = jnp.float32
BF16 = jnp.bfloat16
I32 = jnp.int32
U32 = jnp.uint32

EPS = 1e-6
NEG_INF = -1e30
LOG2E = math.log2(math.e)
CHUNK = 64
BAND_PAST = 8
SSM_GROUP = 16
TOP_K = 2

LANES = 128
SUBLANES = 8
BF16_ROWS = 16
VMEM_CAP = 60 * 1024 * 1024


def _vmem_limit(nbytes):
    return int(min(VMEM_CAP, max(16 * 1024 * 1024, nbytes * 5 // 4 + (4 << 20))))


def _pick(n, candidates):
    for c in candidates:
        if c <= n and n % c == 0:
            return c
    raise ValueError(f"no tile for {n} in {candidates}")


def _nbytes(shape, dtype):
    return math.prod(shape) * jnp.dtype(dtype).itemsize


def _split3(x):
    hi = x.astype(BF16)
    r1 = x - hi.astype(F32)
    mid = r1.astype(BF16)
    lo = (r1 - mid.astype(F32)).astype(BF16)
    return hi, mid, lo


def _dot(a, b):
    return jnp.dot(a, b, preferred_element_type=F32)


def _dot_nt(a, b):
    return lax.dot_general(a, b, (((1,), (1,)), ((), ())), preferred_element_type=F32)


def _dot_hp(a, b, passes=3):
    ah = a.astype(BF16)
    bh = b.astype(BF16)
    if passes == 1:
        return _dot(ah, bh)
    al = (a - ah.astype(F32)).astype(BF16)
    bl = (b - bh.astype(F32)).astype(BF16)
    return _dot(ah, bh) + (_dot(ah, bl) + _dot(al, bh))


def _pair_ones():
    r = lax.broadcasted_iota(I32, (LANES, LANES), 0) // 64
    c = lax.broadcasted_iota(I32, (LANES, LANES), 1) // 64
    return (r == c).astype(BF16)


def _pair_norm(x, g, ones):
    sq = x * x
    hi = sq.astype(BF16)
    lo = (sq - hi.astype(F32)).astype(BF16)
    ss = _dot(hi, ones) + _dot(lo, ones)
    return x * lax.rsqrt(ss * (1.0 / 64.0) + EPS) * g


def _log_sigmoid(x):
    return jnp.minimum(x, 0.0) - jnp.log(1.0 + jnp.exp(-jnp.abs(x)))


def _gelu_tanh(x):
    c = math.sqrt(2.0 / math.pi)
    return 0.5 * x * (1.0 + jnp.tanh(c * (x + 0.044715 * (x * x * x))))


def _mm_body(*refs, widths, norm, n_w, epilogue, has_res, has_gg, stage_a, tm, rc, kc, k_total):
    it = iter(refs)
    a_refs = [next(it) for _ in widths]
    g_ref = next(it) if norm else None
    w_refs = [next(it) for _ in range(n_w)]
    gg_ref = next(it) if has_gg else None
    res_ref = next(it) if has_res else None
    o_ref = next(it)
    abf = next(it) if stage_a else a_refs[0]
    cast_w = w_refs[0].dtype != BF16
    wbfs = [next(it) for _ in range(n_w)] if cast_w else w_refs
    j = pl.program_id(1)

    if stage_a:
        @pl.when(j == 0)
        def _():
            off = 0
            for a_ref, wd in zip(a_refs, widths):
                def chunk(c, carry, a_ref=a_ref, off=off, wd=wd):
                    r0 = pl.multiple_of(c * rc, rc)
                    x = a_ref[pl.ds(r0, rc), :].astype(F32)
                    if norm:
                        ms = jnp.mean(x * x, axis=-1, keepdims=True)
                        x = x * lax.rsqrt(ms + EPS) * g_ref[:, off:off + wd]
                    abf[pl.ds(r0, rc), off:off + wd] = x.astype(BF16)
                    return carry
                lax.fori_loop(0, tm // rc, chunk, 0)
                off += wd

    if cast_w:
        for w_ref, wbf in zip(w_refs, wbfs):
            def cast(c, carry, w_ref=w_ref, wbf=wbf):
                r0 = pl.multiple_of(c * kc, kc)
                wbf[pl.ds(r0, kc), :] = w_ref[pl.ds(r0, kc), :].astype(BF16)
                return carry
            lax.fori_loop(0, k_total // kc, cast, 0)

    def rows(c, carry):
        r0 = pl.multiple_of(c * rc, rc)
        a = abf[pl.ds(r0, rc), :]
        ys = [_dot(a, wbf[...]) for wbf in wbfs]
        if epilogue == "swiglu":
            y = jax.nn.silu(ys[0]) * ys[1]
        elif epilogue == "group_norm":
            parts = []
            for s in range(ys[0].shape[1] // LANES):
                ysl = ys[0][:, s * LANES:(s + 1) * LANES]
                ms = jnp.mean(ysl * ysl, axis=-1, keepdims=True)
                parts.append(ysl * lax.rsqrt(ms + EPS) * gg_ref[:, s * LANES:(s + 1) * LANES])
            y = jnp.concatenate(parts, axis=1)
        else:
            y = ys[0]
        if has_res:
            y = y + res_ref[pl.ds(r0, rc), :]
        o_ref[pl.ds(r0, rc), :] = y.astype(o_ref.dtype)
        return carry
    lax.fori_loop(0, tm // rc, rows, 0)


def _mm(a_parts, w_list, *, gain=None, group_gain=None, residual=None, epilogue="none",
        out_dtype=F32, tm, tn, name):
    m = a_parts[0].shape[0]
    widths = tuple(a.shape[1] for a in a_parts)
    k_total = sum(widths)
    n = w_list[0].shape[1]
    assert m % tm == 0 and n % tn == 0, (m, tm, n, tn)
    norm = gain is not None
    stage_a = norm or len(a_parts) > 1 or a_parts[0].dtype != BF16
    rc = _pick(tm, (512, 384, 352, 256, 176, 128, 64, 32, 16))
    kc = _pick(k_total, (512, 256, 128))
    grid = (m // tm, n // tn)
    in_specs = [pl.BlockSpec((tm, wd), lambda i, j: (i, 0)) for wd in widths]
    args = list(a_parts)
    est = sum(2 * _nbytes((tm, wd), a.dtype) for wd, a in zip(widths, a_parts))
    if norm:
        in_specs.append(pl.BlockSpec((1, k_total), lambda i, j: (0, 0)))
        args.append(gain)
    cast_w = w_list[0].dtype != BF16
    for w in w_list:
        in_specs.append(pl.BlockSpec((k_total, tn), lambda i, j: (0, j)))
        args.append(w)
        est += 2 * _nbytes((k_total, tn), w.dtype) + cast_w * _nbytes((k_total, tn), BF16)
    if group_gain is not None:
        in_specs.append(pl.BlockSpec((1, tn), lambda i, j: (0, j)))
        args.append(group_gain)
    if residual is not None:
        in_specs.append(pl.BlockSpec((tm, tn), lambda i, j: (i, j)))
        args.append(residual)
        est += 2 * _nbytes((tm, tn), F32)
    est += 2 * _nbytes((tm, tn), out_dtype) + stage_a * _nbytes((tm, k_total), BF16)
    est += 4 * _nbytes((rc, tn), F32) * len(w_list)
    body = functools.partial(
        _mm_body, widths=widths, norm=norm, n_w=len(w_list), epilogue=epilogue,
        has_res=residual is not None, has_gg=group_gain is not None, stage_a=stage_a, tm=tm,
        rc=rc, kc=kc, k_total=k_total)
    return pl.pallas_call(
        body,
        out_shape=jax.ShapeDtypeStruct((m, n), out_dtype),
        grid=grid,
        in_specs=in_specs,
        out_specs=pl.BlockSpec((tm, tn), lambda i, j: (i, j)),
        scratch_shapes=[pltpu.VMEM((tm, k_total), BF16)] * stage_a
        + [pltpu.VMEM((k_total, tn), BF16) for _ in w_list] * cast_w,
        compiler_params=pltpu.CompilerParams(
            dimension_semantics=("parallel", "arbitrary"),
            vmem_limit_bytes=_vmem_limit(est)),
        name=name,
    )(*args)


def _regroup_body(w_ref, o_ref, *, cuts, width):
    off = 0
    for lo, hi in cuts:
        o_ref[:, off:off + hi - lo] = w_ref[:, lo:hi].astype(o_ref.dtype)
        off += hi - lo
    if off < width:
        o_ref[:, off:width] = jnp.zeros((o_ref.shape[0], width - off), o_ref.dtype)


def _regroup_columns(w, layer, cuts, width):
    _, k, n = w.shape
    tr = _pick(k, (256, 128, 64, 32, 16))
    return pl.pallas_call(
        functools.partial(_regroup_body, cuts=cuts, width=width),
        out_shape=jax.ShapeDtypeStruct((k, width), BF16),
        grid=(k // tr,),
        in_specs=[pl.BlockSpec((None, tr, n), lambda i: (layer, i, 0))],
        out_specs=pl.BlockSpec((tr, width), lambda i: (i, 0)),
        compiler_params=pltpu.CompilerParams(
            dimension_semantics=("parallel",),
            vmem_limit_bytes=_vmem_limit(2 * _nbytes((tr, n), F32) + 2 * _nbytes((tr, width), BF16)
                                         + 4 * _nbytes((tr, width), F32))),
        name="regroup_w_in",
    )(w)


def _cumsum_body(x_ref, b_ref, lf_ref, f_ref, carry, *, raw_from, valid_to, tt):
    j = pl.program_id(0)

    @pl.when(j == 0)
    def _():
        carry[...] = jnp.zeros_like(carry)

    x = x_ref[...]
    lane = j * tt + lax.broadcasted_iota(I32, x.shape, 1)
    lf = jnp.where(lane >= raw_from, _log_sigmoid(x + b_ref[:, 0:1]), x)
    lf = jnp.where(lane < valid_to, lf, 0.0)
    lf_ref[...] = lf
    tri = (lax.broadcasted_iota(I32, (tt, tt), 0)
           <= lax.broadcasted_iota(I32, (tt, tt), 1)).astype(BF16)
    hi, mid, lo = _split3(lf)
    y = _dot(hi, tri) + _dot(mid, tri) + _dot(lo, tri) + carry[:, 0:1]
    f_ref[...] = y
    carry[...] = jnp.broadcast_to(y[:, tt - 1:tt], carry.shape)


def _logf_cumsum(x, bias, *, raw_from, valid_to):
    r, t = x.shape
    tt = _pick(t, (256, 128))
    return pl.pallas_call(
        functools.partial(_cumsum_body, raw_from=raw_from, valid_to=valid_to, tt=tt),
        out_shape=(jax.ShapeDtypeStruct((r, t), F32), jax.ShapeDtypeStruct((r, t), F32)),
        grid=(t // tt,),
        in_specs=[pl.BlockSpec((r, tt), lambda j: (0, j)),
                  pl.BlockSpec((r, LANES), lambda j: (0, 0))],
        out_specs=(pl.BlockSpec((r, tt), lambda j: (0, j)),
                   pl.BlockSpec((r, tt), lambda j: (0, j))),
        scratch_shapes=[pltpu.VMEM((r, LANES), F32)],
        compiler_params=pltpu.CompilerParams(dimension_semantics=("arbitrary",)),
        name="logf_cumsum",
    )(x, bias)


def _logf_rows_body(fa_ref, b_ref, lf_ref, faug_ref, carry, *, tt, n_heads):
    j = pl.program_id(1)

    @pl.when(j == 0)
    def _():
        carry[...] = jnp.zeros_like(carry)

    lane = lax.broadcasted_iota(I32, (tt, LANES), 1)
    lf = jnp.where(lane < n_heads, _log_sigmoid(fa_ref[...] + b_ref[...]), 0.0)
    lf_ref[...] = lf
    tri = (lax.broadcasted_iota(I32, (tt, tt), 0)
           >= lax.broadcasted_iota(I32, (tt, tt), 1)).astype(BF16)
    hi, mid, lo = _split3(lf)
    f = _dot(tri, hi) + _dot(tri, mid) + _dot(tri, lo) + carry[0:1, :]
    carry[...] = jnp.broadcast_to(f[tt - 1:tt, :], carry.shape)
    width = faug_ref.shape[1]
    head = lax.broadcasted_iota(I32, (LANES, width), 0)
    col = lax.broadcasted_iota(I32, (LANES, width), 1)
    within = col % LANES
    sel = ((within < 12) & (head == 2 * (col // LANES) + (within % 6) // 3)).astype(BF16)
    hi, mid, lo = _split3(f * LOG2E)
    faug_ref[...] = _dot(hi, sel) + _dot(mid, sel) + _dot(lo, sel)


def _logf_rows(proj, bias_row, *, batch, seq, n_heads, n_pairs, col_block):
    tt = _pick(seq, (256, 128))
    nt = seq // tt
    return pl.pallas_call(
        functools.partial(_logf_rows_body, tt=tt, n_heads=n_heads),
        out_shape=(jax.ShapeDtypeStruct((batch * seq, LANES), F32),
                   jax.ShapeDtypeStruct((batch * seq, n_pairs * LANES), F32)),
        grid=(batch, nt),
        in_specs=[pl.BlockSpec((tt, LANES), lambda b, j: (b * nt + j, col_block)),
                  pl.BlockSpec((1, LANES), lambda b, j: (0, 0))],
        out_specs=(pl.BlockSpec((tt, LANES), lambda b, j: (b * nt + j, 0)),
                   pl.BlockSpec((tt, n_pairs * LANES), lambda b, j: (b * nt + j, 0))),
        scratch_shapes=[pltpu.VMEM((SUBLANES, LANES), F32)],
        compiler_params=pltpu.CompilerParams(dimension_semantics=("parallel", "arbitrary")),
        name="logf_rows",
    )(proj, bias_row)


def _stack_heads(qn):
    lane = lax.broadcasted_iota(I32, qn.shape, 1)
    q0 = jnp.where(lane < 64, qn, 0.0)
    q1 = jnp.where(lane < 64, 0.0, qn)
    return jnp.concatenate([q0, q1], axis=0).astype(BF16)


def _unstack_heads(o, tq):
    lane = lax.broadcasted_iota(I32, (tq, LANES), 1)
    return jnp.where(lane < 64, o[:tq], o[tq:])


def _lane_column(block, h):
    lane = lax.broadcasted_iota(I32, block.shape, 1)
    return jnp.sum(jnp.where(lane == h, block, 0.0), axis=-1, keepdims=True)


V_ROWS = LANES + BF16_ROWS


def _split_select(x):
    hi, mid, lo = _split3(x)
    m3 = lax.broadcasted_iota(I32, x.shape, 1) % 3
    return jnp.where(m3 == 0, hi.astype(F32), jnp.where(m3 == 1, mid.astype(F32), lo.astype(F32)))


def _head_masks(shape):
    lane = lax.broadcasted_iota(I32, shape, 1)
    return lane, lane < 64


def _finish_heads(acc, tq):
    o0 = acc[0:64, 0:tq] / acc[LANES:LANES + 1, 0:tq]
    o1 = acc[64:LANES, tq:2 * tq] / acc[LANES:LANES + 1, tq:2 * tq]
    return jnp.concatenate([o0, o1], axis=0).T


def _fox_prompt_body(q_ref, k_ref, v_ref, f_ref, gq_ref, gk_ref, *rest, tq, seq, dh):
    o_ref, knt_ref, vt_ref, kaug, vt3 = rest[-5:]
    ones = _pair_ones()
    nq = seq // tq
    lane, first = _head_masks((tq, LANES))

    for c in range(nq):
        rows = slice(c * tq, (c + 1) * tq)
        kn = _pair_norm(k_ref[rows, :], gk_ref[...], ones)
        knt_ref[:, rows] = kn.T
        kaug[rows, 0:LANES] = kn.astype(BF16)
        sp = _split_select(f_ref[rows, :])
        kaug[rows, LANES:2 * LANES] = jnp.where(
            lane < 6, -sp, jnp.where(lane < 12, 1.0, 0.0)).astype(BF16)
        vt = v_ref[rows, :].T
        vt_ref[:, rows] = vt
        vt3[c, 0:LANES, :] = vt.astype(BF16)
        vt3[c, LANES:V_ROWS, :] = jnp.ones((BF16_ROWS, tq), BF16)

    causal = (lax.broadcasted_iota(I32, (tq, 2 * tq), 0)
              <= lax.broadcasted_iota(I32, (tq, 2 * tq), 1) % tq)

    for qi in range(nq):
        rows = slice(qi * tq, (qi + 1) * tq)
        qn = _pair_norm(q_ref[rows, :], gq_ref[...], ones) * (dh ** -0.5 * LOG2E)
        sp = _split_select(f_ref[rows, :])
        up0 = jnp.where(lane < 3, 1.0, jnp.where((lane >= 6) & (lane < 9), sp, 0.0))
        up1 = jnp.where((lane >= 3) & (lane < 6), 1.0,
                        jnp.where((lane >= 9) & (lane < 12), sp, 0.0))
        qs = jnp.concatenate(
            [jnp.concatenate([jnp.where(first, qn, 0.0), up0], axis=1),
             jnp.concatenate([jnp.where(first, 0.0, qn), up1], axis=1)], axis=0).astype(BF16)

        s = jnp.where(causal, _dot_nt(kaug[rows, :], qs), NEG_INF)
        m = jnp.max(s, axis=0, keepdims=True)
        acc = _dot(vt3[qi], jnp.exp2(s - m).astype(BF16))
        for kj in range(qi):
            s = _dot_nt(kaug[kj * tq:(kj + 1) * tq, :], qs)
            m_new = jnp.maximum(m, jnp.max(s, axis=0, keepdims=True))
            acc = jnp.exp2(m - m_new) * acc + _dot(vt3[kj], jnp.exp2(s - m_new).astype(BF16))
            m = m_new
        o_ref[rows, :] = _finish_heads(acc, tq)


def _layer_cache_outputs(layer, depth, batch, width, seq, prev, n_inputs, index):
    shape = jax.ShapeDtypeStruct((depth, batch, width, seq), F32)
    spec = pl.BlockSpec((None, None, LANES, seq), lambda *g: (layer, *index(*g), 0))
    extra_specs = [] if prev is None else [pl.BlockSpec(memory_space=pl.ANY)] * 2
    aliases = {} if prev is None else {n_inputs: 1, n_inputs + 1: 2}
    return (shape, shape), (spec, spec), extra_specs, aliases, (() if prev is None else tuple(prev))


def _fox_prompt(proj, f_aug, gq, gk, *, batch, seq, n_pairs, col_q, col_k, col_v, m_rows, layer,
                depth, prev):
    tq = _pick(seq, (512, 256, 128))
    nq = seq // tq
    width = n_pairs * LANES
    est = (2 * 7 * _nbytes((seq, LANES), F32)
           + _nbytes((seq, 2 * LANES), BF16) + _nbytes((V_ROWS, seq), BF16)
           + 16 * _nbytes((tq, 2 * tq), F32))
    c_shapes, c_specs, extra_specs, aliases, extra_args = _layer_cache_outputs(
        layer, depth, batch, width, seq, prev, 6, lambda b, hp: (b, hp))
    return pl.pallas_call(
        functools.partial(_fox_prompt_body, tq=tq, seq=seq, dh=64),
        out_shape=(jax.ShapeDtypeStruct((m_rows, width), F32), *c_shapes),
        grid=(batch, n_pairs),
        in_specs=[
            pl.BlockSpec((seq, LANES), lambda b, hp: (b, col_q + hp)),
            pl.BlockSpec((seq, LANES), lambda b, hp: (b, col_k + hp)),
            pl.BlockSpec((seq, LANES), lambda b, hp: (b, col_v + hp)),
            pl.BlockSpec((seq, LANES), lambda b, hp: (b, hp)),
            pl.BlockSpec((1, LANES), lambda b, hp: (0, 0)),
            pl.BlockSpec((1, LANES), lambda b, hp: (0, 0)),
            *extra_specs,
        ],
        out_specs=(pl.BlockSpec((seq, LANES), lambda b, hp: (b, hp)), *c_specs),
        input_output_aliases=aliases,
        scratch_shapes=[pltpu.VMEM((seq, 2 * LANES), BF16), pltpu.VMEM((nq, V_ROWS, tq), BF16)],
        compiler_params=pltpu.CompilerParams(
            dimension_semantics=("parallel", "parallel"),
            vmem_limit_bytes=_vmem_limit(est)),
        name="fox_prompt",
    )(proj, proj, proj, f_aug, gq, gk, *extra_args)


def _band_prompt_body(q_ref, k_ref, v_ref, tab_ref, gq_ref, gk_ref, *rest, tq, seq, past, dh):
    o_ref, knt_ref, vt_ref, kpad, vt3 = rest[-5:]
    ones = _pair_ones()
    nq = seq // tq
    npad = past // tq
    win = past + tq

    kpad[0:past, :] = jnp.zeros((past, LANES), BF16)
    for c in range(npad):
        vt3[c] = jnp.zeros((V_ROWS, tq), BF16)
    for c in range(nq):
        rows = slice(c * tq, (c + 1) * tq)
        kn = _pair_norm(k_ref[rows, :], gk_ref[...], ones)
        knt_ref[:, rows] = kn.T
        kpad[past + c * tq:past + (c + 1) * tq, :] = kn.astype(BF16)
        vt = v_ref[rows, :].T
        vt_ref[:, rows] = vt
        vt3[npad + c, 0:LANES, :] = vt.astype(BF16)
        vt3[npad + c, LANES:V_ROWS, :] = jnp.ones((BF16_ROWS, tq), BF16)

    row = lax.broadcasted_iota(I32, (win, 2 * tq), 0)

    for qi in range(nq):
        r0 = qi * tq
        qn = _pair_norm(q_ref[r0:r0 + tq, :], gq_ref[...], ones) * (dh ** -0.5 * LOG2E)
        s = _dot_nt(kpad[r0:r0 + win, :], _stack_heads(qn)) + tab_ref[...]
        if r0 < past:
            s = jnp.where(row >= past - r0, s, NEG_INF)
        m = jnp.max(s, axis=0, keepdims=True)
        p = jnp.exp2(s - m).astype(BF16)
        acc = _dot(vt3[qi], p[0:tq, :])
        for c in range(1, win // tq):
            acc = acc + _dot(vt3[qi + c], p[c * tq:(c + 1) * tq, :])
        o_ref[r0:r0 + tq, :] = _finish_heads(acc, tq)


def _band_prompt(proj, tab, gq, gk, *, batch, seq, n_pairs, col_q, col_k, col_v, tq, m_rows, layer,
                 depth, prev):
    past = BAND_PAST * CHUNK
    assert past % tq == 0 and seq % tq == 0
    nq = seq // tq
    width = n_pairs * LANES
    win = past + tq
    est = (2 * 5 * _nbytes((seq, LANES), F32)
           + _nbytes((seq + past, LANES), BF16) + _nbytes((V_ROWS, seq + past), BF16)
           + 2 * _nbytes((win, 2 * tq), F32) + 8 * _nbytes((win, 2 * tq), F32))
    c_shapes, c_specs, extra_specs, aliases, extra_args = _layer_cache_outputs(
        layer, depth, batch, width, seq, prev, 6, lambda hp, b: (b, hp))
    return pl.pallas_call(
        functools.partial(_band_prompt_body, tq=tq, seq=seq, past=past, dh=64),
        out_shape=(jax.ShapeDtypeStruct((m_rows, width), F32), *c_shapes),
        grid=(n_pairs, batch),
        in_specs=[
            pl.BlockSpec((seq, LANES), lambda hp, b: (b, col_q + hp)),
            pl.BlockSpec((seq, LANES), lambda hp, b: (b, col_k + hp)),
            pl.BlockSpec((seq, LANES), lambda hp, b: (b, col_v + hp)),
            pl.BlockSpec((None, win, 2 * tq), lambda hp, b: (hp, 0, 0)),
            pl.BlockSpec((1, LANES), lambda hp, b: (0, 0)),
            pl.BlockSpec((1, LANES), lambda hp, b: (0, 0)),
            *extra_specs,
        ],
        out_specs=(pl.BlockSpec((seq, LANES), lambda hp, b: (b, hp)), *c_specs),
        input_output_aliases=aliases,
        scratch_shapes=[pltpu.VMEM((seq + past, LANES), BF16),
                        pltpu.VMEM(((seq + past) // tq, V_ROWS, tq), BF16)],
        compiler_params=pltpu.CompilerParams(
            dimension_semantics=("parallel", "parallel"),
            vmem_limit_bytes=_vmem_limit(est)),
        name="band_prompt",
    )(proj, proj, proj, tab, gq, gk, *extra_args)


def _sample_attn_body(*refs, mode, n_pairs, s_new, n_cache, dh):
    if mode == "fox":
        (q_ref, k_ref, v_ref, ck_ref, cv_ref, fq_ref, ft_ref, gq_ref, gk_ref, _alias,
         o_ref, kn_ref) = refs
    else:
        (q_ref, k_ref, v_ref, ck_ref, cv_ref, tabc_ref, tabn_ref, gq_ref, gk_ref, _alias,
         o_ref, kn_ref) = refs
    ones = _pair_ones()
    row = lax.broadcasted_iota(I32, (s_new, s_new), 0)
    col = lax.broadcasted_iota(I32, (s_new, s_new), 1)
    for hp in range(n_pairs):
        sl = slice(hp * LANES, (hp + 1) * LANES)
        qn = _pair_norm(q_ref[:, sl], gq_ref[...], ones) * (dh ** -0.5)
        kn = _pair_norm(k_ref[:, sl], gk_ref[...], ones)
        kn_ref[:, sl] = kn
        vn = v_ref[:, sl]
        outs = []
        for e in range(2):
            h = 2 * hp + e
            hl = slice(e * dh, (e + 1) * dh)
            qh = qn[:, hl].astype(BF16)
            sc = _dot(qh, ck_ref[h].astype(BF16))
            sn = _dot_nt(qh, kn[:, hl].astype(BF16))
            if mode == "fox":
                fq = _lane_column(fq_ref[...], h)
                fk = ft_ref[h:h + 1, :]
                sc = sc + fq - fk[:, :n_cache]
                sn = jnp.where(col <= row, sn + fq - fk[:, n_cache:n_cache + s_new], NEG_INF)
            else:
                sc = sc + tabc_ref[h]
                sn = sn + tabn_ref[h]
            m = jnp.maximum(jnp.max(sc, axis=-1, keepdims=True),
                            jnp.max(sn, axis=-1, keepdims=True))
            pc = jnp.exp(sc - m)
            pn = jnp.exp(sn - m)
            l = jnp.sum(pc, axis=-1, keepdims=True) + jnp.sum(pn, axis=-1, keepdims=True)
            outs.append((_dot_nt(pc.astype(BF16), cv_ref[h].astype(BF16))
                         + _dot(pn.astype(BF16), vn[:, hl].astype(BF16))) / l)
        o_ref[:, sl] = jnp.concatenate(outs, axis=1)


def _sample_attn(proj, cache_kt, cache_vt, layer, extra_a, extra_b, gq, gk, o_buf, *, mode, batch,
                 s_new, n_pairs, col_q, row0):
    width = n_pairs * LANES
    _, _, n_heads, dh, n_cache = cache_kt.shape
    rb = row0 // s_new
    if mode == "fox":
        ex_specs = [pl.BlockSpec((s_new, LANES), lambda b: (b, 0)),
                    pl.BlockSpec((None, 16, extra_b.shape[2]), lambda b: (b, 0, 0))]
    else:
        ex_specs = [pl.BlockSpec(extra_a.shape, lambda b: (0, 0, 0)),
                    pl.BlockSpec(extra_b.shape, lambda b: (0, 0, 0))]
    est = (2 * 2 * _nbytes((n_cache, width), F32) + 8 * _nbytes((s_new, width), F32)
           + 2 * _nbytes(extra_a.shape, F32) + 16 * _nbytes((2 * s_new, n_cache), F32)
           + 4 * _nbytes((n_cache, LANES), BF16))
    return pl.pallas_call(
        functools.partial(_sample_attn_body, mode=mode, n_pairs=n_pairs, s_new=s_new,
                          n_cache=n_cache, dh=dh),
        out_shape=(jax.ShapeDtypeStruct(o_buf.shape, F32),
                   jax.ShapeDtypeStruct((batch * s_new, width), F32)),
        grid=(batch,),
        in_specs=[
            pl.BlockSpec((s_new, width), lambda b: (rb + b, col_q)),
            pl.BlockSpec((s_new, width), lambda b: (rb + b, col_q + 1)),
            pl.BlockSpec((s_new, width), lambda b: (rb + b, col_q + 2)),
            pl.BlockSpec((None, None, n_heads, dh, n_cache), lambda b: (layer, b, 0, 0, 0)),
            pl.BlockSpec((None, None, n_heads, dh, n_cache), lambda b: (layer, b, 0, 0, 0)),
            *ex_specs,
            pl.BlockSpec((1, LANES), lambda b: (0, 0)),
            pl.BlockSpec((1, LANES), lambda b: (0, 0)),
            pl.BlockSpec(memory_space=pl.ANY),
        ],
        out_specs=(pl.BlockSpec((s_new, width), lambda b: (rb + b, 0)),
                   pl.BlockSpec((s_new, width), lambda b: (b, 0))),
        input_output_aliases={9: 0},
        compiler_params=pltpu.CompilerParams(
            dimension_semantics=("parallel",), vmem_limit_bytes=_vmem_limit(est)),
        name=f"{mode}_sample",
    )(proj, proj, proj, cache_kt, cache_vt, extra_a, extra_b, gq, gk, o_buf)


def _cross_body(*refs, n_heads, dh, aliased):
    if aliased:
        q_ref, k_ref, v_ref, _alias, o_ref = refs
    else:
        q_ref, k_ref, v_ref, o_ref = refs
    for h in range(n_heads):
        sl = slice(h * dh, (h + 1) * dh)
        q = (q_ref[:, sl] * (dh ** -0.5)).astype(BF16)
        s = _dot_nt(q, k_ref[:, sl].astype(BF16))
        m = jnp.max(s, axis=-1, keepdims=True)
        p = jnp.exp(s - m)
        l = jnp.sum(p, axis=-1, keepdims=True)
        o = _dot(p.astype(BF16), v_ref[:, sl].astype(BF16)) / l
        o_ref[:, sl] = o.astype(o_ref.dtype)


def _cross_attn(q_all, k, v, o_buf, *, batch, q_len, n_mem, n_heads, dh, row0, m_rows):
    width = n_heads * dh
    tq = _pick(q_len, (512, 256, 128, 64, 32, 16))
    nq = q_len // tq
    rb = row0 // tq
    aliased = o_buf is not None
    in_specs = [pl.BlockSpec((tq, width), lambda b, qi: (rb + b * nq + qi, 0)),
                pl.BlockSpec((n_mem, width), lambda b, qi: (b, 0)),
                pl.BlockSpec((n_mem, width), lambda b, qi: (b, 0))]
    args = [q_all, k, v]
    if aliased:
        in_specs.append(pl.BlockSpec(memory_space=pl.ANY))
        args.append(o_buf)
    est = (2 * _nbytes((tq, width), F32) + 4 * _nbytes((n_mem, width), F32)
           + 2 * _nbytes((tq, width), BF16) + 12 * _nbytes((tq, n_mem), F32))
    return pl.pallas_call(
        functools.partial(_cross_body, n_heads=n_heads, dh=dh, aliased=aliased),
        out_shape=jax.ShapeDtypeStruct((m_rows, width), BF16),
        grid=(batch, nq),
        in_specs=in_specs,
        out_specs=pl.BlockSpec((tq, width), lambda b, qi: (rb + b * nq + qi, 0)),
        input_output_aliases={3: 0} if aliased else {},
        compiler_params=pltpu.CompilerParams(
            dimension_semantics=("parallel", "arbitrary"), vmem_limit_bytes=_vmem_limit(est)),
        name="cross_attn",
    )(*args)


def _s5_prep_body(lr_ref, li_ref, ldt_ref, br_ref, bi_ref, ar_ref, ai_ref, bbr_ref, bbi_ref):
    lr = lr_ref[...]
    li = li_ref[...]
    dt = jnp.exp(ldt_ref[...])
    mag = jnp.exp(lr * dt)
    a_re = mag * jnp.cos(li * dt)
    a_im = mag * jnp.sin(li * dt)
    den = lr * lr + li * li
    num_re = a_re - 1.0
    coef_re = (num_re * lr + a_im * li) / den
    coef_im = (a_im * lr - num_re * li) / den
    br = br_ref[...]
    bi = bi_ref[...]
    ar_ref[...] = a_re
    ai_ref[...] = a_im
    bbr_ref[...] = coef_re * br - coef_im * bi
    bbi_ref[...] = coef_re * bi + coef_im * br


def _s5_prep(lam_re, lam_im, log_dt, b_re, b_im):
    shape = lam_re.shape
    spec = pl.BlockSpec(shape, lambda: (0, 0))
    return pl.pallas_call(
        _s5_prep_body,
        out_shape=tuple(jax.ShapeDtypeStruct(shape, F32) for _ in range(4)),
        in_specs=[spec] * 5,
        out_specs=tuple([spec] * 4),
        name="s5_discretise",
    )(lam_re, lam_im, log_dt, b_re, b_im)


def _s5_body(*refs, nb, t_chunk, n_blk, passes, per_stream):
    n_u = nb if per_stream else 1
    u_refs = refs[:n_u]
    (x0r_ref, x0i_ref, ar_ref, ai_ref, bre_ref, bim_ref, cre_ref, cim_ref, d_ref, wg_ref, bg_ref,
     o_ref, xr_out, xi_out, bur, bui, st_r, st_i) = refs[n_u:n_u + 18]
    i = pl.program_id(0)
    wc = u_refs[0].shape[1]
    ns = bur.shape[1]
    ub = wc // n_blk
    sb = ns // n_blk

    @pl.when(i == 0)
    def _():
        st_r[...] = x0r_ref[...]
        st_i[...] = x0i_ref[...]

    if per_stream:
        slab = refs[n_u + 18]
        for b in range(nb):
            for k in range(wc // LANES):
                slab[k, pl.ds(b, t_chunk, stride=nb), :] = u_refs[b][:, k * LANES:(k + 1) * LANES]
        u = jnp.concatenate([slab[k] for k in range(wc // LANES)], axis=1)
    else:
        u = u_refs[0][...]
    rows = nb * t_chunk
    n_half = 2 if (t_chunk % 2 == 0 and rows >= 1024) else 1
    hr = rows // n_half
    for h in range(n_half):
        for k in range(n_blk):
            uk = u[h * hr:(h + 1) * hr, k * ub:(k + 1) * ub]
            bur[h * hr:(h + 1) * hr, k * sb:(k + 1) * sb] = _dot_hp(
                uk, bre_ref[k * ub:(k + 1) * ub, k * sb:(k + 1) * sb], passes)
            bui[h * hr:(h + 1) * hr, k * sb:(k + 1) * sb] = _dot_hp(
                uk, bim_ref[k * ub:(k + 1) * ub, k * sb:(k + 1) * sb], passes)

    a_re = ar_ref[...]
    a_im = ai_ref[...]
    xr, xi = st_r[...], st_i[...]
    for t in range(t_chunk):
        r0 = t * nb
        xr, xi = (a_re * xr - a_im * xi + bur[r0:r0 + nb, :],
                  a_re * xi + a_im * xr + bui[r0:r0 + nb, :])
        bur[r0:r0 + nb, :] = xr
        bui[r0:r0 + nb, :] = xi
    st_r[...] = xr
    st_i[...] = xi
    xr_out[...] = xr
    xi_out[...] = xi
    outs = []
    for h in range(n_half):
        ys = []
        for k in range(n_blk):
            xrk = bur[h * hr:(h + 1) * hr, k * sb:(k + 1) * sb].astype(BF16)
            xik = bui[h * hr:(h + 1) * hr, k * sb:(k + 1) * sb].astype(BF16)
            ys.append(_dot(xrk, cre_ref[k * sb:(k + 1) * sb, k * ub:(k + 1) * ub].astype(BF16))
                      - _dot(xik, cim_ref[k * sb:(k + 1) * sb, k * ub:(k + 1) * ub].astype(BF16)))
        y = jnp.concatenate(ys, axis=1) + d_ref[...] * u[h * hr:(h + 1) * hr, :]
        z = _gelu_tanh(y)
        gate = _dot(z.astype(BF16), wg_ref[...].astype(BF16)) + bg_ref[...]
        outs.append(z * jax.nn.sigmoid(gate))
    out = outs[0] if n_half == 1 else jnp.concatenate(outs, axis=0)
    if per_stream:
        for k in range(wc // LANES):
            slab[k] = out[:, k * LANES:(k + 1) * LANES]
        for b in range(nb):
            o_ref[b] = jnp.concatenate(
                [slab[k, pl.ds(b, t_chunk, stride=nb), :] for k in range(wc // LANES)], axis=1)
    else:
        o_ref[...] = out


def _s5(u, x0r, x0i, a_re, a_im, b_re, b_im, c_re, c_im, d, w_glu, b_glu, *, nb, seq, n_blk,
        passes, col_block=None):
    per_stream = col_block is not None
    wc, ns = b_re.shape
    t_chunk = _pick(seq, (128, 64, 32, 16) if nb <= SUBLANES else (16,))
    nc = seq // t_chunk
    rows = nb * t_chunk
    full = lambda shape: pl.BlockSpec(shape, lambda i: (0,) * len(shape))
    est = (6 * _nbytes((rows, wc), F32) + 2 * _nbytes((rows, ns), F32)
           + 2 * 2 * _nbytes((wc, ns), b_re.dtype) + 2 * 2 * _nbytes((wc, ns), c_re.dtype)
           + 2 * _nbytes((wc, wc), F32)
           + 6 * _nbytes((rows // 2, ns // n_blk), F32) + 8 * _nbytes((nb, ns), F32))
    if per_stream:
        u_specs = [pl.BlockSpec((t_chunk, wc), lambda i, s=s: (s * nc + i, col_block))
                   for s in range(nb)]
        u_args = [u] * nb
        o_shape = jax.ShapeDtypeStruct((nb, seq, wc), F32)
        o_spec = pl.BlockSpec((nb, t_chunk, wc), lambda i: (0, i, 0))
        slab = [pltpu.VMEM((wc // LANES, rows, LANES), F32)]
    else:
        u_specs = [pl.BlockSpec((rows, wc), lambda i: (i, 0))]
        u_args = [u]
        o_shape = jax.ShapeDtypeStruct((seq * nb, wc), F32)
        o_spec = pl.BlockSpec((rows, wc), lambda i: (i, 0))
        slab = []
    return pl.pallas_call(
        functools.partial(_s5_body, nb=nb, t_chunk=t_chunk, n_blk=n_blk, passes=passes,
                          per_stream=per_stream),
        out_shape=(o_shape, jax.ShapeDtypeStruct((nb, ns), F32), jax.ShapeDtypeStruct((nb, ns), F32)),
        grid=(nc,),
        in_specs=[*u_specs,
                  full((nb, ns)), full((nb, ns)), full((1, ns)), full((1, ns)),
                  full((wc, ns)), full((wc, ns)), full((ns, wc)), full((ns, wc)),
                  full((1, wc)), full((wc, wc)), full((1, wc))],
        out_specs=(o_spec, full((nb, ns)), full((nb, ns))),
        scratch_shapes=[pltpu.VMEM((rows, ns), F32), pltpu.VMEM((rows, ns), F32),
                        pltpu.VMEM((nb, ns), F32), pltpu.VMEM((nb, ns), F32), *slab],
        compiler_params=pltpu.CompilerParams(
            dimension_semantics=("arbitrary",), vmem_limit_bytes=_vmem_limit(est)),
        name="s5_scan",
    )(*u_args, x0r, x0i, a_re, a_im, b_re, b_im, c_re, c_im, d, w_glu, b_glu)


def _router_body(x_ref, g_ref, wr_ref, pk_ref, idx_ref, gate_ref, *, n_exp, rc):
    tm, d = x_ref.shape
    half = d // 2

    def chunk(c, carry):
        r0 = pl.multiple_of(c * rc, rc)
        x = x_ref[pl.ds(r0, rc), :]
        ms = jnp.mean(x * x, axis=-1, keepdims=True)
        h = x * lax.rsqrt(ms + EPS) * g_ref[...]
        hb = h.astype(BF16).astype(F32)
        lo = lax.shift_right_logical(pltpu.bitcast(hb[:, :half], U32), jnp.uint32(16))
        hi = pltpu.bitcast(hb[:, half:], U32) & jnp.uint32(0xFFFF0000)
        pk_ref[pl.ds(r0, rc), :] = lo | hi
        logits = _dot_hp(h, wr_ref[...])
        lane = lax.broadcasted_iota(I32, logits.shape, 1)
        logits = jnp.where(lane < n_exp, logits, NEG_INF)
        mx = jnp.max(logits, axis=-1, keepdims=True)
        e = jnp.exp(logits - mx)
        probs = e / jnp.sum(e, axis=-1, keepdims=True)
        probs = jnp.where(lane < n_exp, probs, -1.0)
        lane_f = lane.astype(F32)
        p1 = jnp.max(probs, axis=-1, keepdims=True)
        i1 = jnp.min(jnp.where(probs == p1, lane_f, float(LANES)), axis=-1, keepdims=True)
        rest = jnp.where(lane_f == i1, -1.0, probs)
        p2 = jnp.max(rest, axis=-1, keepdims=True)
        i2 = jnp.min(jnp.where(rest == p2, lane_f, float(LANES)), axis=-1, keepdims=True)
        tot = p1 + p2
        idx_ref[pl.ds(r0, rc), :] = jnp.where(lane == 0, i1, jnp.where(lane == 1, i2, 0.0)).astype(I32)
        gate_ref[pl.ds(r0, rc), :] = jnp.where(lane == 0, p1 / tot,
                                                jnp.where(lane == 1, p2 / tot, 0.0))
        return carry
    lax.fori_loop(0, tm // rc, chunk, 0)


def _router(x, g, w_router_pad, *, n_exp, tm):
    m, d = x.shape
    rc = _pick(tm, (256, 176, 128, 64, 32, 16, 8))
    est = 2 * _nbytes((tm, d), F32) + 2 * _nbytes((tm, d // 2), U32) + 16 * _nbytes((rc, d), F32)
    return pl.pallas_call(
        functools.partial(_router_body, n_exp=n_exp, rc=rc),
        out_shape=(jax.ShapeDtypeStruct((m, d // 2), U32),
                   jax.ShapeDtypeStruct((m, LANES), I32),
                   jax.ShapeDtypeStruct((m, LANES), F32)),
        grid=(m // tm,),
        in_specs=[pl.BlockSpec((tm, d), lambda i: (i, 0)),
                  pl.BlockSpec((1, d), lambda i: (0, 0)),
                  pl.BlockSpec((d, LANES), lambda i: (0, 0))],
        out_specs=(pl.BlockSpec((tm, d // 2), lambda i: (i, 0)),
                   pl.BlockSpec((tm, LANES), lambda i: (i, 0)),
                   pl.BlockSpec((tm, LANES), lambda i: (i, 0))),
        compiler_params=pltpu.CompilerParams(
            dimension_semantics=("parallel",), vmem_limit_bytes=_vmem_limit(est)),
        name="moe_router",
    )(x, g, w_router_pad)


GATHER_UNROLL = 8


def _row_copy(src_hbm, row, dst, r, sem):
    return pltpu.make_async_copy(src_hbm.at[pl.ds(row, 1), :], dst.at[pl.ds(r, 1), :], sem)


def _dispatch_body(nt_ref, tok_ref, nxt_ref, pk_hbm, a_ref, buf, sem, *, tm):
    i = pl.program_id(0)
    nt = nt_ref[0]
    half = buf.shape[2]

    def request(ids_ref, slot):
        def issue(c, carry):
            for u in range(GATHER_UNROLL):
                r = c * GATHER_UNROLL + u
                _row_copy(pk_hbm, ids_ref[0, 0, r], buf.at[slot], r, sem.at[slot]).start(
                    priority=u % 2)
            return carry
        lax.fori_loop(0, tm // GATHER_UNROLL, issue, 0)

    @pl.when(i == 0)
    def _():
        request(tok_ref, 0)

    @pl.when(i + 1 < nt)
    def _():
        request(nxt_ref, (i + 1) % 2)

    @pl.when(i < nt)
    def _():
        slot = i % 2
        pltpu.make_async_copy(pk_hbm.at[pl.ds(0, tm), :], buf.at[slot], sem.at[slot]).wait()
        pk = buf[slot]
        lo = pltpu.bitcast(lax.shift_left(pk, jnp.uint32(16)), F32)
        hi = pltpu.bitcast(pk & jnp.uint32(0xFFFF0000), F32)
        a_ref[:, :half] = lo.astype(BF16)
        a_ref[:, half:] = hi.astype(BF16)


def _dispatch(n_tiles, tok_sorted, packed, *, tm, r_max):
    m, half = packed.shape
    t_max = r_max // tm
    assert tm % GATHER_UNROLL == 0
    tok_tiles = tok_sorted.reshape(t_max, 1, tm)
    return pl.pallas_call(
        functools.partial(_dispatch_body, tm=tm),
        out_shape=jax.ShapeDtypeStruct((r_max, 2 * half), BF16),
        grid_spec=pltpu.PrefetchScalarGridSpec(
            num_scalar_prefetch=1,
            grid=(t_max,),
            in_specs=[pl.BlockSpec((1, 1, tm), lambda i, nt: (i, 0, 0), memory_space=pltpu.SMEM),
                      pl.BlockSpec((1, 1, tm), lambda i, nt: (jnp.minimum(i + 1, t_max - 1), 0, 0),
                                   memory_space=pltpu.SMEM),
                      pl.BlockSpec(memory_space=pl.ANY)],
            out_specs=pl.BlockSpec((tm, 2 * half), lambda i, nt: (jnp.minimum(i, nt[0] - 1), 0)),
            scratch_shapes=[pltpu.VMEM((2, tm, half), U32), pltpu.SemaphoreType.DMA((2,))]),
        compiler_params=pltpu.CompilerParams(dimension_semantics=("arbitrary",)),
        name="moe_dispatch",
    )(n_tiles, tok_tiles, tok_tiles, packed)


def _expert_up_body(te_ref, nt_ref, a_ref, w1_ref, w3_ref, h_ref, w1bf, w3bf, *, kc, rem):
    f = pl.program_id(0)
    i = pl.program_id(1)
    last = pl.num_programs(0) - 1
    d, tf = w1bf.shape
    fresh = jnp.logical_or(i == 0, te_ref[i] != te_ref[jnp.maximum(i - 1, 0)])
    live = i < nt_ref[0]

    def run(cols, when):
        @pl.when(jnp.logical_and(when, jnp.logical_and(fresh, live)))
        def _():
            def cast(c, carry):
                r0 = pl.multiple_of(c * kc, kc)
                w1bf[pl.ds(r0, kc), 0:cols] = w1_ref[pl.ds(r0, kc), 0:cols].astype(BF16)
                w3bf[pl.ds(r0, kc), 0:cols] = w3_ref[pl.ds(r0, kc), 0:cols].astype(BF16)
                return carry
            lax.fori_loop(0, d // kc, cast, 0)

        @pl.when(jnp.logical_and(when, live))
        def _():
            a = a_ref[...]
            h_ref[:, 0:cols] = (jax.nn.silu(_dot(a, w1bf[:, 0:cols]))
                                * _dot(a, w3bf[:, 0:cols])).astype(h_ref.dtype)

    if rem:
        run(tf, f < last)
        run(rem, f == last)
    else:
        run(tf, True)


def _expert_up(tile_expert, n_tiles, a_sorted, w1, w3, *, tm, tf):
    r_max, d = a_sorted.shape
    n_exp, _, fe = w1.shape
    t_max = r_max // tm
    kc = _pick(d, (512, 256, 128))
    row = lambda f, i, te, nt: jnp.minimum(i, nt[0] - 1)
    est = (2 * _nbytes((tm, d), BF16) + 2 * 2 * _nbytes((d, tf), F32) + 2 * _nbytes((d, tf), BF16)
           + 2 * _nbytes((tm, tf), BF16) + 6 * _nbytes((tm, tf), F32))
    return pl.pallas_call(
        functools.partial(_expert_up_body, kc=kc, rem=fe % tf),
        out_shape=jax.ShapeDtypeStruct((r_max, fe), BF16),
        grid_spec=pltpu.PrefetchScalarGridSpec(
            num_scalar_prefetch=2,
            grid=(pl.cdiv(fe, tf), t_max),
            in_specs=[pl.BlockSpec((tm, d), lambda f, i, te, nt: (row(f, i, te, nt), 0)),
                      pl.BlockSpec((None, d, tf), lambda f, i, te, nt: (te[i], 0, f)),
                      pl.BlockSpec((None, d, tf), lambda f, i, te, nt: (te[i], 0, f))],
            out_specs=pl.BlockSpec((tm, tf), lambda f, i, te, nt: (row(f, i, te, nt), f)),
            scratch_shapes=[pltpu.VMEM((d, tf), BF16), pltpu.VMEM((d, tf), BF16)]),
        compiler_params=pltpu.CompilerParams(
            dimension_semantics=("arbitrary", "arbitrary"), vmem_limit_bytes=_vmem_limit(est)),
        name="moe_expert_up",
    )(tile_expert, n_tiles, a_sorted, w1, w3)


def _expert_down_body(te_ref, nt_ref, h_ref, w2_ref, y_ref, w2bf, *, kc):
    i = pl.program_id(1)
    fe = h_ref.shape[1]
    fresh = jnp.logical_or(i == 0, te_ref[i] != te_ref[jnp.maximum(i - 1, 0)])

    @pl.when(jnp.logical_and(fresh, i < nt_ref[0]))
    def _():
        def cast(c, carry):
            r0 = pl.multiple_of(c * kc, kc)
            w2bf[pl.ds(r0, kc), :] = w2_ref[pl.ds(r0, kc), :].astype(BF16)
            return carry
        lax.fori_loop(0, fe // kc, cast, 0)

    @pl.when(i < nt_ref[0])
    def _():
        y_ref[...] = _dot(h_ref[...], w2bf[...])


def _expert_down(tile_expert, n_tiles, h_sorted, w2, *, tm, tn):
    r_max, fe = h_sorted.shape
    d = w2.shape[2]
    t_max = r_max // tm
    kc = _pick(fe, (512, 256, 128))
    row = lambda n, i, te, nt: jnp.minimum(i, nt[0] - 1)
    est = (2 * _nbytes((tm, fe), BF16) + 2 * _nbytes((fe, tn), F32) + _nbytes((fe, tn), BF16)
           + 4 * _nbytes((tm, tn), F32))
    return pl.pallas_call(
        functools.partial(_expert_down_body, kc=kc),
        out_shape=jax.ShapeDtypeStruct((r_max, d), F32),
        grid_spec=pltpu.PrefetchScalarGridSpec(
            num_scalar_prefetch=2,
            grid=(d // tn, t_max),
            in_specs=[pl.BlockSpec((tm, fe), lambda n, i, te, nt: (row(n, i, te, nt), 0)),
                      pl.BlockSpec((None, fe, tn), lambda n, i, te, nt: (te[i], 0, n))],
            out_specs=pl.BlockSpec((tm, tn), lambda n, i, te, nt: (row(n, i, te, nt), n)),
            scratch_shapes=[pltpu.VMEM((fe, tn), BF16)]),
        compiler_params=pltpu.CompilerParams(
            dimension_semantics=("arbitrary", "arbitrary"), vmem_limit_bytes=_vmem_limit(est)),
        name="moe_expert_down",
    )(tile_expert, n_tiles, h_sorted, w2)


def _combine_body(slot_ref, nxt_ref, x_ref, gate_ref, y_hbm, *rest, tc, n_first):
    o_refs, (buf, sem) = rest[:-2], rest[-2:]
    i = pl.program_id(0)
    n = pl.num_programs(0)
    unroll = GATHER_UNROLL // TOP_K

    def request(ids_ref, slot):
        def issue(c, carry):
            for u in range(unroll):
                r = c * unroll + u
                for k in range(TOP_K):
                    _row_copy(y_hbm, ids_ref[0, 0, TOP_K * r + k], buf.at[slot, k], r,
                              sem.at[slot]).start(priority=k % 2)
            return carry
        lax.fori_loop(0, tc // unroll, issue, 0)

    @pl.when(i == 0)
    def _():
        request(slot_ref, 0)

    @pl.when(i + 1 < n)
    def _():
        request(nxt_ref, (i + 1) % 2)

    slot = i % 2
    for k in range(TOP_K):
        pltpu.make_async_copy(y_hbm.at[pl.ds(0, tc), :], buf.at[slot, k], sem.at[slot]).wait()
    g = gate_ref[...]
    val = x_ref[...] + (g[:, 0:1] * buf[slot, 0] + g[:, 1:2] * buf[slot, 1])
    if len(o_refs) == 1:
        o_refs[0][...] = val
    else:
        @pl.when(i < n_first)
        def _():
            o_refs[0][...] = val

        @pl.when(i >= n_first)
        def _():
            o_refs[1][...] = val


def _combine(slots, x, gates, y_sorted, *, tc, split=None):
    m, d = x.shape
    n = m // tc
    assert tc % (GATHER_UNROLL // TOP_K) == 0
    est = 4 * _nbytes((tc, d), F32) + 2 * TOP_K * _nbytes((tc, d), F32) + 6 * _nbytes((tc, d), F32)
    slot_tiles = slots.reshape(n, 1, TOP_K * tc)
    if split is None or split % tc or (m - split) % tc:
        n_first = n
        out_shape = jax.ShapeDtypeStruct((m, d), F32)
        out_specs = pl.BlockSpec((tc, d), lambda i: (i, 0))
    else:
        n_first = split // tc
        out_shape = (jax.ShapeDtypeStruct((split, d), F32), jax.ShapeDtypeStruct((m - split, d), F32))
        out_specs = (pl.BlockSpec((tc, d), lambda i: (jnp.minimum(i, n_first - 1), 0)),
                     pl.BlockSpec((tc, d), lambda i: (jnp.maximum(i - n_first, 0), 0)))
    return pl.pallas_call(
        functools.partial(_combine_body, tc=tc, n_first=n_first),
        out_shape=out_shape,
        grid=(n,),
        in_specs=[pl.BlockSpec((1, 1, TOP_K * tc), lambda i: (i, 0, 0), memory_space=pltpu.SMEM),
                  pl.BlockSpec((1, 1, TOP_K * tc), lambda i: (jnp.minimum(i + 1, n - 1), 0, 0),
                               memory_space=pltpu.SMEM),
                  pl.BlockSpec((tc, d), lambda i: (i, 0)),
                  pl.BlockSpec((tc, LANES), lambda i: (i, 0)),
                  pl.BlockSpec(memory_space=pl.ANY)],
        out_specs=out_specs,
        scratch_shapes=[pltpu.VMEM((2, TOP_K, tc, d), F32), pltpu.SemaphoreType.DMA((2,))],
        compiler_params=pltpu.CompilerParams(
            dimension_semantics=("arbitrary",), vmem_limit_bytes=_vmem_limit(est)),
        name="moe_combine",
    )(slot_tiles, slot_tiles, x, gates, y_sorted)


def _moe(x, g_ffn, w_router, w_e1, w_e3, w_e2, *, tm_tok, split=None):
    m, d = x.shape
    n_exp, _, fe = w_e1.shape
    tm = _pick(m * TOP_K, (512, 256, 128, 64, 32, 16))
    wr = jnp.zeros((d, LANES), F32).at[:, :n_exp].set(w_router)
    packed, idx128, gate128 = _router(x, g_ffn, wr, n_exp=n_exp, tm=tm_tok)

    idx = idx128[:, :TOP_K]
    mask = jnp.sum(idx[:, :, None] == jnp.arange(n_exp, dtype=I32)[None, None, :], axis=1).astype(I32)
    counts = jnp.sum(mask, axis=0)
    padded = ((counts + tm - 1) // tm) * tm
    ends = jnp.cumsum(padded)
    starts = ends - padded
    pos = jnp.cumsum(mask, axis=0) - mask
    slot = starts[idx] + jnp.take_along_axis(pos, idx, axis=1)
    t_max = (m * TOP_K) // tm + n_exp
    r_max = t_max * tm
    tok_sorted = jnp.zeros((r_max,), I32).at[slot.reshape(-1)].set(
        jnp.repeat(jnp.arange(m, dtype=I32), TOP_K))
    n_tiles = (ends[-1] // tm).astype(I32).reshape(1)
    tile_start = jnp.minimum(jnp.arange(t_max, dtype=I32), n_tiles[0] - 1) * tm
    tile_expert = jnp.minimum(jnp.sum(ends[None, :] <= tile_start[:, None], axis=1),
                              n_exp - 1).astype(I32)

    a_sorted = _dispatch(n_tiles, tok_sorted, packed, tm=tm, r_max=r_max)
    th = _pick(fe, (256, 128))
    tf = 2 * th if fe >= 2 * th else th
    h_sorted = _expert_up(tile_expert, n_tiles, a_sorted, w_e1, w_e3, tm=tm, tf=tf)
    y_sorted = _expert_down(tile_expert, n_tiles, h_sorted, w_e2, tm=tm,
                            tn=_pick(d, (1024, 512, 256, 128)))
    tc = _pick(m, (256, 128, 64, 32, 16, 8))
    return _combine(slot, x, gate128, y_sorted, tc=tc, split=split)


def _block_diag(blocks):
    g, r, c = blocks.shape
    eye = jnp.eye(g, dtype=blocks.dtype)
    return (blocks[:, :, None, :] * eye[:, None, :, None]).reshape(g * r, g * c)


def _rel_table(rel_bias_l, q_pos, k_pos, max_rel):
    rel = np.clip(q_pos[:, None] - k_pos[None, :], -(CHUNK - 1), max_rel) + (CHUNK - 1)
    return rel_bias_l[:, rel].astype(F32)


def _band_table(rel_bias_l, tq, max_rel):
    n_heads = rel_bias_l.shape[0]
    past = BAND_PAST * CHUNK
    win = past + tq
    ring = -(-(tq + win - 1) // LANES) * LANES
    diff = np.arange(ring)
    diff = np.where(diff < tq, diff, diff - ring)
    idx = np.clip(diff + past, -(CHUNK - 1), max_rel) + (CHUNK - 1)
    by_diff = jnp.pad(rel_bias_l.astype(F32)[:, idx], ((0, 16 - n_heads), (0, 0)))

    def body(u_ref, o_ref):
        hp = pl.program_id(0)
        j = lax.broadcasted_iota(I32, (win, tq), 0)
        i = lax.broadcasted_iota(I32, (win, tq), 1)
        gap = (i + past) // CHUNK - j // CHUNK
        valid = (gap >= 0) & (gap <= BAND_PAST)
        for e in range(2):
            rows = jnp.broadcast_to(u_ref[pl.ds(2 * hp + e, 1), :], (win, ring))
            shifted = pltpu.roll(rows, 0, 1, stride=1, stride_axis=0)
            o_ref[:, e * tq:(e + 1) * tq] = jnp.where(valid, shifted[:, 0:tq] * LOG2E, NEG_INF)

    return pl.pallas_call(
        body,
        out_shape=jax.ShapeDtypeStruct((n_heads // 2, win, 2 * tq), F32),
        grid=(n_heads // 2,),
        in_specs=[pl.BlockSpec((16, ring), lambda hp: (0, 0))],
        out_specs=pl.BlockSpec((None, win, 2 * tq), lambda hp: (hp, 0, 0)),
        compiler_params=pltpu.CompilerParams(dimension_semantics=("parallel",)),
        name="band_table",
    )(by_diff)


def _tile_gain(g):
    return jnp.tile(g.astype(F32), LANES // g.shape[0]).reshape(1, LANES)


def kernel(x_prompt, x_sample, mem_prompt, cache_fox_k, cache_fox_v, cache_fox_logf, cache_band_k, cache_band_v, state_ssm_re, state_ssm_im, cache_mem_k, cache_mem_v, g_mix, w_in, b_f, g_qa, g_ka, g_qb, g_kb, rel_bias, lam_re, lam_im, log_dt, ssm_b_re, ssm_b_im, ssm_c_re, ssm_c_im, ssm_d, w_glu, b_glu, g_mix_out, w_out, g_cross, g_mem, w_cq, w_ck, w_cv, g_cq, g_ck, w_co, g_ffn, w_ff1, w_ff3, w_ff2, w_router, w_e1, w_e3, w_e2):
    batch, seq, d = x_prompt.shape
    dbatch, dseq, _ = x_sample.shape
    depth = g_mix.shape[0]
    past_len, h_a, dh = cache_fox_k.shape[2:]
    band_rows, h_b = cache_band_k.shape[2:4]
    g_c, p_state = lam_re.shape[1:]
    w_a, w_b, w_c = h_a * dh, h_b * dh, g_c * SSM_GROUP
    n_mem, h_m, dh_m = cache_mem_k.shape[2:]
    w_m = h_m * dh_m
    max_rel = rel_bias.shape[2] - CHUNK
    assert dh == 64 and h_a % 2 == 0 and h_b % 2 == 0 and dh_m == LANES
    assert w_a == w_b and w_c <= w_a and h_a <= 16
    mp, ms = batch * seq, dbatch * dseq
    m = mp + ms
    pa, pb = h_a // 2, h_b // 2
    n_state = g_c * p_state
    n_blk = 2 if (w_c % 512 == 0) else 1
    nband = min(BAND_PAST * CHUNK, seq)
    tm = _pick(m, (1056, 1024, 768, 512, 256, 128, 64, 32, 16))
    tq_band = _pick(seq, (256, 128, 64))

    x = jnp.concatenate([x_prompt.reshape(mp, d), x_sample.reshape(ms, d)], axis=0)
    fox_kt, fox_vt, band_kt, band_vt = (jnp.transpose(c, (0, 1, 3, 4, 2)) for c in
                                        (cache_fox_k, cache_fox_v, cache_band_k, cache_band_v))

    outs = {k: [] for k in ("p_fl", "p_sr", "p_si", "p_mk", "p_mv",
                            "s_fk", "s_fv", "s_fl", "s_bk", "s_bv", "s_sr", "s_si")}
    cache_a = cache_b = None
    for l in range(depth):
        sizes = (w_a, w_a, w_a, h_a, w_b, w_b, w_b)
        cuts = [sum(sizes[:i]) for i in range(len(sizes) + 1)]
        w_cat = _regroup_columns(
            w_in, l, ((0, cuts[3]), (cuts[4], cuts[7]), (cuts[7], w_in.shape[2]), (cuts[3], cuts[4])),
            7 * w_a)
        proj = _mm([x], [w_cat], gain=g_mix[l].reshape(1, d), tm=tm, tn=w_a, name="proj_in")
        col_uc = 6 * w_a

        assert (col_uc + w_c) % LANES == 0
        lf_p, f_aug = _logf_rows(proj, jnp.pad(b_f[l].astype(F32), (0, LANES - h_a)).reshape(1, LANES),
                                 batch=batch, seq=seq, n_heads=h_a, n_pairs=pa,
                                 col_block=(col_uc + w_c) // LANES)

        t_all = -(-(past_len + dseq) // LANES) * LANES
        fa_s = proj[mp:, col_uc + w_c:col_uc + w_c + h_a].reshape(dbatch, dseq, h_a)
        x_s = jnp.concatenate([jnp.transpose(cache_fox_logf[l], (0, 2, 1)),
                               jnp.transpose(fa_s, (0, 2, 1)),
                               jnp.zeros((dbatch, h_a, t_all - past_len - dseq), F32)], axis=2)
        bias_s = jnp.broadcast_to(jnp.tile(b_f[l], dbatch)[:, None], (dbatch * h_a, LANES))
        lf_s, f_s = _logf_cumsum(x_s.reshape(dbatch * h_a, t_all), bias_s,
                                 raw_from=past_len, valid_to=past_len + dseq)
        lf_s = lf_s.reshape(dbatch, h_a, t_all)[:, :, past_len:past_len + dseq]
        f_s = f_s.reshape(dbatch, h_a, t_all)
        fs_row = jnp.pad(f_s, ((0, 0), (0, 16 - h_a), (0, 0)))
        fs_col = jnp.pad(jnp.transpose(f_s[:, :, past_len:past_len + dseq], (0, 2, 1)),
                         ((0, 0), (0, 0), (0, LANES - h_a))).reshape(ms, LANES)

        gqa, gka = _tile_gain(g_qa[l]), _tile_gain(g_ka[l])
        oa, *cache_a = _fox_prompt(proj, f_aug, gqa, gka, batch=batch, seq=seq, n_pairs=pa,
                                   col_q=0, col_k=pa, col_v=2 * pa, m_rows=m, layer=l, depth=depth,
                                   prev=cache_a)
        oa, kn_as = _sample_attn(proj, fox_kt, fox_vt, l, fs_col, fs_row,
                                 gqa, gka, oa, mode="fox", batch=dbatch, s_new=dseq, n_pairs=pa,
                                 col_q=0, row0=mp)

        gqb, gkb = _tile_gain(g_qb[l]), _tile_gain(g_kb[l])
        tab = _band_table(rel_bias[l], tq_band, max_rel)
        ob, *cache_b = _band_prompt(proj, tab, gqb, gkb, batch=batch, seq=seq, n_pairs=pb,
                                    col_q=3 * pa, col_k=3 * pa + pb, col_v=3 * pa + 2 * pb,
                                    tq=tq_band, m_rows=m, layer=l, depth=depth, prev=cache_b)
        tab_s = _rel_table(rel_bias[l], band_rows + np.arange(dseq), np.arange(band_rows + dseq),
                           max_rel)
        ob, kn_bs = _sample_attn(proj, band_kt, band_vt, l,
                                 tab_s[:, :, :band_rows], tab_s[:, :, band_rows:], gqb, gkb, ob,
                                 mode="band", batch=dbatch, s_new=dseq, n_pairs=pb, col_q=3, row0=mp)

        rep = lambda a: jnp.repeat(a.astype(F32), SSM_GROUP, axis=0)
        a_re, a_im, bb_re, bb_im = _s5_prep(
            rep(lam_re[l]), rep(lam_im[l]),
            jnp.broadcast_to(rep(log_dt[l])[:, None], (g_c * SSM_GROUP, p_state)),
            jnp.transpose(ssm_b_re[l], (0, 2, 1)).reshape(g_c * SSM_GROUP, p_state),
            jnp.transpose(ssm_b_im[l], (0, 2, 1)).reshape(g_c * SSM_GROUP, p_state))
        a_re = a_re.reshape(g_c, SSM_GROUP, p_state)[:, 0, :].reshape(1, n_state)
        a_im = a_im.reshape(g_c, SSM_GROUP, p_state)[:, 0, :].reshape(1, n_state)
        b_re_d = _block_diag(bb_re.reshape(g_c, SSM_GROUP, p_state))
        b_im_d = _block_diag(bb_im.reshape(g_c, SSM_GROUP, p_state))
        c_re_d = _block_diag(jnp.transpose(ssm_c_re[l], (0, 2, 1))).astype(BF16)
        c_im_d = _block_diag(jnp.transpose(ssm_c_im[l], (0, 2, 1))).astype(BF16)
        d_row = ssm_d[l].reshape(1, w_c)
        s5_tail = (c_re_d, c_im_d, d_row, w_glu[l], b_glu[l].reshape(1, w_c))
        assert col_uc % w_c == 0 and w_c % LANES == 0
        u_s = jnp.transpose(proj[mp:, col_uc:col_uc + w_c].reshape(dbatch, dseq, w_c),
                            (1, 0, 2)).reshape(ms, w_c)
        zeros_p = jnp.zeros((batch, n_state), F32)
        oc_p, sr_p, si_p = _s5(proj, zeros_p, zeros_p, a_re, a_im, b_re_d.astype(BF16),
                               b_im_d.astype(BF16), *s5_tail, nb=batch, seq=seq, n_blk=n_blk,
                               passes=1, col_block=col_uc // w_c)
        oc_s, sr_s, si_s = _s5(u_s, state_ssm_re[l].reshape(dbatch, n_state),
                               state_ssm_im[l].reshape(dbatch, n_state), a_re, a_im, b_re_d, b_im_d,
                               *s5_tail, nb=dbatch, seq=dseq, n_blk=n_blk, passes=3)
        oc = jnp.concatenate(
            [oc_p.reshape(mp, w_c),
             jnp.transpose(oc_s.reshape(dseq, dbatch, w_c), (1, 0, 2)).reshape(ms, w_c)], axis=0)

        x = _mm([oa, ob, oc], [w_out[l].astype(BF16)], gain=g_mix_out[l].reshape(1, -1), residual=x,
                tm=tm, tn=_pick(d, (1024, 512, 256, 128)), name="merge_out")

        outs["p_fl"].append(lf_p[:, :h_a].reshape(batch, seq, h_a))
        outs["p_sr"].append(sr_p.reshape(batch, g_c, p_state))
        outs["p_si"].append(si_p.reshape(batch, g_c, p_state))
        outs["s_fk"].append(kn_as.reshape(dbatch, dseq, h_a, dh))
        outs["s_fv"].append(proj[mp:, 2 * w_a:3 * w_a].reshape(dbatch, dseq, h_a, dh))
        outs["s_fl"].append(jnp.transpose(lf_s, (0, 2, 1)))
        outs["s_bk"].append(kn_bs.reshape(dbatch, dseq, h_b, dh))
        outs["s_bv"].append(proj[mp:, 5 * w_a:6 * w_a].reshape(dbatch, dseq, h_b, dh))
        outs["s_sr"].append(sr_s.reshape(dbatch, g_c, p_state))
        outs["s_si"].append(si_s.reshape(dbatch, g_c, p_state))

        mem2 = mem_prompt.reshape(batch * n_mem, d)
        tmm = _pick(batch * n_mem, (1024, 512, 256, 128))
        gck = jnp.tile(g_ck[l].astype(F32), h_m).reshape(1, w_m)
        gcq = jnp.tile(g_cq[l].astype(F32), h_m).reshape(1, w_m)
        mk = _mm([mem2], [w_ck[l]], gain=g_mem[l].reshape(1, d), group_gain=gck,
                 epilogue="group_norm", tm=tmm, tn=w_m, name="mem_k")
        mv = _mm([mem2], [w_cv[l]], gain=g_mem[l].reshape(1, d), tm=tmm, tn=w_m, name="mem_v")
        outs["p_mk"].append(mk.reshape(batch, n_mem, h_m, dh_m))
        outs["p_mv"].append(mv.reshape(batch, n_mem, h_m, dh_m))
        q_c = _mm([x], [w_cq[l].astype(BF16)], gain=g_cross[l].reshape(1, d), group_gain=gcq,
                  epilogue="group_norm", tm=tm, tn=w_m, name="cross_q")
        o_c = _cross_attn(q_c, mk, mv, None, batch=batch, q_len=seq, n_mem=n_mem, n_heads=h_m,
                          dh=dh_m, row0=0, m_rows=m)
        o_c = _cross_attn(q_c, cache_mem_k[l].reshape(dbatch * n_mem, w_m),
                          cache_mem_v[l].reshape(dbatch * n_mem, w_m), o_c, batch=dbatch,
                          q_len=dseq, n_mem=n_mem, n_heads=h_m, dh=dh_m, row0=mp, m_rows=m)
        x = _mm([o_c], [w_co[l].astype(BF16)], residual=x, tm=tm,
                tn=_pick(d, (1024, 512, 256, 128)), name="cross_out")

        i = l // 2
        if l % 2 == 0:
            hmid = _mm([x], [w_ff1[i].astype(BF16), w_ff3[i].astype(BF16)],
                       gain=g_ffn[l].reshape(1, d), epilogue="swiglu", out_dtype=BF16, tm=tm,
                       tn=_pick(w_ff1.shape[2], (512, 256, 128)), name="ffn_up")
            x = _mm([hmid], [w_ff2[i].astype(BF16)], residual=x, tm=tm,
                    tn=_pick(d, (512, 256, 128)), name="ffn_down")
        else:
            x = _moe(x, g_ffn[l].reshape(1, d), w_router[i], w_e1[i], w_e3[i], w_e2[i], tm_tok=tm,
                     split=mp if l == depth - 1 else None)

    st = lambda k: jnp.stack(outs[k])
    xp, xs = x if isinstance(x, tuple) else (x[:mp], x[mp:])
    per_head = lambda t, h: jnp.transpose(t.reshape(depth, batch, h, dh, -1), (0, 1, 4, 2, 3))
    return (xp.reshape(batch, seq, d), xs.reshape(dbatch, dseq, d),
            per_head(cache_a[0], h_a), per_head(cache_a[1], h_a), st("p_fl"),
            per_head(cache_b[0][..., seq - nband:], h_b), per_head(cache_b[1][..., seq - nband:], h_b),
            st("p_sr"), st("p_si"),
            st("p_mk"), st("p_mv"), st("s_fk"), st("s_fv"), st("s_fl"), st("s_bk"), st("s_bv"),
            st("s_sr"), st("s_si"))
```

```python
import functools
import math

import jax
import jax.numpy as jnp
import numpy as np
from jax import lax
from jax.experimental import pallas as pl
from jax.experimental.pallas import tpu as pltpu

F32 = jnp.float32
BF16 = jnp.bfloat16
I32 = jnp.int32
U32 = jnp.uint32

EPS = 1e-6
NEG_INF = -1e30
LOG2E = math.log2(math.e)
CHUNK = 64
BAND_PAST = 8
SSM_GROUP = 16
TOP_K = 2

LANES = 128
SUBLANES = 8
BF16_ROWS = 16
VMEM_CAP = 60 * 1024 * 1024


def _vmem_limit(nbytes):
    return int(min(VMEM_CAP, max(16 * 1024 * 1024, nbytes * 5 // 4 + (4 << 20))))


def _pick(n, candidates):
    for c in candidates:
        if c <= n and n % c == 0:
            return c
    raise ValueError(f"no tile for {n} in {candidates}")


def _nbytes(shape, dtype):
    return math.prod(shape) * jnp.dtype(dtype).itemsize


def _split3(x):
    hi = x.astype(BF16)
    r1 = x - hi.astype(F32)
    mid = r1.astype(BF16)
    lo = (r1 - mid.astype(F32)).astype(BF16)
    return hi, mid, lo


def _dot(a, b):
    return jnp.dot(a, b, preferred_element_type=F32)


def _dot_nt(a, b):
    return lax.dot_general(a, b, (((1,), (1,)), ((), ())), preferred_element_type=F32)


def _dot_hp(a, b, passes=3):
    ah = a.astype(BF16)
    bh = b.astype(BF16)
    if passes == 1:
        return _dot(ah, bh)
    al = (a - ah.astype(F32)).astype(BF16)
    bl = (b - bh.astype(F32)).astype(BF16)
    return _dot(ah, bh) + (_dot(ah, bl) + _dot(al, bh))


def _pair_ones():
    r = lax.broadcasted_iota(I32, (LANES, LANES), 0) // 64
    c = lax.broadcasted_iota(I32, (LANES, LANES), 1) // 64
    return (r == c).astype(BF16)


def _pair_norm(x, g, ones):
    sq = x * x
    hi = sq.astype(BF16)
    lo = (sq - hi.astype(F32)).astype(BF16)
    ss = _dot(hi, ones) + _dot(lo, ones)
    return x * lax.rsqrt(ss * (1.0 / 64.0) + EPS) * g


def _log_sigmoid(x):
    return jnp.minimum(x, 0.0) - jnp.log(1.0 + jnp.exp(-jnp.abs(x)))


def _gelu_tanh(x):
    c = math.sqrt(2.0 / math.pi)
    return 0.5 * x * (1.0 + jnp.tanh(c * (x + 0.044715 * (x * x * x))))


def _mm_body(*refs, widths, norm, n_w, epilogue, has_res, has_gg, stage_a, tm, rc, kc, k_total):
    it = iter(refs)
    a_refs = [next(it) for _ in widths]
    g_ref = next(it) if norm else None
    w_refs = [next(it) for _ in range(n_w)]
    gg_ref = next(it) if has_gg else None
    res_ref = next(it) if has_res else None
    o_ref = next(it)
    abf = next(it) if stage_a else a_refs[0]
    cast_w = w_refs[0].dtype != BF16
    wbfs = [next(it) for _ in range(n_w)] if cast_w else w_refs
    j = pl.program_id(1)

    if stage_a:
        @pl.when(j == 0)
        def _():
            off = 0
            for a_ref, wd in zip(a_refs, widths):
                def chunk(c, carry, a_ref=a_ref, off=off, wd=wd):
                    r0 = pl.multiple_of(c * rc, rc)
                    x = a_ref[pl.ds(r0, rc), :].astype(F32)
                    if norm:
                        ms = jnp.mean(x * x, axis=-1, keepdims=True)
                        x = x * lax.rsqrt(ms + EPS) * g_ref[:, off:off + wd]
                    abf[pl.ds(r0, rc), off:off + wd] = x.astype(BF16)
                    return carry
                lax.fori_loop(0, tm // rc, chunk, 0)
                off += wd

    if cast_w:
        for w_ref, wbf in zip(w_refs, wbfs):
            def cast(c, carry, w_ref=w_ref, wbf=wbf):
                r0 = pl.multiple_of(c * kc, kc)
                wbf[pl.ds(r0, kc), :] = w_ref[pl.ds(r0, kc), :].astype(BF16)
                return carry
            lax.fori_loop(0, k_total // kc, cast, 0)

    for c in range(tm // rc):
        r0 = c * rc
        a = abf[pl.ds(r0, rc), :]
        ys = [_dot(a, wbf[...]) for wbf in wbfs]
        if epilogue == "swiglu":
            y = jax.nn.silu(ys[0]) * ys[1]
        elif epilogue == "group_norm":
            parts = []
            for s in range(ys[0].shape[1] // LANES):
                ysl = ys[0][:, s * LANES:(s + 1) * LANES]
                ms = jnp.mean(ysl * ysl, axis=-1, keepdims=True)
                parts.append(ysl * lax.rsqrt(ms + EPS) * gg_ref[:, s * LANES:(s + 1) * LANES])
            y = jnp.concatenate(parts, axis=1)
        else:
            y = ys[0]
        if has_res:
            y = y + res_ref[pl.ds(r0, rc), :]
        o_ref[pl.ds(r0, rc), :] = y.astype(o_ref.dtype)


def _mm(a_parts, w_list, *, gain=None, group_gain=None, residual=None, epilogue="none",
        out_dtype=F32, tm, tn, name):
    m = a_parts[0].shape[0]
    widths = tuple(a.shape[1] for a in a_parts)
    k_total = sum(widths)
    n = w_list[0].shape[1]
    assert m % tm == 0 and n % tn == 0, (m, tm, n, tn)
    norm = gain is not None
    stage_a = norm or len(a_parts) > 1 or a_parts[0].dtype != BF16
    rc = _pick(tm, (512, 384, 352, 256, 176, 128, 64, 32, 16))
    kc = _pick(k_total, (512, 256, 128))
    grid = (m // tm, n // tn)
    in_specs = [pl.BlockSpec((tm, wd), lambda i, j: (i, 0)) for wd in widths]
    args = list(a_parts)
    est = sum(2 * _nbytes((tm, wd), a.dtype) for wd, a in zip(widths, a_parts))
    if norm:
        in_specs.append(pl.BlockSpec((1, k_total), lambda i, j: (0, 0)))
        args.append(gain)
    cast_w = w_list[0].dtype != BF16
    for w in w_list:
        in_specs.append(pl.BlockSpec((k_total, tn), lambda i, j: (0, j)))
        args.append(w)
        est += 2 * _nbytes((k_total, tn), w.dtype) + cast_w * _nbytes((k_total, tn), BF16)
    if group_gain is not None:
        in_specs.append(pl.BlockSpec((1, tn), lambda i, j: (0, j)))
        args.append(group_gain)
    if residual is not None:
        in_specs.append(pl.BlockSpec((tm, tn), lambda i, j: (i, j)))
        args.append(residual)
        est += 2 * _nbytes((tm, tn), F32)
    est += 2 * _nbytes((tm, tn), out_dtype) + stage_a * _nbytes((tm, k_total), BF16)
    est += 4 * _nbytes((rc, tn), F32) * len(w_list)
    body = functools.partial(
        _mm_body, widths=widths, norm=norm, n_w=len(w_list), epilogue=epilogue,
        has_res=residual is not None, has_gg=group_gain is not None, stage_a=stage_a, tm=tm,
        rc=rc, kc=kc, k_total=k_total)
    return pl.pallas_call(
        body,
        out_shape=jax.ShapeDtypeStruct((m, n), out_dtype),
        grid=grid,
        in_specs=in_specs,
        out_specs=pl.BlockSpec((tm, tn), lambda i, j: (i, j)),
        scratch_shapes=[pltpu.VMEM((tm, k_total), BF16)] * stage_a
        + [pltpu.VMEM((k_total, tn), BF16) for _ in w_list] * cast_w,
        compiler_params=pltpu.CompilerParams(
            dimension_semantics=("parallel", "arbitrary"),
            vmem_limit_bytes=_vmem_limit(est)),
        name=name,
    )(*args)


def _regroup_body(w_ref, o_ref, *, cuts, width):
    off = 0
    for lo, hi in cuts:
        o_ref[:, off:off + hi - lo] = w_ref[:, lo:hi].astype(o_ref.dtype)
        off += hi - lo
    if off < width:
        o_ref[:, off:width] = jnp.zeros((o_ref.shape[0], width - off), o_ref.dtype)


def _regroup_columns(w, layer, cuts, width):
    _, k, n = w.shape
    tr = _pick(k, (256, 128, 64, 32, 16))
    return pl.pallas_call(
        functools.partial(_regroup_body, cuts=cuts, width=width),
        out_shape=jax.ShapeDtypeStruct((k, width), BF16),
        grid=(k // tr,),
        in_specs=[pl.BlockSpec((None, tr, n), lambda i: (layer, i, 0))],
        out_specs=pl.BlockSpec((tr, width), lambda i: (i, 0)),
        compiler_params=pltpu.CompilerParams(
            dimension_semantics=("parallel",),
            vmem_limit_bytes=_vmem_limit(2 * _nbytes((tr, n), F32) + 2 * _nbytes((tr, width), BF16)
                                         + 4 * _nbytes((tr, width), F32))),
        name="regroup_w_in",
    )(w)


def _cumsum_body(x_ref, b_ref, lf_ref, f_ref, carry, *, raw_from, valid_to, tt):
    j = pl.program_id(0)

    @pl.when(j == 0)
    def _():
        carry[...] = jnp.zeros_like(carry)

    x = x_ref[...]
    lane = j * tt + lax.broadcasted_iota(I32, x.shape, 1)
    lf = jnp.where(lane >= raw_from, _log_sigmoid(x + b_ref[:, 0:1]), x)
    lf = jnp.where(lane < valid_to, lf, 0.0)
    lf_ref[...] = lf
    tri = (lax.broadcasted_iota(I32, (tt, tt), 0)
           <= lax.broadcasted_iota(I32, (tt, tt), 1)).astype(BF16)
    hi, mid, lo = _split3(lf)
    y = _dot(hi, tri) + _dot(mid, tri) + _dot(lo, tri) + carry[:, 0:1]
    f_ref[...] = y
    carry[...] = jnp.broadcast_to(y[:, tt - 1:tt], carry.shape)


def _logf_cumsum(x, bias, *, raw_from, valid_to):
    r, t = x.shape
    tt = _pick(t, (256, 128))
    return pl.pallas_call(
        functools.partial(_cumsum_body, raw_from=raw_from, valid_to=valid_to, tt=tt),
        out_shape=(jax.ShapeDtypeStruct((r, t), F32), jax.ShapeDtypeStruct((r, t), F32)),
        grid=(t // tt,),
        in_specs=[pl.BlockSpec((r, tt), lambda j: (0, j)),
                  pl.BlockSpec((r, LANES), lambda j: (0, 0))],
        out_specs=(pl.BlockSpec((r, tt), lambda j: (0, j)),
                   pl.BlockSpec((r, tt), lambda j: (0, j))),
        scratch_shapes=[pltpu.VMEM((r, LANES), F32)],
        compiler_params=pltpu.CompilerParams(dimension_semantics=("arbitrary",)),
        name="logf_cumsum",
    )(x, bias)


def _logf_rows_body(fa_ref, b_ref, lf_ref, faug_ref, carry, *, tt, n_heads):
    j = pl.program_id(1)

    @pl.when(j == 0)
    def _():
        carry[...] = jnp.zeros_like(carry)

    lane = lax.broadcasted_iota(I32, (tt, LANES), 1)
    lf = jnp.where(lane < n_heads, _log_sigmoid(fa_ref[...] + b_ref[...]), 0.0)
    lf_ref[...] = lf
    tri = (lax.broadcasted_iota(I32, (tt, tt), 0)
           >= lax.broadcasted_iota(I32, (tt, tt), 1)).astype(BF16)
    hi, mid, lo = _split3(lf)
    f = _dot(tri, hi) + _dot(tri, mid) + _dot(tri, lo) + carry[0:1, :]
    carry[...] = jnp.broadcast_to(f[tt - 1:tt, :], carry.shape)
    width = faug_ref.shape[1]
    head = lax.broadcasted_iota(I32, (LANES, width), 0)
    col = lax.broadcasted_iota(I32, (LANES, width), 1)
    within = col % LANES
    sel = ((within < 12) & (head == 2 * (col // LANES) + (within % 6) // 3)).astype(BF16)
    hi, mid, lo = _split3(f * LOG2E)
    faug_ref[...] = _dot(hi, sel) + _dot(mid, sel) + _dot(lo, sel)


def _logf_rows(proj, bias_row, *, batch, seq, n_heads, n_pairs, col_block):
    tt = _pick(seq, (256, 128))
    nt = seq // tt
    return pl.pallas_call(
        functools.partial(_logf_rows_body, tt=tt, n_heads=n_heads),
        out_shape=(jax.ShapeDtypeStruct((batch * seq, LANES), F32),
                   jax.ShapeDtypeStruct((batch * seq, n_pairs * LANES), F32)),
        grid=(batch, nt),
        in_specs=[pl.BlockSpec((tt, LANES), lambda b, j: (b * nt + j, col_block)),
                  pl.BlockSpec((1, LANES), lambda b, j: (0, 0))],
        out_specs=(pl.BlockSpec((tt, LANES), lambda b, j: (b * nt + j, 0)),
                   pl.BlockSpec((tt, n_pairs * LANES), lambda b, j: (b * nt + j, 0))),
        scratch_shapes=[pltpu.VMEM((SUBLANES, LANES), F32)],
        compiler_params=pltpu.CompilerParams(dimension_semantics=("parallel", "arbitrary")),
        name="logf_rows",
    )(proj, bias_row)


def _stack_heads(qn):
    lane = lax.broadcasted_iota(I32, qn.shape, 1)
    q0 = jnp.where(lane < 64, qn, 0.0)
    q1 = jnp.where(lane < 64, 0.0, qn)
    return jnp.concatenate([q0, q1], axis=0).astype(BF16)


def _unstack_heads(o, tq):
    lane = lax.broadcasted_iota(I32, (tq, LANES), 1)
    return jnp.where(lane < 64, o[:tq], o[tq:])


def _lane_column(block, h):
    lane = lax.broadcasted_iota(I32, block.shape, 1)
    return jnp.sum(jnp.where(lane == h, block, 0.0), axis=-1, keepdims=True)


V_ROWS = LANES + BF16_ROWS


def _split_select(x):
    hi, mid, lo = _split3(x)
    m3 = lax.broadcasted_iota(I32, x.shape, 1) % 3
    return jnp.where(m3 == 0, hi.astype(F32), jnp.where(m3 == 1, mid.astype(F32), lo.astype(F32)))


def _head_masks(shape):
    lane = lax.broadcasted_iota(I32, shape, 1)
    return lane, lane < 64


def _finish_heads(acc, tq):
    o0 = acc[0:64, 0:tq] / acc[LANES:LANES + 1, 0:tq]
    o1 = acc[64:LANES, tq:2 * tq] / acc[LANES:LANES + 1, tq:2 * tq]
    return jnp.concatenate([o0, o1], axis=0).T


def _fox_prompt_body(q_ref, k_ref, v_ref, f_ref, gq_ref, gk_ref, *rest, tq, seq, dh):
    o_ref, knt_ref, vt_ref, kaug, vt3 = rest[-5:]
    ones = _pair_ones()
    nq = seq // tq
    lane, first = _head_masks((tq, LANES))

    for c in range(nq):
        rows = slice(c * tq, (c + 1) * tq)
        kn = _pair_norm(k_ref[rows, :], gk_ref[...], ones)
        knt_ref[:, rows] = kn.T
        kaug[rows, 0:LANES] = kn.astype(BF16)
        sp = _split_select(f_ref[rows, :])
        kaug[rows, LANES:2 * LANES] = jnp.where(
            lane < 6, -sp, jnp.where(lane < 12, 1.0, 0.0)).astype(BF16)
        vt = v_ref[rows, :].T
        vt_ref[:, rows] = vt
        vt3[c, 0:LANES, :] = vt.astype(BF16)
        vt3[c, LANES:V_ROWS, :] = jnp.ones((BF16_ROWS, tq), BF16)

    causal = (lax.broadcasted_iota(I32, (tq, 2 * tq), 0)
              <= lax.broadcasted_iota(I32, (tq, 2 * tq), 1) % tq)

    for qi in range(nq):
        rows = slice(qi * tq, (qi + 1) * tq)
        qn = _pair_norm(q_ref[rows, :], gq_ref[...], ones) * (dh ** -0.5 * LOG2E)
        sp = _split_select(f_ref[rows, :])
        up0 = jnp.where(lane < 3, 1.0, jnp.where((lane >= 6) & (lane < 9), sp, 0.0))
        up1 = jnp.where((lane >= 3) & (lane < 6), 1.0,
                        jnp.where((lane >= 9) & (lane < 12), sp, 0.0))
        qs = jnp.concatenate(
            [jnp.concatenate([jnp.where(first, qn, 0.0), up0], axis=1),
             jnp.concatenate([jnp.where(first, 0.0, qn), up1], axis=1)], axis=0).astype(BF16)

        s = jnp.where(causal, _dot_nt(kaug[rows, :], qs), NEG_INF)
        m = jnp.max(s, axis=0, keepdims=True)
        acc = _dot(vt3[qi], jnp.exp2(s - m).astype(BF16))
        for kj in range(qi):
            s = _dot_nt(kaug[kj * tq:(kj + 1) * tq, :], qs)
            m_new = jnp.maximum(m, jnp.max(s, axis=0, keepdims=True))
            acc = jnp.exp2(m - m_new) * acc + _dot(vt3[kj], jnp.exp2(s - m_new).astype(BF16))
            m = m_new
        o_ref[rows, :] = _finish_heads(acc, tq)


def _layer_cache_outputs(layer, depth, batch, width, seq, prev, n_inputs, index):
    shape = jax.ShapeDtypeStruct((depth, batch, width, seq), F32)
    spec = pl.BlockSpec((None, None, LANES, seq), lambda *g: (layer, *index(*g), 0))
    extra_specs = [] if prev is None else [pl.BlockSpec(memory_space=pl.ANY)] * 2
    aliases = {} if prev is None else {n_inputs: 1, n_inputs + 1: 2}
    return (shape, shape), (spec, spec), extra_specs, aliases, (() if prev is None else tuple(prev))


def _fox_prompt(proj, f_aug, gq, gk, *, batch, seq, n_pairs, col_q, col_k, col_v, m_rows, layer,
                depth, prev):
    tq = _pick(seq, (512, 256, 128))
    nq = seq // tq
    width = n_pairs * LANES
    est = (2 * 7 * _nbytes((seq, LANES), F32)
           + _nbytes((seq, 2 * LANES), BF16) + _nbytes((V_ROWS, seq), BF16)
           + 16 * _nbytes((tq, 2 * tq), F32))
    c_shapes, c_specs, extra_specs, aliases, extra_args = _layer_cache_outputs(
        layer, depth, batch, width, seq, prev, 6, lambda b, hp: (b, hp))
    return pl.pallas_call(
        functools.partial(_fox_prompt_body, tq=tq, seq=seq, dh=64),
        out_shape=(jax.ShapeDtypeStruct((m_rows, width), F32), *c_shapes),
        grid=(batch, n_pairs),
        in_specs=[
            pl.BlockSpec((seq, LANES), lambda b, hp: (b, col_q + hp)),
            pl.BlockSpec((seq, LANES), lambda b, hp: (b, col_k + hp)),
            pl.BlockSpec((seq, LANES), lambda b, hp: (b, col_v + hp)),
            pl.BlockSpec((seq, LANES), lambda b, hp: (b, hp)),
            pl.BlockSpec((1, LANES), lambda b, hp: (0, 0)),
            pl.BlockSpec((1, LANES), lambda b, hp: (0, 0)),
            *extra_specs,
        ],
        out_specs=(pl.BlockSpec((seq, LANES), lambda b, hp: (b, hp)), *c_specs),
        input_output_aliases=aliases,
        scratch_shapes=[pltpu.VMEM((seq, 2 * LANES), BF16), pltpu.VMEM((nq, V_ROWS, tq), BF16)],
        compiler_params=pltpu.CompilerParams(
            dimension_semantics=("parallel", "parallel"),
            vmem_limit_bytes=_vmem_limit(est)),
        name="fox_prompt",
    )(proj, proj, proj, f_aug, gq, gk, *extra_args)


def _band_prompt_body(q_ref, k_ref, v_ref, tab_ref, gq_ref, gk_ref, *rest, tq, seq, past, dh):
    o_ref, knt_ref, vt_ref, kpad, vt3 = rest[-5:]
    ones = _pair_ones()
    nq = seq // tq
    npad = past // tq
    win = past + tq

    kpad[0:past, :] = jnp.zeros((past, LANES), BF16)
    for c in range(npad):
        vt3[c] = jnp.zeros((V_ROWS, tq), BF16)
    for c in range(nq):
        rows = slice(c * tq, (c + 1) * tq)
        kn = _pair_norm(k_ref[rows, :], gk_ref[...], ones)
        knt_ref[:, rows] = kn.T
        kpad[past + c * tq:past + (c + 1) * tq, :] = kn.astype(BF16)
        vt = v_ref[rows, :].T
        vt_ref[:, rows] = vt
        vt3[npad + c, 0:LANES, :] = vt.astype(BF16)
        vt3[npad + c, LANES:V_ROWS, :] = jnp.ones((BF16_ROWS, tq), BF16)

    row = lax.broadcasted_iota(I32, (win, 2 * tq), 0)

    for qi in range(nq):
        r0 = qi * tq
        qn = _pair_norm(q_ref[r0:r0 + tq, :], gq_ref[...], ones) * (dh ** -0.5 * LOG2E)
        s = _dot_nt(kpad[r0:r0 + win, :], _stack_heads(qn)) + tab_ref[...]
        if r0 < past:
            s = jnp.where(row >= past - r0, s, NEG_INF)
        m = jnp.max(s, axis=0, keepdims=True)
        p = jnp.exp2(s - m).astype(BF16)
        acc = _dot(vt3[qi], p[0:tq, :])
        for c in range(1, win // tq):
            acc = acc + _dot(vt3[qi + c], p[c * tq:(c + 1) * tq, :])
        o_ref[r0:r0 + tq, :] = _finish_heads(acc, tq)


def _band_prompt(proj, tab, gq, gk, *, batch, seq, n_pairs, col_q, col_k, col_v, tq, m_rows, layer,
                 depth, prev):
    past = BAND_PAST * CHUNK
    assert past % tq == 0 and seq % tq == 0
    nq = seq // tq
    width = n_pairs * LANES
    win = past + tq
    est = (2 * 5 * _nbytes((seq, LANES), F32)
           + _nbytes((seq + past, LANES), BF16) + _nbytes((V_ROWS, seq + past), BF16)
           + 2 * _nbytes((win, 2 * tq), F32) + 8 * _nbytes((win, 2 * tq), F32))
    c_shapes, c_specs, extra_specs, aliases, extra_args = _layer_cache_outputs(
        layer, depth, batch, width, seq, prev, 6, lambda hp, b: (b, hp))
    return pl.pallas_call(
        functools.partial(_band_prompt_body, tq=tq, seq=seq, past=past, dh=64),
        out_shape=(jax.ShapeDtypeStruct((m_rows, width), F32), *c_shapes),
        grid=(n_pairs, batch),
        in_specs=[
            pl.BlockSpec((seq, LANES), lambda hp, b: (b, col_q + hp)),
            pl.BlockSpec((seq, LANES), lambda hp, b: (b, col_k + hp)),
            pl.BlockSpec((seq, LANES), lambda hp, b: (b, col_v + hp)),
            pl.BlockSpec((None, win, 2 * tq), lambda hp, b: (hp, 0, 0)),
            pl.BlockSpec((1, LANES), lambda hp, b: (0, 0)),
            pl.BlockSpec((1, LANES), lambda hp, b: (0, 0)),
            *extra_specs,
        ],
        out_specs=(pl.BlockSpec((seq, LANES), lambda hp, b: (b, hp)), *c_specs),
        input_output_aliases=aliases,
        scratch_shapes=[pltpu.VMEM((seq + past, LANES), BF16),
                        pltpu.VMEM(((seq + past) // tq, V_ROWS, tq), BF16)],
        compiler_params=pltpu.CompilerParams(
            dimension_semantics=("parallel", "parallel"),
            vmem_limit_bytes=_vmem_limit(est)),
        name="band_prompt",
    )(proj, proj, proj, tab, gq, gk, *extra_args)


def _sample_attn_body(*refs, mode, n_pairs, s_new, n_cache, dh):
    if mode == "fox":
        (q_ref, k_ref, v_ref, ck_ref, cv_ref, fq_ref, ft_ref, gq_ref, gk_ref, _alias,
         o_ref, kn_ref) = refs
    else:
        (q_ref, k_ref, v_ref, ck_ref, cv_ref, tabc_ref, tabn_ref, gq_ref, gk_ref, _alias,
         o_ref, kn_ref) = refs
    ones = _pair_ones()
    row = lax.broadcasted_iota(I32, (s_new, s_new), 0)
    col = lax.broadcasted_iota(I32, (s_new, s_new), 1)
    for hp in range(n_pairs):
        sl = slice(hp * LANES, (hp + 1) * LANES)
        qn = _pair_norm(q_ref[:, sl], gq_ref[...], ones) * (dh ** -0.5)
        kn = _pair_norm(k_ref[:, sl], gk_ref[...], ones)
        kn_ref[:, sl] = kn
        vn = v_ref[:, sl]
        outs = []
        for e in range(2):
            h = 2 * hp + e
            hl = slice(e * dh, (e + 1) * dh)
            qh = qn[:, hl].astype(BF16)
            sc = _dot(qh, ck_ref[h].astype(BF16))
            sn = _dot_nt(qh, kn[:, hl].astype(BF16))
            if mode == "fox":
                fq = _lane_column(fq_ref[...], h)
                fk = ft_ref[h:h + 1, :]
                sc = sc + fq - fk[:, :n_cache]
                sn = jnp.where(col <= row, sn + fq - fk[:, n_cache:n_cache + s_new], NEG_INF)
            else:
                sc = sc + tabc_ref[h]
                sn = sn + tabn_ref[h]
            m = jnp.maximum(jnp.max(sc, axis=-1, keepdims=True),
                            jnp.max(sn, axis=-1, keepdims=True))
            pc = jnp.exp(sc - m)
            pn = jnp.exp(sn - m)
            l = jnp.sum(pc, axis=-1, keepdims=True) + jnp.sum(pn, axis=-1, keepdims=True)
            outs.append((_dot_nt(pc.astype(BF16), cv_ref[h].astype(BF16))
                         + _dot(pn.astype(BF16), vn[:, hl].astype(BF16))) / l)
        o_ref[:, sl] = jnp.concatenate(outs, axis=1)


def _sample_attn(proj, cache_kt, cache_vt, layer, extra_a, extra_b, gq, gk, o_buf, *, mode, batch,
                 s_new, n_pairs, col_q, row0):
    width = n_pairs * LANES
    _, _, n_heads, dh, n_cache = cache_kt.shape
    rb = row0 // s_new
    if mode == "fox":
        ex_specs = [pl.BlockSpec((s_new, LANES), lambda b: (b, 0)),
                    pl.BlockSpec((None, 16, extra_b.shape[2]), lambda b: (b, 0, 0))]
    else:
        ex_specs = [pl.BlockSpec(extra_a.shape, lambda b: (0, 0, 0)),
                    pl.BlockSpec(extra_b.shape, lambda b: (0, 0, 0))]
    est = (2 * 2 * _nbytes((n_cache, width), F32) + 8 * _nbytes((s_new, width), F32)
           + 2 * _nbytes(extra_a.shape, F32) + 16 * _nbytes((2 * s_new, n_cache), F32)
           + 4 * _nbytes((n_cache, LANES), BF16))
    return pl.pallas_call(
        functools.partial(_sample_attn_body, mode=mode, n_pairs=n_pairs, s_new=s_new,
                          n_cache=n_cache, dh=dh),
        out_shape=(jax.ShapeDtypeStruct(o_buf.shape, F32),
                   jax.ShapeDtypeStruct((batch * s_new, width), F32)),
        grid=(batch,),
        in_specs=[
            pl.BlockSpec((s_new, width), lambda b: (rb + b, col_q)),
            pl.BlockSpec((s_new, width), lambda b: (rb + b, col_q + 1)),
            pl.BlockSpec((s_new, width), lambda b: (rb + b, col_q + 2)),
            pl.BlockSpec((None, None, n_heads, dh, n_cache), lambda b: (layer, b, 0, 0, 0)),
            pl.BlockSpec((None, None, n_heads, dh, n_cache), lambda b: (layer, b, 0, 0, 0)),
            *ex_specs,
            pl.BlockSpec((1, LANES), lambda b: (0, 0)),
            pl.BlockSpec((1, LANES), lambda b: (0, 0)),
            pl.BlockSpec(memory_space=pl.ANY),
        ],
        out_specs=(pl.BlockSpec((s_new, width), lambda b: (rb + b, 0)),
                   pl.BlockSpec((s_new, width), lambda b: (b, 0))),
        input_output_aliases={9: 0},
        compiler_params=pltpu.CompilerParams(
            dimension_semantics=("parallel",), vmem_limit_bytes=_vmem_limit(est)),
        name=f"{mode}_sample",
    )(proj, proj, proj, cache_kt, cache_vt, extra_a, extra_b, gq, gk, o_buf)


def _cross_body(*refs, n_heads, dh, aliased):
    if aliased:
        q_ref, k_ref, v_ref, _alias, o_ref = refs
    else:
        q_ref, k_ref, v_ref, o_ref = refs
    for h in range(n_heads):
        sl = slice(h * dh, (h + 1) * dh)
        q = (q_ref[:, sl] * (dh ** -0.5)).astype(BF16)
        s = _dot_nt(q, k_ref[:, sl].astype(BF16))
        m = jnp.max(s, axis=-1, keepdims=True)
        p = jnp.exp(s - m)
        l = jnp.sum(p, axis=-1, keepdims=True)
        o = _dot(p.astype(BF16), v_ref[:, sl].astype(BF16)) / l
        o_ref[:, sl] = o.astype(o_ref.dtype)


def _cross_attn(q_all, k, v, o_buf, *, batch, q_len, n_mem, n_heads, dh, row0, m_rows):
    width = n_heads * dh
    tq = _pick(q_len, (512, 256, 128, 64, 32, 16))
    nq = q_len // tq
    rb = row0 // tq
    aliased = o_buf is not None
    in_specs = [pl.BlockSpec((tq, width), lambda b, qi: (rb + b * nq + qi, 0)),
                pl.BlockSpec((n_mem, width), lambda b, qi: (b, 0)),
                pl.BlockSpec((n_mem, width), lambda b, qi: (b, 0))]
    args = [q_all, k, v]
    if aliased:
        in_specs.append(pl.BlockSpec(memory_space=pl.ANY))
        args.append(o_buf)
    est = (2 * _nbytes((tq, width), F32) + 4 * _nbytes((n_mem, width), F32)
           + 2 * _nbytes((tq, width), BF16) + 12 * _nbytes((tq, n_mem), F32))
    return pl.pallas_call(
        functools.partial(_cross_body, n_heads=n_heads, dh=dh, aliased=aliased),
        out_shape=jax.ShapeDtypeStruct((m_rows, width), BF16),
        grid=(batch, nq),
        in_specs=in_specs,
        out_specs=pl.BlockSpec((tq, width), lambda b, qi: (rb + b * nq + qi, 0)),
        input_output_aliases={3: 0} if aliased else {},
        compiler_params=pltpu.CompilerParams(
            dimension_semantics=("parallel", "arbitrary"), vmem_limit_bytes=_vmem_limit(est)),
        name="cross_attn",
    )(*args)


def _s5_prep_body(lr_ref, li_ref, ldt_ref, br_ref, bi_ref, ar_ref, ai_ref, bbr_ref, bbi_ref):
    lr = lr_ref[...]
    li = li_ref[...]
    dt = jnp.exp(ldt_ref[...])
    mag = jnp.exp(lr * dt)
    a_re = mag * jnp.cos(li * dt)
    a_im = mag * jnp.sin(li * dt)
    den = lr * lr + li * li
    num_re = a_re - 1.0
    coef_re = (num_re * lr + a_im * li) / den
    coef_im = (a_im * lr - num_re * li) / den
    br = br_ref[...]
    bi = bi_ref[...]
    ar_ref[...] = a_re
    ai_ref[...] = a_im
    bbr_ref[...] = coef_re * br - coef_im * bi
    bbi_ref[...] = coef_re * bi + coef_im * br


def _s5_prep(lam_re, lam_im, log_dt, b_re, b_im):
    shape = lam_re.shape
    spec = pl.BlockSpec(shape, lambda: (0, 0))
    return pl.pallas_call(
        _s5_prep_body,
        out_shape=tuple(jax.ShapeDtypeStruct(shape, F32) for _ in range(4)),
        in_specs=[spec] * 5,
        out_specs=tuple([spec] * 4),
        name="s5_discretise",
    )(lam_re, lam_im, log_dt, b_re, b_im)


def _s5_body(*refs, nb, t_chunk, n_blk, passes, per_stream):
    n_u = nb if per_stream else 1
    u_refs = refs[:n_u]
    (x0r_ref, x0i_ref, ar_ref, ai_ref, bre_ref, bim_ref, cre_ref, cim_ref, d_ref, wg_ref, bg_ref,
     o_ref, xr_out, xi_out, bur, bui, st_r, st_i) = refs[n_u:n_u + 18]
    i = pl.program_id(0)
    wc = u_refs[0].shape[1]
    ns = bur.shape[1]
    ub = wc // n_blk
    sb = ns // n_blk

    @pl.when(i == 0)
    def _():
        st_r[...] = x0r_ref[...]
        st_i[...] = x0i_ref[...]

    if per_stream:
        slab = refs[n_u + 18]
        for b in range(nb):
            for k in range(wc // LANES):
                slab[k, pl.ds(b, t_chunk, stride=nb), :] = u_refs[b][:, k * LANES:(k + 1) * LANES]
        u = jnp.concatenate([slab[k] for k in range(wc // LANES)], axis=1)
    else:
        u = u_refs[0][...]
    rows = nb * t_chunk
    n_half = 2 if (t_chunk % 2 == 0 and rows >= 1024) else 1
    hr = rows // n_half
    for h in range(n_half):
        for k in range(n_blk):
            uk = u[h * hr:(h + 1) * hr, k * ub:(k + 1) * ub]
            bur[h * hr:(h + 1) * hr, k * sb:(k + 1) * sb] = _dot_hp(
                uk, bre_ref[k * ub:(k + 1) * ub, k * sb:(k + 1) * sb], passes)
            bui[h * hr:(h + 1) * hr, k * sb:(k + 1) * sb] = _dot_hp(
                uk, bim_ref[k * ub:(k + 1) * ub, k * sb:(k + 1) * sb], passes)

    a_re = ar_ref[...]
    a_im = ai_ref[...]
    xr, xi = st_r[...], st_i[...]
    for t in range(t_chunk):
        r0 = t * nb
        xr, xi = (a_re * xr - a_im * xi + bur[r0:r0 + nb, :],
                  a_re * xi + a_im * xr + bui[r0:r0 + nb, :])
        bur[r0:r0 + nb, :] = xr
        bui[r0:r0 + nb, :] = xi
    st_r[...] = xr
    st_i[...] = xi
    xr_out[...] = xr
    xi_out[...] = xi
    outs = []
    for h in range(n_half):
        ys = []
        for k in range(n_blk):
            xrk = bur[h * hr:(h + 1) * hr, k * sb:(k + 1) * sb].astype(BF16)
            xik = bui[h * hr:(h + 1) * hr, k * sb:(k + 1) * sb].astype(BF16)
            ys.append(_dot(xrk, cre_ref[k * sb:(k + 1) * sb, k * ub:(k + 1) * ub].astype(BF16))
                      - _dot(xik, cim_ref[k * sb:(k + 1) * sb, k * ub:(k + 1) * ub].astype(BF16)))
        y = jnp.concatenate(ys, axis=1) + d_ref[...] * u[h * hr:(h + 1) * hr, :]
        z = _gelu_tanh(y)
        gate = _dot(z.astype(BF16), wg_ref[...].astype(BF16)) + bg_ref[...]
        outs.append(z * jax.nn.sigmoid(gate))
    out = outs[0] if n_half == 1 else jnp.concatenate(outs, axis=0)
    if per_stream:
        for k in range(wc // LANES):
            slab[k] = out[:, k * LANES:(k + 1) * LANES]
        for b in range(nb):
            o_ref[b] = jnp.concatenate(
                [slab[k, pl.ds(b, t_chunk, stride=nb), :] for k in range(wc // LANES)], axis=1)
    else:
        o_ref[...] = out


def _s5(u, x0r, x0i, a_re, a_im, b_re, b_im, c_re, c_im, d, w_glu, b_glu, *, nb, seq, n_blk,
        passes, col_block=None):
    per_stream = col_block is not None
    wc, ns = b_re.shape
    t_chunk = _pick(seq, (128, 64, 32, 16) if nb <= SUBLANES else (16,))
    nc = seq // t_chunk
    rows = nb * t_chunk
    full = lambda shape: pl.BlockSpec(shape, lambda i: (0,) * len(shape))
    est = (6 * _nbytes((rows, wc), F32) + 2 * _nbytes((rows, ns), F32)
           + 2 * 2 * _nbytes((wc, ns), b_re.dtype) + 2 * 2 * _nbytes((wc, ns), c_re.dtype)
           + 2 * _nbytes((wc, wc), F32)
           + 6 * _nbytes((rows // 2, ns // n_blk), F32) + 8 * _nbytes((nb, ns), F32))
    if per_stream:
        u_specs = [pl.BlockSpec((t_chunk, wc), lambda i, s=s: (s * nc + i, col_block))
                   for s in range(nb)]
        u_args = [u] * nb
        o_shape = jax.ShapeDtypeStruct((nb, seq, wc), F32)
        o_spec = pl.BlockSpec((nb, t_chunk, wc), lambda i: (0, i, 0))
        slab = [pltpu.VMEM((wc // LANES, rows, LANES), F32)]
    else:
        u_specs = [pl.BlockSpec((rows, wc), lambda i: (i, 0))]
        u_args = [u]
        o_shape = jax.ShapeDtypeStruct((seq * nb, wc), F32)
        o_spec = pl.BlockSpec((rows, wc), lambda i: (i, 0))
        slab = []
    return pl.pallas_call(
        functools.partial(_s5_body, nb=nb, t_chunk=t_chunk, n_blk=n_blk, passes=passes,
                          per_stream=per_stream),
        out_shape=(o_shape, jax.ShapeDtypeStruct((nb, ns), F32), jax.ShapeDtypeStruct((nb, ns), F32)),
        grid=(nc,),
        in_specs=[*u_specs,
                  full((nb, ns)), full((nb, ns)), full((1, ns)), full((1, ns)),
                  full((wc, ns)), full((wc, ns)), full((ns, wc)), full((ns, wc)),
                  full((1, wc)), full((wc, wc)), full((1, wc))],
        out_specs=(o_spec, full((nb, ns)), full((nb, ns))),
        scratch_shapes=[pltpu.VMEM((rows, ns), F32), pltpu.VMEM((rows, ns), F32),
                        pltpu.VMEM((nb, ns), F32), pltpu.VMEM((nb, ns), F32), *slab],
        compiler_params=pltpu.CompilerParams(
            dimension_semantics=("arbitrary",), vmem_limit_bytes=_vmem_limit(est)),
        name="s5_scan",
    )(*u_args, x0r, x0i, a_re, a_im, b_re, b_im, c_re, c_im, d, w_glu, b_glu)


def _router_body(x_ref, g_ref, wr_ref, pk_ref, idx_ref, gate_ref, *, n_exp, rc):
    tm, d = x_ref.shape
    half = d // 2

    def chunk(c, carry):
        r0 = pl.multiple_of(c * rc, rc)
        x = x_ref[pl.ds(r0, rc), :]
        ms = jnp.mean(x * x, axis=-1, keepdims=True)
        h = x * lax.rsqrt(ms + EPS) * g_ref[...]
        hb = h.astype(BF16).astype(F32)
        lo = lax.shift_right_logical(pltpu.bitcast(hb[:, :half], U32), jnp.uint32(16))
        hi = pltpu.bitcast(hb[:, half:], U32) & jnp.uint32(0xFFFF0000)
        pk_ref[pl.ds(r0, rc), :] = lo | hi
        logits = _dot_hp(h, wr_ref[...])
        lane = lax.broadcasted_iota(I32, logits.shape, 1)
        logits = jnp.where(lane < n_exp, logits, NEG_INF)
        mx = jnp.max(logits, axis=-1, keepdims=True)
        e = jnp.exp(logits - mx)
        probs = e / jnp.sum(e, axis=-1, keepdims=True)
        probs = jnp.where(lane < n_exp, probs, -1.0)
        lane_f = lane.astype(F32)
        p1 = jnp.max(probs, axis=-1, keepdims=True)
        i1 = jnp.min(jnp.where(probs == p1, lane_f, float(LANES)), axis=-1, keepdims=True)
        rest = jnp.where(lane_f == i1, -1.0, probs)
        p2 = jnp.max(rest, axis=-1, keepdims=True)
        i2 = jnp.min(jnp.where(rest == p2, lane_f, float(LANES)), axis=-1, keepdims=True)
        tot = p1 + p2
        idx_ref[pl.ds(r0, rc), :] = jnp.where(lane == 0, i1, jnp.where(lane == 1, i2, 0.0)).astype(I32)
        gate_ref[pl.ds(r0, rc), :] = jnp.where(lane == 0, p1 / tot,
                                                jnp.where(lane == 1, p2 / tot, 0.0))
        return carry
    lax.fori_loop(0, tm // rc, chunk, 0)


def _router(x, g, w_router_pad, *, n_exp, tm):
    m, d = x.shape
    rc = _pick(tm, (256, 176, 128, 64, 32, 16, 8))
    est = 2 * _nbytes((tm, d), F32) + 2 * _nbytes((tm, d // 2), U32) + 16 * _nbytes((rc, d), F32)
    return pl.pallas_call(
        functools.partial(_router_body, n_exp=n_exp, rc=rc),
        out_shape=(jax.ShapeDtypeStruct((m, d // 2), U32),
                   jax.ShapeDtypeStruct((m, LANES), I32),
                   jax.ShapeDtypeStruct((m, LANES), F32)),
        grid=(m // tm,),
        in_specs=[pl.BlockSpec((tm, d), lambda i: (i, 0)),
                  pl.BlockSpec((1, d), lambda i: (0, 0)),
                  pl.BlockSpec((d, LANES), lambda i: (0, 0))],
        out_specs=(pl.BlockSpec((tm, d // 2), lambda i: (i, 0)),
                   pl.BlockSpec((tm, LANES), lambda i: (i, 0)),
                   pl.BlockSpec((tm, LANES), lambda i: (i, 0))),
        compiler_params=pltpu.CompilerParams(
            dimension_semantics=("parallel",), vmem_limit_bytes=_vmem_limit(est)),
        name="moe_router",
    )(x, g, w_router_pad)


GATHER_UNROLL = 8


def _row_copy(src_hbm, row, dst, r, sem):
    return pltpu.make_async_copy(src_hbm.at[pl.ds(row, 1), :], dst.at[pl.ds(r, 1), :], sem)


def _dispatch_body(nt_ref, tok_ref, nxt_ref, pk_hbm, a_ref, buf, sem, *, tm):
    i = pl.program_id(0)
    nt = nt_ref[0]
    half = buf.shape[2]

    def request(ids_ref, slot):
        def issue(c, carry):
            for u in range(GATHER_UNROLL):
                r = c * GATHER_UNROLL + u
                _row_copy(pk_hbm, ids_ref[0, 0, r], buf.at[slot], r, sem.at[slot]).start(
                    priority=u % 2)
            return carry
        lax.fori_loop(0, tm // GATHER_UNROLL, issue, 0)

    @pl.when(i == 0)
    def _():
        request(tok_ref, 0)

    @pl.when(i + 1 < nt)
    def _():
        request(nxt_ref, (i + 1) % 2)

    @pl.when(i < nt)
    def _():
        slot = i % 2
        pltpu.make_async_copy(pk_hbm.at[pl.ds(0, tm), :], buf.at[slot], sem.at[slot]).wait()
        pk = buf[slot]
        lo = pltpu.bitcast(lax.shift_left(pk, jnp.uint32(16)), F32)
        hi = pltpu.bitcast(pk & jnp.uint32(0xFFFF0000), F32)
        a_ref[:, :half] = lo.astype(BF16)
        a_ref[:, half:] = hi.astype(BF16)


def _dispatch(n_tiles, tok_sorted, packed, *, tm, r_max):
    m, half = packed.shape
    t_max = r_max // tm
    assert tm % GATHER_UNROLL == 0
    tok_tiles = tok_sorted.reshape(t_max, 1, tm)
    return pl.pallas_call(
        functools.partial(_dispatch_body, tm=tm),
        out_shape=jax.ShapeDtypeStruct((r_max, 2 * half), BF16),
        grid_spec=pltpu.PrefetchScalarGridSpec(
            num_scalar_prefetch=1,
            grid=(t_max,),
            in_specs=[pl.BlockSpec((1, 1, tm), lambda i, nt: (i, 0, 0), memory_space=pltpu.SMEM),
                      pl.BlockSpec((1, 1, tm), lambda i, nt: (jnp.minimum(i + 1, t_max - 1), 0, 0),
                                   memory_space=pltpu.SMEM),
                      pl.BlockSpec(memory_space=pl.ANY)],
            out_specs=pl.BlockSpec((tm, 2 * half), lambda i, nt: (jnp.minimum(i, nt[0] - 1), 0)),
            scratch_shapes=[pltpu.VMEM((2, tm, half), U32), pltpu.SemaphoreType.DMA((2,))]),
        compiler_params=pltpu.CompilerParams(dimension_semantics=("arbitrary",)),
        name="moe_dispatch",
    )(n_tiles, tok_tiles, tok_tiles, packed)


def _expert_up_body(te_ref, nt_ref, a_ref, w1_ref, w3_ref, h_ref, w1bf, w3bf, *, kc, rem):
    f = pl.program_id(0)
    i = pl.program_id(1)
    last = pl.num_programs(0) - 1
    d, tf = w1bf.shape
    fresh = jnp.logical_or(i == 0, te_ref[i] != te_ref[jnp.maximum(i - 1, 0)])
    live = i < nt_ref[0]

    def run(cols, when):
        @pl.when(jnp.logical_and(when, jnp.logical_and(fresh, live)))
        def _():
            def cast(c, carry):
                r0 = pl.multiple_of(c * kc, kc)
                w1bf[pl.ds(r0, kc), 0:cols] = w1_ref[pl.ds(r0, kc), 0:cols].astype(BF16)
                w3bf[pl.ds(r0, kc), 0:cols] = w3_ref[pl.ds(r0, kc), 0:cols].astype(BF16)
                return carry
            lax.fori_loop(0, d // kc, cast, 0)

        @pl.when(jnp.logical_and(when, live))
        def _():
            a = a_ref[...]
            h_ref[:, 0:cols] = (jax.nn.silu(_dot(a, w1bf[:, 0:cols]))
                                * _dot(a, w3bf[:, 0:cols])).astype(h_ref.dtype)

    if rem:
        run(tf, f < last)
        run(rem, f == last)
    else:
        run(tf, True)


def _expert_up(tile_expert, n_tiles, a_sorted, w1, w3, *, tm, tf):
    r_max, d = a_sorted.shape
    n_exp, _, fe = w1.shape
    t_max = r_max // tm
    kc = _pick(d, (512, 256, 128))
    row = lambda f, i, te, nt: jnp.minimum(i, nt[0] - 1)
    est = (2 * _nbytes((tm, d), BF16) + 2 * 2 * _nbytes((d, tf), F32) + 2 * _nbytes((d, tf), BF16)
           + 2 * _nbytes((tm, tf), BF16) + 6 * _nbytes((tm, tf), F32))
    return pl.pallas_call(
        functools.partial(_expert_up_body, kc=kc, rem=fe % tf),
        out_shape=jax.ShapeDtypeStruct((r_max, fe), BF16),
        grid_spec=pltpu.PrefetchScalarGridSpec(
            num_scalar_prefetch=2,
            grid=(pl.cdiv(fe, tf), t_max),
            in_specs=[pl.BlockSpec((tm, d), lambda f, i, te, nt: (row(f, i, te, nt), 0)),
                      pl.BlockSpec((None, d, tf), lambda f, i, te, nt: (te[i], 0, f)),
                      pl.BlockSpec((None, d, tf), lambda f, i, te, nt: (te[i], 0, f))],
            out_specs=pl.BlockSpec((tm, tf), lambda f, i, te, nt: (row(f, i, te, nt), f)),
            scratch_shapes=[pltpu.VMEM((d, tf), BF16), pltpu.VMEM((d, tf), BF16)]),
        compiler_params=pltpu.CompilerParams(
            dimension_semantics=("arbitrary", "arbitrary"), vmem_limit_bytes=_vmem_limit(est)),
        name="moe_expert_up",
    )(tile_expert, n_tiles, a_sorted, w1, w3)


def _expert_down_body(te_ref, nt_ref, h_ref, w2_ref, y_ref, w2bf, *, kc):
    i = pl.program_id(1)
    fe = h_ref.shape[1]
    fresh = jnp.logical_or(i == 0, te_ref[i] != te_ref[jnp.maximum(i - 1, 0)])

    @pl.when(jnp.logical_and(fresh, i < nt_ref[0]))
    def _():
        def cast(c, carry):
            r0 = pl.multiple_of(c * kc, kc)
            w2bf[pl.ds(r0, kc), :] = w2_ref[pl.ds(r0, kc), :].astype(BF16)
            return carry
        lax.fori_loop(0, fe // kc, cast, 0)

    @pl.when(i < nt_ref[0])
    def _():
        y_ref[...] = _dot(h_ref[...], w2bf[...])


def _expert_down(tile_expert, n_tiles, h_sorted, w2, *, tm, tn):
    r_max, fe = h_sorted.shape
    d = w2.shape[2]
    t_max = r_max // tm
    kc = _pick(fe, (512, 256, 128))
    row = lambda n, i, te, nt: jnp.minimum(i, nt[0] - 1)
    est = (2 * _nbytes((tm, fe), BF16) + 2 * _nbytes((fe, tn), F32) + _nbytes((fe, tn), BF16)
           + 4 * _nbytes((tm, tn), F32))
    return pl.pallas_call(
        functools.partial(_expert_down_body, kc=kc),
        out_shape=jax.ShapeDtypeStruct((r_max, d), F32),
        grid_spec=pltpu.PrefetchScalarGridSpec(
            num_scalar_prefetch=2,
            grid=(d // tn, t_max),
            in_specs=[pl.BlockSpec((tm, fe), lambda n, i, te, nt: (row(n, i, te, nt), 0)),
                      pl.BlockSpec((None, fe, tn), lambda n, i, te, nt: (te[i], 0, n))],
            out_specs=pl.BlockSpec((tm, tn), lambda n, i, te, nt: (row(n, i, te, nt), n)),
            scratch_shapes=[pltpu.VMEM((fe, tn), BF16)]),
        compiler_params=pltpu.CompilerParams(
            dimension_semantics=("arbitrary", "arbitrary"), vmem_limit_bytes=_vmem_limit(est)),
        name="moe_expert_down",
    )(tile_expert, n_tiles, h_sorted, w2)


def _combine_body(slot_ref, nxt_ref, x_ref, gate_ref, y_hbm, *rest, tc, n_first):
    o_refs, (buf, sem) = rest[:-2], rest[-2:]
    i = pl.program_id(0)
    n = pl.num_programs(0)
    unroll = GATHER_UNROLL // TOP_K

    def request(ids_ref, slot):
        def issue(c, carry):
            for u in range(unroll):
                r = c * unroll + u
                for k in range(TOP_K):
                    _row_copy(y_hbm, ids_ref[0, 0, TOP_K * r + k], buf.at[slot, k], r,
                              sem.at[slot]).start(priority=k % 2)
            return carry
        lax.fori_loop(0, tc // unroll, issue, 0)

    @pl.when(i == 0)
    def _():
        request(slot_ref, 0)

    @pl.when(i + 1 < n)
    def _():
        request(nxt_ref, (i + 1) % 2)

    slot = i % 2
    for k in range(TOP_K):
        pltpu.make_async_copy(y_hbm.at[pl.ds(0, tc), :], buf.at[slot, k], sem.at[slot]).wait()
    g = gate_ref[...]
    val = x_ref[...] + (g[:, 0:1] * buf[slot, 0] + g[:, 1:2] * buf[slot, 1])
    if len(o_refs) == 1:
        o_refs[0][...] = val
    else:
        @pl.when(i < n_first)
        def _():
            o_refs[0][...] = val

        @pl.when(i >= n_first)
        def _():
            o_refs[1][...] = val


def _combine(slots, x, gates, y_sorted, *, tc, split=None):
    m, d = x.shape
    n = m // tc
    assert tc % (GATHER_UNROLL // TOP_K) == 0
    est = 4 * _nbytes((tc, d), F32) + 2 * TOP_K * _nbytes((tc, d), F32) + 6 * _nbytes((tc, d), F32)
    slot_tiles = slots.reshape(n, 1, TOP_K * tc)
    if split is None or split % tc or (m - split) % tc:
        n_first = n
        out_shape = jax.ShapeDtypeStruct((m, d), F32)
        out_specs = pl.BlockSpec((tc, d), lambda i: (i, 0))
    else:
        n_first = split // tc
        out_shape = (jax.ShapeDtypeStruct((split, d), F32), jax.ShapeDtypeStruct((m - split, d), F32))
        out_specs = (pl.BlockSpec((tc, d), lambda i: (jnp.minimum(i, n_first - 1), 0)),
                     pl.BlockSpec((tc, d), lambda i: (jnp.maximum(i - n_first, 0), 0)))
    return pl.pallas_call(
        functools.partial(_combine_body, tc=tc, n_first=n_first),
        out_shape=out_shape,
        grid=(n,),
        in_specs=[pl.BlockSpec((1, 1, TOP_K * tc), lambda i: (i, 0, 0), memory_space=pltpu.SMEM),
                  pl.BlockSpec((1, 1, TOP_K * tc), lambda i: (jnp.minimum(i + 1, n - 1), 0, 0),
                               memory_space=pltpu.SMEM),
                  pl.BlockSpec((tc, d), lambda i: (i, 0)),
                  pl.BlockSpec((tc, LANES), lambda i: (i, 0)),
                  pl.BlockSpec(memory_space=pl.ANY)],
        out_specs=out_specs,
        scratch_shapes=[pltpu.VMEM((2, TOP_K, tc, d), F32), pltpu.SemaphoreType.DMA((2,))],
        compiler_params=pltpu.CompilerParams(
            dimension_semantics=("arbitrary",), vmem_limit_bytes=_vmem_limit(est)),
        name="moe_combine",
    )(slot_tiles, slot_tiles, x, gates, y_sorted)


def _moe(x, g_ffn, w_router, w_e1, w_e3, w_e2, *, tm_tok, split=None):
    m, d = x.shape
    n_exp, _, fe = w_e1.shape
    tm = _pick(m * TOP_K, (512, 256, 128, 64, 32, 16))
    wr = jnp.zeros((d, LANES), F32).at[:, :n_exp].set(w_router)
    packed, idx128, gate128 = _router(x, g_ffn, wr, n_exp=n_exp, tm=tm_tok)

    idx = idx128[:, :TOP_K]
    mask = jnp.sum(idx[:, :, None] == jnp.arange(n_exp, dtype=I32)[None, None, :], axis=1).astype(I32)
    counts = jnp.sum(mask, axis=0)
    padded = ((counts + tm - 1) // tm) * tm
    ends = jnp.cumsum(padded)
    starts = ends - padded
    pos = jnp.cumsum(mask, axis=0) - mask
    slot = starts[idx] + jnp.take_along_axis(pos, idx, axis=1)
    t_max = (m * TOP_K) // tm + n_exp
    r_max = t_max * tm
    tok_sorted = jnp.zeros((r_max,), I32).at[slot.reshape(-1)].set(
        jnp.repeat(jnp.arange(m, dtype=I32), TOP_K))
    n_tiles = (ends[-1] // tm).astype(I32).reshape(1)
    tile_start = jnp.minimum(jnp.arange(t_max, dtype=I32), n_tiles[0] - 1) * tm
    tile_expert = jnp.minimum(jnp.sum(ends[None, :] <= tile_start[:, None], axis=1),
                              n_exp - 1).astype(I32)

    a_sorted = _dispatch(n_tiles, tok_sorted, packed, tm=tm, r_max=r_max)
    th = _pick(fe, (256, 128))
    tf = 2 * th if fe >= 2 * th else th
    h_sorted = _expert_up(tile_expert, n_tiles, a_sorted, w_e1, w_e3, tm=tm, tf=tf)
    y_sorted = _expert_down(tile_expert, n_tiles, h_sorted, w_e2, tm=tm,
                            tn=_pick(d, (1024, 512, 256, 128)))
    tc = _pick(m, (256, 128, 64, 32, 16, 8))
    return _combine(slot, x, gate128, y_sorted, tc=tc, split=split)


def _block_diag(blocks):
    g, r, c = blocks.shape
    eye = jnp.eye(g, dtype=blocks.dtype)
    return (blocks[:, :, None, :] * eye[:, None, :, None]).reshape(g * r, g * c)


def _rel_table(rel_bias_l, q_pos, k_pos, max_rel):
    rel = np.clip(q_pos[:, None] - k_pos[None, :], -(CHUNK - 1), max_rel) + (CHUNK - 1)
    return rel_bias_l[:, rel].astype(F32)


def _band_table(rel_bias_l, tq, max_rel):
    n_heads = rel_bias_l.shape[0]
    past = BAND_PAST * CHUNK
    win = past + tq
    ring = -(-(tq + win - 1) // LANES) * LANES
    diff = np.arange(ring)
    diff = np.where(diff < tq, diff, diff - ring)
    idx = np.clip(diff + past, -(CHUNK - 1), max_rel) + (CHUNK - 1)
    by_diff = jnp.pad(rel_bias_l.astype(F32)[:, idx], ((0, 16 - n_heads), (0, 0)))

    def body(u_ref, o_ref):
        hp = pl.program_id(0)
        j = lax.broadcasted_iota(I32, (win, tq), 0)
        i = lax.broadcasted_iota(I32, (win, tq), 1)
        gap = (i + past) // CHUNK - j // CHUNK
        valid = (gap >= 0) & (gap <= BAND_PAST)
        for e in range(2):
            rows = jnp.broadcast_to(u_ref[pl.ds(2 * hp + e, 1), :], (win, ring))
            shifted = pltpu.roll(rows, 0, 1, stride=1, stride_axis=0)
            o_ref[:, e * tq:(e + 1) * tq] = jnp.where(valid, shifted[:, 0:tq] * LOG2E, NEG_INF)

    return pl.pallas_call(
        body,
        out_shape=jax.ShapeDtypeStruct((n_heads // 2, win, 2 * tq), F32),
        grid=(n_heads // 2,),
        in_specs=[pl.BlockSpec((16, ring), lambda hp: (0, 0))],
        out_specs=pl.BlockSpec((None, win, 2 * tq), lambda hp: (hp, 0, 0)),
        compiler_params=pltpu.CompilerParams(dimension_semantics=("parallel",)),
        name="band_table",
    )(by_diff)


def _tile_gain(g):
    return jnp.tile(g.astype(F32), LANES // g.shape[0]).reshape(1, LANES)


def kernel(x_prompt, x_sample, mem_prompt, cache_fox_k, cache_fox_v, cache_fox_logf, cache_band_k, cache_band_v, state_ssm_re, state_ssm_im, cache_mem_k, cache_mem_v, g_mix, w_in, b_f, g_qa, g_ka, g_qb, g_kb, rel_bias, lam_re, lam_im, log_dt, ssm_b_re, ssm_b_im, ssm_c_re, ssm_c_im, ssm_d, w_glu, b_glu, g_mix_out, w_out, g_cross, g_mem, w_cq, w_ck, w_cv, g_cq, g_ck, w_co, g_ffn, w_ff1, w_ff3, w_ff2, w_router, w_e1, w_e3, w_e2):
    batch, seq, d = x_prompt.shape
    dbatch, dseq, _ = x_sample.shape
    depth = g_mix.shape[0]
    past_len, h_a, dh = cache_fox_k.shape[2:]
    band_rows, h_b = cache_band_k.shape[2:4]
    g_c, p_state = lam_re.shape[1:]
    w_a, w_b, w_c = h_a * dh, h_b * dh, g_c * SSM_GROUP
    n_mem, h_m, dh_m = cache_mem_k.shape[2:]
    w_m = h_m * dh_m
    max_rel = rel_bias.shape[2] - CHUNK
    assert dh == 64 and h_a % 2 == 0 and h_b % 2 == 0 and dh_m == LANES
    assert w_a == w_b and w_c <= w_a and h_a <= 16
    mp, ms = batch * seq, dbatch * dseq
    m = mp + ms
    pa, pb = h_a // 2, h_b // 2
    n_state = g_c * p_state
    n_blk = 2 if (w_c % 512 == 0) else 1
    nband = min(BAND_PAST * CHUNK, seq)
    tm = _pick(m, (1056, 1024, 768, 512, 256, 128, 64, 32, 16))
    tq_band = _pick(seq, (256, 128, 64))

    x = jnp.concatenate([x_prompt.reshape(mp, d), x_sample.reshape(ms, d)], axis=0)
    fox_kt, fox_vt, band_kt, band_vt = (jnp.transpose(c, (0, 1, 3, 4, 2)) for c in
                                        (cache_fox_k, cache_fox_v, cache_band_k, cache_band_v))

    outs = {k: [] for k in ("p_fl", "p_sr", "p_si", "p_mk", "p_mv",
                            "s_fk", "s_fv", "s_fl", "s_bk", "s_bv", "s_sr", "s_si")}
    cache_a = cache_b = None
    for l in range(depth):
        sizes = (w_a, w_a, w_a, h_a, w_b, w_b, w_b)
        cuts = [sum(sizes[:i]) for i in range(len(sizes) + 1)]
        w_cat = _regroup_columns(
            w_in, l, ((0, cuts[3]), (cuts[4], cuts[7]), (cuts[7], w_in.shape[2]), (cuts[3], cuts[4])),
            7 * w_a)
        proj = _mm([x], [w_cat], gain=g_mix[l].reshape(1, d), tm=tm, tn=w_a, name="proj_in")
        col_uc = 6 * w_a

        assert (col_uc + w_c) % LANES == 0
        lf_p, f_aug = _logf_rows(proj, jnp.pad(b_f[l].astype(F32), (0, LANES - h_a)).reshape(1, LANES),
                                 batch=batch, seq=seq, n_heads=h_a, n_pairs=pa,
                                 col_block=(col_uc + w_c) // LANES)

        t_all = -(-(past_len + dseq) // LANES) * LANES
        fa_s = proj[mp:, col_uc + w_c:col_uc + w_c + h_a].reshape(dbatch, dseq, h_a)
        x_s = jnp.concatenate([jnp.transpose(cache_fox_logf[l], (0, 2, 1)),
                               jnp.transpose(fa_s, (0, 2, 1)),
                               jnp.zeros((dbatch, h_a, t_all - past_len - dseq), F32)], axis=2)
        bias_s = jnp.broadcast_to(jnp.tile(b_f[l], dbatch)[:, None], (dbatch * h_a, LANES))
        lf_s, f_s = _logf_cumsum(x_s.reshape(dbatch * h_a, t_all), bias_s,
                                 raw_from=past_len, valid_to=past_len + dseq)
        lf_s = lf_s.reshape(dbatch, h_a, t_all)[:, :, past_len:past_len + dseq]
        f_s = f_s.reshape(dbatch, h_a, t_all)
        fs_row = jnp.pad(f_s, ((0, 0), (0, 16 - h_a), (0, 0)))
        fs_col = jnp.pad(jnp.transpose(f_s[:, :, past_len:past_len + dseq], (0, 2, 1)),
                         ((0, 0), (0, 0), (0, LANES - h_a))).reshape(ms, LANES)

        gqa, gka = _tile_gain(g_qa[l]), _tile_gain(g_ka[l])
        oa, *cache_a = _fox_prompt(proj, f_aug, gqa, gka, batch=batch, seq=seq, n_pairs=pa,
                                   col_q=0, col_k=pa, col_v=2 * pa, m_rows=m, layer=l, depth=depth,
                                   prev=cache_a)
        oa, kn_as = _sample_attn(proj, fox_kt, fox_vt, l, fs_col, fs_row,
                                 gqa, gka, oa, mode="fox", batch=dbatch, s_new=dseq, n_pairs=pa,
                                 col_q=0, row0=mp)

        gqb, gkb = _tile_gain(g_qb[l]), _tile_gain(g_kb[l])
        tab = _band_table(rel_bias[l], tq_band, max_rel)
        ob, *cache_b = _band_prompt(proj, tab, gqb, gkb, batch=batch, seq=seq, n_pairs=pb,
                                    col_q=3 * pa, col_k=3 * pa + pb, col_v=3 * pa + 2 * pb,
                                    tq=tq_band, m_rows=m, layer=l, depth=depth, prev=cache_b)
        tab_s = _rel_table(rel_bias[l], band_rows + np.arange(dseq), np.arange(band_rows + dseq),
                           max_rel)
        ob, kn_bs = _sample_attn(proj, band_kt, band_vt, l,
                                 tab_s[:, :, :band_rows], tab_s[:, :, band_rows:], gqb, gkb, ob,
                                 mode="band", batch=dbatch, s_new=dseq, n_pairs=pb, col_q=3, row0=mp)

        rep = lambda a: jnp.repeat(a.astype(F32), SSM_GROUP, axis=0)
        a_re, a_im, bb_re, bb_im = _s5_prep(
            rep(lam_re[l]), rep(lam_im[l]),
            jnp.broadcast_to(rep(log_dt[l])[:, None], (g_c * SSM_GROUP, p_state)),
            jnp.transpose(ssm_b_re[l], (0, 2, 1)).reshape(g_c * SSM_GROUP, p_state),
            jnp.transpose(ssm_b_im[l], (0, 2, 1)).reshape(g_c * SSM_GROUP, p_state))
        a_re = a_re.reshape(g_c, SSM_GROUP, p_state)[:, 0, :].reshape(1, n_state)
        a_im = a_im.reshape(g_c, SSM_GROUP, p_state)[:, 0, :].reshape(1, n_state)
        b_re_d = _block_diag(bb_re.reshape(g_c, SSM_GROUP, p_state))
        b_im_d = _block_diag(bb_im.reshape(g_c, SSM_GROUP, p_state))
        c_re_d = _block_diag(jnp.transpose(ssm_c_re[l], (0, 2, 1))).astype(BF16)
        c_im_d = _block_diag(jnp.transpose(ssm_c_im[l], (0, 2, 1))).astype(BF16)
        d_row = ssm_d[l].reshape(1, w_c)
        s5_tail = (c_re_d, c_im_d, d_row, w_glu[l], b_glu[l].reshape(1, w_c))
        assert col_uc % w_c == 0 and w_c % LANES == 0
        u_s = jnp.transpose(proj[mp:, col_uc:col_uc + w_c].reshape(dbatch, dseq, w_c),
                            (1, 0, 2)).reshape(ms, w_c)
        zeros_p = jnp.zeros((batch, n_state), F32)
        oc_p, sr_p, si_p = _s5(proj, zeros_p, zeros_p, a_re, a_im, b_re_d.astype(BF16),
                               b_im_d.astype(BF16), *s5_tail, nb=batch, seq=seq, n_blk=n_blk,
                               passes=1, col_block=col_uc // w_c)
        oc_s, sr_s, si_s = _s5(u_s, state_ssm_re[l].reshape(dbatch, n_state),
                               state_ssm_im[l].reshape(dbatch, n_state), a_re, a_im, b_re_d, b_im_d,
                               *s5_tail, nb=dbatch, seq=dseq, n_blk=n_blk, passes=3)
        oc = jnp.concatenate(
            [oc_p.reshape(mp, w_c),
             jnp.transpose(oc_s.reshape(dseq, dbatch, w_c), (1, 0, 2)).reshape(ms, w_c)], axis=0)

        x = _mm([oa, ob, oc], [w_out[l].astype(BF16)], gain=g_mix_out[l].reshape(1, -1), residual=x,
                tm=tm, tn=_pick(d, (1024, 512, 256, 128)), name="merge_out")

        outs["p_fl"].append(lf_p[:, :h_a].reshape(batch, seq, h_a))
        outs["p_sr"].append(sr_p.reshape(batch, g_c, p_state))
        outs["p_si"].append(si_p.reshape(batch, g_c, p_state))
        outs["s_fk"].append(kn_as.reshape(dbatch, dseq, h_a, dh))
        outs["s_fv"].append(proj[mp:, 2 * w_a:3 * w_a].reshape(dbatch, dseq, h_a, dh))
        outs["s_fl"].append(jnp.transpose(lf_s, (0, 2, 1)))
        outs["s_bk"].append(kn_bs.reshape(dbatch, dseq, h_b, dh))
        outs["s_bv"].append(proj[mp:, 5 * w_a:6 * w_a].reshape(dbatch, dseq, h_b, dh))
        outs["s_sr"].append(sr_s.reshape(dbatch, g_c, p_state))
        outs["s_si"].append(si_s.reshape(dbatch, g_c, p_state))

        mem2 = mem_prompt.reshape(batch * n_mem, d)
        tmm = _pick(batch * n_mem, (1024, 512, 256, 128))
        gck = jnp.tile(g_ck[l].astype(F32), h_m).reshape(1, w_m)
        gcq = jnp.tile(g_cq[l].astype(F32), h_m).reshape(1, w_m)
        mk = _mm([mem2], [w_ck[l]], gain=g_mem[l].reshape(1, d), group_gain=gck,
                 epilogue="group_norm", tm=tmm, tn=w_m, name="mem_k")
        mv = _mm([mem2], [w_cv[l]], gain=g_mem[l].reshape(1, d), tm=tmm, tn=w_m, name="mem_v")
        outs["p_mk"].append(mk.reshape(batch, n_mem, h_m, dh_m))
        outs["p_mv"].append(mv.reshape(batch, n_mem, h_m, dh_m))
        q_c = _mm([x], [w_cq[l].astype(BF16)], gain=g_cross[l].reshape(1, d), group_gain=gcq,
                  epilogue="group_norm", tm=tm, tn=w_m, name="cross_q")
        o_c = _cross_attn(q_c, mk, mv, None, batch=batch, q_len=seq, n_mem=n_mem, n_heads=h_m,
                          dh=dh_m, row0=0, m_rows=m)
        o_c = _cross_attn(q_c, cache_mem_k[l].reshape(dbatch * n_mem, w_m),
                          cache_mem_v[l].reshape(dbatch * n_mem, w_m), o_c, batch=dbatch,
                          q_len=dseq, n_mem=n_mem, n_heads=h_m, dh=dh_m, row0=mp, m_rows=m)
        x = _mm([o_c], [w_co[l].astype(BF16)], residual=x, tm=tm,
                tn=_pick(d, (1024, 512, 256, 128)), name="cross_out")

        i = l // 2
        if l % 2 == 0:
            hmid = _mm([x], [w_ff1[i].astype(BF16), w_ff3[i].astype(BF16)],
                       gain=g_ffn[l].reshape(1, d), epilogue="swiglu", out_dtype=BF16, tm=tm,
                       tn=_pick(w_ff1.shape[2], (512, 256, 128)), name="ffn_up")
            x = _mm([hmid], [w_ff2[i].astype(BF16)], residual=x, tm=tm,
                    tn=_pick(d, (512, 256, 128)), name="ffn_down")
        else:
            x = _moe(x, g_ffn[l].reshape(1, d), w_router[i], w_e1[i], w_e3[i], w_e2[i], tm_tok=tm,
                     split=mp if l == depth - 1 else None)

    st = lambda k: jnp.stack(outs[k])
    xp, xs = x if isinstance(x, tuple) else (x[:mp], x[mp:])
    per_head = lambda t, h: jnp.transpose(t.reshape(depth, batch, h, dh, -1), (0, 1, 4, 2, 3))
    return (xp.reshape(batch, seq, d), xs.reshape(dbatch, dseq, d),
            per_head(cache_a[0], h_a), per_head(cache_a[1], h_a), st("p_fl"),
            per_head(cache_b[0][..., seq - nband:], h_b), per_head(cache_b[1][..., seq - nband:], h_b),
            st("p_sr"), st("p_si"),
            st("p_mk"), st("p_mv"), st("s_fk"), st("s_fv"), st("s_fl"), st("s_bk"), st("s_bv"),
            st("s_sr"), st("s_si"))
```

```python
import functools
import math

import jax
import jax.numpy as jnp
import numpy as np
from jax import lax
from jax.experimental import pallas as pl
from jax.experimental.pallas import tpu as pltpu

F32 = jnp.float32
BF16 = jnp.bfloat16
I32 = jnp.int32
U32 = jnp.uint32

EPS = 1e-6
NEG_INF = -1e30
LOG2E = math.log2(math.e)
CHUNK = 64
BAND_PAST = 8
SSM_GROUP = 16
TOP_K = 2

LANES = 128
SUBLANES = 8
BF16_ROWS = 16
VMEM_CAP = 60 * 1024 * 1024


def _vmem_limit(nbytes):
    return int(min(VMEM_CAP, max(16 * 1024 * 1024, nbytes * 5 // 4 + (4 << 20))))


def _pick(n, candidates):
    for c in candidates:
        if c <= n and n % c == 0:
            return c
    raise ValueError(f"no tile for {n} in {candidates}")


def _nbytes(shape, dtype):
    return math.prod(shape) * jnp.dtype(dtype).itemsize


def _split3(x):
    hi = x.astype(BF16)
    r1 = x - hi.astype(F32)
    mid = r1.astype(BF16)
    lo = (r1 - mid.astype(F32)).astype(BF16)
    return hi, mid, lo


def _dot(a, b):
    return jnp.dot(a, b, preferred_element_type=F32)


def _dot_nt(a, b):
    return lax.dot_general(a, b, (((1,), (1,)), ((), ())), preferred_element_type=F32)


def _dot_hp(a, b, passes=3):
    ah = a.astype(BF16)
    bh = b.astype(BF16)
    if passes == 1:
        return _dot(ah, bh)
    al = (a - ah.astype(F32)).astype(BF16)
    bl = (b - bh.astype(F32)).astype(BF16)
    return _dot(ah, bh) + (_dot(ah, bl) + _dot(al, bh))


def _pair_ones():
    r = lax.broadcasted_iota(I32, (LANES, LANES), 0) // 64
    c = lax.broadcasted_iota(I32, (LANES, LANES), 1) // 64
    return (r == c).astype(BF16)


def _pair_norm(x, g, ones):
    sq = x * x
    hi = sq.astype(BF16)
    lo = (sq - hi.astype(F32)).astype(BF16)
    ss = _dot(hi, ones) + _dot(lo, ones)
    return x * lax.rsqrt(ss * (1.0 / 64.0) + EPS) * g


def _log_sigmoid(x):
    return jnp.minimum(x, 0.0) - jnp.log(1.0 + jnp.exp(-jnp.abs(x)))


def _gelu_tanh(x):
    c = math.sqrt(2.0 / math.pi)
    return 0.5 * x * (1.0 + jnp.tanh(c * (x + 0.044715 * (x * x * x))))


def _mm_body(*refs, widths, norm, n_w, epilogue, has_res, has_gg, stage_a, tm, rc, kc, k_total):
    it = iter(refs)
    a_refs = [next(it) for _ in widths]
    g_ref = next(it) if norm else None
    w_refs = [next(it) for _ in range(n_w)]
    gg_ref = next(it) if has_gg else None
    res_ref = next(it) if has_res else None
    o_ref = next(it)
    abf = next(it) if stage_a else a_refs[0]
    cast_w = w_refs[0].dtype != BF16
    wbfs = [next(it) for _ in range(n_w)] if cast_w else w_refs
    j = pl.program_id(1)

    if stage_a:
        @pl.when(j == 0)
        def _():
            off = 0
            for a_ref, wd in zip(a_refs, widths):
                def chunk(c, carry, a_ref=a_ref, off=off, wd=wd):
                    r0 = pl.multiple_of(c * rc, rc)
                    x = a_ref[pl.ds(r0, rc), :].astype(F32)
                    if norm:
                        ms = jnp.mean(x * x, axis=-1, keepdims=True)
                        x = x * lax.rsqrt(ms + EPS) * g_ref[:, off:off + wd]
                    abf[pl.ds(r0, rc), off:off + wd] = x.astype(BF16)
                    return carry
                lax.fori_loop(0, tm // rc, chunk, 0)
                off += wd

    if cast_w:
        for w_ref, wbf in zip(w_refs, wbfs):
            def cast(c, carry, w_ref=w_ref, wbf=wbf):
                r0 = pl.multiple_of(c * kc, kc)
                wbf[pl.ds(r0, kc), :] = w_ref[pl.ds(r0, kc), :].astype(BF16)
                return carry
            lax.fori_loop(0, k_total // kc, cast, 0)

    for c in range(tm // rc):
        r0 = c * rc
        a = abf[pl.ds(r0, rc), :]
        ys = [_dot(a, wbf[...]) for wbf in wbfs]
        if epilogue == "swiglu":
            y = jax.nn.silu(ys[0]) * ys[1]
        elif epilogue == "group_norm":
            parts = []
            for s in range(ys[0].shape[1] // LANES):
                ysl = ys[0][:, s * LANES:(s + 1) * LANES]
                ms = jnp.mean(ysl * ysl, axis=-1, keepdims=True)
                parts.append(ysl * lax.rsqrt(ms + EPS) * gg_ref[:, s * LANES:(s + 1) * LANES])
            y = jnp.concatenate(parts, axis=1)
        else:
            y = ys[0]
        if has_res:
            y = y + res_ref[pl.ds(r0, rc), :]
        o_ref[pl.ds(r0, rc), :] = y.astype(o_ref.dtype)


def _mm(a_parts, w_list, *, gain=None, group_gain=None, residual=None, epilogue="none",
        out_dtype=F32, tm, tn, name):
    m = a_parts[0].shape[0]
    widths = tuple(a.shape[1] for a in a_parts)
    k_total = sum(widths)
    n = w_list[0].shape[1]
    assert m % tm == 0 and n % tn == 0, (m, tm, n, tn)
    norm = gain is not None
    stage_a = norm or len(a_parts) > 1 or a_parts[0].dtype != BF16
    rc = _pick(tm, (512, 384, 352, 256, 176, 128, 64, 32, 16))
    kc = _pick(k_total, (512, 256, 128))
    grid = (m // tm, n // tn)
    in_specs = [pl.BlockSpec((tm, wd), lambda i, j: (i, 0)) for wd in widths]
    args = list(a_parts)
    est = sum(2 * _nbytes((tm, wd), a.dtype) for wd, a in zip(widths, a_parts))
    if norm:
        in_specs.append(pl.BlockSpec((1, k_total), lambda i, j: (0, 0)))
        args.append(gain)
    cast_w = w_list[0].dtype != BF16
    for w in w_list:
        in_specs.append(pl.BlockSpec((k_total, tn), lambda i, j: (0, j)))
        args.append(w)
        est += 2 * _nbytes((k_total, tn), w.dtype) + cast_w * _nbytes((k_total, tn), BF16)
    if group_gain is not None:
        in_specs.append(pl.BlockSpec((1, tn), lambda i, j: (0, j)))
        args.append(group_gain)
    if residual is not None:
        in_specs.append(pl.BlockSpec((tm, tn), lambda i, j: (i, j)))
        args.append(residual)
        est += 2 * _nbytes((tm, tn), F32)
    est += 2 * _nbytes((tm, tn), out_dtype) + stage_a * _nbytes((tm, k_total), BF16)
    est += 4 * _nbytes((rc, tn), F32) * len(w_list)
    body = functools.partial(
        _mm_body, widths=widths, norm=norm, n_w=len(w_list), epilogue=epilogue,
        has_res=residual is not None, has_gg=group_gain is not None, stage_a=stage_a, tm=tm,
        rc=rc, kc=kc, k_total=k_total)
    return pl.pallas_call(
        body,
        out_shape=jax.ShapeDtypeStruct((m, n), out_dtype),
        grid=grid,
        in_specs=in_specs,
        out_specs=pl.BlockSpec((tm, tn), lambda i, j: (i, j)),
        scratch_shapes=[pltpu.VMEM((tm, k_total), BF16)] * stage_a
        + [pltpu.VMEM((k_total, tn), BF16) for _ in w_list] * cast_w,
        compiler_params=pltpu.CompilerParams(
            dimension_semantics=("parallel", "arbitrary"),
            vmem_limit_bytes=_vmem_limit(est)),
        name=name,
    )(*args)


def _regroup_body(w_ref, o_ref, *, cuts, width):
    off = 0
    for lo, hi in cuts:
        o_ref[:, off:off + hi - lo] = w_ref[:, lo:hi].astype(o_ref.dtype)
        off += hi - lo
    if off < width:
        o_ref[:, off:width] = jnp.zeros((o_ref.shape[0], width - off), o_ref.dtype)


def _regroup_columns(w, layer, cuts, width):
    _, k, n = w.shape
    tr = _pick(k, (256, 128, 64, 32, 16))
    return pl.pallas_call(
        functools.partial(_regroup_body, cuts=cuts, width=width),
        out_shape=jax.ShapeDtypeStruct((k, width), BF16),
        grid=(k // tr,),
        in_specs=[pl.BlockSpec((None, tr, n), lambda i: (layer, i, 0))],
        out_specs=pl.BlockSpec((tr, width), lambda i: (i, 0)),
        compiler_params=pltpu.CompilerParams(
            dimension_semantics=("parallel",),
            vmem_limit_bytes=_vmem_limit(2 * _nbytes((tr, n), F32) + 2 * _nbytes((tr, width), BF16)
                                         + 4 * _nbytes((tr, width), F32))),
        name="regroup_w_in",
    )(w)


def _cumsum_body(x_ref, b_ref, lf_ref, f_ref, carry, *, raw_from, valid_to, tt):
    j = pl.program_id(0)

    @pl.when(j == 0)
    def _():
        carry[...] = jnp.zeros_like(carry)

    x = x_ref[...]
    lane = j * tt + lax.broadcasted_iota(I32, x.shape, 1)
    lf = jnp.where(lane >= raw_from, _log_sigmoid(x + b_ref[:, 0:1]), x)
    lf = jnp.where(lane < valid_to, lf, 0.0)
    lf_ref[...] = lf
    tri = (lax.broadcasted_iota(I32, (tt, tt), 0)
           <= lax.broadcasted_iota(I32, (tt, tt), 1)).astype(BF16)
    hi, mid, lo = _split3(lf)
    y = _dot(hi, tri) + _dot(mid, tri) + _dot(lo, tri) + carry[:, 0:1]
    f_ref[...] = y
    carry[...] = jnp.broadcast_to(y[:, tt - 1:tt], carry.shape)


def _logf_cumsum(x, bias, *, raw_from, valid_to):
    r, t = x.shape
    tt = _pick(t, (256, 128))
    return pl.pallas_call(
        functools.partial(_cumsum_body, raw_from=raw_from, valid_to=valid_to, tt=tt),
        out_shape=(jax.ShapeDtypeStruct((r, t), F32), jax.ShapeDtypeStruct((r, t), F32)),
        grid=(t // tt,),
        in_specs=[pl.BlockSpec((r, tt), lambda j: (0, j)),
                  pl.BlockSpec((r, LANES), lambda j: (0, 0))],
        out_specs=(pl.BlockSpec((r, tt), lambda j: (0, j)),
                   pl.BlockSpec((r, tt), lambda j: (0, j))),
        scratch_shapes=[pltpu.VMEM((r, LANES), F32)],
        compiler_params=pltpu.CompilerParams(dimension_semantics=("arbitrary",)),
        name="logf_cumsum",
    )(x, bias)


def _logf_rows_body(fa_ref, b_ref, lf_ref, faug_ref, carry, *, tt, n_heads):
    j = pl.program_id(1)

    @pl.when(j == 0)
    def _():
        carry[...] = jnp.zeros_like(carry)

    lane = lax.broadcasted_iota(I32, (tt, LANES), 1)
    lf = jnp.where(lane < n_heads, _log_sigmoid(fa_ref[...] + b_ref[...]), 0.0)
    lf_ref[...] = lf
    tri = (lax.broadcasted_iota(I32, (tt, tt), 0)
           >= lax.broadcasted_iota(I32, (tt, tt), 1)).astype(BF16)
    hi, mid, lo = _split3(lf)
    f = _dot(tri, hi) + _dot(tri, mid) + _dot(tri, lo) + carry[0:1, :]
    carry[...] = jnp.broadcast_to(f[tt - 1:tt, :], carry.shape)
    width = faug_ref.shape[1]
    head = lax.broadcasted_iota(I32, (LANES, width), 0)
    col = lax.broadcasted_iota(I32, (LANES, width), 1)
    within = col % LANES
    sel = ((within < 12) & (head == 2 * (col // LANES) + (within % 6) // 3)).astype(BF16)
    hi, mid, lo = _split3(f * LOG2E)
    faug_ref[...] = _dot(hi, sel) + _dot(mid, sel) + _dot(lo, sel)


def _logf_rows(proj, bias_row, *, batch, seq, n_heads, n_pairs, col_block):
    tt = _pick(seq, (256, 128))
    nt = seq // tt
    return pl.pallas_call(
        functools.partial(_logf_rows_body, tt=tt, n_heads=n_heads),
        out_shape=(jax.ShapeDtypeStruct((batch * seq, LANES), F32),
                   jax.ShapeDtypeStruct((batch * seq, n_pairs * LANES), F32)),
        grid=(batch, nt),
        in_specs=[pl.BlockSpec((tt, LANES), lambda b, j: (b * nt + j, col_block)),
                  pl.BlockSpec((1, LANES), lambda b, j: (0, 0))],
        out_specs=(pl.BlockSpec((tt, LANES), lambda b, j: (b * nt + j, 0)),
                   pl.BlockSpec((tt, n_pairs * LANES), lambda b, j: (b * nt + j, 0))),
        scratch_shapes=[pltpu.VMEM((SUBLANES, LANES), F32)],
        compiler_params=pltpu.CompilerParams(dimension_semantics=("parallel", "arbitrary")),
        name="logf_rows",
    )(proj, bias_row)


def _stack_heads(qn):
    lane = lax.broadcasted_iota(I32, qn.shape, 1)
    q0 = jnp.where(lane < 64, qn, 0.0)
    q1 = jnp.where(lane < 64, 0.0, qn)
    return jnp.concatenate([q0, q1], axis=0).astype(BF16)


def _lane_column(block, h):
    lane = lax.broadcasted_iota(I32, block.shape, 1)
    return jnp.sum(jnp.where(lane == h, block, 0.0), axis=-1, keepdims=True)


V_ROWS = LANES + BF16_ROWS


def _split_select(x):
    hi, mid, lo = _split3(x)
    m3 = lax.broadcasted_iota(I32, x.shape, 1) % 3
    return jnp.where(m3 == 0, hi.astype(F32), jnp.where(m3 == 1, mid.astype(F32), lo.astype(F32)))


def _head_masks(shape):
    lane = lax.broadcasted_iota(I32, shape, 1)
    return lane, lane < 64


def _finish_heads(acc, tq):
    o0 = acc[0:64, 0:tq] / acc[LANES:LANES + 1, 0:tq]
    o1 = acc[64:LANES, tq:2 * tq] / acc[LANES:LANES + 1, tq:2 * tq]
    return jnp.concatenate([o0, o1], axis=0).T


def _fox_prompt_body(q_ref, k_ref, v_ref, f_ref, gq_ref, gk_ref, *rest, tq, seq, dh):
    o_ref, knt_ref, vt_ref, kaug, vt3 = rest[-5:]
    ones = _pair_ones()
    nq = seq // tq
    lane, first = _head_masks((tq, LANES))

    for c in range(nq):
        rows = slice(c * tq, (c + 1) * tq)
        kn = _pair_norm(k_ref[rows, :], gk_ref[...], ones)
        knt_ref[:, rows] = kn.T
        kaug[rows, 0:LANES] = kn.astype(BF16)
        sp = _split_select(f_ref[rows, :])
        kaug[rows, LANES:2 * LANES] = jnp.where(
            lane < 6, -sp, jnp.where(lane < 12, 1.0, 0.0)).astype(BF16)
        vt = v_ref[rows, :].T
        vt_ref[:, rows] = vt
        vt3[c, 0:LANES, :] = vt.astype(BF16)
        vt3[c, LANES:V_ROWS, :] = jnp.ones((BF16_ROWS, tq), BF16)

    causal = (lax.broadcasted_iota(I32, (tq, 2 * tq), 0)
              <= lax.broadcasted_iota(I32, (tq, 2 * tq), 1) % tq)

    for qi in range(nq):
        rows = slice(qi * tq, (qi + 1) * tq)
        qn = _pair_norm(q_ref[rows, :], gq_ref[...], ones) * (dh ** -0.5 * LOG2E)
        sp = _split_select(f_ref[rows, :])
        up0 = jnp.where(lane < 3, 1.0, jnp.where((lane >= 6) & (lane < 9), sp, 0.0))
        up1 = jnp.where((lane >= 3) & (lane < 6), 1.0,
                        jnp.where((lane >= 9) & (lane < 12), sp, 0.0))
        qs = jnp.concatenate(
            [jnp.concatenate([jnp.where(first, qn, 0.0), up0], axis=1),
             jnp.concatenate([jnp.where(first, 0.0, qn), up1], axis=1)], axis=0).astype(BF16)

        s = jnp.where(causal, _dot_nt(kaug[rows, :], qs), NEG_INF)
        m = jnp.max(s, axis=0, keepdims=True)
        acc = _dot(vt3[qi], jnp.exp2(s - m).astype(BF16))
        for kj in range(qi):
            s = _dot_nt(kaug[kj * tq:(kj + 1) * tq, :], qs)
            m_new = jnp.maximum(m, jnp.max(s, axis=0, keepdims=True))
            acc = jnp.exp2(m - m_new) * acc + _dot(vt3[kj], jnp.exp2(s - m_new).astype(BF16))
            m = m_new
        o_ref[rows, :] = _finish_heads(acc, tq)


def _layer_cache_outputs(layer, depth, batch, width, seq, prev, n_inputs, index):
    shape = jax.ShapeDtypeStruct((depth, batch, width, seq), F32)
    spec = pl.BlockSpec((None, None, LANES, seq), lambda *g: (layer, *index(*g), 0))
    extra_specs = [] if prev is None else [pl.BlockSpec(memory_space=pl.ANY)] * 2
    aliases = {} if prev is None else {n_inputs: 1, n_inputs + 1: 2}
    return (shape, shape), (spec, spec), extra_specs, aliases, (() if prev is None else tuple(prev))


def _fox_prompt(proj, f_aug, gq, gk, *, batch, seq, n_pairs, col_q, col_k, col_v, m_rows, layer,
                depth, prev):
    tq = _pick(seq, (512, 256, 128))
    nq = seq // tq
    width = n_pairs * LANES
    est = (2 * 7 * _nbytes((seq, LANES), F32)
           + _nbytes((seq, 2 * LANES), BF16) + _nbytes((V_ROWS, seq), BF16)
           + 16 * _nbytes((tq, 2 * tq), F32))
    c_shapes, c_specs, extra_specs, aliases, extra_args = _layer_cache_outputs(
        layer, depth, batch, width, seq, prev, 6, lambda b, hp: (b, hp))
    return pl.pallas_call(
        functools.partial(_fox_prompt_body, tq=tq, seq=seq, dh=64),
        out_shape=(jax.ShapeDtypeStruct((m_rows, width), F32), *c_shapes),
        grid=(batch, n_pairs),
        in_specs=[
            pl.BlockSpec((seq, LANES), lambda b, hp: (b, col_q + hp)),
            pl.BlockSpec((seq, LANES), lambda b, hp: (b, col_k + hp)),
            pl.BlockSpec((seq, LANES), lambda b, hp: (b, col_v + hp)),
            pl.BlockSpec((seq, LANES), lambda b, hp: (b, hp)),
            pl.BlockSpec((1, LANES), lambda b, hp: (0, 0)),
            pl.BlockSpec((1, LANES), lambda b, hp: (0, 0)),
            *extra_specs,
        ],
        out_specs=(pl.BlockSpec((seq, LANES), lambda b, hp: (b, hp)), *c_specs),
        input_output_aliases=aliases,
        scratch_shapes=[pltpu.VMEM((seq, 2 * LANES), BF16), pltpu.VMEM((nq, V_ROWS, tq), BF16)],
        compiler_params=pltpu.CompilerParams(
            dimension_semantics=("parallel", "parallel"),
            vmem_limit_bytes=_vmem_limit(est)),
        name="fox_prompt",
    )(proj, proj, proj, f_aug, gq, gk, *extra_args)


def _band_prompt_body(q_ref, k_ref, v_ref, tab_ref, gq_ref, gk_ref, *rest, tq, seq, past, dh):
    o_ref, knt_ref, vt_ref, kpad, vt3 = rest[-5:]
    ones = _pair_ones()
    nq = seq // tq
    npad = past // tq
    win = past + tq

    kpad[0:past, :] = jnp.zeros((past, LANES), BF16)
    for c in range(npad):
        vt3[c] = jnp.zeros((V_ROWS, tq), BF16)
    for c in range(nq):
        rows = slice(c * tq, (c + 1) * tq)
        kn = _pair_norm(k_ref[rows, :], gk_ref[...], ones)
        knt_ref[:, rows] = kn.T
        kpad[past + c * tq:past + (c + 1) * tq, :] = kn.astype(BF16)
        vt = v_ref[rows, :].T
        vt_ref[:, rows] = vt
        vt3[npad + c, 0:LANES, :] = vt.astype(BF16)
        vt3[npad + c, LANES:V_ROWS, :] = jnp.ones((BF16_ROWS, tq), BF16)

    row = lax.broadcasted_iota(I32, (win, 2 * tq), 0)

    for qi in range(nq):
        r0 = qi * tq
        qn = _pair_norm(q_ref[r0:r0 + tq, :], gq_ref[...], ones) * (dh ** -0.5 * LOG2E)
        s = _dot_nt(kpad[r0:r0 + win, :], _stack_heads(qn)) + tab_ref[...]
        if r0 < past:
            s = jnp.where(row >= past - r0, s, NEG_INF)
        m = jnp.max(s, axis=0, keepdims=True)
        p = jnp.exp2(s - m).astype(BF16)
        acc = _dot(vt3[qi], p[0:tq, :])
        for c in range(1, win // tq):
            acc = acc + _dot(vt3[qi + c], p[c * tq:(c + 1) * tq, :])
        o_ref[r0:r0 + tq, :] = _finish_heads(acc, tq)


def _band_prompt(proj, tab, gq, gk, *, batch, seq, n_pairs, col_q, col_k, col_v, tq, m_rows, layer,
                 depth, prev):
    past = BAND_PAST * CHUNK
    assert past % tq == 0 and seq % tq == 0
    nq = seq // tq
    width = n_pairs * LANES
    win = past + tq
    est = (2 * 5 * _nbytes((seq, LANES), F32)
           + _nbytes((seq + past, LANES), BF16) + _nbytes((V_ROWS, seq + past), BF16)
           + 2 * _nbytes((win, 2 * tq), F32) + 8 * _nbytes((win, 2 * tq), F32))
    c_shapes, c_specs, extra_specs, aliases, extra_args = _layer_cache_outputs(
        layer, depth, batch, width, seq, prev, 6, lambda hp, b: (b, hp))
    return pl.pallas_call(
        functools.partial(_band_prompt_body, tq=tq, seq=seq, past=past, dh=64),
        out_shape=(jax.ShapeDtypeStruct((m_rows, width), F32), *c_shapes),
        grid=(n_pairs, batch),
        in_specs=[
            pl.BlockSpec((seq, LANES), lambda hp, b: (b, col_q + hp)),
            pl.BlockSpec((seq, LANES), lambda hp, b: (b, col_k + hp)),
            pl.BlockSpec((seq, LANES), lambda hp, b: (b, col_v + hp)),
            pl.BlockSpec((None, win, 2 * tq), lambda hp, b: (hp, 0, 0)),
            pl.BlockSpec((1, LANES), lambda hp, b: (0, 0)),
            pl.BlockSpec((1, LANES), lambda hp, b: (0, 0)),
            *extra_specs,
        ],
        out_specs=(pl.BlockSpec((seq, LANES), lambda hp, b: (b, hp)), *c_specs),
        input_output_aliases=aliases,
        scratch_shapes=[pltpu.VMEM((seq + past, LANES), BF16),
                        pltpu.VMEM(((seq + past) // tq, V_ROWS, tq), BF16)],
        compiler_params=pltpu.CompilerParams(
            dimension_semantics=("parallel", "parallel"),
            vmem_limit_bytes=_vmem_limit(est)),
        name="band_prompt",
    )(proj, proj, proj, tab, gq, gk, *extra_args)


def _sample_attn_body(*refs, mode, n_pairs, s_new, n_cache, dh):
    if mode == "fox":
        (q_ref, k_ref, v_ref, ck_ref, cv_ref, fq_ref, ft_ref, gq_ref, gk_ref, _alias,
         o_ref, kn_ref) = refs
    else:
        (q_ref, k_ref, v_ref, ck_ref, cv_ref, tabc_ref, tabn_ref, gq_ref, gk_ref, _alias,
         o_ref, kn_ref) = refs
    ones = _pair_ones()
    row = lax.broadcasted_iota(I32, (s_new, s_new), 0)
    col = lax.broadcasted_iota(I32, (s_new, s_new), 1)
    for hp in range(n_pairs):
        sl = slice(hp * LANES, (hp + 1) * LANES)
        qn = _pair_norm(q_ref[:, sl], gq_ref[...], ones) * (dh ** -0.5)
        kn = _pair_norm(k_ref[:, sl], gk_ref[...], ones)
        kn_ref[:, sl] = kn
        vn = v_ref[:, sl]
        outs = []
        for e in range(2):
            h = 2 * hp + e
            hl = slice(e * dh, (e + 1) * dh)
            qh = qn[:, hl].astype(BF16)
            sc = _dot(qh, ck_ref[h].astype(BF16))
            sn = _dot_nt(qh, kn[:, hl].astype(BF16))
            if mode == "fox":
                fq = _lane_column(fq_ref[...], h)
                fk = ft_ref[h:h + 1, :]
                sc = sc + fq - fk[:, :n_cache]
                sn = jnp.where(col <= row, sn + fq - fk[:, n_cache:n_cache + s_new], NEG_INF)
            else:
                sc = sc + tabc_ref[h]
                sn = sn + tabn_ref[h]
            m = jnp.maximum(jnp.max(sc, axis=-1, keepdims=True),
                            jnp.max(sn, axis=-1, keepdims=True))
            pc = jnp.exp(sc - m)
            pn = jnp.exp(sn - m)
            l = jnp.sum(pc, axis=-1, keepdims=True) + jnp.sum(pn, axis=-1, keepdims=True)
            outs.append((_dot_nt(pc.astype(BF16), cv_ref[h].astype(BF16))
                         + _dot(pn.astype(BF16), vn[:, hl].astype(BF16))) / l)
        o_ref[:, sl] = jnp.concatenate(outs, axis=1)


def _sample_attn(proj, cache_kt, cache_vt, layer, extra_a, extra_b, gq, gk, o_buf, *, mode, batch,
                 s_new, n_pairs, col_q, row0):
    width = n_pairs * LANES
    _, _, n_heads, dh, n_cache = cache_kt.shape
    rb = row0 // s_new
    if mode == "fox":
        ex_specs = [pl.BlockSpec((s_new, LANES), lambda b: (b, 0)),
                    pl.BlockSpec((None, 16, extra_b.shape[2]), lambda b: (b, 0, 0))]
    else:
        ex_specs = [pl.BlockSpec(extra_a.shape, lambda b: (0, 0, 0)),
                    pl.BlockSpec(extra_b.shape, lambda b: (0, 0, 0))]
    est = (2 * 2 * _nbytes((n_cache, width), F32) + 8 * _nbytes((s_new, width), F32)
           + 2 * _nbytes(extra_a.shape, F32) + 16 * _nbytes((2 * s_new, n_cache), F32)
           + 4 * _nbytes((n_cache, LANES), BF16))
    return pl.pallas_call(
        functools.partial(_sample_attn_body, mode=mode, n_pairs=n_pairs, s_new=s_new,
                          n_cache=n_cache, dh=dh),
        out_shape=(jax.ShapeDtypeStruct(o_buf.shape, F32),
                   jax.ShapeDtypeStruct((batch * s_new, width), F32)),
        grid=(batch,),
        in_specs=[
            pl.BlockSpec((s_new, width), lambda b: (rb + b, col_q)),
            pl.BlockSpec((s_new, width), lambda b: (rb + b, col_q + 1)),
            pl.BlockSpec((s_new, width), lambda b: (rb + b, col_q + 2)),
            pl.BlockSpec((None, None, n_heads, dh, n_cache), lambda b: (layer, b, 0, 0, 0)),
            pl.BlockSpec((None, None, n_heads, dh, n_cache), lambda b: (layer, b, 0, 0, 0)),
            *ex_specs,
            pl.BlockSpec((1, LANES), lambda b: (0, 0)),
            pl.BlockSpec((1, LANES), lambda b: (0, 0)),
            pl.BlockSpec(memory_space=pl.ANY),
        ],
        out_specs=(pl.BlockSpec((s_new, width), lambda b: (rb + b, 0)),
                   pl.BlockSpec((s_new, width), lambda b: (b, 0))),
        input_output_aliases={9: 0},
        compiler_params=pltpu.CompilerParams(
            dimension_semantics=("parallel",), vmem_limit_bytes=_vmem_limit(est)),
        name=f"{mode}_sample",
    )(proj, proj, proj, cache_kt, cache_vt, extra_a, extra_b, gq, gk, o_buf)


def _cross_body(*refs, n_heads, dh, aliased):
    if aliased:
        q_ref, k_ref, v_ref, _alias, o_ref = refs
    else:
        q_ref, k_ref, v_ref, o_ref = refs
    per_head = len(k_ref.shape) == 3
    for h in range(n_heads):
        sl = slice(h * dh, (h + 1) * dh)
        kh = k_ref[:, h, :] if per_head else k_ref[:, sl]
        vh = v_ref[:, h, :] if per_head else v_ref[:, sl]
        q = (q_ref[:, sl] * (dh ** -0.5)).astype(BF16)
        s = _dot_nt(q, kh.astype(BF16))
        m = jnp.max(s, axis=-1, keepdims=True)
        p = jnp.exp(s - m)
        l = jnp.sum(p, axis=-1, keepdims=True)
        o = _dot(p.astype(BF16), vh.astype(BF16)) / l
        o_ref[:, sl] = o.astype(o_ref.dtype)


def _cross_attn(q_all, k, v, o_buf, *, batch, q_len, n_mem, n_heads, dh, row0, m_rows, layer=None):
    width = n_heads * dh
    tq = _pick(q_len, (512, 256, 128, 64, 32, 16))
    nq = q_len // tq
    rb = row0 // tq
    aliased = o_buf is not None
    if layer is None:
        kv_spec = pl.BlockSpec((n_mem, width), lambda b, qi: (b, 0))
    else:
        kv_spec = pl.BlockSpec((None, None, n_mem, n_heads, dh), lambda b, qi: (layer, b, 0, 0, 0))
    in_specs = [pl.BlockSpec((tq, width), lambda b, qi: (rb + b * nq + qi, 0)), kv_spec, kv_spec]
    args = [q_all, k, v]
    if aliased:
        in_specs.append(pl.BlockSpec(memory_space=pl.ANY))
        args.append(o_buf)
    est = (2 * _nbytes((tq, width), F32) + 4 * _nbytes((n_mem, width), F32)
           + 2 * _nbytes((tq, width), BF16) + 12 * _nbytes((tq, n_mem), F32))
    return pl.pallas_call(
        functools.partial(_cross_body, n_heads=n_heads, dh=dh, aliased=aliased),
        out_shape=jax.ShapeDtypeStruct((m_rows, width), BF16),
        grid=(batch, nq),
        in_specs=in_specs,
        out_specs=pl.BlockSpec((tq, width), lambda b, qi: (rb + b * nq + qi, 0)),
        input_output_aliases={3: 0} if aliased else {},
        compiler_params=pltpu.CompilerParams(
            dimension_semantics=("parallel", "arbitrary"), vmem_limit_bytes=_vmem_limit(est)),
        name="cross_attn",
    )(*args)


def _s5_prep_body(lr_ref, li_ref, ldt_ref, br_ref, bi_ref, ar_ref, ai_ref, bbr_ref, bbi_ref):
    lr = lr_ref[...]
    li = li_ref[...]
    dt = jnp.exp(ldt_ref[...])
    mag = jnp.exp(lr * dt)
    a_re = mag * jnp.cos(li * dt)
    a_im = mag * jnp.sin(li * dt)
    den = lr * lr + li * li
    num_re = a_re - 1.0
    coef_re = (num_re * lr + a_im * li) / den
    coef_im = (a_im * lr - num_re * li) / den
    br = br_ref[...]
    bi = bi_ref[...]
    ar_ref[...] = a_re
    ai_ref[...] = a_im
    bbr_ref[...] = coef_re * br - coef_im * bi
    bbi_ref[...] = coef_re * bi + coef_im * br


def _s5_prep(lam_re, lam_im, log_dt, b_re, b_im):
    shape = lam_re.shape
    spec = pl.BlockSpec(shape, lambda: (0, 0))
    return pl.pallas_call(
        _s5_prep_body,
        out_shape=tuple(jax.ShapeDtypeStruct(shape, F32) for _ in range(4)),
        in_specs=[spec] * 5,
        out_specs=tuple([spec] * 4),
        name="s5_discretise",
    )(lam_re, lam_im, log_dt, b_re, b_im)


def _s5_body(*refs, nb, t_chunk, n_blk, passes, per_stream):
    n_u = nb if per_stream else 1
    u_refs = refs[:n_u]
    (x0r_ref, x0i_ref, ar_ref, ai_ref, bre_ref, bim_ref, cre_ref, cim_ref, d_ref, wg_ref, bg_ref,
     o_ref, xr_out, xi_out, bur, bui, st_r, st_i) = refs[n_u:n_u + 18]
    i = pl.program_id(0)
    wc = u_refs[0].shape[1]
    ns = bur.shape[1]
    ub = wc // n_blk
    sb = ns // n_blk

    @pl.when(i == 0)
    def _():
        st_r[...] = x0r_ref[...]
        st_i[...] = x0i_ref[...]

    if per_stream:
        slab = refs[n_u + 18]
        for b in range(nb):
            for k in range(wc // LANES):
                slab[k, pl.ds(b, t_chunk, stride=nb), :] = u_refs[b][:, k * LANES:(k + 1) * LANES]
        u = jnp.concatenate([slab[k] for k in range(wc // LANES)], axis=1)
    else:
        u = u_refs[0][...]
    rows = nb * t_chunk
    n_half = 2 if (t_chunk % 2 == 0 and rows >= 1024) else 1
    hr = rows // n_half
    for h in range(n_half):
        for k in range(n_blk):
            uk = u[h * hr:(h + 1) * hr, k * ub:(k + 1) * ub]
            bur[h * hr:(h + 1) * hr, k * sb:(k + 1) * sb] = _dot_hp(
                uk, bre_ref[k * ub:(k + 1) * ub, k * sb:(k + 1) * sb], passes)
            bui[h * hr:(h + 1) * hr, k * sb:(k + 1) * sb] = _dot_hp(
                uk, bim_ref[k * ub:(k + 1) * ub, k * sb:(k + 1) * sb], passes)

    a_re = ar_ref[...]
    a_im = ai_ref[...]
    xr, xi = st_r[...], st_i[...]
    for t in range(t_chunk):
        r0 = t * nb
        xr, xi = (a_re * xr - a_im * xi + bur[r0:r0 + nb, :],
                  a_re * xi + a_im * xr + bui[r0:r0 + nb, :])
        bur[r0:r0 + nb, :] = xr
        bui[r0:r0 + nb, :] = xi
    st_r[...] = xr
    st_i[...] = xi
    xr_out[...] = xr
    xi_out[...] = xi
    outs = []
    for h in range(n_half):
        ys = []
        for k in range(n_blk):
            xrk = bur[h * hr:(h + 1) * hr, k * sb:(k + 1) * sb].astype(BF16)
            xik = bui[h * hr:(h + 1) * hr, k * sb:(k + 1) * sb].astype(BF16)
            ys.append(_dot(xrk, cre_ref[k * sb:(k + 1) * sb, k * ub:(k + 1) * ub].astype(BF16))
                      - _dot(xik, cim_ref[k * sb:(k + 1) * sb, k * ub:(k + 1) * ub].astype(BF16)))
        y = jnp.concatenate(ys, axis=1) + d_ref[...] * u[h * hr:(h + 1) * hr, :]
        z = _gelu_tanh(y)
        gate = _dot(z.astype(BF16), wg_ref[...].astype(BF16)) + bg_ref[...]
        outs.append(z * jax.nn.sigmoid(gate))
    out = outs[0] if n_half == 1 else jnp.concatenate(outs, axis=0)
    if per_stream:
        for k in range(wc // LANES):
            slab[k] = out[:, k * LANES:(k + 1) * LANES]
        for b in range(nb):
            o_ref[b] = jnp.concatenate(
                [slab[k, pl.ds(b, t_chunk, stride=nb), :] for k in range(wc // LANES)], axis=1)
    else:
        o_ref[...] = out


def _s5(u, x0r, x0i, a_re, a_im, b_re, b_im, c_re, c_im, d, w_glu, b_glu, *, nb, seq, n_blk,
        passes, col_block=None):
    per_stream = col_block is not None
    wc, ns = b_re.shape
    t_chunk = _pick(seq, (128, 64, 32, 16) if nb <= SUBLANES else (16,))
    nc = seq // t_chunk
    rows = nb * t_chunk
    full = lambda shape: pl.BlockSpec(shape, lambda i: (0,) * len(shape))
    est = (6 * _nbytes((rows, wc), F32) + 2 * _nbytes((rows, ns), F32)
           + 2 * 2 * _nbytes((wc, ns), b_re.dtype) + 2 * 2 * _nbytes((wc, ns), c_re.dtype)
           + 2 * _nbytes((wc, wc), F32)
           + 6 * _nbytes((rows // 2, ns // n_blk), F32) + 8 * _nbytes((nb, ns), F32))
    if per_stream:
        u_specs = [pl.BlockSpec((t_chunk, wc), lambda i, s=s: (s * nc + i, col_block))
                   for s in range(nb)]
        u_args = [u] * nb
        o_shape = jax.ShapeDtypeStruct((nb, seq, wc), F32)
        o_spec = pl.BlockSpec((nb, t_chunk, wc), lambda i: (0, i, 0))
        slab = [pltpu.VMEM((wc // LANES, rows, LANES), F32)]
    else:
        u_specs = [pl.BlockSpec((rows, wc), lambda i: (i, 0))]
        u_args = [u]
        o_shape = jax.ShapeDtypeStruct((seq * nb, wc), F32)
        o_spec = pl.BlockSpec((rows, wc), lambda i: (i, 0))
        slab = []
    return pl.pallas_call(
        functools.partial(_s5_body, nb=nb, t_chunk=t_chunk, n_blk=n_blk, passes=passes,
                          per_stream=per_stream),
        out_shape=(o_shape, jax.ShapeDtypeStruct((nb, ns), F32), jax.ShapeDtypeStruct((nb, ns), F32)),
        grid=(nc,),
        in_specs=[*u_specs,
                  full((nb, ns)), full((nb, ns)), full((1, ns)), full((1, ns)),
                  full((wc, ns)), full((wc, ns)), full((ns, wc)), full((ns, wc)),
                  full((1, wc)), full((wc, wc)), full((1, wc))],
        out_specs=(o_spec, full((nb, ns)), full((nb, ns))),
        scratch_shapes=[pltpu.VMEM((rows, ns), F32), pltpu.VMEM((rows, ns), F32),
                        pltpu.VMEM((nb, ns), F32), pltpu.VMEM((nb, ns), F32), *slab],
        compiler_params=pltpu.CompilerParams(
            dimension_semantics=("arbitrary",), vmem_limit_bytes=_vmem_limit(est)),
        name="s5_scan",
    )(*u_args, x0r, x0i, a_re, a_im, b_re, b_im, c_re, c_im, d, w_glu, b_glu)


def _router_body(x_ref, g_ref, wr_ref, pk_ref, idx_ref, gate_ref, *, n_exp, rc):
    tm, d = x_ref.shape
    half = d // 2

    for c in range(tm // rc):
        r0 = c * rc
        x = x_ref[pl.ds(r0, rc), :]
        ms = jnp.mean(x * x, axis=-1, keepdims=True)
        h = x * lax.rsqrt(ms + EPS) * g_ref[...]
        hb = h.astype(BF16).astype(F32)
        lo = lax.shift_right_logical(pltpu.bitcast(hb[:, :half], U32), jnp.uint32(16))
        hi = pltpu.bitcast(hb[:, half:], U32) & jnp.uint32(0xFFFF0000)
        pk_ref[pl.ds(r0, rc), :] = lo | hi
        logits = _dot_hp(h, wr_ref[...])
        lane = lax.broadcasted_iota(I32, logits.shape, 1)
        logits = jnp.where(lane < n_exp, logits, NEG_INF)
        mx = jnp.max(logits, axis=-1, keepdims=True)
        e = jnp.exp(logits - mx)
        probs = e / jnp.sum(e, axis=-1, keepdims=True)
        probs = jnp.where(lane < n_exp, probs, -1.0)
        lane_f = lane.astype(F32)
        p1 = jnp.max(probs, axis=-1, keepdims=True)
        i1 = jnp.min(jnp.where(probs == p1, lane_f, float(LANES)), axis=-1, keepdims=True)
        rest = jnp.where(lane_f == i1, -1.0, probs)
        p2 = jnp.max(rest, axis=-1, keepdims=True)
        i2 = jnp.min(jnp.where(rest == p2, lane_f, float(LANES)), axis=-1, keepdims=True)
        tot = p1 + p2
        idx_ref[pl.ds(r0, rc), :] = jnp.where(lane == 0, i1, jnp.where(lane == 1, i2, 0.0)).astype(I32)
        gate_ref[pl.ds(r0, rc), :] = jnp.where(lane == 0, p1 / tot,
                                                jnp.where(lane == 1, p2 / tot, 0.0))


def _router(x, g, w_router_pad, *, n_exp, tm):
    m, d = x.shape
    rc = _pick(tm, (256, 176, 128, 64, 32, 16, 8))
    est = 2 * _nbytes((tm, d), F32) + 2 * _nbytes((tm, d // 2), U32) + 16 * _nbytes((rc, d), F32)
    return pl.pallas_call(
        functools.partial(_router_body, n_exp=n_exp, rc=rc),
        out_shape=(jax.ShapeDtypeStruct((m, d // 2), U32),
                   jax.ShapeDtypeStruct((m, LANES), I32),
                   jax.ShapeDtypeStruct((m, LANES), F32)),
        grid=(m // tm,),
        in_specs=[pl.BlockSpec((tm, d), lambda i: (i, 0)),
                  pl.BlockSpec((1, d), lambda i: (0, 0)),
                  pl.BlockSpec((d, LANES), lambda i: (0, 0))],
        out_specs=(pl.BlockSpec((tm, d // 2), lambda i: (i, 0)),
                   pl.BlockSpec((tm, LANES), lambda i: (i, 0)),
                   pl.BlockSpec((tm, LANES), lambda i: (i, 0))),
        compiler_params=pltpu.CompilerParams(
            dimension_semantics=("parallel",), vmem_limit_bytes=_vmem_limit(est)),
        name="moe_router",
    )(x, g, w_router_pad)


GATHER_UNROLL = 8


def _row_copy(src_hbm, row, dst, r, sem):
    return pltpu.make_async_copy(src_hbm.at[pl.ds(row, 1), :], dst.at[pl.ds(r, 1), :], sem)


def _dispatch_body(nt_ref, tok_ref, nxt_ref, pk_hbm, a_ref, buf, sem, *, tm):
    i = pl.program_id(0)
    nt = nt_ref[0]
    half = buf.shape[2]

    def request(ids_ref, slot):
        def issue(c, carry):
            for u in range(GATHER_UNROLL):
                r = c * GATHER_UNROLL + u
                _row_copy(pk_hbm, ids_ref[0, 0, r], buf.at[slot], r, sem.at[slot]).start(
                    priority=u % 2)
            return carry
        lax.fori_loop(0, tm // GATHER_UNROLL, issue, 0)

    @pl.when(i == 0)
    def _():
        request(tok_ref, 0)

    for slot in range(2):
        @pl.when(jnp.logical_and(i + 1 < nt, (i + 1) % 2 == slot))
        def _(slot=slot):
            request(nxt_ref, slot)

    for slot in range(2):
        @pl.when(jnp.logical_and(i < nt, i % 2 == slot))
        def _(slot=slot):
            pltpu.make_async_copy(pk_hbm.at[pl.ds(0, tm), :], buf.at[slot], sem.at[slot]).wait()
            pk = buf[slot]
            lo = pltpu.bitcast(lax.shift_left(pk, jnp.uint32(16)), F32)
            hi = pltpu.bitcast(pk & jnp.uint32(0xFFFF0000), F32)
            a_ref[:, :half] = lo.astype(BF16)
            a_ref[:, half:] = hi.astype(BF16)


def _dispatch(n_tiles, tok_sorted, packed, *, tm, r_max):
    m, half = packed.shape
    t_max = r_max // tm
    assert tm % GATHER_UNROLL == 0
    tok_tiles = tok_sorted.reshape(t_max, 1, tm)
    return pl.pallas_call(
        functools.partial(_dispatch_body, tm=tm),
        out_shape=jax.ShapeDtypeStruct((r_max, 2 * half), BF16),
        grid_spec=pltpu.PrefetchScalarGridSpec(
            num_scalar_prefetch=1,
            grid=(t_max,),
            in_specs=[pl.BlockSpec((1, 1, tm), lambda i, nt: (i, 0, 0), memory_space=pltpu.SMEM),
                      pl.BlockSpec((1, 1, tm), lambda i, nt: (jnp.minimum(i + 1, t_max - 1), 0, 0),
                                   memory_space=pltpu.SMEM),
                      pl.BlockSpec(memory_space=pl.ANY)],
            out_specs=pl.BlockSpec((tm, 2 * half), lambda i, nt: (jnp.minimum(i, nt[0] - 1), 0)),
            scratch_shapes=[pltpu.VMEM((2, tm, half), U32), pltpu.SemaphoreType.DMA((2,))]),
        compiler_params=pltpu.CompilerParams(dimension_semantics=("arbitrary",)),
        name="moe_dispatch",
    )(n_tiles, tok_tiles, tok_tiles, packed)


def _expert_up_body(te_ref, nt_ref, a_ref, w1_ref, w3_ref, h_ref, w1bf, w3bf, *, kc, rem):
    f = pl.program_id(0)
    i = pl.program_id(1)
    last = pl.num_programs(0) - 1
    d, tf = w1bf.shape
    fresh = jnp.logical_or(i == 0, te_ref[i] != te_ref[jnp.maximum(i - 1, 0)])
    live = i < nt_ref[0]

    def run(cols, when):
        @pl.when(jnp.logical_and(when, jnp.logical_and(fresh, live)))
        def _():
            def cast(c, carry):
                r0 = pl.multiple_of(c * kc, kc)
                w1bf[pl.ds(r0, kc), 0:cols] = w1_ref[pl.ds(r0, kc), 0:cols].astype(BF16)
                w3bf[pl.ds(r0, kc), 0:cols] = w3_ref[pl.ds(r0, kc), 0:cols].astype(BF16)
                return carry
            lax.fori_loop(0, d // kc, cast, 0)

        @pl.when(jnp.logical_and(when, live))
        def _():
            a = a_ref[...]
            h_ref[:, 0:cols] = (jax.nn.silu(_dot(a, w1bf[:, 0:cols]))
                                * _dot(a, w3bf[:, 0:cols])).astype(h_ref.dtype)

    if rem:
        run(tf, f < last)
        run(rem, f == last)
    else:
        run(tf, True)


def _expert_up(tile_expert, n_tiles, a_sorted, w1, w3, *, tm, tf):
    r_max, d = a_sorted.shape
    n_exp, _, fe = w1.shape
    t_max = r_max // tm
    kc = _pick(d, (512, 256, 128))
    row = lambda f, i, te, nt: jnp.minimum(i, nt[0] - 1)
    est = (2 * _nbytes((tm, d), BF16) + 2 * 2 * _nbytes((d, tf), F32) + 2 * _nbytes((d, tf), BF16)
           + 2 * _nbytes((tm, tf), BF16) + 6 * _nbytes((tm, tf), F32))
    return pl.pallas_call(
        functools.partial(_expert_up_body, kc=kc, rem=fe % tf),
        out_shape=jax.ShapeDtypeStruct((r_max, fe), BF16),
        grid_spec=pltpu.PrefetchScalarGridSpec(
            num_scalar_prefetch=2,
            grid=(pl.cdiv(fe, tf), t_max),
            in_specs=[pl.BlockSpec((tm, d), lambda f, i, te, nt: (row(f, i, te, nt), 0)),
                      pl.BlockSpec((None, d, tf), lambda f, i, te, nt: (te[i], 0, f)),
                      pl.BlockSpec((None, d, tf), lambda f, i, te, nt: (te[i], 0, f))],
            out_specs=pl.BlockSpec((tm, tf), lambda f, i, te, nt: (row(f, i, te, nt), f)),
            scratch_shapes=[pltpu.VMEM((d, tf), BF16), pltpu.VMEM((d, tf), BF16)]),
        compiler_params=pltpu.CompilerParams(
            dimension_semantics=("arbitrary", "arbitrary"), vmem_limit_bytes=_vmem_limit(est)),
        name="moe_expert_up",
    )(tile_expert, n_tiles, a_sorted, w1, w3)


def _expert_down_body(te_ref, nt_ref, h_ref, w2_ref, y_ref, w2bf, *, kc):
    i = pl.program_id(1)
    fe = h_ref.shape[1]
    fresh = jnp.logical_or(i == 0, te_ref[i] != te_ref[jnp.maximum(i - 1, 0)])

    @pl.when(jnp.logical_and(fresh, i < nt_ref[0]))
    def _():
        def cast(c, carry):
            r0 = pl.multiple_of(c * kc, kc)
            w2bf[pl.ds(r0, kc), :] = w2_ref[pl.ds(r0, kc), :].astype(BF16)
            return carry
        lax.fori_loop(0, fe // kc, cast, 0)

    @pl.when(i < nt_ref[0])
    def _():
        y_ref[...] = _dot(h_ref[...], w2bf[...])


def _expert_down(tile_expert, n_tiles, h_sorted, w2, *, tm, tn):
    r_max, fe = h_sorted.shape
    d = w2.shape[2]
    t_max = r_max // tm
    kc = _pick(fe, (512, 256, 128))
    row = lambda n, i, te, nt: jnp.minimum(i, nt[0] - 1)
    est = (2 * _nbytes((tm, fe), BF16) + 2 * _nbytes((fe, tn), F32) + _nbytes((fe, tn), BF16)
           + 4 * _nbytes((tm, tn), F32))
    return pl.pallas_call(
        functools.partial(_expert_down_body, kc=kc),
        out_shape=jax.ShapeDtypeStruct((r_max, d), F32),
        grid_spec=pltpu.PrefetchScalarGridSpec(
            num_scalar_prefetch=2,
            grid=(d // tn, t_max),
            in_specs=[pl.BlockSpec((tm, fe), lambda n, i, te, nt: (row(n, i, te, nt), 0)),
                      pl.BlockSpec((None, fe, tn), lambda n, i, te, nt: (te[i], 0, n))],
            out_specs=pl.BlockSpec((tm, tn), lambda n, i, te, nt: (row(n, i, te, nt), n)),
            scratch_shapes=[pltpu.VMEM((fe, tn), BF16)]),
        compiler_params=pltpu.CompilerParams(
            dimension_semantics=("arbitrary", "arbitrary"), vmem_limit_bytes=_vmem_limit(est)),
        name="moe_expert_down",
    )(tile_expert, n_tiles, h_sorted, w2)


def _combine_body(slot_ref, nxt_ref, x_ref, gate_ref, y_hbm, *rest, tc, n_first):
    o_refs, (buf, sem) = rest[:-2], rest[-2:]
    i = pl.program_id(0)
    n = pl.num_programs(0)
    unroll = GATHER_UNROLL // TOP_K

    def request(ids_ref, slot):
        def issue(c, carry):
            for u in range(unroll):
                r = c * unroll + u
                for k in range(TOP_K):
                    _row_copy(y_hbm, ids_ref[0, 0, TOP_K * r + k], buf.at[slot, k], r,
                              sem.at[slot]).start(priority=k % 2)
            return carry
        lax.fori_loop(0, tc // unroll, issue, 0)

    @pl.when(i == 0)
    def _():
        request(slot_ref, 0)

    for s in range(2):
        @pl.when(jnp.logical_and(i + 1 < n, (i + 1) % 2 == s))
        def _(s=s):
            request(nxt_ref, s)

    slot = i % 2
    for k in range(TOP_K):
        pltpu.make_async_copy(y_hbm.at[pl.ds(0, tc), :], buf.at[slot, k], sem.at[slot]).wait()
    g = gate_ref[...]
    val = x_ref[...] + (g[:, 0:1] * buf[slot, 0] + g[:, 1:2] * buf[slot, 1])
    if len(o_refs) == 1:
        o_refs[0][...] = val
    else:
        @pl.when(i < n_first)
        def _():
            o_refs[0][...] = val

        @pl.when(i >= n_first)
        def _():
            o_refs[1][...] = val


def _combine(slots, x, gates, y_sorted, *, tc, split=None):
    m, d = x.shape
    n = m // tc
    assert tc % (GATHER_UNROLL // TOP_K) == 0
    est = 4 * _nbytes((tc, d), F32) + 2 * TOP_K * _nbytes((tc, d), F32) + 6 * _nbytes((tc, d), F32)
    slot_tiles = slots.reshape(n, 1, TOP_K * tc)
    if split is None or split % tc or (m - split) % tc:
        n_first = n
        out_shape = jax.ShapeDtypeStruct((m, d), F32)
        out_specs = pl.BlockSpec((tc, d), lambda i: (i, 0))
    else:
        n_first = split // tc
        out_shape = (jax.ShapeDtypeStruct((split, d), F32), jax.ShapeDtypeStruct((m - split, d), F32))
        out_specs = (pl.BlockSpec((tc, d), lambda i: (jnp.minimum(i, n_first - 1), 0)),
                     pl.BlockSpec((tc, d), lambda i: (jnp.maximum(i - n_first, 0), 0)))
    return pl.pallas_call(
        functools.partial(_combine_body, tc=tc, n_first=n_first),
        out_shape=out_shape,
        grid=(n,),
        in_specs=[pl.BlockSpec((1, 1, TOP_K * tc), lambda i: (i, 0, 0), memory_space=pltpu.SMEM),
                  pl.BlockSpec((1, 1, TOP_K * tc), lambda i: (jnp.minimum(i + 1, n - 1), 0, 0),
                               memory_space=pltpu.SMEM),
                  pl.BlockSpec((tc, d), lambda i: (i, 0)),
                  pl.BlockSpec((tc, LANES), lambda i: (i, 0)),
                  pl.BlockSpec(memory_space=pl.ANY)],
        out_specs=out_specs,
        scratch_shapes=[pltpu.VMEM((2, TOP_K, tc, d), F32), pltpu.SemaphoreType.DMA((2,))],
        compiler_params=pltpu.CompilerParams(
            dimension_semantics=("arbitrary",), vmem_limit_bytes=_vmem_limit(est)),
        name="moe_combine",
    )(slot_tiles, slot_tiles, x, gates, y_sorted)


def _moe(x, g_ffn, w_router, w_e1, w_e3, w_e2, *, tm_tok, split=None):
    m, d = x.shape
    n_exp, _, fe = w_e1.shape
    tm = _pick(m * TOP_K, (512, 256, 128, 64, 32, 16))
    wr = jnp.zeros((d, LANES), F32).at[:, :n_exp].set(w_router)
    packed, idx128, gate128 = _router(x, g_ffn, wr, n_exp=n_exp, tm=tm_tok)

    idx = idx128[:, :TOP_K]
    mask = jnp.sum(idx[:, :, None] == jnp.arange(n_exp, dtype=I32)[None, None, :], axis=1).astype(I32)
    counts = jnp.sum(mask, axis=0)
    padded = ((counts + tm - 1) // tm) * tm
    ends = jnp.cumsum(padded)
    starts = ends - padded
    pos = jnp.cumsum(mask, axis=0) - mask
    slot = starts[idx] + jnp.take_along_axis(pos, idx, axis=1)
    t_max = (m * TOP_K) // tm + n_exp
    r_max = t_max * tm
    tok_sorted = jnp.zeros((r_max,), I32).at[slot.reshape(-1)].set(
        jnp.repeat(jnp.arange(m, dtype=I32), TOP_K))
    n_tiles = (ends[-1] // tm).astype(I32).reshape(1)
    tile_start = jnp.minimum(jnp.arange(t_max, dtype=I32), n_tiles[0] - 1) * tm
    tile_expert = jnp.minimum(jnp.sum(ends[None, :] <= tile_start[:, None], axis=1),
                              n_exp - 1).astype(I32)

    a_sorted = _dispatch(n_tiles, tok_sorted, packed, tm=tm, r_max=r_max)
    th = _pick(fe, (256, 128))
    tf = 2 * th if fe >= 2 * th else th
    h_sorted = _expert_up(tile_expert, n_tiles, a_sorted, w_e1, w_e3, tm=tm, tf=tf)
    y_sorted = _expert_down(tile_expert, n_tiles, h_sorted, w_e2, tm=tm,
                            tn=_pick(d, (1024, 512, 256, 128)))
    tc = _pick(m, (256, 128, 64, 32, 16, 8))
    return _combine(slot, x, gate128, y_sorted, tc=tc, split=split)


def _block_diag(blocks):
    g, r, c = blocks.shape
    eye = jnp.eye(g, dtype=blocks.dtype)
    return (blocks[:, :, None, :] * eye[:, None, :, None]).reshape(g * r, g * c)


def _rel_table(rel_bias_l, q_pos, k_pos, max_rel):
    rel = np.clip(q_pos[:, None] - k_pos[None, :], -(CHUNK - 1), max_rel) + (CHUNK - 1)
    return rel_bias_l[:, rel].astype(F32)


def _band_table(rel_bias_l, tq, max_rel):
    n_heads = rel_bias_l.shape[0]
    past = BAND_PAST * CHUNK
    win = past + tq
    ring = -(-(tq + win - 1) // LANES) * LANES
    diff = np.arange(ring)
    diff = np.where(diff < tq, diff, diff - ring)
    idx = np.clip(diff + past, -(CHUNK - 1), max_rel) + (CHUNK - 1)
    by_diff = jnp.pad(rel_bias_l.astype(F32)[:, idx], ((0, 16 - n_heads), (0, 0)))

    def body(u_ref, o_ref):
        hp = pl.program_id(0)
        j = lax.broadcasted_iota(I32, (win, tq), 0)
        i = lax.broadcasted_iota(I32, (win, tq), 1)
        gap = (i + past) // CHUNK - j // CHUNK
        valid = (gap >= 0) & (gap <= BAND_PAST)
        for e in range(2):
            rows = jnp.broadcast_to(u_ref[pl.ds(2 * hp + e, 1), :], (win, ring))
            shifted = pltpu.roll(rows, 0, 1, stride=1, stride_axis=0)
            o_ref[:, e * tq:(e + 1) * tq] = jnp.where(valid, shifted[:, 0:tq] * LOG2E, NEG_INF)

    return pl.pallas_call(
        body,
        out_shape=jax.ShapeDtypeStruct((n_heads // 2, win, 2 * tq), F32),
        grid=(n_heads // 2,),
        in_specs=[pl.BlockSpec((16, ring), lambda hp: (0, 0))],
        out_specs=pl.BlockSpec((None, win, 2 * tq), lambda hp: (hp, 0, 0)),
        compiler_params=pltpu.CompilerParams(dimension_semantics=("parallel",)),
        name="band_table",
    )(by_diff)


def _tile_gain(g):
    return jnp.tile(g.astype(F32), LANES // g.shape[0]).reshape(1, LANES)


def kernel(x_prompt, x_sample, mem_prompt, cache_fox_k, cache_fox_v, cache_fox_logf, cache_band_k, cache_band_v, state_ssm_re, state_ssm_im, cache_mem_k, cache_mem_v, g_mix, w_in, b_f, g_qa, g_ka, g_qb, g_kb, rel_bias, lam_re, lam_im, log_dt, ssm_b_re, ssm_b_im, ssm_c_re, ssm_c_im, ssm_d, w_glu, b_glu, g_mix_out, w_out, g_cross, g_mem, w_cq, w_ck, w_cv, g_cq, g_ck, w_co, g_ffn, w_ff1, w_ff3, w_ff2, w_router, w_e1, w_e3, w_e2):
    batch, seq, d = x_prompt.shape
    dbatch, dseq, _ = x_sample.shape
    depth = g_mix.shape[0]
    past_len, h_a, dh = cache_fox_k.shape[2:]
    band_rows, h_b = cache_band_k.shape[2:4]
    g_c, p_state = lam_re.shape[1:]
    w_a, w_b, w_c = h_a * dh, h_b * dh, g_c * SSM_GROUP
    n_mem, h_m, dh_m = cache_mem_k.shape[2:]
    w_m = h_m * dh_m
    max_rel = rel_bias.shape[2] - CHUNK
    assert dh == 64 and h_a % 2 == 0 and h_b % 2 == 0 and dh_m == LANES
    assert w_a == w_b and w_c <= w_a and h_a <= 16
    mp, ms = batch * seq, dbatch * dseq
    m = mp + ms
    pa, pb = h_a // 2, h_b // 2
    n_state = g_c * p_state
    n_blk = 2 if (w_c % 512 == 0) else 1
    nband = min(BAND_PAST * CHUNK, seq)
    tm = _pick(m, (1056, 1024, 768, 512, 256, 128, 64, 32, 16))
    tq_band = _pick(seq, (256, 128, 64))

    x = jnp.concatenate([x_prompt.reshape(mp, d), x_sample.reshape(ms, d)], axis=0)
    fox_kt, fox_vt, band_kt, band_vt = (jnp.transpose(c, (0, 1, 3, 4, 2)) for c in
                                        (cache_fox_k, cache_fox_v, cache_band_k, cache_band_v))

    outs = {k: [] for k in ("p_fl", "p_sr", "p_si", "p_mk", "p_mv",
                            "s_fk", "s_fv", "s_fl", "s_bk", "s_bv", "s_sr", "s_si")}
    cache_a = cache_b = None
    for l in range(depth):
        sizes = (w_a, w_a, w_a, h_a, w_b, w_b, w_b)
        cuts = [sum(sizes[:i]) for i in range(len(sizes) + 1)]
        w_cat = _regroup_columns(
            w_in, l, ((0, cuts[3]), (cuts[4], cuts[7]), (cuts[7], w_in.shape[2]), (cuts[3], cuts[4])),
            7 * w_a)
        proj = _mm([x], [w_cat], gain=g_mix[l].reshape(1, d), tm=tm, tn=w_a, name="proj_in")
        col_uc = 6 * w_a

        assert (col_uc + w_c) % LANES == 0
        lf_p, f_aug = _logf_rows(proj, jnp.pad(b_f[l].astype(F32), (0, LANES - h_a)).reshape(1, LANES),
                                 batch=batch, seq=seq, n_heads=h_a, n_pairs=pa,
                                 col_block=(col_uc + w_c) // LANES)

        t_all = -(-(past_len + dseq) // LANES) * LANES
        fa_s = proj[mp:, col_uc + w_c:col_uc + w_c + h_a].reshape(dbatch, dseq, h_a)
        x_s = jnp.concatenate([jnp.transpose(cache_fox_logf[l], (0, 2, 1)),
                               jnp.transpose(fa_s, (0, 2, 1)),
                               jnp.zeros((dbatch, h_a, t_all - past_len - dseq), F32)], axis=2)
        bias_s = jnp.broadcast_to(jnp.tile(b_f[l], dbatch)[:, None], (dbatch * h_a, LANES))
        lf_s, f_s = _logf_cumsum(x_s.reshape(dbatch * h_a, t_all), bias_s,
                                 raw_from=past_len, valid_to=past_len + dseq)
        lf_s = lf_s.reshape(dbatch, h_a, t_all)[:, :, past_len:past_len + dseq]
        f_s = f_s.reshape(dbatch, h_a, t_all)
        fs_row = jnp.pad(f_s, ((0, 0), (0, 16 - h_a), (0, 0)))
        fs_col = jnp.pad(jnp.transpose(f_s[:, :, past_len:past_len + dseq], (0, 2, 1)),
                         ((0, 0), (0, 0), (0, LANES - h_a))).reshape(ms, LANES)

        gqa, gka = _tile_gain(g_qa[l]), _tile_gain(g_ka[l])
        oa, *cache_a = _fox_prompt(proj, f_aug, gqa, gka, batch=batch, seq=seq, n_pairs=pa,
                                   col_q=0, col_k=pa, col_v=2 * pa, m_rows=m, layer=l, depth=depth,
                                   prev=cache_a)
        oa, kn_as = _sample_attn(proj, fox_kt, fox_vt, l, fs_col, fs_row,
                                 gqa, gka, oa, mode="fox", batch=dbatch, s_new=dseq, n_pairs=pa,
                                 col_q=0, row0=mp)

        gqb, gkb = _tile_gain(g_qb[l]), _tile_gain(g_kb[l])
        tab = _band_table(rel_bias[l], tq_band, max_rel)
        ob, *cache_b = _band_prompt(proj, tab, gqb, gkb, batch=batch, seq=seq, n_pairs=pb,
                                    col_q=3 * pa, col_k=3 * pa + pb, col_v=3 * pa + 2 * pb,
                                    tq=tq_band, m_rows=m, layer=l, depth=depth, prev=cache_b)
        tab_s = _rel_table(rel_bias[l], band_rows + np.arange(dseq), np.arange(band_rows + dseq),
                           max_rel)
        ob, kn_bs = _sample_attn(proj, band_kt, band_vt, l,
                                 tab_s[:, :, :band_rows], tab_s[:, :, band_rows:], gqb, gkb, ob,
                                 mode="band", batch=dbatch, s_new=dseq, n_pairs=pb, col_q=3, row0=mp)

        rep = lambda a: jnp.repeat(a.astype(F32), SSM_GROUP, axis=0)
        a_re, a_im, bb_re, bb_im = _s5_prep(
            rep(lam_re[l]), rep(lam_im[l]),
            jnp.broadcast_to(rep(log_dt[l])[:, None], (g_c * SSM_GROUP, p_state)),
            jnp.transpose(ssm_b_re[l], (0, 2, 1)).reshape(g_c * SSM_GROUP, p_state),
            jnp.transpose(ssm_b_im[l], (0, 2, 1)).reshape(g_c * SSM_GROUP, p_state))
        a_re = a_re.reshape(g_c, SSM_GROUP, p_state)[:, 0, :].reshape(1, n_state)
        a_im = a_im.reshape(g_c, SSM_GROUP, p_state)[:, 0, :].reshape(1, n_state)
        b_re_d = _block_diag(bb_re.reshape(g_c, SSM_GROUP, p_state))
        b_im_d = _block_diag(bb_im.reshape(g_c, SSM_GROUP, p_state))
        c_re_d = _block_diag(jnp.transpose(ssm_c_re[l], (0, 2, 1))).astype(BF16)
        c_im_d = _block_diag(jnp.transpose(ssm_c_im[l], (0, 2, 1))).astype(BF16)
        d_row = ssm_d[l].reshape(1, w_c)
        s5_tail = (c_re_d, c_im_d, d_row, w_glu[l], b_glu[l].reshape(1, w_c))
        assert col_uc % w_c == 0 and w_c % LANES == 0
        u_s = jnp.transpose(proj[mp:, col_uc:col_uc + w_c].reshape(dbatch, dseq, w_c),
                            (1, 0, 2)).reshape(ms, w_c)
        zeros_p = jnp.zeros((batch, n_state), F32)
        oc_p, sr_p, si_p = _s5(proj, zeros_p, zeros_p, a_re, a_im, b_re_d.astype(BF16),
                               b_im_d.astype(BF16), *s5_tail, nb=batch, seq=seq, n_blk=n_blk,
                               passes=1, col_block=col_uc // w_c)
        oc_s, sr_s, si_s = _s5(u_s, state_ssm_re[l].reshape(dbatch, n_state),
                               state_ssm_im[l].reshape(dbatch, n_state), a_re, a_im, b_re_d, b_im_d,
                               *s5_tail, nb=dbatch, seq=dseq, n_blk=n_blk, passes=3)
        oc = jnp.concatenate(
            [oc_p.reshape(mp, w_c),
             jnp.transpose(oc_s.reshape(dseq, dbatch, w_c), (1, 0, 2)).reshape(ms, w_c)], axis=0)

        x = _mm([oa, ob, oc], [w_out[l].astype(BF16)], gain=g_mix_out[l].reshape(1, -1), residual=x,
                tm=tm, tn=_pick(d, (1024, 512, 256, 128)), name="merge_out")

        outs["p_fl"].append(lf_p[:, :h_a].reshape(batch, seq, h_a))
        outs["p_sr"].append(sr_p.reshape(batch, g_c, p_state))
        outs["p_si"].append(si_p.reshape(batch, g_c, p_state))
        outs["s_fk"].append(kn_as.reshape(dbatch, dseq, h_a, dh))
        outs["s_fv"].append(proj[mp:, 2 * w_a:3 * w_a].reshape(dbatch, dseq, h_a, dh))
        outs["s_fl"].append(jnp.transpose(lf_s, (0, 2, 1)))
        outs["s_bk"].append(kn_bs.reshape(dbatch, dseq, h_b, dh))
        outs["s_bv"].append(proj[mp:, 5 * w_a:6 * w_a].reshape(dbatch, dseq, h_b, dh))
        outs["s_sr"].append(sr_s.reshape(dbatch, g_c, p_state))
        outs["s_si"].append(si_s.reshape(dbatch, g_c, p_state))

        mem2 = mem_prompt.reshape(batch * n_mem, d)
        tmm = _pick(batch * n_mem, (1024, 512, 256, 128))
        gck = jnp.tile(g_ck[l].astype(F32), h_m).reshape(1, w_m)
        gcq = jnp.tile(g_cq[l].astype(F32), h_m).reshape(1, w_m)
        mk = _mm([mem2], [w_ck[l]], gain=g_mem[l].reshape(1, d), group_gain=gck,
                 epilogue="group_norm", tm=tmm, tn=w_m, name="mem_k")
        mv = _mm([mem2], [w_cv[l]], gain=g_mem[l].reshape(1, d), tm=tmm, tn=w_m, name="mem_v")
        outs["p_mk"].append(mk.reshape(batch, n_mem, h_m, dh_m))
        outs["p_mv"].append(mv.reshape(batch, n_mem, h_m, dh_m))
        q_c = _mm([x], [w_cq[l].astype(BF16)], gain=g_cross[l].reshape(1, d), group_gain=gcq,
                  epilogue="group_norm", tm=tm, tn=w_m, name="cross_q")
        o_c = _cross_attn(q_c, mk, mv, None, batch=batch, q_len=seq, n_mem=n_mem, n_heads=h_m,
                          dh=dh_m, row0=0, m_rows=m)
        o_c = _cross_attn(q_c, cache_mem_k, cache_mem_v, o_c, batch=dbatch, q_len=dseq, n_mem=n_mem,
                          n_heads=h_m, dh=dh_m, row0=mp, m_rows=m, layer=l)
        x = _mm([o_c], [w_co[l].astype(BF16)], residual=x, tm=tm,
                tn=_pick(d, (1024, 512, 256, 128)), name="cross_out")

        i = l // 2
        if l % 2 == 0:
            hmid = _mm([x], [w_ff1[i].astype(BF16), w_ff3[i].astype(BF16)],
                       gain=g_ffn[l].reshape(1, d), epilogue="swiglu", out_dtype=BF16, tm=tm,
                       tn=_pick(w_ff1.shape[2], (512, 256, 128)), name="ffn_up")
            x = _mm([hmid], [w_ff2[i].astype(BF16)], residual=x, tm=tm,
                    tn=_pick(d, (512, 256, 128)), name="ffn_down")
        else:
            x = _moe(x, g_ffn[l].reshape(1, d), w_router[i], w_e1[i], w_e3[i], w_e2[i], tm_tok=tm,
                     split=mp if l == depth - 1 else None)

    st = lambda k: jnp.stack(outs[k])
    xp, xs = x if isinstance(x, tuple) else (x[:mp], x[mp:])
    per_head = lambda t, h: jnp.transpose(t.reshape(depth, batch, h, dh, -1), (0, 1, 4, 2, 3))
    return (xp.reshape(batch, seq, d), xs.reshape(dbatch, dseq, d),
            per_head(cache_a[0], h_a), per_head(cache_a[1], h_a), st("p_fl"),
            per_head(cache_b[0][..., seq - nband:], h_b), per_head(cache_b[1][..., seq - nband:], h_b),
            st("p_sr"), st("p_si"),
            st("p_mk"), st("p_mv"), st("s_fk"), st("s_fv"), st("s_fl"), st("s_bk"), st("s_bv"),
            st("s_sr"), st("s_si"))
```

```python
import functools
import math

import jax
import jax.numpy as jnp
import numpy as np
from jax import lax
from jax.experimental import pallas as pl
from jax.experimental.pallas import tpu as pltpu

F32 = jnp.float32
BF16 = jnp.bfloat16
I32 = jnp.int32
U32 = jnp.uint32

EPS = 1e-6
NEG_INF = -1e30
LOG2E = math.log2(math.e)
CHUNK = 64
BAND_PAST = 8
SSM_GROUP = 16
TOP_K = 2

LANES = 128
SUBLANES = 8
BF16_ROWS = 16
VMEM_CAP = 60 * 1024 * 1024


def _vmem_limit(nbytes):
    return int(min(VMEM_CAP, max(16 * 1024 * 1024, nbytes * 5 // 4 + (4 << 20))))


def _pick(n, candidates):
    for c in candidates:
        if c <= n and n % c == 0:
            return c
    raise ValueError(f"no tile for {n} in {candidates}")


def _nbytes(shape, dtype):
    return math.prod(shape) * jnp.dtype(dtype).itemsize


def _split3(x):
    hi = x.astype(BF16)
    r1 = x - hi.astype(F32)
    mid = r1.astype(BF16)
    lo = (r1 - mid.astype(F32)).astype(BF16)
    return hi, mid, lo


def _dot(a, b):
    return jnp.dot(a, b, preferred_element_type=F32)


def _dot_nt(a, b):
    return lax.dot_general(a, b, (((1,), (1,)), ((), ())), preferred_element_type=F32)


def _dot_hp(a, b, passes=3):
    ah = a.astype(BF16)
    bh = b.astype(BF16)
    if passes == 1:
        return _dot(ah, bh)
    al = (a - ah.astype(F32)).astype(BF16)
    bl = (b - bh.astype(F32)).astype(BF16)
    return _dot(ah, bh) + (_dot(ah, bl) + _dot(al, bh))


def _pair_ones():
    r = lax.broadcasted_iota(I32, (LANES, LANES), 0) // 64
    c = lax.broadcasted_iota(I32, (LANES, LANES), 1) // 64
    return (r == c).astype(BF16)


def _pair_norm(x, g, ones):
    sq = x * x
    hi = sq.astype(BF16)
    lo = (sq - hi.astype(F32)).astype(BF16)
    ss = _dot(hi, ones) + _dot(lo, ones)
    return x * lax.rsqrt(ss * (1.0 / 64.0) + EPS) * g


def _log_sigmoid(x):
    return jnp.minimum(x, 0.0) - jnp.log(1.0 + jnp.exp(-jnp.abs(x)))


def _gelu_tanh(x):
    c = math.sqrt(2.0 / math.pi)
    return 0.5 * x * (1.0 + jnp.tanh(c * (x + 0.044715 * (x * x * x))))


def _mm_body(*refs, widths, norm, n_w, epilogue, has_res, has_gg, stage_a, tm, rc, kc, k_total,
             n_main, a_tail, res_tail):
    it = iter(refs)
    a_refs = [next(it) for _ in widths]
    at_ref = next(it) if a_tail else None
    g_ref = next(it) if norm else None
    w_refs = [next(it) for _ in range(n_w)]
    gg_ref = next(it) if has_gg else None
    res_ref = next(it) if has_res else None
    rt_ref = next(it) if res_tail else None
    o_ref = next(it)
    abf = next(it) if stage_a else a_refs[0]
    cast_w = w_refs[0].dtype != BF16
    wbfs = [next(it) for _ in range(n_w)] if cast_w else w_refs
    i = pl.program_id(0)
    j = pl.program_id(1)

    if stage_a:
        def stage(srcs, when):
            @pl.when(jnp.logical_and(j == 0, when))
            def _():
                off = 0
                for a_ref, wd in zip(srcs, widths):
                    def chunk(c, carry, a_ref=a_ref, off=off, wd=wd):
                        r0 = pl.multiple_of(c * rc, rc)
                        x = a_ref[pl.ds(r0, rc), :].astype(F32)
                        if norm:
                            ms = jnp.mean(x * x, axis=-1, keepdims=True)
                            x = x * lax.rsqrt(ms + EPS) * g_ref[:, off:off + wd]
                        abf[pl.ds(r0, rc), off:off + wd] = x.astype(BF16)
                        return carry
                    lax.fori_loop(0, tm // rc, chunk, 0)
                    off += wd
        if a_tail:
            stage(a_refs, i < n_main)
            stage([at_ref], i >= n_main)
        else:
            stage(a_refs, True)

    if cast_w:
        for w_ref, wbf in zip(w_refs, wbfs):
            def cast(c, carry, w_ref=w_ref, wbf=wbf):
                r0 = pl.multiple_of(c * kc, kc)
                wbf[pl.ds(r0, kc), :] = w_ref[pl.ds(r0, kc), :].astype(BF16)
                return carry
            lax.fori_loop(0, k_total // kc, cast, 0)

    for c in range(tm // rc):
        r0 = c * rc
        a = abf[pl.ds(r0, rc), :]
        ys = [_dot(a, wbf[...]) for wbf in wbfs]
        if epilogue == "swiglu":
            y = jax.nn.silu(ys[0]) * ys[1]
        elif epilogue == "group_norm":
            parts = []
            for s in range(ys[0].shape[1] // LANES):
                ysl = ys[0][:, s * LANES:(s + 1) * LANES]
                ms = jnp.mean(ysl * ysl, axis=-1, keepdims=True)
                parts.append(ysl * lax.rsqrt(ms + EPS) * gg_ref[:, s * LANES:(s + 1) * LANES])
            y = jnp.concatenate(parts, axis=1)
        else:
            y = ys[0]
        if has_res:
            res = res_ref[pl.ds(r0, rc), :]
            if res_tail:
                res = jnp.where(i < n_main, res, rt_ref[pl.ds(r0, rc), :])
            y = y + res
        o_ref[pl.ds(r0, rc), :] = y.astype(o_ref.dtype)


def _mm(a_parts, w_list, *, gain=None, group_gain=None, residual=None, epilogue="none",
        out_dtype=F32, tm, tn, name, m_rows=None, a_tail=None, res_tail=None):
    m = a_parts[0].shape[0] if m_rows is None else m_rows
    widths = tuple(a.shape[1] for a in a_parts)
    k_total = sum(widths)
    n = w_list[0].shape[1]
    assert m % tm == 0 and n % tn == 0, (m, tm, n, tn)
    n_main = m // tm - 1 if (a_tail is not None or res_tail is not None) else m // tm
    main = lambda i: jnp.minimum(i, n_main - 1)
    norm = gain is not None
    stage_a = norm or len(a_parts) > 1 or a_parts[0].dtype != BF16
    rc = _pick(tm, (512, 384, 352, 256, 176, 128, 64, 32, 16))
    kc = _pick(k_total, (512, 256, 128))
    grid = (m // tm, n // tn)
    a_row = main if a_tail is not None else (lambda i: i)
    in_specs = [pl.BlockSpec((tm, wd), lambda i, j: (a_row(i), 0)) for wd in widths]
    args = list(a_parts)
    est = sum(2 * _nbytes((tm, wd), a.dtype) for wd, a in zip(widths, a_parts))
    if a_tail is not None:
        assert len(a_parts) == 1 and a_tail.shape == (tm, k_total) and stage_a
        in_specs.append(pl.BlockSpec((tm, k_total), lambda i, j: (0, 0),
                                     pipeline_mode=pl.Buffered(1)))
        args.append(a_tail)
        est += _nbytes((tm, k_total), a_tail.dtype)
    if norm:
        in_specs.append(pl.BlockSpec((1, k_total), lambda i, j: (0, 0)))
        args.append(gain)
    cast_w = w_list[0].dtype != BF16
    for w in w_list:
        in_specs.append(pl.BlockSpec((k_total, tn), lambda i, j: (0, j)))
        args.append(w)
        est += 2 * _nbytes((k_total, tn), w.dtype) + cast_w * _nbytes((k_total, tn), BF16)
    if group_gain is not None:
        in_specs.append(pl.BlockSpec((1, tn), lambda i, j: (0, j)))
        args.append(group_gain)
    if residual is not None:
        r_row = main if res_tail is not None else (lambda i: i)
        in_specs.append(pl.BlockSpec((tm, tn), lambda i, j: (r_row(i), j)))
        args.append(residual)
        est += 2 * _nbytes((tm, tn), F32)
        if res_tail is not None:
            assert res_tail.shape == (tm, n)
            in_specs.append(pl.BlockSpec((tm, tn), lambda i, j: (0, j)))
            args.append(res_tail)
            est += 2 * _nbytes((tm, tn), F32)
    est += 2 * _nbytes((tm, tn), out_dtype) + stage_a * _nbytes((tm, k_total), BF16)
    est += 4 * _nbytes((rc, tn), F32) * len(w_list)
    body = functools.partial(
        _mm_body, widths=widths, norm=norm, n_w=len(w_list), epilogue=epilogue,
        has_res=residual is not None, has_gg=group_gain is not None, stage_a=stage_a, tm=tm,
        rc=rc, kc=kc, k_total=k_total, n_main=n_main, a_tail=a_tail is not None,
        res_tail=res_tail is not None)
    return pl.pallas_call(
        body,
        out_shape=jax.ShapeDtypeStruct((m, n), out_dtype),
        grid=grid,
        in_specs=in_specs,
        out_specs=pl.BlockSpec((tm, tn), lambda i, j: (i, j)),
        scratch_shapes=[pltpu.VMEM((tm, k_total), BF16)] * stage_a
        + [pltpu.VMEM((k_total, tn), BF16) for _ in w_list] * cast_w,
        compiler_params=pltpu.CompilerParams(
            dimension_semantics=("parallel", "arbitrary"),
            vmem_limit_bytes=_vmem_limit(est)),
        name=name,
    )(*args)


def _regroup_body(w_ref, o_ref, *, cuts, width):
    off = 0
    for lo, hi in cuts:
        o_ref[:, off:off + hi - lo] = w_ref[:, lo:hi].astype(o_ref.dtype)
        off += hi - lo
    if off < width:
        o_ref[:, off:width] = jnp.zeros((o_ref.shape[0], width - off), o_ref.dtype)


def _regroup_columns(w, layer, cuts, width):
    _, k, n = w.shape
    tr = _pick(k, (256, 128, 64, 32, 16))
    return pl.pallas_call(
        functools.partial(_regroup_body, cuts=cuts, width=width),
        out_shape=jax.ShapeDtypeStruct((k, width), BF16),
        grid=(k // tr,),
        in_specs=[pl.BlockSpec((None, tr, n), lambda i: (layer, i, 0))],
        out_specs=pl.BlockSpec((tr, width), lambda i: (i, 0)),
        compiler_params=pltpu.CompilerParams(
            dimension_semantics=("parallel",),
            vmem_limit_bytes=_vmem_limit(2 * _nbytes((tr, n), F32) + 2 * _nbytes((tr, width), BF16)
                                         + 4 * _nbytes((tr, width), F32))),
        name="regroup_w_in",
    )(w)


def _cumsum_body(x_ref, b_ref, lf_ref, f_ref, carry, *, raw_from, valid_to, tt):
    j = pl.program_id(0)

    @pl.when(j == 0)
    def _():
        carry[...] = jnp.zeros_like(carry)

    x = x_ref[...]
    lane = j * tt + lax.broadcasted_iota(I32, x.shape, 1)
    lf = jnp.where(lane >= raw_from, _log_sigmoid(x + b_ref[:, 0:1]), x)
    lf = jnp.where(lane < valid_to, lf, 0.0)
    lf_ref[...] = lf
    tri = (lax.broadcasted_iota(I32, (tt, tt), 0)
           <= lax.broadcasted_iota(I32, (tt, tt), 1)).astype(BF16)
    hi, mid, lo = _split3(lf)
    y = _dot(hi, tri) + _dot(mid, tri) + _dot(lo, tri) + carry[:, 0:1]
    f_ref[...] = y
    carry[...] = jnp.broadcast_to(y[:, tt - 1:tt], carry.shape)


def _logf_cumsum(x, bias, *, raw_from, valid_to):
    r, t = x.shape
    tt = _pick(t, (256, 128))
    return pl.pallas_call(
        functools.partial(_cumsum_body, raw_from=raw_from, valid_to=valid_to, tt=tt),
        out_shape=(jax.ShapeDtypeStruct((r, t), F32), jax.ShapeDtypeStruct((r, t), F32)),
        grid=(t // tt,),
        in_specs=[pl.BlockSpec((r, tt), lambda j: (0, j)),
                  pl.BlockSpec((r, LANES), lambda j: (0, 0))],
        out_specs=(pl.BlockSpec((r, tt), lambda j: (0, j)),
                   pl.BlockSpec((r, tt), lambda j: (0, j))),
        scratch_shapes=[pltpu.VMEM((r, LANES), F32)],
        compiler_params=pltpu.CompilerParams(dimension_semantics=("arbitrary",)),
        name="logf_cumsum",
    )(x, bias)


def _logf_rows_body(fa_ref, b_ref, lf_ref, faug_ref, carry, *, tt, n_heads):
    j = pl.program_id(1)

    @pl.when(j == 0)
    def _():
        carry[...] = jnp.zeros_like(carry)

    lane = lax.broadcasted_iota(I32, (tt, LANES), 1)
    lf = jnp.where(lane < n_heads, _log_sigmoid(fa_ref[...] + b_ref[...]), 0.0)
    lf_ref[...] = lf
    tri = (lax.broadcasted_iota(I32, (tt, tt), 0)
           >= lax.broadcasted_iota(I32, (tt, tt), 1)).astype(BF16)
    hi, mid, lo = _split3(lf)
    f = _dot(tri, hi) + _dot(tri, mid) + _dot(tri, lo) + carry[0:1, :]
    carry[...] = jnp.broadcast_to(f[tt - 1:tt, :], carry.shape)
    width = faug_ref.shape[1]
    head = lax.broadcasted_iota(I32, (LANES, width), 0)
    col = lax.broadcasted_iota(I32, (LANES, width), 1)
    within = col % LANES
    sel = ((within < 12) & (head == 2 * (col // LANES) + (within % 6) // 3)).astype(BF16)
    hi, mid, lo = _split3(f * LOG2E)
    faug_ref[...] = _dot(hi, sel) + _dot(mid, sel) + _dot(lo, sel)


def _logf_rows(proj, bias_row, *, batch, seq, n_heads, n_pairs, col_block):
    tt = _pick(seq, (256, 128))
    nt = seq // tt
    return pl.pallas_call(
        functools.partial(_logf_rows_body, tt=tt, n_heads=n_heads),
        out_shape=(jax.ShapeDtypeStruct((batch * seq, LANES), F32),
                   jax.ShapeDtypeStruct((batch * seq, n_pairs * LANES), F32)),
        grid=(batch, nt),
        in_specs=[pl.BlockSpec((tt, LANES), lambda b, j: (b * nt + j, col_block)),
                  pl.BlockSpec((1, LANES), lambda b, j: (0, 0))],
        out_specs=(pl.BlockSpec((tt, LANES), lambda b, j: (b * nt + j, 0)),
                   pl.BlockSpec((tt, n_pairs * LANES), lambda b, j: (b * nt + j, 0))),
        scratch_shapes=[pltpu.VMEM((SUBLANES, LANES), F32)],
        compiler_params=pltpu.CompilerParams(dimension_semantics=("parallel", "arbitrary")),
        name="logf_rows",
    )(proj, bias_row)


def _stack_heads(qn):
    lane = lax.broadcasted_iota(I32, qn.shape, 1)
    q0 = jnp.where(lane < 64, qn, 0.0)
    q1 = jnp.where(lane < 64, 0.0, qn)
    return jnp.concatenate([q0, q1], axis=0).astype(BF16)


def _lane_column(block, h):
    lane = lax.broadcasted_iota(I32, block.shape, 1)
    return jnp.sum(jnp.where(lane == h, block, 0.0), axis=-1, keepdims=True)


V_ROWS = LANES + BF16_ROWS


def _split_select(x):
    hi, mid, lo = _split3(x)
    m3 = lax.broadcasted_iota(I32, x.shape, 1) % 3
    return jnp.where(m3 == 0, hi.astype(F32), jnp.where(m3 == 1, mid.astype(F32), lo.astype(F32)))


def _head_masks(shape):
    lane = lax.broadcasted_iota(I32, shape, 1)
    return lane, lane < 64


def _finish_heads(acc, tq):
    o0 = acc[0:64, 0:tq] / acc[LANES:LANES + 1, 0:tq]
    o1 = acc[64:LANES, tq:2 * tq] / acc[LANES:LANES + 1, tq:2 * tq]
    return jnp.concatenate([o0, o1], axis=0).T


def _fox_prompt_body(q_ref, k_ref, v_ref, f_ref, gq_ref, gk_ref, *rest, tq, seq, dh):
    o_ref, knt_ref, vt_ref, kaug, vt3 = rest[-5:]
    ones = _pair_ones()
    nq = seq // tq
    lane, first = _head_masks((tq, LANES))

    for c in range(nq):
        rows = slice(c * tq, (c + 1) * tq)
        kn = _pair_norm(k_ref[rows, :], gk_ref[...], ones)
        knt_ref[:, rows] = kn.T
        kaug[rows, 0:LANES] = kn.astype(BF16)
        sp = _split_select(f_ref[rows, :])
        kaug[rows, LANES:2 * LANES] = jnp.where(
            lane < 6, -sp, jnp.where(lane < 12, 1.0, 0.0)).astype(BF16)
        vt = v_ref[rows, :].T
        vt_ref[:, rows] = vt
        vt3[c, 0:LANES, :] = vt.astype(BF16)
        vt3[c, LANES:V_ROWS, :] = jnp.ones((BF16_ROWS, tq), BF16)

    causal = (lax.broadcasted_iota(I32, (tq, 2 * tq), 0)
              <= lax.broadcasted_iota(I32, (tq, 2 * tq), 1) % tq)

    for qi in range(nq):
        rows = slice(qi * tq, (qi + 1) * tq)
        qn = _pair_norm(q_ref[rows, :], gq_ref[...], ones) * (dh ** -0.5 * LOG2E)
        sp = _split_select(f_ref[rows, :])
        up0 = jnp.where(lane < 3, 1.0, jnp.where((lane >= 6) & (lane < 9), sp, 0.0))
        up1 = jnp.where((lane >= 3) & (lane < 6), 1.0,
                        jnp.where((lane >= 9) & (lane < 12), sp, 0.0))
        qs = jnp.concatenate(
            [jnp.concatenate([jnp.where(first, qn, 0.0), up0], axis=1),
             jnp.concatenate([jnp.where(first, 0.0, qn), up1], axis=1)], axis=0).astype(BF16)

        s = jnp.where(causal, _dot_nt(kaug[rows, :], qs), NEG_INF)
        m = jnp.max(s, axis=0, keepdims=True)
        acc = _dot(vt3[qi], jnp.exp2(s - m).astype(BF16))
        for kj in range(qi):
            s = _dot_nt(kaug[kj * tq:(kj + 1) * tq, :], qs)
            m_new = jnp.maximum(m, jnp.max(s, axis=0, keepdims=True))
            acc = jnp.exp2(m - m_new) * acc + _dot(vt3[kj], jnp.exp2(s - m_new).astype(BF16))
            m = m_new
        o_ref[rows, :] = _finish_heads(acc, tq)


def _layer_cache_outputs(layer, depth, batch, width, seq, prev, n_inputs, index):
    shape = jax.ShapeDtypeStruct((depth, batch, width, seq), F32)
    spec = pl.BlockSpec((None, None, LANES, seq), lambda *g: (layer, *index(*g), 0))
    extra_specs = [] if prev is None else [pl.BlockSpec(memory_space=pl.ANY)] * 2
    aliases = {} if prev is None else {n_inputs: 1, n_inputs + 1: 2}
    return (shape, shape), (spec, spec), extra_specs, aliases, (() if prev is None else tuple(prev))


def _fox_prompt(proj, f_aug, gq, gk, *, batch, seq, n_pairs, col_q, col_k, col_v, m_rows, layer,
                depth, prev):
    tq = _pick(seq, (512, 256, 128))
    nq = seq // tq
    width = n_pairs * LANES
    est = (2 * 7 * _nbytes((seq, LANES), F32)
           + _nbytes((seq, 2 * LANES), BF16) + _nbytes((V_ROWS, seq), BF16)
           + 16 * _nbytes((tq, 2 * tq), F32))
    c_shapes, c_specs, extra_specs, aliases, extra_args = _layer_cache_outputs(
        layer, depth, batch, width, seq, prev, 6, lambda b, hp: (b, hp))
    return pl.pallas_call(
        functools.partial(_fox_prompt_body, tq=tq, seq=seq, dh=64),
        out_shape=(jax.ShapeDtypeStruct((m_rows, width), F32), *c_shapes),
        grid=(batch, n_pairs),
        in_specs=[
            pl.BlockSpec((seq, LANES), lambda b, hp: (b, col_q + hp)),
            pl.BlockSpec((seq, LANES), lambda b, hp: (b, col_k + hp)),
            pl.BlockSpec((seq, LANES), lambda b, hp: (b, col_v + hp)),
            pl.BlockSpec((seq, LANES), lambda b, hp: (b, hp)),
            pl.BlockSpec((1, LANES), lambda b, hp: (0, 0)),
            pl.BlockSpec((1, LANES), lambda b, hp: (0, 0)),
            *extra_specs,
        ],
        out_specs=(pl.BlockSpec((seq, LANES), lambda b, hp: (b, hp)), *c_specs),
        input_output_aliases=aliases,
        scratch_shapes=[pltpu.VMEM((seq, 2 * LANES), BF16), pltpu.VMEM((nq, V_ROWS, tq), BF16)],
        compiler_params=pltpu.CompilerParams(
            dimension_semantics=("parallel", "parallel"),
            vmem_limit_bytes=_vmem_limit(est)),
        name="fox_prompt",
    )(proj, proj, proj, f_aug, gq, gk, *extra_args)


def _band_prompt_body(q_ref, k_ref, v_ref, tab_ref, gq_ref, gk_ref, *rest, tq, seq, past, dh):
    o_ref, knt_ref, vt_ref, kpad, vt3 = rest[-5:]
    ones = _pair_ones()
    nq = seq // tq
    npad = past // tq
    win = past + tq

    kpad[0:past, :] = jnp.zeros((past, LANES), BF16)
    for c in range(npad):
        vt3[c] = jnp.zeros((V_ROWS, tq), BF16)
    for c in range(nq):
        rows = slice(c * tq, (c + 1) * tq)
        kn = _pair_norm(k_ref[rows, :], gk_ref[...], ones)
        knt_ref[:, rows] = kn.T
        kpad[past + c * tq:past + (c + 1) * tq, :] = kn.astype(BF16)
        vt = v_ref[rows, :].T
        vt_ref[:, rows] = vt
        vt3[npad + c, 0:LANES, :] = vt.astype(BF16)
        vt3[npad + c, LANES:V_ROWS, :] = jnp.ones((BF16_ROWS, tq), BF16)

    row = lax.broadcasted_iota(I32, (win, 2 * tq), 0)

    for qi in range(nq):
        r0 = qi * tq
        qn = _pair_norm(q_ref[r0:r0 + tq, :], gq_ref[...], ones) * (dh ** -0.5 * LOG2E)
        s = _dot_nt(kpad[r0:r0 + win, :], _stack_heads(qn)) + tab_ref[...]
        if r0 < past:
            s = jnp.where(row >= past - r0, s, NEG_INF)
        m = jnp.max(s, axis=0, keepdims=True)
        p = jnp.exp2(s - m).astype(BF16)
        acc = _dot(vt3[qi], p[0:tq, :])
        for c in range(1, win // tq):
            acc = acc + _dot(vt3[qi + c], p[c * tq:(c + 1) * tq, :])
        o_ref[r0:r0 + tq, :] = _finish_heads(acc, tq)


def _band_prompt(proj, tab, gq, gk, *, batch, seq, n_pairs, col_q, col_k, col_v, tq, m_rows, layer,
                 depth, prev):
    past = BAND_PAST * CHUNK
    assert past % tq == 0 and seq % tq == 0
    nq = seq // tq
    width = n_pairs * LANES
    win = past + tq
    est = (2 * 5 * _nbytes((seq, LANES), F32)
           + _nbytes((seq + past, LANES), BF16) + _nbytes((V_ROWS, seq + past), BF16)
           + 2 * _nbytes((win, 2 * tq), F32) + 8 * _nbytes((win, 2 * tq), F32))
    c_shapes, c_specs, extra_specs, aliases, extra_args = _layer_cache_outputs(
        layer, depth, batch, width, seq, prev, 6, lambda hp, b: (b, hp))
    return pl.pallas_call(
        functools.partial(_band_prompt_body, tq=tq, seq=seq, past=past, dh=64),
        out_shape=(jax.ShapeDtypeStruct((m_rows, width), F32), *c_shapes),
        grid=(n_pairs, batch),
        in_specs=[
            pl.BlockSpec((seq, LANES), lambda hp, b: (b, col_q + hp)),
            pl.BlockSpec((seq, LANES), lambda hp, b: (b, col_k + hp)),
            pl.BlockSpec((seq, LANES), lambda hp, b: (b, col_v + hp)),
            pl.BlockSpec((None, win, 2 * tq), lambda hp, b: (hp, 0, 0)),
            pl.BlockSpec((1, LANES), lambda hp, b: (0, 0)),
            pl.BlockSpec((1, LANES), lambda hp, b: (0, 0)),
            *extra_specs,
        ],
        out_specs=(pl.BlockSpec((seq, LANES), lambda hp, b: (b, hp)), *c_specs),
        input_output_aliases=aliases,
        scratch_shapes=[pltpu.VMEM((seq + past, LANES), BF16),
                        pltpu.VMEM(((seq + past) // tq, V_ROWS, tq), BF16)],
        compiler_params=pltpu.CompilerParams(
            dimension_semantics=("parallel", "parallel"),
            vmem_limit_bytes=_vmem_limit(est)),
        name="band_prompt",
    )(proj, proj, proj, tab, gq, gk, *extra_args)


def _sample_attn_body(*refs, mode, n_pairs, s_new, n_cache, dh):
    if mode == "fox":
        (q_ref, k_ref, v_ref, ck_ref, cv_ref, fq_ref, ft_ref, gq_ref, gk_ref, _alias,
         o_ref, kn_ref) = refs
    else:
        (q_ref, k_ref, v_ref, ck_ref, cv_ref, tabc_ref, tabn_ref, gq_ref, gk_ref, _alias,
         o_ref, kn_ref) = refs
    ones = _pair_ones()
    row = lax.broadcasted_iota(I32, (s_new, s_new), 0)
    col = lax.broadcasted_iota(I32, (s_new, s_new), 1)
    for hp in range(n_pairs):
        sl = slice(hp * LANES, (hp + 1) * LANES)
        qn = _pair_norm(q_ref[:, sl], gq_ref[...], ones) * (dh ** -0.5)
        kn = _pair_norm(k_ref[:, sl], gk_ref[...], ones)
        kn_ref[:, sl] = kn
        vn = v_ref[:, sl]
        outs = []
        for e in range(2):
            h = 2 * hp + e
            hl = slice(e * dh, (e + 1) * dh)
            qh = qn[:, hl].astype(BF16)
            sc = _dot(qh, ck_ref[h].astype(BF16))
            sn = _dot_nt(qh, kn[:, hl].astype(BF16))
            if mode == "fox":
                fq = _lane_column(fq_ref[...], h)
                fk = ft_ref[h:h + 1, :]
                sc = sc + fq - fk[:, :n_cache]
                sn = jnp.where(col <= row, sn + fq - fk[:, n_cache:n_cache + s_new], NEG_INF)
            else:
                sc = sc + tabc_ref[h]
                sn = sn + tabn_ref[h]
            m = jnp.maximum(jnp.max(sc, axis=-1, keepdims=True),
                            jnp.max(sn, axis=-1, keepdims=True))
            pc = jnp.exp(sc - m)
            pn = jnp.exp(sn - m)
            l = jnp.sum(pc, axis=-1, keepdims=True) + jnp.sum(pn, axis=-1, keepdims=True)
            outs.append((_dot_nt(pc.astype(BF16), cv_ref[h].astype(BF16))
                         + _dot(pn.astype(BF16), vn[:, hl].astype(BF16))) / l)
        o_ref[:, sl] = jnp.concatenate(outs, axis=1)


def _sample_attn(proj, cache_kt, cache_vt, layer, extra_a, extra_b, gq, gk, o_buf, *, mode, batch,
                 s_new, n_pairs, col_q, row0):
    width = n_pairs * LANES
    _, _, n_heads, dh, n_cache = cache_kt.shape
    rb = row0 // s_new
    if mode == "fox":
        ex_specs = [pl.BlockSpec((s_new, LANES), lambda b: (b, 0)),
                    pl.BlockSpec((None, 16, extra_b.shape[2]), lambda b: (b, 0, 0))]
    else:
        ex_specs = [pl.BlockSpec(extra_a.shape, lambda b: (0, 0, 0)),
                    pl.BlockSpec(extra_b.shape, lambda b: (0, 0, 0))]
    est = (2 * 2 * _nbytes((n_cache, width), F32) + 8 * _nbytes((s_new, width), F32)
           + 2 * _nbytes(extra_a.shape, F32) + 16 * _nbytes((2 * s_new, n_cache), F32)
           + 4 * _nbytes((n_cache, LANES), BF16))
    return pl.pallas_call(
        functools.partial(_sample_attn_body, mode=mode, n_pairs=n_pairs, s_new=s_new,
                          n_cache=n_cache, dh=dh),
        out_shape=(jax.ShapeDtypeStruct(o_buf.shape, F32),
                   jax.ShapeDtypeStruct((batch * s_new, width), F32)),
        grid=(batch,),
        in_specs=[
            pl.BlockSpec((s_new, width), lambda b: (rb + b, col_q)),
            pl.BlockSpec((s_new, width), lambda b: (rb + b, col_q + 1)),
            pl.BlockSpec((s_new, width), lambda b: (rb + b, col_q + 2)),
            pl.BlockSpec((None, None, n_heads, dh, n_cache), lambda b: (layer, b, 0, 0, 0)),
            pl.BlockSpec((None, None, n_heads, dh, n_cache), lambda b: (layer, b, 0, 0, 0)),
            *ex_specs,
            pl.BlockSpec((1, LANES), lambda b: (0, 0)),
            pl.BlockSpec((1, LANES), lambda b: (0, 0)),
            pl.BlockSpec(memory_space=pl.ANY),
        ],
        out_specs=(pl.BlockSpec((s_new, width), lambda b: (rb + b, 0)),
                   pl.BlockSpec((s_new, width), lambda b: (b, 0))),
        input_output_aliases={9: 0},
        compiler_params=pltpu.CompilerParams(
            dimension_semantics=("parallel",), vmem_limit_bytes=_vmem_limit(est)),
        name=f"{mode}_sample",
    )(proj, proj, proj, cache_kt, cache_vt, extra_a, extra_b, gq, gk, o_buf)


def _cross_body(*refs, n_heads, dh, aliased):
    if aliased:
        q_ref, k_ref, v_ref, _alias, o_ref = refs
    else:
        q_ref, k_ref, v_ref, o_ref = refs
    per_head = len(k_ref.shape) == 3
    for h in range(n_heads):
        sl = slice(h * dh, (h + 1) * dh)
        kh = k_ref[:, h, :] if per_head else k_ref[:, sl]
        vh = v_ref[:, h, :] if per_head else v_ref[:, sl]
        q = (q_ref[:, sl] * (dh ** -0.5)).astype(BF16)
        s = _dot_nt(q, kh.astype(BF16))
        m = jnp.max(s, axis=-1, keepdims=True)
        p = jnp.exp(s - m)
        l = jnp.sum(p, axis=-1, keepdims=True)
        o = _dot(p.astype(BF16), vh.astype(BF16)) / l
        o_ref[:, sl] = o.astype(o_ref.dtype)


def _cross_attn(q_all, k, v, o_buf, *, batch, q_len, n_mem, n_heads, dh, row0, m_rows, layer=None):
    width = n_heads * dh
    tq = _pick(q_len, (512, 256, 128, 64, 32, 16))
    nq = q_len // tq
    rb = row0 // tq
    aliased = o_buf is not None
    if layer is None:
        kv_spec = pl.BlockSpec((n_mem, width), lambda b, qi: (b, 0))
    else:
        kv_spec = pl.BlockSpec((None, None, n_mem, n_heads, dh), lambda b, qi: (layer, b, 0, 0, 0))
    in_specs = [pl.BlockSpec((tq, width), lambda b, qi: (rb + b * nq + qi, 0)), kv_spec, kv_spec]
    args = [q_all, k, v]
    if aliased:
        in_specs.append(pl.BlockSpec(memory_space=pl.ANY))
        args.append(o_buf)
    est = (2 * _nbytes((tq, width), F32) + 4 * _nbytes((n_mem, width), F32)
           + 2 * _nbytes((tq, width), BF16) + 12 * _nbytes((tq, n_mem), F32))
    return pl.pallas_call(
        functools.partial(_cross_body, n_heads=n_heads, dh=dh, aliased=aliased),
        out_shape=jax.ShapeDtypeStruct((m_rows, width), BF16),
        grid=(batch, nq),
        in_specs=in_specs,
        out_specs=pl.BlockSpec((tq, width), lambda b, qi: (rb + b * nq + qi, 0)),
        input_output_aliases={3: 0} if aliased else {},
        compiler_params=pltpu.CompilerParams(
            dimension_semantics=("parallel", "arbitrary"), vmem_limit_bytes=_vmem_limit(est)),
        name="cross_attn",
    )(*args)


def _s5_prep_body(lr_ref, li_ref, ldt_ref, br_ref, bi_ref, ar_ref, ai_ref, bbr_ref, bbi_ref):
    lr = lr_ref[...]
    li = li_ref[...]
    dt = jnp.exp(ldt_ref[...])
    mag = jnp.exp(lr * dt)
    a_re = mag * jnp.cos(li * dt)
    a_im = mag * jnp.sin(li * dt)
    den = lr * lr + li * li
    num_re = a_re - 1.0
    coef_re = (num_re * lr + a_im * li) / den
    coef_im = (a_im * lr - num_re * li) / den
    br = br_ref[...]
    bi = bi_ref[...]
    ar_ref[...] = a_re
    ai_ref[...] = a_im
    bbr_ref[...] = coef_re * br - coef_im * bi
    bbi_ref[...] = coef_re * bi + coef_im * br


def _s5_prep(lam_re, lam_im, log_dt, b_re, b_im):
    shape = lam_re.shape
    spec = pl.BlockSpec(shape, lambda: (0, 0))
    return pl.pallas_call(
        _s5_prep_body,
        out_shape=tuple(jax.ShapeDtypeStruct(shape, F32) for _ in range(4)),
        in_specs=[spec] * 5,
        out_specs=tuple([spec] * 4),
        name="s5_discretise",
    )(lam_re, lam_im, log_dt, b_re, b_im)


def _s5_body(*refs, nb, t_chunk, n_blk, passes, per_stream):
    n_u = nb if per_stream else 1
    u_refs = refs[:n_u]
    (x0r_ref, x0i_ref, ar_ref, ai_ref, bre_ref, bim_ref, cre_ref, cim_ref, d_ref, wg_ref, bg_ref,
     o_ref, xr_out, xi_out, bur, bui, st_r, st_i) = refs[n_u:n_u + 18]
    i = pl.program_id(0)
    wc = u_refs[0].shape[1]
    ns = bur.shape[1]
    ub = wc // n_blk
    sb = ns // n_blk

    @pl.when(i == 0)
    def _():
        st_r[...] = x0r_ref[...]
        st_i[...] = x0i_ref[...]

    if per_stream:
        slab = refs[n_u + 18]
        for b in range(nb):
            for k in range(wc // LANES):
                slab[k, pl.ds(b, t_chunk, stride=nb), :] = u_refs[b][:, k * LANES:(k + 1) * LANES]
        u = jnp.concatenate([slab[k] for k in range(wc // LANES)], axis=1)
    else:
        u = u_refs[0][...]
    rows = nb * t_chunk
    n_half = 2 if (t_chunk % 2 == 0 and rows >= 1024) else 1
    hr = rows // n_half
    for h in range(n_half):
        for k in range(n_blk):
            uk = u[h * hr:(h + 1) * hr, k * ub:(k + 1) * ub]
            bur[h * hr:(h + 1) * hr, k * sb:(k + 1) * sb] = _dot_hp(
                uk, bre_ref[k * ub:(k + 1) * ub, k * sb:(k + 1) * sb], passes)
            bui[h * hr:(h + 1) * hr, k * sb:(k + 1) * sb] = _dot_hp(
                uk, bim_ref[k * ub:(k + 1) * ub, k * sb:(k + 1) * sb], passes)

    a_re = ar_ref[...]
    a_im = ai_ref[...]
    xr, xi = st_r[...], st_i[...]
    for t in range(t_chunk):
        r0 = t * nb
        xr, xi = (a_re * xr - a_im * xi + bur[r0:r0 + nb, :],
                  a_re * xi + a_im * xr + bui[r0:r0 + nb, :])
        bur[r0:r0 + nb, :] = xr
        bui[r0:r0 + nb, :] = xi
    st_r[...] = xr
    st_i[...] = xi
    xr_out[...] = xr
    xi_out[...] = xi
    outs = []
    for h in range(n_half):
        ys = []
        for k in range(n_blk):
            xrk = bur[h * hr:(h + 1) * hr, k * sb:(k + 1) * sb].astype(BF16)
            xik = bui[h * hr:(h + 1) * hr, k * sb:(k + 1) * sb].astype(BF16)
            ys.append(_dot(xrk, cre_ref[k * sb:(k + 1) * sb, k * ub:(k + 1) * ub].astype(BF16))
                      - _dot(xik, cim_ref[k * sb:(k + 1) * sb, k * ub:(k + 1) * ub].astype(BF16)))
        y = jnp.concatenate(ys, axis=1) + d_ref[...] * u[h * hr:(h + 1) * hr, :]
        z = _gelu_tanh(y)
        gate = _dot(z.astype(BF16), wg_ref[...].astype(BF16)) + bg_ref[...]
        outs.append(z * jax.nn.sigmoid(gate))
    out = outs[0] if n_half == 1 else jnp.concatenate(outs, axis=0)
    if per_stream:
        for k in range(wc // LANES):
            slab[k] = out[:, k * LANES:(k + 1) * LANES]
        for b in range(nb):
            o_ref[b] = jnp.concatenate(
                [slab[k, pl.ds(b, t_chunk, stride=nb), :] for k in range(wc // LANES)], axis=1)
    else:
        o_ref[...] = out


def _s5(u, x0r, x0i, a_re, a_im, b_re, b_im, c_re, c_im, d, w_glu, b_glu, *, nb, seq, n_blk,
        passes, col_block=None):
    per_stream = col_block is not None
    wc, ns = b_re.shape
    t_chunk = _pick(seq, (128, 64, 32, 16) if nb <= SUBLANES else (16,))
    nc = seq // t_chunk
    rows = nb * t_chunk
    full = lambda shape: pl.BlockSpec(shape, lambda i: (0,) * len(shape))
    est = (6 * _nbytes((rows, wc), F32) + 2 * _nbytes((rows, ns), F32)
           + 2 * 2 * _nbytes((wc, ns), b_re.dtype) + 2 * 2 * _nbytes((wc, ns), c_re.dtype)
           + 2 * _nbytes((wc, wc), F32)
           + 6 * _nbytes((rows // 2, ns // n_blk), F32) + 8 * _nbytes((nb, ns), F32))
    if per_stream:
        u_specs = [pl.BlockSpec((t_chunk, wc), lambda i, s=s: (s * nc + i, col_block))
                   for s in range(nb)]
        u_args = [u] * nb
        o_shape = jax.ShapeDtypeStruct((nb, seq, wc), F32)
        o_spec = pl.BlockSpec((nb, t_chunk, wc), lambda i: (0, i, 0))
        slab = [pltpu.VMEM((wc // LANES, rows, LANES), F32)]
    else:
        u_specs = [pl.BlockSpec((rows, wc), lambda i: (i, 0))]
        u_args = [u]
        o_shape = jax.ShapeDtypeStruct((seq * nb, wc), F32)
        o_spec = pl.BlockSpec((rows, wc), lambda i: (i, 0))
        slab = []
    return pl.pallas_call(
        functools.partial(_s5_body, nb=nb, t_chunk=t_chunk, n_blk=n_blk, passes=passes,
                          per_stream=per_stream),
        out_shape=(o_shape, jax.ShapeDtypeStruct((nb, ns), F32), jax.ShapeDtypeStruct((nb, ns), F32)),
        grid=(nc,),
        in_specs=[*u_specs,
                  full((nb, ns)), full((nb, ns)), full((1, ns)), full((1, ns)),
                  full((wc, ns)), full((wc, ns)), full((ns, wc)), full((ns, wc)),
                  full((1, wc)), full((wc, wc)), full((1, wc))],
        out_specs=(o_spec, full((nb, ns)), full((nb, ns))),
        scratch_shapes=[pltpu.VMEM((rows, ns), F32), pltpu.VMEM((rows, ns), F32),
                        pltpu.VMEM((nb, ns), F32), pltpu.VMEM((nb, ns), F32), *slab],
        compiler_params=pltpu.CompilerParams(
            dimension_semantics=("arbitrary",), vmem_limit_bytes=_vmem_limit(est)),
        name="s5_scan",
    )(*u_args, x0r, x0i, a_re, a_im, b_re, b_im, c_re, c_im, d, w_glu, b_glu)


def _router_body(x_ref, g_ref, wr_ref, pk_ref, idx_ref, gate_ref, *, n_exp, rc):
    tm, d = x_ref.shape
    half = d // 2

    for c in range(tm // rc):
        r0 = c * rc
        x = x_ref[pl.ds(r0, rc), :]
        ms = jnp.mean(x * x, axis=-1, keepdims=True)
        h = x * lax.rsqrt(ms + EPS) * g_ref[...]
        hb = h.astype(BF16).astype(F32)
        lo = lax.shift_right_logical(pltpu.bitcast(hb[:, :half], U32), jnp.uint32(16))
        hi = pltpu.bitcast(hb[:, half:], U32) & jnp.uint32(0xFFFF0000)
        pk_ref[pl.ds(r0, rc), :] = lo | hi
        logits = _dot_hp(h, wr_ref[...])
        lane = lax.broadcasted_iota(I32, logits.shape, 1)
        logits = jnp.where(lane < n_exp, logits, NEG_INF)
        mx = jnp.max(logits, axis=-1, keepdims=True)
        e = jnp.exp(logits - mx)
        probs = e / jnp.sum(e, axis=-1, keepdims=True)
        probs = jnp.where(lane < n_exp, probs, -1.0)
        lane_f = lane.astype(F32)
        p1 = jnp.max(probs, axis=-1, keepdims=True)
        i1 = jnp.min(jnp.where(probs == p1, lane_f, float(LANES)), axis=-1, keepdims=True)
        rest = jnp.where(lane_f == i1, -1.0, probs)
        p2 = jnp.max(rest, axis=-1, keepdims=True)
        i2 = jnp.min(jnp.where(rest == p2, lane_f, float(LANES)), axis=-1, keepdims=True)
        tot = p1 + p2
        idx_ref[pl.ds(r0, rc), :] = jnp.where(lane == 0, i1, jnp.where(lane == 1, i2, 0.0)).astype(I32)
        gate_ref[pl.ds(r0, rc), :] = jnp.where(lane == 0, p1 / tot,
                                                jnp.where(lane == 1, p2 / tot, 0.0))


def _router(x, g, w_router_pad, *, n_exp, tm):
    m, d = x.shape
    rc = _pick(tm, (256, 176, 128, 64, 32, 16, 8))
    est = 2 * _nbytes((tm, d), F32) + 2 * _nbytes((tm, d // 2), U32) + 16 * _nbytes((rc, d), F32)
    return pl.pallas_call(
        functools.partial(_router_body, n_exp=n_exp, rc=rc),
        out_shape=(jax.ShapeDtypeStruct((m, d // 2), U32),
                   jax.ShapeDtypeStruct((m, LANES), I32),
                   jax.ShapeDtypeStruct((m, LANES), F32)),
        grid=(m // tm,),
        in_specs=[pl.BlockSpec((tm, d), lambda i: (i, 0)),
                  pl.BlockSpec((1, d), lambda i: (0, 0)),
                  pl.BlockSpec((d, LANES), lambda i: (0, 0))],
        out_specs=(pl.BlockSpec((tm, d // 2), lambda i: (i, 0)),
                   pl.BlockSpec((tm, LANES), lambda i: (i, 0)),
                   pl.BlockSpec((tm, LANES), lambda i: (i, 0))),
        compiler_params=pltpu.CompilerParams(
            dimension_semantics=("parallel",), vmem_limit_bytes=_vmem_limit(est)),
        name="moe_router",
    )(x, g, w_router_pad)


GATHER_UNROLL = 8


def _row_copy(src_hbm, row, dst, r, sem):
    return pltpu.make_async_copy(src_hbm.at[pl.ds(row, 1), :], dst.at[pl.ds(r, 1), :], sem)


def _dispatch_body(nt_ref, tok_ref, nxt_ref, pk_hbm, a_ref, buf, sem, *, tm):
    i = pl.program_id(0)
    nt = nt_ref[0]
    half = buf.shape[2]

    def request(ids_ref, slot):
        def issue(c, carry):
            for u in range(GATHER_UNROLL):
                r = c * GATHER_UNROLL + u
                _row_copy(pk_hbm, ids_ref[0, 0, r], buf.at[slot], r, sem.at[slot]).start(
                    priority=u % 2)
            return carry
        lax.fori_loop(0, tm // GATHER_UNROLL, issue, 0)

    @pl.when(i == 0)
    def _():
        request(tok_ref, 0)

    for slot in range(2):
        @pl.when(jnp.logical_and(i + 1 < nt, (i + 1) % 2 == slot))
        def _(slot=slot):
            request(nxt_ref, slot)

    for slot in range(2):
        @pl.when(jnp.logical_and(i < nt, i % 2 == slot))
        def _(slot=slot):
            pltpu.make_async_copy(pk_hbm.at[pl.ds(0, tm), :], buf.at[slot], sem.at[slot]).wait()
            pk = buf[slot]
            lo = pltpu.bitcast(lax.shift_left(pk, jnp.uint32(16)), F32)
            hi = pltpu.bitcast(pk & jnp.uint32(0xFFFF0000), F32)
            a_ref[:, :half] = lo.astype(BF16)
            a_ref[:, half:] = hi.astype(BF16)


def _dispatch(n_tiles, tok_sorted, packed, *, tm, r_max):
    m, half = packed.shape
    t_max = r_max // tm
    assert tm % GATHER_UNROLL == 0
    tok_tiles = tok_sorted.reshape(t_max, 1, tm)
    return pl.pallas_call(
        functools.partial(_dispatch_body, tm=tm),
        out_shape=jax.ShapeDtypeStruct((r_max, 2 * half), BF16),
        grid_spec=pltpu.PrefetchScalarGridSpec(
            num_scalar_prefetch=1,
            grid=(t_max,),
            in_specs=[pl.BlockSpec((1, 1, tm), lambda i, nt: (i, 0, 0), memory_space=pltpu.SMEM),
                      pl.BlockSpec((1, 1, tm), lambda i, nt: (jnp.minimum(i + 1, t_max - 1), 0, 0),
                                   memory_space=pltpu.SMEM),
                      pl.BlockSpec(memory_space=pl.ANY)],
            out_specs=pl.BlockSpec((tm, 2 * half), lambda i, nt: (jnp.minimum(i, nt[0] - 1), 0)),
            scratch_shapes=[pltpu.VMEM((2, tm, half), U32), pltpu.SemaphoreType.DMA((2,))]),
        compiler_params=pltpu.CompilerParams(dimension_semantics=("arbitrary",)),
        name="moe_dispatch",
    )(n_tiles, tok_tiles, tok_tiles, packed)


def _expert_up_body(te_ref, nt_ref, a_ref, w1_ref, w3_ref, h_ref, w1bf, w3bf, *, kc, rem):
    f = pl.program_id(0)
    i = pl.program_id(1)
    last = pl.num_programs(0) - 1
    d, tf = w1bf.shape
    fresh = jnp.logical_or(i == 0, te_ref[i] != te_ref[jnp.maximum(i - 1, 0)])
    live = i < nt_ref[0]

    def run(cols, when):
        @pl.when(jnp.logical_and(when, jnp.logical_and(fresh, live)))
        def _():
            def cast(c, carry):
                r0 = pl.multiple_of(c * kc, kc)
                w1bf[pl.ds(r0, kc), 0:cols] = w1_ref[pl.ds(r0, kc), 0:cols].astype(BF16)
                w3bf[pl.ds(r0, kc), 0:cols] = w3_ref[pl.ds(r0, kc), 0:cols].astype(BF16)
                return carry
            lax.fori_loop(0, d // kc, cast, 0)

        @pl.when(jnp.logical_and(when, live))
        def _():
            a = a_ref[...]
            h_ref[:, 0:cols] = (jax.nn.silu(_dot(a, w1bf[:, 0:cols]))
                                * _dot(a, w3bf[:, 0:cols])).astype(h_ref.dtype)

    if rem:
        run(tf, f < last)
        run(rem, f == last)
    else:
        run(tf, True)


def _expert_up(tile_expert, n_tiles, a_sorted, w1, w3, *, tm, tf):
    r_max, d = a_sorted.shape
    n_exp, _, fe = w1.shape
    t_max = r_max // tm
    kc = _pick(d, (512, 256, 128))
    row = lambda f, i, te, nt: jnp.minimum(i, nt[0] - 1)
    est = (2 * _nbytes((tm, d), BF16) + 2 * 2 * _nbytes((d, tf), F32) + 2 * _nbytes((d, tf), BF16)
           + 2 * _nbytes((tm, tf), BF16) + 6 * _nbytes((tm, tf), F32))
    return pl.pallas_call(
        functools.partial(_expert_up_body, kc=kc, rem=fe % tf),
        out_shape=jax.ShapeDtypeStruct((r_max, fe), BF16),
        grid_spec=pltpu.PrefetchScalarGridSpec(
            num_scalar_prefetch=2,
            grid=(pl.cdiv(fe, tf), t_max),
            in_specs=[pl.BlockSpec((tm, d), lambda f, i, te, nt: (row(f, i, te, nt), 0)),
                      pl.BlockSpec((None, d, tf), lambda f, i, te, nt: (te[i], 0, f)),
                      pl.BlockSpec((None, d, tf), lambda f, i, te, nt: (te[i], 0, f))],
            out_specs=pl.BlockSpec((tm, tf), lambda f, i, te, nt: (row(f, i, te, nt), f)),
            scratch_shapes=[pltpu.VMEM((d, tf), BF16), pltpu.VMEM((d, tf), BF16)]),
        compiler_params=pltpu.CompilerParams(
            dimension_semantics=("arbitrary", "arbitrary"), vmem_limit_bytes=_vmem_limit(est)),
        name="moe_expert_up",
    )(tile_expert, n_tiles, a_sorted, w1, w3)


def _expert_down_body(te_ref, nt_ref, h_ref, w2_ref, y_ref, w2bf, *, kc):
    i = pl.program_id(1)
    fe = h_ref.shape[1]
    fresh = jnp.logical_or(i == 0, te_ref[i] != te_ref[jnp.maximum(i - 1, 0)])

    @pl.when(jnp.logical_and(fresh, i < nt_ref[0]))
    def _():
        def cast(c, carry):
            r0 = pl.multiple_of(c * kc, kc)
            w2bf[pl.ds(r0, kc), :] = w2_ref[pl.ds(r0, kc), :].astype(BF16)
            return carry
        lax.fori_loop(0, fe // kc, cast, 0)

    @pl.when(i < nt_ref[0])
    def _():
        y_ref[...] = _dot(h_ref[...], w2bf[...])


def _expert_down(tile_expert, n_tiles, h_sorted, w2, *, tm, tn):
    r_max, fe = h_sorted.shape
    d = w2.shape[2]
    t_max = r_max // tm
    kc = _pick(fe, (512, 256, 128))
    row = lambda n, i, te, nt: jnp.minimum(i, nt[0] - 1)
    est = (2 * _nbytes((tm, fe), BF16) + 2 * _nbytes((fe, tn), F32) + _nbytes((fe, tn), BF16)
           + 4 * _nbytes((tm, tn), F32))
    return pl.pallas_call(
        functools.partial(_expert_down_body, kc=kc),
        out_shape=jax.ShapeDtypeStruct((r_max, d), F32),
        grid_spec=pltpu.PrefetchScalarGridSpec(
            num_scalar_prefetch=2,
            grid=(d // tn, t_max),
            in_specs=[pl.BlockSpec((tm, fe), lambda n, i, te, nt: (row(n, i, te, nt), 0)),
                      pl.BlockSpec((None, fe, tn), lambda n, i, te, nt: (te[i], 0, n))],
            out_specs=pl.BlockSpec((tm, tn), lambda n, i, te, nt: (row(n, i, te, nt), n)),
            scratch_shapes=[pltpu.VMEM((fe, tn), BF16)]),
        compiler_params=pltpu.CompilerParams(
            dimension_semantics=("arbitrary", "arbitrary"), vmem_limit_bytes=_vmem_limit(est)),
        name="moe_expert_down",
    )(tile_expert, n_tiles, h_sorted, w2)


def _combine_body(slot_ref, nxt_ref, x_ref, gate_ref, y_hbm, *rest, tc, n_first):
    o_refs, (buf, sem) = rest[:-2], rest[-2:]
    i = pl.program_id(0)
    n = pl.num_programs(0)
    unroll = GATHER_UNROLL // TOP_K

    def request(ids_ref, slot):
        def issue(c, carry):
            for u in range(unroll):
                r = c * unroll + u
                for k in range(TOP_K):
                    _row_copy(y_hbm, ids_ref[0, 0, TOP_K * r + k], buf.at[slot, k], r,
                              sem.at[slot]).start(priority=k % 2)
            return carry
        lax.fori_loop(0, tc // unroll, issue, 0)

    @pl.when(i == 0)
    def _():
        request(slot_ref, 0)

    for s in range(2):
        @pl.when(jnp.logical_and(i + 1 < n, (i + 1) % 2 == s))
        def _(s=s):
            request(nxt_ref, s)

    slot = i % 2
    for k in range(TOP_K):
        pltpu.make_async_copy(y_hbm.at[pl.ds(0, tc), :], buf.at[slot, k], sem.at[slot]).wait()
    g = gate_ref[...]
    val = x_ref[...] + (g[:, 0:1] * buf[slot, 0] + g[:, 1:2] * buf[slot, 1])
    if len(o_refs) == 1:
        o_refs[0][...] = val
    else:
        @pl.when(i < n_first)
        def _():
            o_refs[0][...] = val

        @pl.when(i >= n_first)
        def _():
            o_refs[1][...] = val


def _combine(slots, x, gates, y_sorted, *, tc, split=None):
    m, d = x.shape
    n = m // tc
    assert tc % (GATHER_UNROLL // TOP_K) == 0
    est = 4 * _nbytes((tc, d), F32) + 2 * TOP_K * _nbytes((tc, d), F32) + 6 * _nbytes((tc, d), F32)
    slot_tiles = slots.reshape(n, 1, TOP_K * tc)
    if split is None or split % tc or (m - split) % tc:
        n_first = n
        out_shape = jax.ShapeDtypeStruct((m, d), F32)
        out_specs = pl.BlockSpec((tc, d), lambda i: (i, 0))
    else:
        n_first = split // tc
        out_shape = (jax.ShapeDtypeStruct((split, d), F32), jax.ShapeDtypeStruct((m - split, d), F32))
        out_specs = (pl.BlockSpec((tc, d), lambda i: (jnp.minimum(i, n_first - 1), 0)),
                     pl.BlockSpec((tc, d), lambda i: (jnp.maximum(i - n_first, 0), 0)))
    return pl.pallas_call(
        functools.partial(_combine_body, tc=tc, n_first=n_first),
        out_shape=out_shape,
        grid=(n,),
        in_specs=[pl.BlockSpec((1, 1, TOP_K * tc), lambda i: (i, 0, 0), memory_space=pltpu.SMEM),
                  pl.BlockSpec((1, 1, TOP_K * tc), lambda i: (jnp.minimum(i + 1, n - 1), 0, 0),
                               memory_space=pltpu.SMEM),
                  pl.BlockSpec((tc, d), lambda i: (i, 0)),
                  pl.BlockSpec((tc, LANES), lambda i: (i, 0)),
                  pl.BlockSpec(memory_space=pl.ANY)],
        out_specs=out_specs,
        scratch_shapes=[pltpu.VMEM((2, TOP_K, tc, d), F32), pltpu.SemaphoreType.DMA((2,))],
        compiler_params=pltpu.CompilerParams(
            dimension_semantics=("arbitrary",), vmem_limit_bytes=_vmem_limit(est)),
        name="moe_combine",
    )(slot_tiles, slot_tiles, x, gates, y_sorted)


def _moe(x, g_ffn, w_router, w_e1, w_e3, w_e2, *, tm_tok, split=None):
    m, d = x.shape
    n_exp, _, fe = w_e1.shape
    tm = _pick(m * TOP_K, (512, 256, 128, 64, 32, 16))
    wr = jnp.zeros((d, LANES), F32).at[:, :n_exp].set(w_router)
    packed, idx128, gate128 = _router(x, g_ffn, wr, n_exp=n_exp, tm=tm_tok)

    idx = idx128[:, :TOP_K]
    mask = jnp.sum(idx[:, :, None] == jnp.arange(n_exp, dtype=I32)[None, None, :], axis=1).astype(I32)
    counts = jnp.sum(mask, axis=0)
    padded = ((counts + tm - 1) // tm) * tm
    ends = jnp.cumsum(padded)
    starts = ends - padded
    pos = jnp.cumsum(mask, axis=0) - mask
    slot = starts[idx] + jnp.take_along_axis(pos, idx, axis=1)
    t_max = (m * TOP_K) // tm + n_exp
    r_max = t_max * tm
    tok_sorted = jnp.zeros((r_max,), I32).at[slot.reshape(-1)].set(
        jnp.repeat(jnp.arange(m, dtype=I32), TOP_K))
    n_tiles = (ends[-1] // tm).astype(I32).reshape(1)
    tile_start = jnp.minimum(jnp.arange(t_max, dtype=I32), n_tiles[0] - 1) * tm
    tile_expert = jnp.minimum(jnp.sum(ends[None, :] <= tile_start[:, None], axis=1),
                              n_exp - 1).astype(I32)

    a_sorted = _dispatch(n_tiles, tok_sorted, packed, tm=tm, r_max=r_max)
    th = _pick(fe, (256, 128))
    tf = 2 * th if fe >= 2 * th else th
    h_sorted = _expert_up(tile_expert, n_tiles, a_sorted, w_e1, w_e3, tm=tm, tf=tf)
    y_sorted = _expert_down(tile_expert, n_tiles, h_sorted, w_e2, tm=tm,
                            tn=_pick(d, (1024, 512, 256, 128)))
    tc = _pick(m, (256, 128, 64, 32, 16, 8))
    return _combine(slot, x, gate128, y_sorted, tc=tc, split=split)


def _block_diag(blocks):
    g, r, c = blocks.shape
    eye = jnp.eye(g, dtype=blocks.dtype)
    return (blocks[:, :, None, :] * eye[:, None, :, None]).reshape(g * r, g * c)


def _rel_table(rel_bias_l, q_pos, k_pos, max_rel):
    rel = np.clip(q_pos[:, None] - k_pos[None, :], -(CHUNK - 1), max_rel) + (CHUNK - 1)
    return rel_bias_l[:, rel].astype(F32)


def _band_table(rel_bias_l, tq, max_rel):
    n_heads = rel_bias_l.shape[0]
    past = BAND_PAST * CHUNK
    win = past + tq
    ring = -(-(tq + win - 1) // LANES) * LANES
    diff = np.arange(ring)
    diff = np.where(diff < tq, diff, diff - ring)
    idx = np.clip(diff + past, -(CHUNK - 1), max_rel) + (CHUNK - 1)
    by_diff = jnp.pad(rel_bias_l.astype(F32)[:, idx], ((0, 16 - n_heads), (0, 0)))

    def body(u_ref, o_ref):
        hp = pl.program_id(0)
        j = lax.broadcasted_iota(I32, (win, tq), 0)
        i = lax.broadcasted_iota(I32, (win, tq), 1)
        gap = (i + past) // CHUNK - j // CHUNK
        valid = (gap >= 0) & (gap <= BAND_PAST)
        for e in range(2):
            rows = jnp.broadcast_to(u_ref[pl.ds(2 * hp + e, 1), :], (win, ring))
            shifted = pltpu.roll(rows, 0, 1, stride=1, stride_axis=0)
            o_ref[:, e * tq:(e + 1) * tq] = jnp.where(valid, shifted[:, 0:tq] * LOG2E, NEG_INF)

    return pl.pallas_call(
        body,
        out_shape=jax.ShapeDtypeStruct((n_heads // 2, win, 2 * tq), F32),
        grid=(n_heads // 2,),
        in_specs=[pl.BlockSpec((16, ring), lambda hp: (0, 0))],
        out_specs=pl.BlockSpec((None, win, 2 * tq), lambda hp: (hp, 0, 0)),
        compiler_params=pltpu.CompilerParams(dimension_semantics=("parallel",)),
        name="band_table",
    )(by_diff)


def _tile_gain(g):
    return jnp.tile(g.astype(F32), LANES // g.shape[0]).reshape(1, LANES)


def kernel(x_prompt, x_sample, mem_prompt, cache_fox_k, cache_fox_v, cache_fox_logf, cache_band_k, cache_band_v, state_ssm_re, state_ssm_im, cache_mem_k, cache_mem_v, g_mix, w_in, b_f, g_qa, g_ka, g_qb, g_kb, rel_bias, lam_re, lam_im, log_dt, ssm_b_re, ssm_b_im, ssm_c_re, ssm_c_im, ssm_d, w_glu, b_glu, g_mix_out, w_out, g_cross, g_mem, w_cq, w_ck, w_cv, g_cq, g_ck, w_co, g_ffn, w_ff1, w_ff3, w_ff2, w_router, w_e1, w_e3, w_e2):
    batch, seq, d = x_prompt.shape
    dbatch, dseq, _ = x_sample.shape
    depth = g_mix.shape[0]
    past_len, h_a, dh = cache_fox_k.shape[2:]
    band_rows, h_b = cache_band_k.shape[2:4]
    g_c, p_state = lam_re.shape[1:]
    w_a, w_b, w_c = h_a * dh, h_b * dh, g_c * SSM_GROUP
    n_mem, h_m, dh_m = cache_mem_k.shape[2:]
    w_m = h_m * dh_m
    max_rel = rel_bias.shape[2] - CHUNK
    assert dh == 64 and h_a % 2 == 0 and h_b % 2 == 0 and dh_m == LANES
    assert w_a == w_b and w_c <= w_a and h_a <= 16
    mp, ms = batch * seq, dbatch * dseq
    m = mp + ms
    pa, pb = h_a // 2, h_b // 2
    n_state = g_c * p_state
    n_blk = 2 if (w_c % 512 == 0) else 1
    nband = min(BAND_PAST * CHUNK, seq)
    tm = _pick(m, (1056, 1024, 768, 512, 256, 128, 64, 32, 16))
    tq_band = _pick(seq, (256, 128, 64))

    xp2, xs2 = x_prompt.reshape(mp, d), x_sample.reshape(ms, d)
    tail0 = m - tm
    split_x = ms <= tm and tail0 <= mp and tail0 >= tm
    x_tail = jnp.concatenate([xp2[tail0:], xs2], axis=0) if split_x else None
    x = xp2 if split_x else jnp.concatenate([xp2, xs2], axis=0)
    fox_kt, fox_vt, band_kt, band_vt = (jnp.transpose(c, (0, 1, 3, 4, 2)) for c in
                                        (cache_fox_k, cache_fox_v, cache_band_k, cache_band_v))

    outs = {k: [] for k in ("p_fl", "p_sr", "p_si", "p_mk", "p_mv",
                            "s_fk", "s_fv", "s_fl", "s_bk", "s_bv", "s_sr", "s_si")}
    cache_a = cache_b = None
    for l in range(depth):
        sizes = (w_a, w_a, w_a, h_a, w_b, w_b, w_b)
        cuts = [sum(sizes[:i]) for i in range(len(sizes) + 1)]
        w_cat = _regroup_columns(
            w_in, l, ((0, cuts[3]), (cuts[4], cuts[7]), (cuts[7], w_in.shape[2]), (cuts[3], cuts[4])),
            7 * w_a)
        first = split_x and l == 0
        proj = _mm([x], [w_cat], gain=g_mix[l].reshape(1, d), tm=tm, tn=w_a, name="proj_in",
                   m_rows=m, a_tail=x_tail if first else None)
        col_uc = 6 * w_a

        assert (col_uc + w_c) % LANES == 0
        lf_p, f_aug = _logf_rows(proj, jnp.pad(b_f[l].astype(F32), (0, LANES - h_a)).reshape(1, LANES),
                                 batch=batch, seq=seq, n_heads=h_a, n_pairs=pa,
                                 col_block=(col_uc + w_c) // LANES)

        t_all = -(-(past_len + dseq) // LANES) * LANES
        fa_s = proj[mp:, col_uc + w_c:col_uc + w_c + h_a].reshape(dbatch, dseq, h_a)
        x_s = jnp.concatenate([jnp.transpose(cache_fox_logf[l], (0, 2, 1)),
                               jnp.transpose(fa_s, (0, 2, 1)),
                               jnp.zeros((dbatch, h_a, t_all - past_len - dseq), F32)], axis=2)
        bias_s = jnp.broadcast_to(jnp.tile(b_f[l], dbatch)[:, None], (dbatch * h_a, LANES))
        lf_s, f_s = _logf_cumsum(x_s.reshape(dbatch * h_a, t_all), bias_s,
                                 raw_from=past_len, valid_to=past_len + dseq)
        lf_s = lf_s.reshape(dbatch, h_a, t_all)[:, :, past_len:past_len + dseq]
        f_s = f_s.reshape(dbatch, h_a, t_all)
        fs_row = jnp.pad(f_s, ((0, 0), (0, 16 - h_a), (0, 0)))
        fs_col = jnp.pad(jnp.transpose(f_s[:, :, past_len:past_len + dseq], (0, 2, 1)),
                         ((0, 0), (0, 0), (0, LANES - h_a))).reshape(ms, LANES)

        gqa, gka = _tile_gain(g_qa[l]), _tile_gain(g_ka[l])
        oa, *cache_a = _fox_prompt(proj, f_aug, gqa, gka, batch=batch, seq=seq, n_pairs=pa,
                                   col_q=0, col_k=pa, col_v=2 * pa, m_rows=m, layer=l, depth=depth,
                                   prev=cache_a)
        oa, kn_as = _sample_attn(proj, fox_kt, fox_vt, l, fs_col, fs_row,
                                 gqa, gka, oa, mode="fox", batch=dbatch, s_new=dseq, n_pairs=pa,
                                 col_q=0, row0=mp)

        gqb, gkb = _tile_gain(g_qb[l]), _tile_gain(g_kb[l])
        tab = _band_table(rel_bias[l], tq_band, max_rel)
        ob, *cache_b = _band_prompt(proj, tab, gqb, gkb, batch=batch, seq=seq, n_pairs=pb,
                                    col_q=3 * pa, col_k=3 * pa + pb, col_v=3 * pa + 2 * pb,
                                    tq=tq_band, m_rows=m, layer=l, depth=depth, prev=cache_b)
        tab_s = _rel_table(rel_bias[l], band_rows + np.arange(dseq), np.arange(band_rows + dseq),
                           max_rel)
        ob, kn_bs = _sample_attn(proj, band_kt, band_vt, l,
                                 tab_s[:, :, :band_rows], tab_s[:, :, band_rows:], gqb, gkb, ob,
                                 mode="band", batch=dbatch, s_new=dseq, n_pairs=pb, col_q=3, row0=mp)

        rep = lambda a: jnp.repeat(a.astype(F32), SSM_GROUP, axis=0)
        a_re, a_im, bb_re, bb_im = _s5_prep(
            rep(lam_re[l]), rep(lam_im[l]),
            jnp.broadcast_to(rep(log_dt[l])[:, None], (g_c * SSM_GROUP, p_state)),
            jnp.transpose(ssm_b_re[l], (0, 2, 1)).reshape(g_c * SSM_GROUP, p_state),
            jnp.transpose(ssm_b_im[l], (0, 2, 1)).reshape(g_c * SSM_GROUP, p_state))
        a_re = a_re.reshape(g_c, SSM_GROUP, p_state)[:, 0, :].reshape(1, n_state)
        a_im = a_im.reshape(g_c, SSM_GROUP, p_state)[:, 0, :].reshape(1, n_state)
        b_re_d = _block_diag(bb_re.reshape(g_c, SSM_GROUP, p_state))
        b_im_d = _block_diag(bb_im.reshape(g_c, SSM_GROUP, p_state))
        c_re_d = _block_diag(jnp.transpose(ssm_c_re[l], (0, 2, 1))).astype(BF16)
        c_im_d = _block_diag(jnp.transpose(ssm_c_im[l], (0, 2, 1))).astype(BF16)
        d_row = ssm_d[l].reshape(1, w_c)
        s5_tail = (c_re_d, c_im_d, d_row, w_glu[l], b_glu[l].reshape(1, w_c))
        assert col_uc % w_c == 0 and w_c % LANES == 0
        u_s = jnp.transpose(proj[mp:, col_uc:col_uc + w_c].reshape(dbatch, dseq, w_c),
                            (1, 0, 2)).reshape(ms, w_c)
        zeros_p = jnp.zeros((batch, n_state), F32)
        oc_p, sr_p, si_p = _s5(proj, zeros_p, zeros_p, a_re, a_im, b_re_d.astype(BF16),
                               b_im_d.astype(BF16), *s5_tail, nb=batch, seq=seq, n_blk=n_blk,
                               passes=1, col_block=col_uc // w_c)
        oc_s, sr_s, si_s = _s5(u_s, state_ssm_re[l].reshape(dbatch, n_state),
                               state_ssm_im[l].reshape(dbatch, n_state), a_re, a_im, b_re_d, b_im_d,
                               *s5_tail, nb=dbatch, seq=dseq, n_blk=n_blk, passes=3)
        oc = jnp.concatenate(
            [oc_p.reshape(mp, w_c),
             jnp.transpose(oc_s.reshape(dseq, dbatch, w_c), (1, 0, 2)).reshape(ms, w_c)], axis=0)

        x = _mm([oa, ob, oc], [w_out[l].astype(BF16)], gain=g_mix_out[l].reshape(1, -1), residual=x,
                tm=tm, tn=_pick(d, (1024, 512, 256, 128)), name="merge_out",
                res_tail=x_tail if first else None)

        outs["p_fl"].append(lf_p[:, :h_a].reshape(batch, seq, h_a))
        outs["p_sr"].append(sr_p.reshape(batch, g_c, p_state))
        outs["p_si"].append(si_p.reshape(batch, g_c, p_state))
        outs["s_fk"].append(kn_as.reshape(dbatch, dseq, h_a, dh))
        outs["s_fv"].append(proj[mp:, 2 * w_a:3 * w_a].reshape(dbatch, dseq, h_a, dh))
        outs["s_fl"].append(jnp.transpose(lf_s, (0, 2, 1)))
        outs["s_bk"].append(kn_bs.reshape(dbatch, dseq, h_b, dh))
        outs["s_bv"].append(proj[mp:, 5 * w_a:6 * w_a].reshape(dbatch, dseq, h_b, dh))
        outs["s_sr"].append(sr_s.reshape(dbatch, g_c, p_state))
        outs["s_si"].append(si_s.reshape(dbatch, g_c, p_state))

        mem2 = mem_prompt.reshape(batch * n_mem, d)
        tmm = _pick(batch * n_mem, (1024, 512, 256, 128))
        gck = jnp.tile(g_ck[l].astype(F32), h_m).reshape(1, w_m)
        gcq = jnp.tile(g_cq[l].astype(F32), h_m).reshape(1, w_m)
        mk = _mm([mem2], [w_ck[l]], gain=g_mem[l].reshape(1, d), group_gain=gck,
                 epilogue="group_norm", tm=tmm, tn=w_m, name="mem_k")
        mv = _mm([mem2], [w_cv[l]], gain=g_mem[l].reshape(1, d), tm=tmm, tn=w_m, name="mem_v")
        outs["p_mk"].append(mk.reshape(batch, n_mem, h_m, dh_m))
        outs["p_mv"].append(mv.reshape(batch, n_mem, h_m, dh_m))
        q_c = _mm([x], [w_cq[l].astype(BF16)], gain=g_cross[l].reshape(1, d), group_gain=gcq,
                  epilogue="group_norm", tm=tm, tn=w_m, name="cross_q")
        o_c = _cross_attn(q_c, mk, mv, None, batch=batch, q_len=seq, n_mem=n_mem, n_heads=h_m,
                          dh=dh_m, row0=0, m_rows=m)
        o_c = _cross_attn(q_c, cache_mem_k, cache_mem_v, o_c, batch=dbatch, q_len=dseq, n_mem=n_mem,
                          n_heads=h_m, dh=dh_m, row0=mp, m_rows=m, layer=l)
        x = _mm([o_c], [w_co[l].astype(BF16)], residual=x, tm=tm,
                tn=_pick(d, (1024, 512, 256, 128)), name="cross_out")

        i = l // 2
        if l % 2 == 0:
            hmid = _mm([x], [w_ff1[i].astype(BF16), w_ff3[i].astype(BF16)],
                       gain=g_ffn[l].reshape(1, d), epilogue="swiglu", out_dtype=BF16, tm=tm,
                       tn=_pick(w_ff1.shape[2], (512, 256, 128)), name="ffn_up")
            x = _mm([hmid], [w_ff2[i].astype(BF16)], residual=x, tm=tm,
                    tn=_pick(d, (512, 256, 128)), name="ffn_down")
        else:
            x = _moe(x, g_ffn[l].reshape(1, d), w_router[i], w_e1[i], w_e3[i], w_e2[i], tm_tok=tm,
                     split=mp if l == depth - 1 else None)

    st = lambda k: jnp.stack(outs[k])
    xp, xs = x if isinstance(x, tuple) else (x[:mp], x[mp:])
    per_head = lambda t, h: jnp.transpose(t.reshape(depth, batch, h, dh, -1), (0, 1, 4, 2, 3))
    return (xp.reshape(batch, seq, d), xs.reshape(dbatch, dseq, d),
            per_head(cache_a[0], h_a), per_head(cache_a[1], h_a), st("p_fl"),
            per_head(cache_b[0][..., seq - nband:], h_b), per_head(cache_b[1][..., seq - nband:], h_b),
            st("p_sr"), st("p_si"),
            st("p_mk"), st("p_mv"), st("s_fk"), st("s_fv"), st("s_fl"), st("s_bk"), st("s_bv"),
            st("s_sr"), st("s_si"))
```

```python
import functools
import math

import jax
import jax.numpy as jnp
import numpy as np
from jax import lax
from jax.experimental import pallas as pl
from jax.experimental.pallas import tpu as pltpu

F32 = jnp.float32
BF16 = jnp.bfloat16
I32 = jnp.int32
U32 = jnp.uint32

EPS = 1e-6
NEG_INF = -1e30
LOG2E = math.log2(math.e)
CHUNK = 64
BAND_PAST = 8
SSM_GROUP = 16
TOP_K = 2

LANES = 128
SUBLANES = 8
BF16_ROWS = 16
VMEM_CAP = 60 * 1024 * 1024


def _vmem_limit(nbytes):
    return int(min(VMEM_CAP, max(16 * 1024 * 1024, nbytes * 5 // 4 + (4 << 20))))


def _pick(n, candidates):
    for c in candidates:
        if c <= n and n % c == 0:
            return c
    raise ValueError(f"no tile for {n} in {candidates}")


def _nbytes(shape, dtype):
    return math.prod(shape) * jnp.dtype(dtype).itemsize


def _split3(x):
    hi = x.astype(BF16)
    r1 = x - hi.astype(F32)
    mid = r1.astype(BF16)
    lo = (r1 - mid.astype(F32)).astype(BF16)
    return hi, mid, lo


def _dot(a, b):
    return jnp.dot(a, b, preferred_element_type=F32)


def _dot_nt(a, b):
    return lax.dot_general(a, b, (((1,), (1,)), ((), ())), preferred_element_type=F32)


def _dot_hp(a, b, passes=3):
    ah = a.astype(BF16)
    bh = b.astype(BF16)
    if passes == 1:
        return _dot(ah, bh)
    al = (a - ah.astype(F32)).astype(BF16)
    bl = (b - bh.astype(F32)).astype(BF16)
    return _dot(ah, bh) + (_dot(ah, bl) + _dot(al, bh))


def _pair_ones():
    r = lax.broadcasted_iota(I32, (LANES, LANES), 0) // 64
    c = lax.broadcasted_iota(I32, (LANES, LANES), 1) // 64
    return (r == c).astype(BF16)


def _pair_norm(x, g, ones):
    sq = x * x
    hi = sq.astype(BF16)
    lo = (sq - hi.astype(F32)).astype(BF16)
    ss = _dot(hi, ones) + _dot(lo, ones)
    return x * lax.rsqrt(ss * (1.0 / 64.0) + EPS) * g


def _log_sigmoid(x):
    return jnp.minimum(x, 0.0) - jnp.log(1.0 + jnp.exp(-jnp.abs(x)))


def _gelu_tanh(x):
    c = math.sqrt(2.0 / math.pi)
    return 0.5 * x * (1.0 + jnp.tanh(c * (x + 0.044715 * (x * x * x))))


def _mm_body(*refs, widths, norm, n_w, epilogue, has_res, has_gg, stage_a, tm, rc, kc, k_total,
             n_main, a_tail, res_tail):
    it = iter(refs)
    a_refs = [next(it) for _ in widths]
    at_ref = next(it) if a_tail else None
    g_ref = next(it) if norm else None
    w_refs = [next(it) for _ in range(n_w)]
    gg_ref = next(it) if has_gg else None
    res_ref = next(it) if has_res else None
    rt_ref = next(it) if res_tail else None
    o_ref = next(it)
    abf = next(it) if stage_a else a_refs[0]
    cast_w = w_refs[0].dtype != BF16
    wbfs = [next(it) for _ in range(n_w)] if cast_w else w_refs
    i = pl.program_id(0)
    j = pl.program_id(1)

    if stage_a:
        def stage(srcs, when):
            @pl.when(jnp.logical_and(j == 0, when))
            def _():
                off = 0
                for a_ref, wd in zip(srcs, widths):
                    def chunk(c, carry, a_ref=a_ref, off=off, wd=wd):
                        r0 = pl.multiple_of(c * rc, rc)
                        x = a_ref[pl.ds(r0, rc), :].astype(F32)
                        if norm:
                            ms = jnp.mean(x * x, axis=-1, keepdims=True)
                            x = x * lax.rsqrt(ms + EPS) * g_ref[:, off:off + wd]
                        abf[pl.ds(r0, rc), off:off + wd] = x.astype(BF16)
                        return carry
                    lax.fori_loop(0, tm // rc, chunk, 0)
                    off += wd
        if a_tail:
            stage(a_refs, i < n_main)
            stage([at_ref], i >= n_main)
        else:
            stage(a_refs, True)

    if cast_w:
        for w_ref, wbf in zip(w_refs, wbfs):
            def cast(c, carry, w_ref=w_ref, wbf=wbf):
                r0 = pl.multiple_of(c * kc, kc)
                wbf[pl.ds(r0, kc), :] = w_ref[pl.ds(r0, kc), :].astype(BF16)
                return carry
            lax.fori_loop(0, k_total // kc, cast, 0)

    for c in range(tm // rc):
        r0 = c * rc
        a = abf[pl.ds(r0, rc), :]
        ys = [_dot(a, wbf[...]) for wbf in wbfs]
        if epilogue == "swiglu":
            y = jax.nn.silu(ys[0]) * ys[1]
        elif epilogue == "group_norm":
            parts = []
            for s in range(ys[0].shape[1] // LANES):
                ysl = ys[0][:, s * LANES:(s + 1) * LANES]
                ms = jnp.mean(ysl * ysl, axis=-1, keepdims=True)
                parts.append(ysl * lax.rsqrt(ms + EPS) * gg_ref[:, s * LANES:(s + 1) * LANES])
            y = jnp.concatenate(parts, axis=1)
        else:
            y = ys[0]
        if has_res:
            res = res_ref[pl.ds(r0, rc), :]
            if res_tail:
                res = jnp.where(i < n_main, res, rt_ref[pl.ds(r0, rc), :])
            y = y + res
        o_ref[pl.ds(r0, rc), :] = y.astype(o_ref.dtype)


def _mm(a_parts, w_list, *, gain=None, group_gain=None, residual=None, epilogue="none",
        out_dtype=F32, tm, tn, name, m_rows=None, a_tail=None, res_tail=None):
    m = a_parts[0].shape[0] if m_rows is None else m_rows
    widths = tuple(a.shape[1] for a in a_parts)
    k_total = sum(widths)
    n = w_list[0].shape[1]
    assert m % tm == 0 and n % tn == 0, (m, tm, n, tn)
    n_main = m // tm - 1 if (a_tail is not None or res_tail is not None) else m // tm
    main = lambda i: jnp.minimum(i, n_main - 1)
    norm = gain is not None
    stage_a = norm or len(a_parts) > 1 or a_parts[0].dtype != BF16
    rc = _pick(tm, (512, 384, 352, 256, 176, 128, 64, 32, 16))
    kc = _pick(k_total, (512, 256, 128))
    grid = (m // tm, n // tn)
    a_row = main if a_tail is not None else (lambda i: i)
    in_specs = [pl.BlockSpec((tm, wd), lambda i, j: (a_row(i), 0)) for wd in widths]
    args = list(a_parts)
    est = sum(2 * _nbytes((tm, wd), a.dtype) for wd, a in zip(widths, a_parts))
    if a_tail is not None:
        assert len(a_parts) == 1 and a_tail.shape == (tm, k_total) and stage_a
        in_specs.append(pl.BlockSpec((tm, k_total), lambda i, j: (0, 0),
                                     pipeline_mode=pl.Buffered(1)))
        args.append(a_tail)
        est += _nbytes((tm, k_total), a_tail.dtype)
    if norm:
        in_specs.append(pl.BlockSpec((1, k_total), lambda i, j: (0, 0)))
        args.append(gain)
    cast_w = w_list[0].dtype != BF16
    for w in w_list:
        in_specs.append(pl.BlockSpec((k_total, tn), lambda i, j: (0, j)))
        args.append(w)
        est += 2 * _nbytes((k_total, tn), w.dtype) + cast_w * _nbytes((k_total, tn), BF16)
    if group_gain is not None:
        in_specs.append(pl.BlockSpec((1, tn), lambda i, j: (0, j)))
        args.append(group_gain)
    if residual is not None:
        r_row = main if res_tail is not None else (lambda i: i)
        in_specs.append(pl.BlockSpec((tm, tn), lambda i, j: (r_row(i), j)))
        args.append(residual)
        est += 2 * _nbytes((tm, tn), F32)
        if res_tail is not None:
            assert res_tail.shape == (tm, n)
            in_specs.append(pl.BlockSpec((tm, tn), lambda i, j: (0, j)))
            args.append(res_tail)
            est += 2 * _nbytes((tm, tn), F32)
    est += 2 * _nbytes((tm, tn), out_dtype) + stage_a * _nbytes((tm, k_total), BF16)
    est += 4 * _nbytes((rc, tn), F32) * len(w_list)
    body = functools.partial(
        _mm_body, widths=widths, norm=norm, n_w=len(w_list), epilogue=epilogue,
        has_res=residual is not None, has_gg=group_gain is not None, stage_a=stage_a, tm=tm,
        rc=rc, kc=kc, k_total=k_total, n_main=n_main, a_tail=a_tail is not None,
        res_tail=res_tail is not None)
    return pl.pallas_call(
        body,
        out_shape=jax.ShapeDtypeStruct((m, n), out_dtype),
        grid=grid,
        in_specs=in_specs,
        out_specs=pl.BlockSpec((tm, tn), lambda i, j: (i, j)),
        scratch_shapes=[pltpu.VMEM((tm, k_total), BF16)] * stage_a
        + [pltpu.VMEM((k_total, tn), BF16) for _ in w_list] * cast_w,
        compiler_params=pltpu.CompilerParams(
            dimension_semantics=("parallel", "arbitrary"),
            vmem_limit_bytes=_vmem_limit(est)),
        name=name,
    )(*args)


def _regroup_body(w_ref, o_ref, *, cuts, width):
    off = 0
    for lo, hi in cuts:
        o_ref[:, off:off + hi - lo] = w_ref[:, lo:hi].astype(o_ref.dtype)
        off += hi - lo
    if off < width:
        o_ref[:, off:width] = jnp.zeros((o_ref.shape[0], width - off), o_ref.dtype)


def _regroup_columns(w, layer, cuts, width):
    _, k, n = w.shape
    tr = _pick(k, (256, 128, 64, 32, 16))
    return pl.pallas_call(
        functools.partial(_regroup_body, cuts=cuts, width=width),
        out_shape=jax.ShapeDtypeStruct((k, width), BF16),
        grid=(k // tr,),
        in_specs=[pl.BlockSpec((None, tr, n), lambda i: (layer, i, 0))],
        out_specs=pl.BlockSpec((tr, width), lambda i: (i, 0)),
        compiler_params=pltpu.CompilerParams(
            dimension_semantics=("parallel",),
            vmem_limit_bytes=_vmem_limit(2 * _nbytes((tr, n), F32) + 2 * _nbytes((tr, width), BF16)
                                         + 4 * _nbytes((tr, width), F32))),
        name="regroup_w_in",
    )(w)


def _cumsum_body(x_ref, b_ref, lf_ref, f_ref, carry, *, raw_from, valid_to, tt):
    j = pl.program_id(0)

    @pl.when(j == 0)
    def _():
        carry[...] = jnp.zeros_like(carry)

    x = x_ref[...]
    lane = j * tt + lax.broadcasted_iota(I32, x.shape, 1)
    lf = jnp.where(lane >= raw_from, _log_sigmoid(x + b_ref[:, 0:1]), x)
    lf = jnp.where(lane < valid_to, lf, 0.0)
    lf_ref[...] = lf
    tri = (lax.broadcasted_iota(I32, (tt, tt), 0)
           <= lax.broadcasted_iota(I32, (tt, tt), 1)).astype(BF16)
    hi, mid, lo = _split3(lf)
    y = _dot(hi, tri) + _dot(mid, tri) + _dot(lo, tri) + carry[:, 0:1]
    f_ref[...] = y
    carry[...] = jnp.broadcast_to(y[:, tt - 1:tt], carry.shape)


def _logf_cumsum(x, bias, *, raw_from, valid_to):
    r, t = x.shape
    tt = _pick(t, (256, 128))
    return pl.pallas_call(
        functools.partial(_cumsum_body, raw_from=raw_from, valid_to=valid_to, tt=tt),
        out_shape=(jax.ShapeDtypeStruct((r, t), F32), jax.ShapeDtypeStruct((r, t), F32)),
        grid=(t // tt,),
        in_specs=[pl.BlockSpec((r, tt), lambda j: (0, j)),
                  pl.BlockSpec((r, LANES), lambda j: (0, 0))],
        out_specs=(pl.BlockSpec((r, tt), lambda j: (0, j)),
                   pl.BlockSpec((r, tt), lambda j: (0, j))),
        scratch_shapes=[pltpu.VMEM((r, LANES), F32)],
        compiler_params=pltpu.CompilerParams(dimension_semantics=("arbitrary",)),
        name="logf_cumsum",
    )(x, bias)


def _logf_rows_body(fa_ref, b_ref, lf_ref, faug_ref, carry, *, tt, n_heads):
    j = pl.program_id(1)

    @pl.when(j == 0)
    def _():
        carry[...] = jnp.zeros_like(carry)

    lane = lax.broadcasted_iota(I32, (tt, LANES), 1)
    lf = jnp.where(lane < n_heads, _log_sigmoid(fa_ref[...] + b_ref[...]), 0.0)
    lf_ref[...] = lf
    tri = (lax.broadcasted_iota(I32, (tt, tt), 0)
           >= lax.broadcasted_iota(I32, (tt, tt), 1)).astype(BF16)
    hi, mid, lo = _split3(lf)
    f = _dot(tri, hi) + _dot(tri, mid) + _dot(tri, lo) + carry[0:1, :]
    carry[...] = jnp.broadcast_to(f[tt - 1:tt, :], carry.shape)
    width = faug_ref.shape[1]
    head = lax.broadcasted_iota(I32, (LANES, width), 0)
    col = lax.broadcasted_iota(I32, (LANES, width), 1)
    within = col % LANES
    sel = ((within < 12) & (head == 2 * (col // LANES) + (within % 6) // 3)).astype(BF16)
    hi, mid, lo = _split3(f * LOG2E)
    faug_ref[...] = _dot(hi, sel) + _dot(mid, sel) + _dot(lo, sel)


def _logf_rows(proj, bias_row, *, batch, seq, n_heads, n_pairs, col_block):
    tt = _pick(seq, (256, 128))
    nt = seq // tt
    return pl.pallas_call(
        functools.partial(_logf_rows_body, tt=tt, n_heads=n_heads),
        out_shape=(jax.ShapeDtypeStruct((batch * seq, LANES), F32),
                   jax.ShapeDtypeStruct((batch * seq, n_pairs * LANES), F32)),
        grid=(batch, nt),
        in_specs=[pl.BlockSpec((tt, LANES), lambda b, j: (b * nt + j, col_block)),
                  pl.BlockSpec((1, LANES), lambda b, j: (0, 0))],
        out_specs=(pl.BlockSpec((tt, LANES), lambda b, j: (b * nt + j, 0)),
                   pl.BlockSpec((tt, n_pairs * LANES), lambda b, j: (b * nt + j, 0))),
        scratch_shapes=[pltpu.VMEM((SUBLANES, LANES), F32)],
        compiler_params=pltpu.CompilerParams(dimension_semantics=("parallel", "arbitrary")),
        name="logf_rows",
    )(proj, bias_row)


def _stack_heads(qn):
    lane = lax.broadcasted_iota(I32, qn.shape, 1)
    q0 = jnp.where(lane < 64, qn, 0.0)
    q1 = jnp.where(lane < 64, 0.0, qn)
    return jnp.concatenate([q0, q1], axis=0).astype(BF16)


def _lane_column(block, h):
    lane = lax.broadcasted_iota(I32, block.shape, 1)
    return jnp.sum(jnp.where(lane == h, block, 0.0), axis=-1, keepdims=True)


V_ROWS = LANES + BF16_ROWS


def _split_select(x):
    hi, mid, lo = _split3(x)
    m3 = lax.broadcasted_iota(I32, x.shape, 1) % 3
    return jnp.where(m3 == 0, hi.astype(F32), jnp.where(m3 == 1, mid.astype(F32), lo.astype(F32)))


def _head_masks(shape):
    lane = lax.broadcasted_iota(I32, shape, 1)
    return lane, lane < 64


def _finish_heads(acc, tq):
    o0 = acc[0:64, 0:tq] / acc[LANES:LANES + 1, 0:tq]
    o1 = acc[64:LANES, tq:2 * tq] / acc[LANES:LANES + 1, tq:2 * tq]
    return jnp.concatenate([o0, o1], axis=0).T


def _fox_prompt_body(q_ref, k_ref, v_ref, f_ref, gq_ref, gk_ref, *rest, tq, seq, dh):
    o_ref, knt_ref, vt_ref, kaug, vt3 = rest[-5:]
    ones = _pair_ones()
    nq = seq // tq
    lane, first = _head_masks((tq, LANES))

    for c in range(nq):
        rows = slice(c * tq, (c + 1) * tq)
        kn = _pair_norm(k_ref[rows, :], gk_ref[...], ones)
        knt_ref[:, rows] = kn.T
        kaug[rows, 0:LANES] = kn.astype(BF16)
        sp = _split_select(f_ref[rows, :])
        kaug[rows, LANES:2 * LANES] = jnp.where(
            lane < 6, -sp, jnp.where(lane < 12, 1.0, 0.0)).astype(BF16)
        vt = v_ref[rows, :].T
        vt_ref[:, rows] = vt
        vt3[c, 0:LANES, :] = vt.astype(BF16)
        vt3[c, LANES:V_ROWS, :] = jnp.ones((BF16_ROWS, tq), BF16)

    causal = (lax.broadcasted_iota(I32, (tq, 2 * tq), 0)
              <= lax.broadcasted_iota(I32, (tq, 2 * tq), 1) % tq)

    for qi in range(nq):
        rows = slice(qi * tq, (qi + 1) * tq)
        qn = _pair_norm(q_ref[rows, :], gq_ref[...], ones) * (dh ** -0.5 * LOG2E)
        sp = _split_select(f_ref[rows, :])
        up0 = jnp.where(lane < 3, 1.0, jnp.where((lane >= 6) & (lane < 9), sp, 0.0))
        up1 = jnp.where((lane >= 3) & (lane < 6), 1.0,
                        jnp.where((lane >= 9) & (lane < 12), sp, 0.0))
        qs = jnp.concatenate(
            [jnp.concatenate([jnp.where(first, qn, 0.0), up0], axis=1),
             jnp.concatenate([jnp.where(first, 0.0, qn), up1], axis=1)], axis=0).astype(BF16)

        s = jnp.where(causal, _dot_nt(kaug[rows, :], qs), NEG_INF)
        m = jnp.max(s, axis=0, keepdims=True)
        acc = _dot(vt3[qi], jnp.exp2(s - m).astype(BF16))
        for kj in range(qi):
            s = _dot_nt(kaug[kj * tq:(kj + 1) * tq, :], qs)
            m_new = jnp.maximum(m, jnp.max(s, axis=0, keepdims=True))
            acc = jnp.exp2(m - m_new) * acc + _dot(vt3[kj], jnp.exp2(s - m_new).astype(BF16))
            m = m_new
        o_ref[rows, :] = _finish_heads(acc, tq)


def _layer_cache_outputs(layer, depth, batch, width, seq, prev, n_inputs, index):
    shape = jax.ShapeDtypeStruct((depth, batch, width, seq), F32)
    spec = pl.BlockSpec((None, None, LANES, seq), lambda *g: (layer, *index(*g), 0))
    extra_specs = [] if prev is None else [pl.BlockSpec(memory_space=pl.ANY)] * 2
    aliases = {} if prev is None else {n_inputs: 1, n_inputs + 1: 2}
    return (shape, shape), (spec, spec), extra_specs, aliases, (() if prev is None else tuple(prev))


def _fox_prompt(proj, f_aug, gq, gk, *, batch, seq, n_pairs, col_q, col_k, col_v, m_rows, layer,
                depth, prev):
    tq = _pick(seq, (512, 256, 128))
    nq = seq // tq
    width = n_pairs * LANES
    est = (2 * 7 * _nbytes((seq, LANES), F32)
           + _nbytes((seq, 2 * LANES), BF16) + _nbytes((V_ROWS, seq), BF16)
           + 16 * _nbytes((tq, 2 * tq), F32))
    c_shapes, c_specs, extra_specs, aliases, extra_args = _layer_cache_outputs(
        layer, depth, batch, width, seq, prev, 6, lambda b, hp: (b, hp))
    return pl.pallas_call(
        functools.partial(_fox_prompt_body, tq=tq, seq=seq, dh=64),
        out_shape=(jax.ShapeDtypeStruct((m_rows, width), F32), *c_shapes),
        grid=(batch, n_pairs),
        in_specs=[
            pl.BlockSpec((seq, LANES), lambda b, hp: (b, col_q + hp)),
            pl.BlockSpec((seq, LANES), lambda b, hp: (b, col_k + hp)),
            pl.BlockSpec((seq, LANES), lambda b, hp: (b, col_v + hp)),
            pl.BlockSpec((seq, LANES), lambda b, hp: (b, hp)),
            pl.BlockSpec((1, LANES), lambda b, hp: (0, 0)),
            pl.BlockSpec((1, LANES), lambda b, hp: (0, 0)),
            *extra_specs,
        ],
        out_specs=(pl.BlockSpec((seq, LANES), lambda b, hp: (b, hp)), *c_specs),
        input_output_aliases=aliases,
        scratch_shapes=[pltpu.VMEM((seq, 2 * LANES), BF16), pltpu.VMEM((nq, V_ROWS, tq), BF16)],
        compiler_params=pltpu.CompilerParams(
            dimension_semantics=("parallel", "parallel"),
            vmem_limit_bytes=_vmem_limit(est)),
        name="fox_prompt",
    )(proj, proj, proj, f_aug, gq, gk, *extra_args)


def _band_prompt_body(q_ref, k_ref, v_ref, tab_ref, gq_ref, gk_ref, *rest, tq, seq, past, dh):
    o_ref, knt_ref, vt_ref, kpad, vt3 = rest[-5:]
    ones = _pair_ones()
    nq = seq // tq
    npad = past // tq
    win = past + tq

    kpad[0:past, :] = jnp.zeros((past, LANES), BF16)
    for c in range(npad):
        vt3[c] = jnp.zeros((V_ROWS, tq), BF16)
    for c in range(nq):
        rows = slice(c * tq, (c + 1) * tq)
        kn = _pair_norm(k_ref[rows, :], gk_ref[...], ones)
        knt_ref[:, rows] = kn.T
        kpad[past + c * tq:past + (c + 1) * tq, :] = kn.astype(BF16)
        vt = v_ref[rows, :].T
        vt_ref[:, rows] = vt
        vt3[npad + c, 0:LANES, :] = vt.astype(BF16)
        vt3[npad + c, LANES:V_ROWS, :] = jnp.ones((BF16_ROWS, tq), BF16)

    row = lax.broadcasted_iota(I32, (win, 2 * tq), 0)

    for qi in range(nq):
        r0 = qi * tq
        qn = _pair_norm(q_ref[r0:r0 + tq, :], gq_ref[...], ones) * (dh ** -0.5 * LOG2E)
        s = _dot_nt(kpad[r0:r0 + win, :], _stack_heads(qn)) + tab_ref[...]
        if r0 < past:
            s = jnp.where(row >= past - r0, s, NEG_INF)
        m = jnp.max(s, axis=0, keepdims=True)
        p = jnp.exp2(s - m).astype(BF16)
        acc = _dot(vt3[qi], p[0:tq, :])
        for c in range(1, win // tq):
            acc = acc + _dot(vt3[qi + c], p[c * tq:(c + 1) * tq, :])
        o_ref[r0:r0 + tq, :] = _finish_heads(acc, tq)


def _band_prompt(proj, tab, gq, gk, *, batch, seq, n_pairs, col_q, col_k, col_v, tq, m_rows, layer,
                 depth, prev):
    past = BAND_PAST * CHUNK
    assert past % tq == 0 and seq % tq == 0
    nq = seq // tq
    width = n_pairs * LANES
    win = past + tq
    est = (2 * 5 * _nbytes((seq, LANES), F32)
           + _nbytes((seq + past, LANES), BF16) + _nbytes((V_ROWS, seq + past), BF16)
           + 2 * _nbytes((win, 2 * tq), F32) + 8 * _nbytes((win, 2 * tq), F32))
    c_shapes, c_specs, extra_specs, aliases, extra_args = _layer_cache_outputs(
        layer, depth, batch, width, seq, prev, 6, lambda hp, b: (b, hp))
    return pl.pallas_call(
        functools.partial(_band_prompt_body, tq=tq, seq=seq, past=past, dh=64),
        out_shape=(jax.ShapeDtypeStruct((m_rows, width), F32), *c_shapes),
        grid=(n_pairs, batch),
        in_specs=[
            pl.BlockSpec((seq, LANES), lambda hp, b: (b, col_q + hp)),
            pl.BlockSpec((seq, LANES), lambda hp, b: (b, col_k + hp)),
            pl.BlockSpec((seq, LANES), lambda hp, b: (b, col_v + hp)),
            pl.BlockSpec((None, win, 2 * tq), lambda hp, b: (hp, 0, 0)),
            pl.BlockSpec((1, LANES), lambda hp, b: (0, 0)),
            pl.BlockSpec((1, LANES), lambda hp, b: (0, 0)),
            *extra_specs,
        ],
        out_specs=(pl.BlockSpec((seq, LANES), lambda hp, b: (b, hp)), *c_specs),
        input_output_aliases=aliases,
        scratch_shapes=[pltpu.VMEM((seq + past, LANES), BF16),
                        pltpu.VMEM(((seq + past) // tq, V_ROWS, tq), BF16)],
        compiler_params=pltpu.CompilerParams(
            dimension_semantics=("parallel", "parallel"),
            vmem_limit_bytes=_vmem_limit(est)),
        name="band_prompt",
    )(proj, proj, proj, tab, gq, gk, *extra_args)


def _sample_attn_body(*refs, mode, n_pairs, s_new, n_cache, dh):
    if mode == "fox":
        (q_ref, k_ref, v_ref, ck_ref, cv_ref, fq_ref, ft_ref, gq_ref, gk_ref, _alias,
         o_ref, kn_ref) = refs
    else:
        (q_ref, k_ref, v_ref, ck_ref, cv_ref, tabc_ref, tabn_ref, gq_ref, gk_ref, _alias,
         o_ref, kn_ref) = refs
    ones = _pair_ones()
    row = lax.broadcasted_iota(I32, (s_new, s_new), 0)
    col = lax.broadcasted_iota(I32, (s_new, s_new), 1)
    for hp in range(n_pairs):
        sl = slice(hp * LANES, (hp + 1) * LANES)
        qn = _pair_norm(q_ref[:, sl], gq_ref[...], ones) * (dh ** -0.5)
        kn = _pair_norm(k_ref[:, sl], gk_ref[...], ones)
        kn_ref[:, sl] = kn
        vn = v_ref[:, sl]
        outs = []
        for e in range(2):
            h = 2 * hp + e
            hl = slice(e * dh, (e + 1) * dh)
            qh = qn[:, hl].astype(BF16)
            sc = _dot(qh, ck_ref[h].astype(BF16))
            sn = _dot_nt(qh, kn[:, hl].astype(BF16))
            if mode == "fox":
                fq = _lane_column(fq_ref[...], h)
                fk = ft_ref[h:h + 1, :]
                sc = sc + fq - fk[:, :n_cache]
                sn = jnp.where(col <= row, sn + fq - fk[:, n_cache:n_cache + s_new], NEG_INF)
            else:
                sc = sc + tabc_ref[h]
                sn = sn + tabn_ref[h]
            m = jnp.maximum(jnp.max(sc, axis=-1, keepdims=True),
                            jnp.max(sn, axis=-1, keepdims=True))
            pc = jnp.exp(sc - m)
            pn = jnp.exp(sn - m)
            l = jnp.sum(pc, axis=-1, keepdims=True) + jnp.sum(pn, axis=-1, keepdims=True)
            outs.append((_dot_nt(pc.astype(BF16), cv_ref[h].astype(BF16))
                         + _dot(pn.astype(BF16), vn[:, hl].astype(BF16))) / l)
        o_ref[:, sl] = jnp.concatenate(outs, axis=1)


def _sample_attn(proj, cache_kt, cache_vt, layer, extra_a, extra_b, gq, gk, o_buf, *, mode, batch,
                 s_new, n_pairs, col_q, row0):
    width = n_pairs * LANES
    _, _, n_heads, dh, n_cache = cache_kt.shape
    rb = row0 // s_new
    if mode == "fox":
        ex_specs = [pl.BlockSpec((s_new, LANES), lambda b: (b, 0)),
                    pl.BlockSpec((None, 16, extra_b.shape[2]), lambda b: (b, 0, 0))]
    else:
        ex_specs = [pl.BlockSpec(extra_a.shape, lambda b: (0, 0, 0)),
                    pl.BlockSpec(extra_b.shape, lambda b: (0, 0, 0))]
    est = (2 * 2 * _nbytes((n_cache, width), F32) + 8 * _nbytes((s_new, width), F32)
           + 2 * _nbytes(extra_a.shape, F32) + 16 * _nbytes((2 * s_new, n_cache), F32)
           + 4 * _nbytes((n_cache, LANES), BF16))
    return pl.pallas_call(
        functools.partial(_sample_attn_body, mode=mode, n_pairs=n_pairs, s_new=s_new,
                          n_cache=n_cache, dh=dh),
        out_shape=(jax.ShapeDtypeStruct(o_buf.shape, F32),
                   jax.ShapeDtypeStruct((batch * s_new, width), F32)),
        grid=(batch,),
        in_specs=[
            pl.BlockSpec((s_new, width), lambda b: (rb + b, col_q)),
            pl.BlockSpec((s_new, width), lambda b: (rb + b, col_q + 1)),
            pl.BlockSpec((s_new, width), lambda b: (rb + b, col_q + 2)),
            pl.BlockSpec((None, None, n_heads, dh, n_cache), lambda b: (layer, b, 0, 0, 0)),
            pl.BlockSpec((None, None, n_heads, dh, n_cache), lambda b: (layer, b, 0, 0, 0)),
            *ex_specs,
            pl.BlockSpec((1, LANES), lambda b: (0, 0)),
            pl.BlockSpec((1, LANES), lambda b: (0, 0)),
            pl.BlockSpec(memory_space=pl.ANY),
        ],
        out_specs=(pl.BlockSpec((s_new, width), lambda b: (rb + b, 0)),
                   pl.BlockSpec((s_new, width), lambda b: (b, 0))),
        input_output_aliases={9: 0},
        compiler_params=pltpu.CompilerParams(
            dimension_semantics=("parallel",), vmem_limit_bytes=_vmem_limit(est)),
        name=f"{mode}_sample",
    )(proj, proj, proj, cache_kt, cache_vt, extra_a, extra_b, gq, gk, o_buf)


def _cross_body(*refs, n_heads, dh, aliased):
    if aliased:
        q_ref, k_ref, v_ref, _alias, o_ref = refs
    else:
        q_ref, k_ref, v_ref, o_ref = refs
    per_head = len(k_ref.shape) == 3
    for h in range(n_heads):
        sl = slice(h * dh, (h + 1) * dh)
        kh = k_ref[:, h, :] if per_head else k_ref[:, sl]
        vh = v_ref[:, h, :] if per_head else v_ref[:, sl]
        q = (q_ref[:, sl] * (dh ** -0.5)).astype(BF16)
        s = _dot_nt(q, kh.astype(BF16))
        m = jnp.max(s, axis=-1, keepdims=True)
        p = jnp.exp(s - m)
        l = jnp.sum(p, axis=-1, keepdims=True)
        o = _dot(p.astype(BF16), vh.astype(BF16)) / l
        o_ref[:, sl] = o.astype(o_ref.dtype)


def _cross_attn(q_all, k, v, o_buf, *, batch, q_len, n_mem, n_heads, dh, row0, m_rows, layer=None):
    width = n_heads * dh
    tq = _pick(q_len, (512, 256, 128, 64, 32, 16))
    nq = q_len // tq
    rb = row0 // tq
    aliased = o_buf is not None
    if layer is None:
        kv_spec = pl.BlockSpec((n_mem, width), lambda b, qi: (b, 0))
    else:
        kv_spec = pl.BlockSpec((None, None, n_mem, n_heads, dh), lambda b, qi: (layer, b, 0, 0, 0))
    in_specs = [pl.BlockSpec((tq, width), lambda b, qi: (rb + b * nq + qi, 0)), kv_spec, kv_spec]
    args = [q_all, k, v]
    if aliased:
        in_specs.append(pl.BlockSpec(memory_space=pl.ANY))
        args.append(o_buf)
    est = (2 * _nbytes((tq, width), F32) + 4 * _nbytes((n_mem, width), F32)
           + 2 * _nbytes((tq, width), BF16) + 12 * _nbytes((tq, n_mem), F32))
    return pl.pallas_call(
        functools.partial(_cross_body, n_heads=n_heads, dh=dh, aliased=aliased),
        out_shape=jax.ShapeDtypeStruct((m_rows, width), BF16),
        grid=(batch, nq),
        in_specs=in_specs,
        out_specs=pl.BlockSpec((tq, width), lambda b, qi: (rb + b * nq + qi, 0)),
        input_output_aliases={3: 0} if aliased else {},
        compiler_params=pltpu.CompilerParams(
            dimension_semantics=("parallel", "arbitrary"), vmem_limit_bytes=_vmem_limit(est)),
        name="cross_attn",
    )(*args)


def _s5_prep_body(lr_ref, li_ref, ldt_ref, br_ref, bi_ref, ar_ref, ai_ref, bbr_ref, bbi_ref):
    lr = lr_ref[...]
    li = li_ref[...]
    dt = jnp.exp(ldt_ref[...])
    mag = jnp.exp(lr * dt)
    a_re = mag * jnp.cos(li * dt)
    a_im = mag * jnp.sin(li * dt)
    den = lr * lr + li * li
    num_re = a_re - 1.0
    coef_re = (num_re * lr + a_im * li) / den
    coef_im = (a_im * lr - num_re * li) / den
    br = br_ref[...]
    bi = bi_ref[...]
    ar_ref[...] = a_re
    ai_ref[...] = a_im
    bbr_ref[...] = coef_re * br - coef_im * bi
    bbi_ref[...] = coef_re * bi + coef_im * br


def _s5_prep(lam_re, lam_im, log_dt, b_re, b_im):
    shape = lam_re.shape
    spec = pl.BlockSpec(shape, lambda: (0, 0))
    return pl.pallas_call(
        _s5_prep_body,
        out_shape=tuple(jax.ShapeDtypeStruct(shape, F32) for _ in range(4)),
        in_specs=[spec] * 5,
        out_specs=tuple([spec] * 4),
        name="s5_discretise",
    )(lam_re, lam_im, log_dt, b_re, b_im)


def _s5_body(*refs, nb, t_chunk, n_blk, passes, per_stream):
    n_u = nb if per_stream else 1
    u_refs = refs[:n_u]
    (x0r_ref, x0i_ref, ar_ref, ai_ref, bre_ref, bim_ref, cre_ref, cim_ref, d_ref, wg_ref, bg_ref,
     o_ref, xr_out, xi_out, bur, bui, st_r, st_i) = refs[n_u:n_u + 18]
    i = pl.program_id(0)
    wc = u_refs[0].shape[1]
    ns = bur.shape[1]
    ub = wc // n_blk
    sb = ns // n_blk

    @pl.when(i == 0)
    def _():
        st_r[...] = x0r_ref[...]
        st_i[...] = x0i_ref[...]

    if per_stream:
        slab = refs[n_u + 18]
        for b in range(nb):
            for k in range(wc // LANES):
                slab[k, pl.ds(b, t_chunk, stride=nb), :] = u_refs[b][:, k * LANES:(k + 1) * LANES]
        u = jnp.concatenate([slab[k] for k in range(wc // LANES)], axis=1)
    else:
        u = u_refs[0][...]
    rows = nb * t_chunk
    n_half = 2 if (t_chunk % 2 == 0 and rows >= 1024) else 1
    hr = rows // n_half
    for h in range(n_half):
        for k in range(n_blk):
            uk = u[h * hr:(h + 1) * hr, k * ub:(k + 1) * ub]
            bur[h * hr:(h + 1) * hr, k * sb:(k + 1) * sb] = _dot_hp(
                uk, bre_ref[k * ub:(k + 1) * ub, k * sb:(k + 1) * sb], passes)
            bui[h * hr:(h + 1) * hr, k * sb:(k + 1) * sb] = _dot_hp(
                uk, bim_ref[k * ub:(k + 1) * ub, k * sb:(k + 1) * sb], passes)

    a_re = ar_ref[...]
    a_im = ai_ref[...]
    xr, xi = st_r[...], st_i[...]
    for t in range(t_chunk):
        r0 = t * nb
        xr, xi = (a_re * xr - a_im * xi + bur[r0:r0 + nb, :],
                  a_re * xi + a_im * xr + bui[r0:r0 + nb, :])
        bur[r0:r0 + nb, :] = xr
        bui[r0:r0 + nb, :] = xi
    st_r[...] = xr
    st_i[...] = xi
    xr_out[...] = xr
    xi_out[...] = xi
    outs = []
    for h in range(n_half):
        ys = []
        for k in range(n_blk):
            xrk = bur[h * hr:(h + 1) * hr, k * sb:(k + 1) * sb].astype(BF16)
            xik = bui[h * hr:(h + 1) * hr, k * sb:(k + 1) * sb].astype(BF16)
            ys.append(_dot(xrk, cre_ref[k * sb:(k + 1) * sb, k * ub:(k + 1) * ub].astype(BF16))
                      - _dot(xik, cim_ref[k * sb:(k + 1) * sb, k * ub:(k + 1) * ub].astype(BF16)))
        y = jnp.concatenate(ys, axis=1) + d_ref[...] * u[h * hr:(h + 1) * hr, :]
        z = _gelu_tanh(y)
        gate = _dot(z.astype(BF16), wg_ref[...].astype(BF16)) + bg_ref[...]
        outs.append(z * jax.nn.sigmoid(gate))
    out = outs[0] if n_half == 1 else jnp.concatenate(outs, axis=0)
    if per_stream:
        for k in range(wc // LANES):
            slab[k] = out[:, k * LANES:(k + 1) * LANES]
        for b in range(nb):
            o_ref[b] = jnp.concatenate(
                [slab[k, pl.ds(b, t_chunk, stride=nb), :] for k in range(wc // LANES)], axis=1)
    else:
        o_ref[...] = out


def _s5(u, x0r, x0i, a_re, a_im, b_re, b_im, c_re, c_im, d, w_glu, b_glu, *, nb, seq, n_blk,
        passes, col_block=None):
    per_stream = col_block is not None
    wc, ns = b_re.shape
    t_chunk = _pick(seq, (128, 64, 32, 16) if nb <= SUBLANES else (16,))
    nc = seq // t_chunk
    rows = nb * t_chunk
    full = lambda shape: pl.BlockSpec(shape, lambda i: (0,) * len(shape))
    est = (6 * _nbytes((rows, wc), F32) + 2 * _nbytes((rows, ns), F32)
           + 2 * 2 * _nbytes((wc, ns), b_re.dtype) + 2 * 2 * _nbytes((wc, ns), c_re.dtype)
           + 2 * _nbytes((wc, wc), F32)
           + 6 * _nbytes((rows // 2, ns // n_blk), F32) + 8 * _nbytes((nb, ns), F32))
    if per_stream:
        u_specs = [pl.BlockSpec((t_chunk, wc), lambda i, s=s: (s * nc + i, col_block))
                   for s in range(nb)]
        u_args = [u] * nb
        o_shape = jax.ShapeDtypeStruct((nb, seq, wc), F32)
        o_spec = pl.BlockSpec((nb, t_chunk, wc), lambda i: (0, i, 0))
        slab = [pltpu.VMEM((wc // LANES, rows, LANES), F32)]
    else:
        u_specs = [pl.BlockSpec((rows, wc), lambda i: (i, 0))]
        u_args = [u]
        o_shape = jax.ShapeDtypeStruct((seq * nb, wc), F32)
        o_spec = pl.BlockSpec((rows, wc), lambda i: (i, 0))
        slab = []
    return pl.pallas_call(
        functools.partial(_s5_body, nb=nb, t_chunk=t_chunk, n_blk=n_blk, passes=passes,
                          per_stream=per_stream),
        out_shape=(o_shape, jax.ShapeDtypeStruct((nb, ns), F32), jax.ShapeDtypeStruct((nb, ns), F32)),
        grid=(nc,),
        in_specs=[*u_specs,
                  full((nb, ns)), full((nb, ns)), full((1, ns)), full((1, ns)),
                  full((wc, ns)), full((wc, ns)), full((ns, wc)), full((ns, wc)),
                  full((1, wc)), full((wc, wc)), full((1, wc))],
        out_specs=(o_spec, full((nb, ns)), full((nb, ns))),
        scratch_shapes=[pltpu.VMEM((rows, ns), F32), pltpu.VMEM((rows, ns), F32),
                        pltpu.VMEM((nb, ns), F32), pltpu.VMEM((nb, ns), F32), *slab],
        compiler_params=pltpu.CompilerParams(
            dimension_semantics=("arbitrary",), vmem_limit_bytes=_vmem_limit(est)),
        name="s5_scan",
    )(*u_args, x0r, x0i, a_re, a_im, b_re, b_im, c_re, c_im, d, w_glu, b_glu)


def _router_body(x_ref, g_ref, wr_ref, pk_ref, idx_ref, gate_ref, *, n_exp, rc):
    tm, d = x_ref.shape
    half = d // 2

    for c in range(tm // rc):
        r0 = c * rc
        x = x_ref[pl.ds(r0, rc), :]
        ms = jnp.mean(x * x, axis=-1, keepdims=True)
        h = x * lax.rsqrt(ms + EPS) * g_ref[...]
        hb = h.astype(BF16).astype(F32)
        lo = lax.shift_right_logical(pltpu.bitcast(hb[:, :half], U32), jnp.uint32(16))
        hi = pltpu.bitcast(hb[:, half:], U32) & jnp.uint32(0xFFFF0000)
        pk_ref[pl.ds(r0, rc), :] = lo | hi
        logits = _dot_hp(h, wr_ref[...])
        lane = lax.broadcasted_iota(I32, logits.shape, 1)
        logits = jnp.where(lane < n_exp, logits, NEG_INF)
        mx = jnp.max(logits, axis=-1, keepdims=True)
        e = jnp.exp(logits - mx)
        probs = e / jnp.sum(e, axis=-1, keepdims=True)
        probs = jnp.where(lane < n_exp, probs, -1.0)
        lane_f = lane.astype(F32)
        p1 = jnp.max(probs, axis=-1, keepdims=True)
        i1 = jnp.min(jnp.where(probs == p1, lane_f, float(LANES)), axis=-1, keepdims=True)
        rest = jnp.where(lane_f == i1, -1.0, probs)
        p2 = jnp.max(rest, axis=-1, keepdims=True)
        i2 = jnp.min(jnp.where(rest == p2, lane_f, float(LANES)), axis=-1, keepdims=True)
        tot = p1 + p2
        idx_ref[pl.ds(r0, rc), :] = jnp.where(lane == 0, i1, jnp.where(lane == 1, i2, 0.0)).astype(I32)
        gate_ref[pl.ds(r0, rc), :] = jnp.where(lane == 0, p1 / tot,
                                                jnp.where(lane == 1, p2 / tot, 0.0))


def _router(x, g, w_router_pad, *, n_exp, tm):
    m, d = x.shape
    rc = _pick(tm, (256, 176, 128, 64, 32, 16, 8))
    est = 2 * _nbytes((tm, d), F32) + 2 * _nbytes((tm, d // 2), U32) + 16 * _nbytes((rc, d), F32)
    return pl.pallas_call(
        functools.partial(_router_body, n_exp=n_exp, rc=rc),
        out_shape=(jax.ShapeDtypeStruct((m, d // 2), U32),
                   jax.ShapeDtypeStruct((m, LANES), I32),
                   jax.ShapeDtypeStruct((m, LANES), F32)),
        grid=(m // tm,),
        in_specs=[pl.BlockSpec((tm, d), lambda i: (i, 0)),
                  pl.BlockSpec((1, d), lambda i: (0, 0)),
                  pl.BlockSpec((d, LANES), lambda i: (0, 0))],
        out_specs=(pl.BlockSpec((tm, d // 2), lambda i: (i, 0)),
                   pl.BlockSpec((tm, LANES), lambda i: (i, 0)),
                   pl.BlockSpec((tm, LANES), lambda i: (i, 0))),
        compiler_params=pltpu.CompilerParams(
            dimension_semantics=("parallel",), vmem_limit_bytes=_vmem_limit(est)),
        name="moe_router",
    )(x, g, w_router_pad)


GATHER_UNROLL = 8


def _row_copy(src_hbm, row, dst, r, sem):
    return pltpu.make_async_copy(src_hbm.at[pl.ds(row, 1), :], dst.at[pl.ds(r, 1), :], sem)


def _dispatch_body(nt_ref, tok_ref, nxt_ref, pk_hbm, a_ref, buf, sem, *, tm):
    i = pl.program_id(0)
    nt = nt_ref[0]
    half = buf.shape[2]

    def request(ids_ref, slot):
        def issue(c, carry):
            for u in range(GATHER_UNROLL):
                r = c * GATHER_UNROLL + u
                _row_copy(pk_hbm, ids_ref[0, 0, r], buf.at[slot], r, sem.at[slot]).start(
                    priority=u % 2)
            return carry
        lax.fori_loop(0, tm // GATHER_UNROLL, issue, 0)

    @pl.when(i == 0)
    def _():
        request(tok_ref, 0)

    def unpack(slot, r0, nrows):
        pk = buf[slot, pl.ds(r0, nrows), :]
        lo = pltpu.bitcast(lax.shift_left(pk, jnp.uint32(16)), F32)
        hi = pltpu.bitcast(pk & jnp.uint32(0xFFFF0000), F32)
        a_ref[pl.ds(r0, nrows), :half] = lo.astype(BF16)
        a_ref[pl.ds(r0, nrows), half:] = hi.astype(BF16)

    for slot in range(2):
        @pl.when(jnp.logical_and(i < nt, i % 2 == slot))
        def _(slot=slot):
            pltpu.make_async_copy(pk_hbm.at[pl.ds(0, tm), :], buf.at[slot], sem.at[slot]).wait()

            @pl.when(i + 1 < nt)
            def _():
                def both(c, carry):
                    r0 = pl.multiple_of(c * BF16_ROWS, BF16_ROWS)
                    for u in range(BF16_ROWS):
                        _row_copy(pk_hbm, nxt_ref[0, 0, r0 + u], buf.at[1 - slot], r0 + u,
                                  sem.at[1 - slot]).start(priority=u % 2)
                    unpack(slot, r0, BF16_ROWS)
                    return carry
                lax.fori_loop(0, tm // BF16_ROWS, both, 0)

            @pl.when(i + 1 >= nt)
            def _():
                unpack(slot, 0, tm)


def _dispatch(n_tiles, tok_sorted, packed, *, tm, r_max):
    m, half = packed.shape
    t_max = r_max // tm
    assert tm % GATHER_UNROLL == 0 and tm % BF16_ROWS == 0
    tok_tiles = tok_sorted.reshape(t_max, 1, tm)
    return pl.pallas_call(
        functools.partial(_dispatch_body, tm=tm),
        out_shape=jax.ShapeDtypeStruct((r_max, 2 * half), BF16),
        grid_spec=pltpu.PrefetchScalarGridSpec(
            num_scalar_prefetch=1,
            grid=(t_max,),
            in_specs=[pl.BlockSpec((1, 1, tm), lambda i, nt: (i, 0, 0), memory_space=pltpu.SMEM),
                      pl.BlockSpec((1, 1, tm), lambda i, nt: (jnp.minimum(i + 1, t_max - 1), 0, 0),
                                   memory_space=pltpu.SMEM),
                      pl.BlockSpec(memory_space=pl.ANY)],
            out_specs=pl.BlockSpec((tm, 2 * half), lambda i, nt: (jnp.minimum(i, nt[0] - 1), 0)),
            scratch_shapes=[pltpu.VMEM((2, tm, half), U32), pltpu.SemaphoreType.DMA((2,))]),
        compiler_params=pltpu.CompilerParams(dimension_semantics=("arbitrary",)),
        name="moe_dispatch",
    )(n_tiles, tok_tiles, tok_tiles, packed)


def _expert_up_body(te_ref, nt_ref, a_ref, w1_ref, w3_ref, h_ref, w1bf, w3bf, *, kc, rem):
    f = pl.program_id(0)
    i = pl.program_id(1)
    last = pl.num_programs(0) - 1
    d, tf = w1bf.shape
    fresh = jnp.logical_or(i == 0, te_ref[i] != te_ref[jnp.maximum(i - 1, 0)])
    live = i < nt_ref[0]

    def run(cols, when):
        @pl.when(jnp.logical_and(when, jnp.logical_and(fresh, live)))
        def _():
            def cast(c, carry):
                r0 = pl.multiple_of(c * kc, kc)
                w1bf[pl.ds(r0, kc), 0:cols] = w1_ref[pl.ds(r0, kc), 0:cols].astype(BF16)
                w3bf[pl.ds(r0, kc), 0:cols] = w3_ref[pl.ds(r0, kc), 0:cols].astype(BF16)
                return carry
            lax.fori_loop(0, d // kc, cast, 0)

        @pl.when(jnp.logical_and(when, live))
        def _():
            a = a_ref[...]
            h_ref[:, 0:cols] = (jax.nn.silu(_dot(a, w1bf[:, 0:cols]))
                                * _dot(a, w3bf[:, 0:cols])).astype(h_ref.dtype)

    if rem:
        run(tf, f < last)
        run(rem, f == last)
    else:
        run(tf, True)


def _expert_up(tile_expert, n_tiles, a_sorted, w1, w3, *, tm, tf):
    r_max, d = a_sorted.shape
    n_exp, _, fe = w1.shape
    t_max = r_max // tm
    kc = _pick(d, (512, 256, 128))
    row = lambda f, i, te, nt: jnp.minimum(i, nt[0] - 1)
    est = (2 * _nbytes((tm, d), BF16) + 2 * 2 * _nbytes((d, tf), F32) + 2 * _nbytes((d, tf), BF16)
           + 2 * _nbytes((tm, tf), BF16) + 6 * _nbytes((tm, tf), F32))
    return pl.pallas_call(
        functools.partial(_expert_up_body, kc=kc, rem=fe % tf),
        out_shape=jax.ShapeDtypeStruct((r_max, fe), BF16),
        grid_spec=pltpu.PrefetchScalarGridSpec(
            num_scalar_prefetch=2,
            grid=(pl.cdiv(fe, tf), t_max),
            in_specs=[pl.BlockSpec((tm, d), lambda f, i, te, nt: (row(f, i, te, nt), 0)),
                      pl.BlockSpec((None, d, tf), lambda f, i, te, nt: (te[i], 0, f)),
                      pl.BlockSpec((None, d, tf), lambda f, i, te, nt: (te[i], 0, f))],
            out_specs=pl.BlockSpec((tm, tf), lambda f, i, te, nt: (row(f, i, te, nt), f)),
            scratch_shapes=[pltpu.VMEM((d, tf), BF16), pltpu.VMEM((d, tf), BF16)]),
        compiler_params=pltpu.CompilerParams(
            dimension_semantics=("arbitrary", "arbitrary"), vmem_limit_bytes=_vmem_limit(est)),
        name="moe_expert_up",
    )(tile_expert, n_tiles, a_sorted, w1, w3)


def _expert_down_body(te_ref, nt_ref, h_ref, w2_ref, y_ref, w2bf, *, kc):
    i = pl.program_id(1)
    fe = h_ref.shape[1]
    fresh = jnp.logical_or(i == 0, te_ref[i] != te_ref[jnp.maximum(i - 1, 0)])

    @pl.when(jnp.logical_and(fresh, i < nt_ref[0]))
    def _():
        def cast(c, carry):
            r0 = pl.multiple_of(c * kc, kc)
            w2bf[pl.ds(r0, kc), :] = w2_ref[pl.ds(r0, kc), :].astype(BF16)
            return carry
        lax.fori_loop(0, fe // kc, cast, 0)

    @pl.when(i < nt_ref[0])
    def _():
        y_ref[...] = _dot(h_ref[...], w2bf[...])


def _expert_down(tile_expert, n_tiles, h_sorted, w2, *, tm, tn):
    r_max, fe = h_sorted.shape
    d = w2.shape[2]
    t_max = r_max // tm
    kc = _pick(fe, (512, 256, 128))
    row = lambda n, i, te, nt: jnp.minimum(i, nt[0] - 1)
    est = (2 * _nbytes((tm, fe), BF16) + 2 * _nbytes((fe, tn), F32) + _nbytes((fe, tn), BF16)
           + 4 * _nbytes((tm, tn), F32))
    return pl.pallas_call(
        functools.partial(_expert_down_body, kc=kc),
        out_shape=jax.ShapeDtypeStruct((r_max, d), F32),
        grid_spec=pltpu.PrefetchScalarGridSpec(
            num_scalar_prefetch=2,
            grid=(d // tn, t_max),
            in_specs=[pl.BlockSpec((tm, fe), lambda n, i, te, nt: (row(n, i, te, nt), 0)),
                      pl.BlockSpec((None, fe, tn), lambda n, i, te, nt: (te[i], 0, n))],
            out_specs=pl.BlockSpec((tm, tn), lambda n, i, te, nt: (row(n, i, te, nt), n)),
            scratch_shapes=[pltpu.VMEM((fe, tn), BF16)]),
        compiler_params=pltpu.CompilerParams(
            dimension_semantics=("arbitrary", "arbitrary"), vmem_limit_bytes=_vmem_limit(est)),
        name="moe_expert_down",
    )(tile_expert, n_tiles, h_sorted, w2)


def _combine_body(slot_ref, nxt_ref, x_ref, gate_ref, y_hbm, *rest, tc, n_first):
    o_refs, (buf, sem) = rest[:-2], rest[-2:]
    i = pl.program_id(0)
    n = pl.num_programs(0)
    unroll = GATHER_UNROLL // TOP_K

    def request(ids_ref, slot):
        def issue(c, carry):
            for u in range(unroll):
                r = c * unroll + u
                for k in range(TOP_K):
                    _row_copy(y_hbm, ids_ref[0, 0, TOP_K * r + k], buf.at[slot, k], r,
                              sem.at[slot]).start(priority=k % 2)
            return carry
        lax.fori_loop(0, tc // unroll, issue, 0)

    @pl.when(i == 0)
    def _():
        request(slot_ref, 0)

    for s in range(2):
        @pl.when(jnp.logical_and(i + 1 < n, (i + 1) % 2 == s))
        def _(s=s):
            request(nxt_ref, s)

    slot = i % 2
    for k in range(TOP_K):
        pltpu.make_async_copy(y_hbm.at[pl.ds(0, tc), :], buf.at[slot, k], sem.at[slot]).wait()
    g = gate_ref[...]
    val = x_ref[...] + (g[:, 0:1] * buf[slot, 0] + g[:, 1:2] * buf[slot, 1])
    if len(o_refs) == 1:
        o_refs[0][...] = val
    else:
        @pl.when(i < n_first)
        def _():
            o_refs[0][...] = val

        @pl.when(i >= n_first)
        def _():
            o_refs[1][...] = val


def _combine(slots, x, gates, y_sorted, *, tc, split=None):
    m, d = x.shape
    n = m // tc
    assert tc % (GATHER_UNROLL // TOP_K) == 0
    est = 4 * _nbytes((tc, d), F32) + 2 * TOP_K * _nbytes((tc, d), F32) + 6 * _nbytes((tc, d), F32)
    slot_tiles = slots.reshape(n, 1, TOP_K * tc)
    if split is None or split % tc or (m - split) % tc:
        n_first = n
        out_shape = jax.ShapeDtypeStruct((m, d), F32)
        out_specs = pl.BlockSpec((tc, d), lambda i: (i, 0))
    else:
        n_first = split // tc
        out_shape = (jax.ShapeDtypeStruct((split, d), F32), jax.ShapeDtypeStruct((m - split, d), F32))
        out_specs = (pl.BlockSpec((tc, d), lambda i: (jnp.minimum(i, n_first - 1), 0)),
                     pl.BlockSpec((tc, d), lambda i: (jnp.maximum(i - n_first, 0), 0)))
    return pl.pallas_call(
        functools.partial(_combine_body, tc=tc, n_first=n_first),
        out_shape=out_shape,
        grid=(n,),
        in_specs=[pl.BlockSpec((1, 1, TOP_K * tc), lambda i: (i, 0, 0), memory_space=pltpu.SMEM),
                  pl.BlockSpec((1, 1, TOP_K * tc), lambda i: (jnp.minimum(i + 1, n - 1), 0, 0),
                               memory_space=pltpu.SMEM),
                  pl.BlockSpec((tc, d), lambda i: (i, 0)),
                  pl.BlockSpec((tc, LANES), lambda i: (i, 0)),
                  pl.BlockSpec(memory_space=pl.ANY)],
        out_specs=out_specs,
        scratch_shapes=[pltpu.VMEM((2, TOP_K, tc, d), F32), pltpu.SemaphoreType.DMA((2,))],
        compiler_params=pltpu.CompilerParams(
            dimension_semantics=("arbitrary",), vmem_limit_bytes=_vmem_limit(est)),
        name="moe_combine",
    )(slot_tiles, slot_tiles, x, gates, y_sorted)


def _moe(x, g_ffn, w_router, w_e1, w_e3, w_e2, *, tm_tok, split=None):
    m, d = x.shape
    n_exp, _, fe = w_e1.shape
    tm = _pick(m * TOP_K, (512, 256, 128, 64, 32, 16))
    wr = jnp.zeros((d, LANES), F32).at[:, :n_exp].set(w_router)
    packed, idx128, gate128 = _router(x, g_ffn, wr, n_exp=n_exp, tm=tm_tok)

    idx = idx128[:, :TOP_K]
    mask = jnp.sum(idx[:, :, None] == jnp.arange(n_exp, dtype=I32)[None, None, :], axis=1).astype(I32)
    counts = jnp.sum(mask, axis=0)
    padded = ((counts + tm - 1) // tm) * tm
    ends = jnp.cumsum(padded)
    starts = ends - padded
    pos = jnp.cumsum(mask, axis=0) - mask
    slot = starts[idx] + jnp.take_along_axis(pos, idx, axis=1)
    t_max = (m * TOP_K) // tm + n_exp
    r_max = t_max * tm
    tok_sorted = jnp.zeros((r_max,), I32).at[slot.reshape(-1)].set(
        jnp.repeat(jnp.arange(m, dtype=I32), TOP_K))
    n_tiles = (ends[-1] // tm).astype(I32).reshape(1)
    tile_start = jnp.minimum(jnp.arange(t_max, dtype=I32), n_tiles[0] - 1) * tm
    tile_expert = jnp.minimum(jnp.sum(ends[None, :] <= tile_start[:, None], axis=1),
                              n_exp - 1).astype(I32)

    a_sorted = _dispatch(n_tiles, tok_sorted, packed, tm=tm, r_max=r_max)
    th = _pick(fe, (256, 128))
    tf = 2 * th if fe >= 2 * th else th
    h_sorted = _expert_up(tile_expert, n_tiles, a_sorted, w_e1, w_e3, tm=tm, tf=tf)
    y_sorted = _expert_down(tile_expert, n_tiles, h_sorted, w_e2, tm=tm,
                            tn=_pick(d, (1024, 512, 256, 128)))
    tc = _pick(m, (256, 128, 64, 32, 16, 8))
    return _combine(slot, x, gate128, y_sorted, tc=tc, split=split)


def _block_diag(blocks):
    g, r, c = blocks.shape
    eye = jnp.eye(g, dtype=blocks.dtype)
    return (blocks[:, :, None, :] * eye[:, None, :, None]).reshape(g * r, g * c)


def _rel_table(rel_bias_l, q_pos, k_pos, max_rel):
    rel = np.clip(q_pos[:, None] - k_pos[None, :], -(CHUNK - 1), max_rel) + (CHUNK - 1)
    return rel_bias_l[:, rel].astype(F32)


def _band_table(rel_bias_l, tq, max_rel):
    n_heads = rel_bias_l.shape[0]
    past = BAND_PAST * CHUNK
    win = past + tq
    ring = -(-(tq + win - 1) // LANES) * LANES
    diff = np.arange(ring)
    diff = np.where(diff < tq, diff, diff - ring)
    idx = np.clip(diff + past, -(CHUNK - 1), max_rel) + (CHUNK - 1)
    by_diff = jnp.pad(rel_bias_l.astype(F32)[:, idx], ((0, 16 - n_heads), (0, 0)))

    def body(u_ref, o_ref):
        hp = pl.program_id(0)
        j = lax.broadcasted_iota(I32, (win, tq), 0)
        i = lax.broadcasted_iota(I32, (win, tq), 1)
        gap = (i + past) // CHUNK - j // CHUNK
        valid = (gap >= 0) & (gap <= BAND_PAST)
        for e in range(2):
            rows = jnp.broadcast_to(u_ref[pl.ds(2 * hp + e, 1), :], (win, ring))
            shifted = pltpu.roll(rows, 0, 1, stride=1, stride_axis=0)
            o_ref[:, e * tq:(e + 1) * tq] = jnp.where(valid, shifted[:, 0:tq] * LOG2E, NEG_INF)

    return pl.pallas_call(
        body,
        out_shape=jax.ShapeDtypeStruct((n_heads // 2, win, 2 * tq), F32),
        grid=(n_heads // 2,),
        in_specs=[pl.BlockSpec((16, ring), lambda hp: (0, 0))],
        out_specs=pl.BlockSpec((None, win, 2 * tq), lambda hp: (hp, 0, 0)),
        compiler_params=pltpu.CompilerParams(dimension_semantics=("parallel",)),
        name="band_table",
    )(by_diff)


def _tile_gain(g):
    return jnp.tile(g.astype(F32), LANES // g.shape[0]).reshape(1, LANES)


def kernel(x_prompt, x_sample, mem_prompt, cache_fox_k, cache_fox_v, cache_fox_logf, cache_band_k, cache_band_v, state_ssm_re, state_ssm_im, cache_mem_k, cache_mem_v, g_mix, w_in, b_f, g_qa, g_ka, g_qb, g_kb, rel_bias, lam_re, lam_im, log_dt, ssm_b_re, ssm_b_im, ssm_c_re, ssm_c_im, ssm_d, w_glu, b_glu, g_mix_out, w_out, g_cross, g_mem, w_cq, w_ck, w_cv, g_cq, g_ck, w_co, g_ffn, w_ff1, w_ff3, w_ff2, w_router, w_e1, w_e3, w_e2):
    batch, seq, d = x_prompt.shape
    dbatch, dseq, _ = x_sample.shape
    depth = g_mix.shape[0]
    past_len, h_a, dh = cache_fox_k.shape[2:]
    band_rows, h_b = cache_band_k.shape[2:4]
    g_c, p_state = lam_re.shape[1:]
    w_a, w_b, w_c = h_a * dh, h_b * dh, g_c * SSM_GROUP
    n_mem, h_m, dh_m = cache_mem_k.shape[2:]
    w_m = h_m * dh_m
    max_rel = rel_bias.shape[2] - CHUNK
    assert dh == 64 and h_a % 2 == 0 and h_b % 2 == 0 and dh_m == LANES
    assert w_a == w_b and w_c <= w_a and h_a <= 16
    mp, ms = batch * seq, dbatch * dseq
    m = mp + ms
    pa, pb = h_a // 2, h_b // 2
    n_state = g_c * p_state
    n_blk = 2 if (w_c % 512 == 0) else 1
    nband = min(BAND_PAST * CHUNK, seq)
    tm = _pick(m, (1056, 1024, 768, 512, 256, 128, 64, 32, 16))
    tq_band = _pick(seq, (256, 128, 64))

    xp2, xs2 = x_prompt.reshape(mp, d), x_sample.reshape(ms, d)
    tail0 = m - tm
    split_x = ms <= tm and tail0 <= mp and tail0 >= tm
    x_tail = jnp.concatenate([xp2[tail0:], xs2], axis=0) if split_x else None
    x = xp2 if split_x else jnp.concatenate([xp2, xs2], axis=0)
    fox_kt, fox_vt, band_kt, band_vt = (jnp.transpose(c, (0, 1, 3, 4, 2)) for c in
                                        (cache_fox_k, cache_fox_v, cache_band_k, cache_band_v))

    outs = {k: [] for k in ("p_fl", "p_sr", "p_si", "p_mk", "p_mv",
                            "s_fk", "s_fv", "s_fl", "s_bk", "s_bv", "s_sr", "s_si")}
    cache_a = cache_b = None
    for l in range(depth):
        sizes = (w_a, w_a, w_a, h_a, w_b, w_b, w_b)
        cuts = [sum(sizes[:i]) for i in range(len(sizes) + 1)]
        w_cat = _regroup_columns(
            w_in, l, ((0, cuts[3]), (cuts[4], cuts[7]), (cuts[7], w_in.shape[2]), (cuts[3], cuts[4])),
            7 * w_a)
        first = split_x and l == 0
        proj = _mm([x], [w_cat], gain=g_mix[l].reshape(1, d), tm=tm, tn=w_a, name="proj_in",
                   m_rows=m, a_tail=x_tail if first else None)
        col_uc = 6 * w_a

        assert (col_uc + w_c) % LANES == 0
        lf_p, f_aug = _logf_rows(proj, jnp.pad(b_f[l].astype(F32), (0, LANES - h_a)).reshape(1, LANES),
                                 batch=batch, seq=seq, n_heads=h_a, n_pairs=pa,
                                 col_block=(col_uc + w_c) // LANES)

        t_all = -(-(past_len + dseq) // LANES) * LANES
        fa_s = proj[mp:, col_uc + w_c:col_uc + w_c + h_a].reshape(dbatch, dseq, h_a)
        x_s = jnp.concatenate([jnp.transpose(cache_fox_logf[l], (0, 2, 1)),
                               jnp.transpose(fa_s, (0, 2, 1)),
                               jnp.zeros((dbatch, h_a, t_all - past_len - dseq), F32)], axis=2)
        bias_s = jnp.broadcast_to(jnp.tile(b_f[l], dbatch)[:, None], (dbatch * h_a, LANES))
        lf_s, f_s = _logf_cumsum(x_s.reshape(dbatch * h_a, t_all), bias_s,
                                 raw_from=past_len, valid_to=past_len + dseq)
        lf_s = lf_s.reshape(dbatch, h_a, t_all)[:, :, past_len:past_len + dseq]
        f_s = f_s.reshape(dbatch, h_a, t_all)
        fs_row = jnp.pad(f_s, ((0, 0), (0, 16 - h_a), (0, 0)))
        fs_col = jnp.pad(jnp.transpose(f_s[:, :, past_len:past_len + dseq], (0, 2, 1)),
                         ((0, 0), (0, 0), (0, LANES - h_a))).reshape(ms, LANES)

        gqa, gka = _tile_gain(g_qa[l]), _tile_gain(g_ka[l])
        oa, *cache_a = _fox_prompt(proj, f_aug, gqa, gka, batch=batch, seq=seq, n_pairs=pa,
                                   col_q=0, col_k=pa, col_v=2 * pa, m_rows=m, layer=l, depth=depth,
                                   prev=cache_a)
        oa, kn_as = _sample_attn(proj, fox_kt, fox_vt, l, fs_col, fs_row,
                                 gqa, gka, oa, mode="fox", batch=dbatch, s_new=dseq, n_pairs=pa,
                                 col_q=0, row0=mp)

        gqb, gkb = _tile_gain(g_qb[l]), _tile_gain(g_kb[l])
        tab = _band_table(rel_bias[l], tq_band, max_rel)
        ob, *cache_b = _band_prompt(proj, tab, gqb, gkb, batch=batch, seq=seq, n_pairs=pb,
                                    col_q=3 * pa, col_k=3 * pa + pb, col_v=3 * pa + 2 * pb,
                                    tq=tq_band, m_rows=m, layer=l, depth=depth, prev=cache_b)
        tab_s = _rel_table(rel_bias[l], band_rows + np.arange(dseq), np.arange(band_rows + dseq),
                           max_rel)
        ob, kn_bs = _sample_attn(proj, band_kt, band_vt, l,
                                 tab_s[:, :, :band_rows], tab_s[:, :, band_rows:], gqb, gkb, ob,
                                 mode="band", batch=dbatch, s_new=dseq, n_pairs=pb, col_q=3, row0=mp)

        rep = lambda a: jnp.repeat(a.astype(F32), SSM_GROUP, axis=0)
        a_re, a_im, bb_re, bb_im = _s5_prep(
            rep(lam_re[l]), rep(lam_im[l]),
            jnp.broadcast_to(rep(log_dt[l])[:, None], (g_c * SSM_GROUP, p_state)),
            jnp.transpose(ssm_b_re[l], (0, 2, 1)).reshape(g_c * SSM_GROUP, p_state),
            jnp.transpose(ssm_b_im[l], (0, 2, 1)).reshape(g_c * SSM_GROUP, p_state))
        a_re = a_re.reshape(g_c, SSM_GROUP, p_state)[:, 0, :].reshape(1, n_state)
        a_im = a_im.reshape(g_c, SSM_GROUP, p_state)[:, 0, :].reshape(1, n_state)
        b_re_d = _block_diag(bb_re.reshape(g_c, SSM_GROUP, p_state))
        b_im_d = _block_diag(bb_im.reshape(g_c, SSM_GROUP, p_state))
        c_re_d = _block_diag(jnp.transpose(ssm_c_re[l], (0, 2, 1))).astype(BF16)
        c_im_d = _block_diag(jnp.transpose(ssm_c_im[l], (0, 2, 1))).astype(BF16)
        d_row = ssm_d[l].reshape(1, w_c)
        s5_tail = (c_re_d, c_im_d, d_row, w_glu[l], b_glu[l].reshape(1, w_c))
        assert col_uc % w_c == 0 and w_c % LANES == 0
        u_s = jnp.transpose(proj[mp:, col_uc:col_uc + w_c].reshape(dbatch, dseq, w_c),
                            (1, 0, 2)).reshape(ms, w_c)
        zeros_p = jnp.zeros((batch, n_state), F32)
        oc_p, sr_p, si_p = _s5(proj, zeros_p, zeros_p, a_re, a_im, b_re_d.astype(BF16),
                               b_im_d.astype(BF16), *s5_tail, nb=batch, seq=seq, n_blk=n_blk,
                               passes=1, col_block=col_uc // w_c)
        oc_s, sr_s, si_s = _s5(u_s, state_ssm_re[l].reshape(dbatch, n_state),
                               state_ssm_im[l].reshape(dbatch, n_state), a_re, a_im, b_re_d, b_im_d,
                               *s5_tail, nb=dbatch, seq=dseq, n_blk=n_blk, passes=3)
        oc = jnp.concatenate(
            [oc_p.reshape(mp, w_c),
             jnp.transpose(oc_s.reshape(dseq, dbatch, w_c), (1, 0, 2)).reshape(ms, w_c)], axis=0)

        x = _mm([oa, ob, oc], [w_out[l].astype(BF16)], gain=g_mix_out[l].reshape(1, -1), residual=x,
                tm=tm, tn=_pick(d, (1024, 512, 256, 128)), name="merge_out",
                res_tail=x_tail if first else None)

        outs["p_fl"].append(lf_p[:, :h_a].reshape(batch, seq, h_a))
        outs["p_sr"].append(sr_p.reshape(batch, g_c, p_state))
        outs["p_si"].append(si_p.reshape(batch, g_c, p_state))
        outs["s_fk"].append(kn_as.reshape(dbatch, dseq, h_a, dh))
        outs["s_fv"].append(proj[mp:, 2 * w_a:3 * w_a].reshape(dbatch, dseq, h_a, dh))
        outs["s_fl"].append(jnp.transpose(lf_s, (0, 2, 1)))
        outs["s_bk"].append(kn_bs.reshape(dbatch, dseq, h_b, dh))
        outs["s_bv"].append(proj[mp:, 5 * w_a:6 * w_a].reshape(dbatch, dseq, h_b, dh))
        outs["s_sr"].append(sr_s.reshape(dbatch, g_c, p_state))
        outs["s_si"].append(si_s.reshape(dbatch, g_c, p_state))

        mem2 = mem_prompt.reshape(batch * n_mem, d)
        tmm = _pick(batch * n_mem, (1024, 512, 256, 128))
        gck = jnp.tile(g_ck[l].astype(F32), h_m).reshape(1, w_m)
        gcq = jnp.tile(g_cq[l].astype(F32), h_m).reshape(1, w_m)
        mk = _mm([mem2], [w_ck[l]], gain=g_mem[l].reshape(1, d), group_gain=gck,
                 epilogue="group_norm", tm=tmm, tn=w_m, name="mem_k")
        mv = _mm([mem2], [w_cv[l]], gain=g_mem[l].reshape(1, d), tm=tmm, tn=w_m, name="mem_v")
        outs["p_mk"].append(mk.reshape(batch, n_mem, h_m, dh_m))
        outs["p_mv"].append(mv.reshape(batch, n_mem, h_m, dh_m))
        q_c = _mm([x], [w_cq[l].astype(BF16)], gain=g_cross[l].reshape(1, d), group_gain=gcq,
                  epilogue="group_norm", tm=tm, tn=w_m, name="cross_q")
        o_c = _cross_attn(q_c, mk, mv, None, batch=batch, q_len=seq, n_mem=n_mem, n_heads=h_m,
                          dh=dh_m, row0=0, m_rows=m)
        o_c = _cross_attn(q_c, cache_mem_k, cache_mem_v, o_c, batch=dbatch, q_len=dseq, n_mem=n_mem,
                          n_heads=h_m, dh=dh_m, row0=mp, m_rows=m, layer=l)
        x = _mm([o_c], [w_co[l].astype(BF16)], residual=x, tm=tm,
                tn=_pick(d, (1024, 512, 256, 128)), name="cross_out")

        i = l // 2
        if l % 2 == 0:
            hmid = _mm([x], [w_ff1[i].astype(BF16), w_ff3[i].astype(BF16)],
                       gain=g_ffn[l].reshape(1, d), epilogue="swiglu", out_dtype=BF16, tm=tm,
                       tn=_pick(w_ff1.shape[2], (512, 256, 128)), name="ffn_up")
            x = _mm([hmid], [w_ff2[i].astype(BF16)], residual=x, tm=tm,
                    tn=_pick(d, (512, 256, 128)), name="ffn_down")
        else:
            x = _moe(x, g_ffn[l].reshape(1, d), w_router[i], w_e1[i], w_e3[i], w_e2[i], tm_tok=tm,
                     split=mp if l == depth - 1 else None)

    st = lambda k: jnp.stack(outs[k])
    xp, xs = x if isinstance(x, tuple) else (x[:mp], x[mp:])
    per_head = lambda t, h: jnp.transpose(t.reshape(depth, batch, h, dh, -1), (0, 1, 4, 2, 3))
    return (xp.reshape(batch, seq, d), xs.reshape(dbatch, dseq, d),
            per_head(cache_a[0], h_a), per_head(cache_a[1], h_a), st("p_fl"),
            per_head(cache_b[0][..., seq - nband:], h_b), per_head(cache_b[1][..., seq - nband:], h_b),
            st("p_sr"), st("p_si"),
            st("p_mk"), st("p_mv"), st("s_fk"), st("s_fv"), st("s_fl"), st("s_bk"), st("s_bv"),
            st("s_sr"), st("s_si"))
```
